```python
import jax, jax.numpy as jnp
from jax import lax
import numpy as np

D_MODEL = 2048
BATCH = 2
SEQ = 4096
DEPTH = 1

D_MIX = D_MODEL
D_CONV = D_MIX // 2
N_CONV_GROUPS = 8
CONV_WIDTH = 3
D_SGU = D_MIX - D_CONV
N_SGU_HEADS = 8
SGU_HEAD_DIM = D_SGU // N_SGU_HEADS
CHUNK = 128
D_IN_PROJ = 3 * D_CONV + 2 * D_SGU

N_EXPERTS = 32
TOP_K = 4
D_FF = D_MODEL
SWIGLU_LIMIT = 7.0
SWIGLU_ALPHA = 1.702
EXPERT_BLOCK = 128

RMS_EPS = 1e-5
LN_EPS = 1e-5

kernel_name = "hybrid_shortconv_sgu_moe_block"


def rms_norm(x, g):
    xf = x.astype(jnp.float32)
    y = xf * lax.rsqrt(jnp.mean(xf * xf, axis=-1, keepdims=True) + RMS_EPS)
    return (y * g.astype(jnp.float32)).astype(x.dtype)


def group_rms_norm(x, g, n_groups):
    shp = x.shape
    xf = x.astype(jnp.float32).reshape(shp[:-1] + (n_groups, shp[-1] // n_groups))
    y = xf * lax.rsqrt(jnp.mean(xf * xf, axis=-1, keepdims=True) + RMS_EPS)
    return (y.reshape(shp) * g.astype(jnp.float32)).astype(x.dtype)


def group_layer_norm(x, g, b, n_groups):
    shp = x.shape
    xf = x.astype(jnp.float32).reshape(shp[:-1] + (n_groups, shp[-1] // n_groups))
    mu = jnp.mean(xf, axis=-1, keepdims=True)
    xc = xf - mu
    y = xc * lax.rsqrt(jnp.mean(xc * xc, axis=-1, keepdims=True) + LN_EPS)
    return (y.reshape(shp) * g.astype(jnp.float32) + b.astype(jnp.float32)).astype(x.dtype)


def causal_dwconv(z, w):
    s = z.shape[1]
    zp = jnp.pad(z, ((0, 0), (CONV_WIDTH - 1, 0), (0, 0)))
    return sum(zp[:, k:k + s, :] * w[k] for k in range(CONV_WIDTH))


def spatial_gating(u, v, sgu_ln_g, sgu_ln_b, sgu_w, sgu_b):
    bsz, s, _ = v.shape
    n_chunks = s // CHUNK
    vn = group_layer_norm(v, sgu_ln_g, sgu_ln_b, N_SGU_HEADS)
    vc = vn.reshape(bsz, n_chunks, CHUNK, N_SGU_HEADS, SGU_HEAD_DIM)
    causal = jnp.tril(jnp.ones((CHUNK, CHUNK), dtype=bool))
    w_masked = jnp.where(causal[None], sgu_w, jnp.zeros_like(sgu_w))
    mixed = jnp.einsum('hts,bcshd->bcthd', w_masked, vc)
    mixed = mixed + jnp.transpose(sgu_b)[None, None, :, :, None]
    return u * mixed.reshape(bsz, s, D_SGU)


def hybrid_mixer(xn, w_in, conv_w, sgu_ln_g, sgu_ln_b, sgu_w, sgu_b, gn_conv, gn_sgu, w_out):
    proj = jnp.einsum('bsd,de->bse', xn, w_in)
    b_gate, c_gate, h, u, v = jnp.split(
        proj, [D_CONV, 2 * D_CONV, 3 * D_CONV, 3 * D_CONV + D_SGU], axis=-1)
    y_conv = b_gate * causal_dwconv(c_gate * h, conv_w)
    y_sgu = spatial_gating(jax.nn.gelu(u, approximate=False),
                           jax.nn.gelu(v, approximate=False),
                           sgu_ln_g, sgu_ln_b, sgu_w, sgu_b)
    y = jnp.concatenate([group_rms_norm(y_conv, gn_conv, N_CONV_GROUPS),
                         group_rms_norm(y_sgu, gn_sgu, N_SGU_HEADS)], axis=-1)
    return jnp.einsum('bse,ed->bsd', y, w_out)


def moe_ffn(xn, w_router, b_router, w_gate_up, b_gate_up, w_down, b_down):
    bsz, s, d = xn.shape
    n_tok = bsz * s
    xt = xn.reshape(n_tok, d)
    logits = (xt @ w_router + b_router).astype(jnp.float32)
    top_vals, top_idx = lax.top_k(logits, TOP_K)
    gates = jax.nn.softmax(top_vals, axis=-1).astype(xn.dtype)

    m = n_tok * TOP_K
    flat_e = top_idx.reshape(m).astype(jnp.int32)
    flat_tok = jnp.repeat(jnp.arange(n_tok, dtype=jnp.int32), TOP_K)
    flat_g = gates.reshape(m)
    order = jnp.argsort(flat_e, stable=True)
    sorted_e = flat_e[order]
    sorted_tok = flat_tok[order]
    sorted_g = flat_g[order]
    sizes = jnp.bincount(flat_e, length=N_EXPERTS).astype(jnp.int32)
    padded = ((sizes + EXPERT_BLOCK - 1) // EXPERT_BLOCK) * EXPERT_BLOCK
    starts = jnp.cumsum(sizes) - sizes
    padded_ends = jnp.cumsum(padded)
    padded_starts = padded_ends - padded
    dest = padded_starts[sorted_e] + jnp.arange(m, dtype=jnp.int32) - starts[sorted_e]
    n_blocks = -(-(m + N_EXPERTS * EXPERT_BLOCK) // EXPERT_BLOCK)
    p_rows = n_blocks * EXPERT_BLOCK
    buf_tok = jnp.full((p_rows,), n_tok, dtype=jnp.int32).at[dest].set(sorted_tok)
    buf_g = jnp.zeros((p_rows,), dtype=xn.dtype).at[dest].set(sorted_g)
    block_start = jnp.arange(n_blocks, dtype=jnp.int32) * EXPERT_BLOCK
    block_expert = jnp.minimum(
        jnp.searchsorted(padded_ends, block_start, side='right'), N_EXPERTS - 1).astype(jnp.int32)
    x_pad = jnp.concatenate([xt, jnp.zeros((1, d), dtype=xt.dtype)], axis=0)
    buf_x = x_pad[buf_tok].reshape(n_blocks, EXPERT_BLOCK, d)

    def expert_block(args):
        xb, e = args
        gu = xb @ w_gate_up[e] + b_gate_up[e]
        gate, up = jnp.split(gu, 2, axis=-1)
        gate = jnp.minimum(gate, SWIGLU_LIMIT)
        up = jnp.clip(up, -SWIGLU_LIMIT, SWIGLU_LIMIT)
        glu = gate * jax.nn.sigmoid(gate * SWIGLU_ALPHA)
        return ((up + 1.0) * glu) @ w_down[e] + b_down[e]

    out = lax.map(expert_block, (buf_x, block_expert)).reshape(p_rows, d)
    out = out * buf_g[:, None]
    y = jax.ops.segment_sum(out, buf_tok, num_segments=n_tok + 1)[:n_tok]
    return y.reshape(bsz, s, d)


def setup_inputs(seed: int = 0) -> dict:
    key = jax.random.key(seed)
    ks = jax.random.split(key, 20)
    f32 = jnp.float32

    def nrm(k, shape, scale):
        return jax.random.normal(k, shape, dtype=f32) * scale

    def gain(k, n):
        return 1.0 + 0.02 * jax.random.normal(k, (n,), dtype=f32)

    return {
        "x": jax.random.normal(ks[0], (BATCH, SEQ, D_MODEL), dtype=f32),
        "ln_mix_g": gain(ks[1], D_MODEL),
        "w_in": nrm(ks[2], (D_MODEL, D_IN_PROJ), D_MODEL ** -0.5),
        "conv_w": nrm(ks[3], (CONV_WIDTH, D_CONV), CONV_WIDTH ** -0.5),
        "sgu_ln_g": gain(ks[4], D_SGU),
        "sgu_ln_b": nrm(ks[5], (D_SGU,), 0.02),
        "sgu_w": nrm(ks[6], (N_SGU_HEADS, CHUNK, CHUNK), CHUNK ** -0.5),
        "sgu_b": 1.0 + nrm(ks[7], (N_SGU_HEADS, CHUNK), 0.02),
        "gn_conv": gain(ks[8], D_CONV),
        "gn_sgu": gain(ks[9], D_SGU),
        "w_out": nrm(ks[10], (D_MIX, D_MODEL), D_MIX ** -0.5),
        "ln_ffn_g": gain(ks[11], D_MODEL),
        "w_router": nrm(ks[12], (D_MODEL, N_EXPERTS), D_MODEL ** -0.5),
        "b_router": nrm(ks[13], (N_EXPERTS,), 0.01),
        "w_gate_up": nrm(ks[14], (N_EXPERTS, D_MODEL, 2 * D_FF), D_MODEL ** -0.5),
        "b_gate_up": nrm(ks[15], (N_EXPERTS, 2 * D_FF), 0.02),
        "w_down": nrm(ks[16], (N_EXPERTS, D_FF, D_MODEL), D_FF ** -0.5),
        "b_down": nrm(ks[17], (N_EXPERTS, D_MODEL), 0.02),
        "ln_final_g": gain(ks[18], D_MODEL),
    }


def reference(x, ln_mix_g, w_in, conv_w, sgu_ln_g, sgu_ln_b, sgu_w, sgu_b, gn_conv, gn_sgu,
              w_out, ln_ffn_g, w_router, b_router, w_gate_up, b_gate_up, w_down, b_down,
              ln_final_g):
    h = x
    for _ in range(DEPTH):
        h = h + hybrid_mixer(rms_norm(h, ln_mix_g), w_in, conv_w, sgu_ln_g, sgu_ln_b,
                             sgu_w, sgu_b, gn_conv, gn_sgu, w_out)
        h = h + moe_ffn(rms_norm(h, ln_ffn_g), w_router, b_router, w_gate_up, b_gate_up,
                        w_down, b_down)
    return rms_norm(h, ln_final_g)
```

```python
import functools

import jax
import jax.numpy as jnp
from jax import lax
from jax.experimental import pallas as pl
from jax.experimental.pallas import tpu as pltpu

D_MODEL = 2048
BATCH = 2
SEQ = 4096
N_TOK = BATCH * SEQ

D_CONV = 1024
N_CONV_GROUPS = 8
CONV_WIDTH = 3
D_SGU = 1024
N_SGU_HEADS = 8
SGU_HEAD_DIM = 128
CHUNK = 128
D_IN_PROJ = 3 * D_CONV + 2 * D_SGU

N_EXPERTS = 32
TOP_K = 4
D_FF = 2048
SWIGLU_LIMIT = 7.0
SWIGLU_ALPHA = 1.702
RMS_EPS = 1e-5
LN_EPS = 1e-5

LANES = 128
SUBLANES = 8
VMEM_LIMIT = 56 * 1024 * 1024

TM = 256
SUPER_ROWS = 1024
ROW_CHUNK = 256
FF_TILE = 256
N_FF_TILES = D_FF // FF_TILE
MAX_SUPER = (N_TOK * TOP_K + N_EXPERTS * (SUPER_ROWS - 1)) // SUPER_ROWS
TD = 256
TC = 256
DISPATCH_UNROLL = 8

_F32 = jnp.float32
_BF16 = jnp.bfloat16


def _dot(a, b):
    return jnp.dot(a, b, preferred_element_type=_F32)


def _gelu_exact(x):
    return 0.5 * x * (1.0 + lax.erf(x * (2.0 ** -0.5)))


def _rms_rows(x, gain):
    return x * lax.rsqrt(jnp.mean(x * x, axis=-1, keepdims=True) + RMS_EPS) * gain


def _mixer_router_kernel(x_ref, lng_ref, win_ref, convw_ref, slg_ref, slb_ref, sw_ref, sb_ref,
                         gnc_ref, gns_ref, wout_ref, lnf_ref, wr_ref, br_ref,
                         h1_ref, xn2_ref, route_ref, gates_ref, counts_ref,
                         cbuf, carry, ybuf):
    i = pl.program_id(0)

    @pl.when(i % (SEQ // TM) == 0)
    def _():
        cbuf[0:SUBLANES, :] = jnp.zeros((SUBLANES, D_CONV), _F32)

    @pl.when(i == 0)
    def _():
        carry[...] = jnp.zeros_like(carry)

    x = x_ref[...]
    xn = _rms_rows(x, lng_ref[...]).astype(_BF16)

    b_gate = _dot(xn, win_ref[:, 0:D_CONV])
    c_gate = _dot(xn, win_ref[:, D_CONV:2 * D_CONV])
    hh = _dot(xn, win_ref[:, 2 * D_CONV:3 * D_CONV])
    ch = c_gate * hh
    cbuf[SUBLANES:SUBLANES + TM, :] = ch
    ch1 = cbuf[SUBLANES - 1:SUBLANES - 1 + TM, :]
    ch2 = cbuf[SUBLANES - 2:SUBLANES - 2 + TM, :]
    conv = convw_ref[0:1, :] * ch2 + convw_ref[1:2, :] * ch1 + convw_ref[2:3, :] * ch
    cbuf[0:SUBLANES, :] = cbuf[TM:TM + SUBLANES, :]
    y_conv = b_gate * conv
    for g in range(N_CONV_GROUPS):
        sl = slice(g * LANES, (g + 1) * LANES)
        blk = y_conv[:, sl]
        ms = jnp.mean(blk * blk, axis=-1, keepdims=True)
        ybuf[:, sl] = (blk * lax.rsqrt(ms + RMS_EPS) * gnc_ref[:, sl]).astype(_BF16)

    gu = _gelu_exact(_dot(xn, win_ref[:, 3 * D_CONV:3 * D_CONV + D_SGU]))
    gv = _gelu_exact(_dot(xn, win_ref[:, 3 * D_CONV + D_SGU:D_IN_PROJ]))
    row_c = lax.broadcasted_iota(jnp.int32, (CHUNK, CHUNK), 0)
    col_c = lax.broadcasted_iota(jnp.int32, (CHUNK, CHUNK), 1)
    causal = row_c >= col_c
    for h in range(N_SGU_HEADS):
        sl = slice(h * SGU_HEAD_DIM, (h + 1) * SGU_HEAD_DIM)
        vh = gv[:, sl]
        mu = jnp.mean(vh, axis=-1, keepdims=True)
        xc = vh - mu
        var = jnp.mean(xc * xc, axis=-1, keepdims=True)
        vn = (xc * lax.rsqrt(var + LN_EPS) * slg_ref[:, sl] + slb_ref[:, sl]).astype(_BF16)
        wm = jnp.where(causal, sw_ref[h], 0.0).astype(_BF16)
        for c in range(TM // CHUNK):
            rows = slice(c * CHUNK, (c + 1) * CHUNK)
            mixed = _dot(wm, vn[rows, :]) + sb_ref[:, sl]
            ys = gu[rows, sl] * mixed
            ms = jnp.mean(ys * ys, axis=-1, keepdims=True)
            ybuf[rows, D_CONV + h * SGU_HEAD_DIM:D_CONV + (h + 1) * SGU_HEAD_DIM] = (
                ys * lax.rsqrt(ms + RMS_EPS) * gns_ref[:, sl]).astype(_BF16)

    h1 = x + _dot(ybuf[...], wout_ref[...])
    h1_ref[...] = h1
    xn2 = _rms_rows(h1, lnf_ref[...])
    xn2_ref[...] = xn2

    x_hi = xn2.astype(_BF16)
    x_lo = (xn2 - x_hi.astype(_F32)).astype(_BF16)
    p = _dot(x_hi, wr_ref[...]) + _dot(x_lo, wr_ref[...])
    logits = p[:, :LANES] + p[:, LANES:] + br_ref[...]
    lane = lax.broadcasted_iota(jnp.int32, (TM, LANES), 1)
    lane_f = lane.astype(_F32)
    neg_inf = jnp.float32(-jnp.inf)
    logits = jnp.where(lane < N_EXPERTS, logits, neg_inf)

    vals, ids, onehots = [], [], []
    cur = logits
    for _ in range(TOP_K):
        m = jnp.max(cur, axis=-1, keepdims=True)
        idx = jnp.min(jnp.where(cur == m, lane_f, float(LANES)), axis=-1, keepdims=True)
        oh = lane_f == idx
        vals.append(m)
        ids.append(idx)
        onehots.append(oh)
        cur = jnp.where(oh, neg_inf, cur)
    exps = [jnp.exp(v - vals[0]) for v in vals]
    denom = exps[0] + exps[1] + exps[2] + exps[3]
    gates = [e / denom for e in exps]

    mask = (onehots[0] | onehots[1] | onehots[2] | onehots[3]).astype(_F32)
    row_t = lax.broadcasted_iota(jnp.int32, (TM, TM), 0)
    col_t = lax.broadcasted_iota(jnp.int32, (TM, TM), 1)
    strict_lower = (row_t > col_t).astype(_BF16)
    before = _dot(strict_lower, mask.astype(_BF16)) + carry[...]
    ranks = [jnp.sum(jnp.where(oh, before, 0.0), axis=-1, keepdims=True) for oh in onehots]
    carry[...] = carry[...] + jnp.sum(mask, axis=0, keepdims=True)
    counts_ref[...] = jnp.broadcast_to(carry[...], counts_ref.shape)

    route = jnp.zeros((TM, LANES), _F32)
    gate_out = jnp.zeros((TM, LANES), _F32)
    for k in range(TOP_K):
        route = jnp.where(lane == k, ids[k], route)
        route = jnp.where(lane == TOP_K + k, ranks[k], route)
        gate_out = jnp.where(lane == k, gates[k], gate_out)
    route_ref[...] = route.astype(jnp.int32)
    gates_ref[...] = gate_out


def _mixer_router(x2d, ln_mix_g, w_in_bf, conv_w, sgu_ln_g, sgu_ln_b, sgu_w, sgu_b_full,
                  gn_conv, gn_sgu, w_out_bf, ln_ffn_g, wr_split, br_pad):
    def full(a):
        return pl.BlockSpec(a.shape, lambda i: (0,) * a.ndim)

    row_blk = lambda w: pl.BlockSpec((TM, w), lambda i: (i, 0))
    ins = [x2d, ln_mix_g, w_in_bf, conv_w, sgu_ln_g, sgu_ln_b, sgu_w, sgu_b_full,
           gn_conv, gn_sgu, w_out_bf, ln_ffn_g, wr_split, br_pad]
    in_specs = [row_blk(D_MODEL)] + [full(a) for a in ins[1:]]
    return pl.pallas_call(
        _mixer_router_kernel,
        grid=(N_TOK // TM,),
        in_specs=in_specs,
        out_specs=[row_blk(D_MODEL), row_blk(D_MODEL), row_blk(LANES), row_blk(LANES),
                   pl.BlockSpec((SUBLANES, LANES), lambda i: (0, 0))],
        out_shape=[jax.ShapeDtypeStruct((N_TOK, D_MODEL), _F32),
                   jax.ShapeDtypeStruct((N_TOK, D_MODEL), _F32),
                   jax.ShapeDtypeStruct((N_TOK, LANES), jnp.int32),
                   jax.ShapeDtypeStruct((N_TOK, LANES), _F32),
                   jax.ShapeDtypeStruct((SUBLANES, LANES), _F32)],
        scratch_shapes=[pltpu.VMEM((TM + SUBLANES, D_CONV), _F32),
                        pltpu.VMEM((1, LANES), _F32),
                        pltpu.VMEM((TM, D_MODEL), _BF16)],
        compiler_params=pltpu.CompilerParams(dimension_semantics=("arbitrary",),
                                             vmem_limit_bytes=VMEM_LIMIT),
        name="mixer_router",
    )(*ins)


def _dispatch_kernel(eid_ref, rank_ref, rstart_ref, zrow_ref, zflag_ref,
                     xn2_hbm, bufx_hbm, zbuf, zsem, sem):
    i = pl.program_id(0)
    n_steps = pl.num_programs(0)

    @pl.when(i == 0)
    def _():
        zbuf[...] = jnp.zeros_like(zbuf)

        def zero_copy(e):
            z0 = pl.multiple_of(zrow_ref[e], ROW_CHUNK)
            return pltpu.make_async_copy(zbuf, bufx_hbm.at[pl.ds(z0, ROW_CHUNK)], zsem)

        for e in range(N_EXPERTS):
            @pl.when(zflag_ref[e] > 0)
            def _(e=e):
                zero_copy(e).start()
        for e in range(N_EXPERTS):
            @pl.when(zflag_ref[e] > 0)
            def _(e=e):
                zero_copy(e).wait()

    def row_copy(t, k):
        a = t * TOP_K + k
        dst = rstart_ref[eid_ref[a]] + rank_ref[a]
        return pltpu.make_async_copy(xn2_hbm.at[pl.ds(t, 1)], bufx_hbm.at[pl.ds(dst, 1)], sem)

    def issue_block(blk):
        def body(t, carry):
            for k in range(TOP_K):
                row_copy(blk * TD + t, k).start()
            return carry
        lax.fori_loop(0, TD, body, 0, unroll=DISPATCH_UNROLL)

    def wait_block(blk):
        def body(t, carry):
            for k in range(TOP_K):
                row_copy(blk * TD + t, k).wait()
            return carry
        lax.fori_loop(0, TD, body, 0, unroll=DISPATCH_UNROLL)

    issue_block(i)

    @pl.when(i > 0)
    def _():
        wait_block(i - 1)

    @pl.when(i == n_steps - 1)
    def _():
        wait_block(i)


def _dispatch(eid, rank, row_start, zrow, zflag, xn2):
    return pl.pallas_call(
        _dispatch_kernel,
        grid_spec=pltpu.PrefetchScalarGridSpec(
            num_scalar_prefetch=5,
            grid=(N_TOK // TD,),
            in_specs=[pl.BlockSpec(memory_space=pl.ANY)],
            out_specs=pl.BlockSpec(memory_space=pl.ANY),
            scratch_shapes=[pltpu.VMEM((ROW_CHUNK, D_MODEL), _F32),
                            pltpu.SemaphoreType.DMA(()),
                            pltpu.SemaphoreType.DMA(())],
        ),
        out_shape=jax.ShapeDtypeStruct((MAX_SUPER * SUPER_ROWS, D_MODEL), _F32),
        compiler_params=pltpu.CompilerParams(dimension_semantics=("arbitrary",),
                                             vmem_limit_bytes=VMEM_LIMIT),
        name="dispatch",
    )(eid, rank, row_start, zrow, zflag, xn2)


def _expert_kernel(sup_e_ref, sup_blk_ref, nchunk_ref,
                   x_ref, wg_ref, wu_ref, bg_ref, bu_ref, wd_ref, bd_ref, out_ref,
                   wg_s, wu_s, wd_s):
    s = pl.program_id(0)
    j = pl.program_id(1)
    n_chunks = nchunk_ref[s]

    @pl.when(n_chunks > 0)
    def _():
        wg_s[...] = wg_ref[0].astype(_BF16)
        wu_s[...] = wu_ref[0].astype(_BF16)
        wd_s[...] = wd_ref[0].astype(_BF16)

        def partial_out(c):
            r0 = pl.multiple_of(c * ROW_CHUNK, ROW_CHUNK)
            xb = x_ref[pl.ds(r0, ROW_CHUNK), :].astype(_BF16)
            gate = _dot(xb, wg_s[...]) + bg_ref[0]
            up = _dot(xb, wu_s[...]) + bu_ref[0]
            gate = jnp.minimum(gate, SWIGLU_LIMIT)
            up = jnp.clip(up, -SWIGLU_LIMIT, SWIGLU_LIMIT)
            glu = gate * jax.nn.sigmoid(gate * SWIGLU_ALPHA)
            act = ((up + 1.0) * glu).astype(_BF16)
            return r0, _dot(act, wd_s[...])

        @pl.when(j == 0)
        def _():
            def body(c, carry):
                r0, o = partial_out(c)
                out_ref[pl.ds(r0, ROW_CHUNK), :] = o + bd_ref[0]
                return carry
            lax.fori_loop(0, n_chunks, body, 0)

        @pl.when(j > 0)
        def _():
            def body(c, carry):
                r0, o = partial_out(c)
                out_ref[pl.ds(r0, ROW_CHUNK), :] += o
                return carry
            lax.fori_loop(0, n_chunks, body, 0)


def _experts(sup_e, sup_blk, nchunk, buf_x, w_gate_up, b_gate_up3, w_down, b_down3):
    def ff_tile(j, nchunk, s):
        return jnp.where(nchunk[s] > 0, j, N_FF_TILES - 1)

    x_spec = pl.BlockSpec((SUPER_ROWS, D_MODEL), lambda s, j, se, sb, nc: (sb[s], 0))
    wg_spec = pl.BlockSpec((1, D_MODEL, FF_TILE),
                           lambda s, j, se, sb, nc: (se[s], 0, ff_tile(j, nc, s)))
    wu_spec = pl.BlockSpec((1, D_MODEL, FF_TILE),
                           lambda s, j, se, sb, nc: (se[s], 0, N_FF_TILES + ff_tile(j, nc, s)))
    bg_spec = pl.BlockSpec((1, 1, FF_TILE),
                           lambda s, j, se, sb, nc: (se[s], 0, ff_tile(j, nc, s)))
    bu_spec = pl.BlockSpec((1, 1, FF_TILE),
                           lambda s, j, se, sb, nc: (se[s], 0, N_FF_TILES + ff_tile(j, nc, s)))
    wd_spec = pl.BlockSpec((1, FF_TILE, D_MODEL),
                           lambda s, j, se, sb, nc: (se[s], ff_tile(j, nc, s), 0))
    bd_spec = pl.BlockSpec((1, 1, D_MODEL), lambda s, j, se, sb, nc: (se[s], 0, 0))
    out_spec = pl.BlockSpec((SUPER_ROWS, D_MODEL), lambda s, j, se, sb, nc: (sb[s], 0))
    return pl.pallas_call(
        _expert_kernel,
        grid_spec=pltpu.PrefetchScalarGridSpec(
            num_scalar_prefetch=3,
            grid=(MAX_SUPER, N_FF_TILES),
            in_specs=[x_spec, wg_spec, wu_spec, bg_spec, bu_spec, wd_spec, bd_spec],
            out_specs=out_spec,
            scratch_shapes=[pltpu.VMEM((D_MODEL, FF_TILE), _BF16),
                            pltpu.VMEM((D_MODEL, FF_TILE), _BF16),
                            pltpu.VMEM((FF_TILE, D_MODEL), _BF16)],
        ),
        out_shape=jax.ShapeDtypeStruct((MAX_SUPER * SUPER_ROWS, D_MODEL), _F32),
        compiler_params=pltpu.CompilerParams(dimension_semantics=("arbitrary", "arbitrary"),
                                             vmem_limit_bytes=VMEM_LIMIT),
        name="experts",
    )(sup_e, sup_blk, nchunk, buf_x, w_gate_up, w_gate_up, b_gate_up3, b_gate_up3, w_down, b_down3)


def _combine_kernel(eid_ref, rank_ref, rstart_ref,
                    h1_ref, gates_ref, lng_ref, y_hbm, out_ref, rows, sem):
    i = pl.program_id(0)
    n_steps = pl.num_programs(0)

    def row_copy(tile, slot, t, k):
        a = (tile * TC + t) * TOP_K + k
        src = rstart_ref[eid_ref[a]] + rank_ref[a]
        return pltpu.make_async_copy(y_hbm.at[pl.ds(src, 1)], rows.at[slot, k, pl.ds(t, 1)],
                                     sem.at[slot])

    def issue_tile(tile, slot):
        def body(t, carry):
            for k in range(TOP_K):
                row_copy(tile, slot, t, k).start()
            return carry
        lax.fori_loop(0, TC, body, 0, unroll=DISPATCH_UNROLL)

    def wait_tile(tile, slot):
        def body(t, carry):
            for k in range(TOP_K):
                row_copy(tile, slot, t, k).wait()
            return carry
        lax.fori_loop(0, TC, body, 0, unroll=DISPATCH_UNROLL)

    slot = i % 2

    @pl.when(i == 0)
    def _():
        issue_tile(0, 0)

    @pl.when(i + 1 < n_steps)
    def _():
        issue_tile(i + 1, 1 - slot)

    wait_tile(i, slot)

    g = gates_ref[...]
    acc = h1_ref[...]
    for k in range(TOP_K):
        acc = acc + rows[slot, k] * g[:, k:k + 1]
    out_ref[...] = _rms_rows(acc, lng_ref[...])


def _combine(eid, rank, row_start, h1, gates, ln_final_g, y_sorted):
    return pl.pallas_call(
        _combine_kernel,
        grid_spec=pltpu.PrefetchScalarGridSpec(
            num_scalar_prefetch=3,
            grid=(N_TOK // TC,),
            in_specs=[pl.BlockSpec((TC, D_MODEL), lambda i, *_: (i, 0)),
                      pl.BlockSpec((TC, LANES), lambda i, *_: (i, 0)),
                      pl.BlockSpec((1, D_MODEL), lambda i, *_: (0, 0)),
                      pl.BlockSpec(memory_space=pl.ANY)],
            out_specs=pl.BlockSpec((TC, D_MODEL), lambda i, *_: (i, 0)),
            scratch_shapes=[pltpu.VMEM((2, TOP_K, TC, D_MODEL), _F32),
                            pltpu.SemaphoreType.DMA((2,))],
        ),
        out_shape=jax.ShapeDtypeStruct((N_TOK, D_MODEL), _F32),
        compiler_params=pltpu.CompilerParams(dimension_semantics=("arbitrary",),
                                             vmem_limit_bytes=VMEM_LIMIT),
        name="combine",
    )(eid, rank, row_start, h1, gates, ln_final_g, y_sorted)


def kernel(x, ln_mix_g, w_in, conv_w, sgu_ln_g, sgu_ln_b, sgu_w, sgu_b, gn_conv, gn_sgu, w_out,
           ln_ffn_g, w_router, b_router, w_gate_up, b_gate_up, w_down, b_down, ln_final_g):
    row = lambda v: v.reshape(1, -1)
    x2d = x.reshape(N_TOK, D_MODEL)

    sgu_b_full = jnp.repeat(jnp.transpose(sgu_b), SGU_HEAD_DIM, axis=1)
    wr_hi = w_router.astype(_BF16)
    wr_lo = (w_router - wr_hi.astype(_F32)).astype(_BF16)
    pad = ((0, 0), (0, LANES - N_EXPERTS))
    wr_split = jnp.concatenate([jnp.pad(wr_hi, pad), jnp.pad(wr_lo, pad)], axis=1)
    br_pad = jnp.pad(b_router, (0, LANES - N_EXPERTS)).reshape(1, LANES)

    h1, xn2, route, gates, counts = _mixer_router(
        x2d, row(ln_mix_g), w_in.astype(_BF16), conv_w, row(sgu_ln_g), row(sgu_ln_b), sgu_w,
        sgu_b_full, row(gn_conv), row(gn_sgu), w_out.astype(_BF16), row(ln_ffn_g), wr_split, br_pad)

    sizes = counts[0, :N_EXPERTS].astype(jnp.int32)
    n_sup = (sizes + SUPER_ROWS - 1) // SUPER_ROWS
    sup_end = jnp.cumsum(n_sup)
    sup_start = sup_end - n_sup
    total_sup = sup_end[-1]
    row_start = (sup_start * SUPER_ROWS).astype(jnp.int32)
    sid = jnp.arange(MAX_SUPER, dtype=jnp.int32)
    sid_eff = jnp.minimum(sid, total_sup - 1)
    sup_e = jnp.minimum(jnp.searchsorted(sup_end, sid_eff, side="right"),
                        N_EXPERTS - 1).astype(jnp.int32)
    rows_left = sizes[sup_e] - (sid_eff - sup_start[sup_e]) * SUPER_ROWS
    valid = jnp.clip(rows_left, 0, SUPER_ROWS)
    nchunk = jnp.where(sid < total_sup, (valid + ROW_CHUNK - 1) // ROW_CHUNK, 0).astype(jnp.int32)
    zflag = (sizes > 0).astype(jnp.int32)
    zrow = (row_start + ((jnp.maximum(sizes, 1) - 1) // ROW_CHUNK) * ROW_CHUNK).astype(jnp.int32)

    eid = route[:, 0:TOP_K].reshape(-1)
    rank = route[:, TOP_K:2 * TOP_K].reshape(-1)

    buf_x = _dispatch(eid, rank, row_start, zrow, zflag, xn2)
    y_sorted = _experts(sup_e, sid_eff.astype(jnp.int32), nchunk, buf_x, w_gate_up,
                        b_gate_up.reshape(N_EXPERTS, 1, 2 * D_FF), w_down,
                        b_down.reshape(N_EXPERTS, 1, D_MODEL))
    out = _combine(eid, rank, row_start, h1, gates, row(ln_final_g), y_sorted)
    return out.reshape(BATCH, SEQ, D_MODEL)
```

```python
import functools

import jax
import jax.numpy as jnp
from jax import lax
from jax.experimental import pallas as pl
from jax.experimental.pallas import tpu as pltpu

D_MODEL = 2048
BATCH = 2
SEQ = 4096
N_TOK = BATCH * SEQ

D_CONV = 1024
N_CONV_GROUPS = 8
CONV_WIDTH = 3
D_SGU = 1024
N_SGU_HEADS = 8
SGU_HEAD_DIM = 128
CHUNK = 128
D_IN_PROJ = 3 * D_CONV + 2 * D_SGU

N_EXPERTS = 32
TOP_K = 4
D_FF = 2048
SWIGLU_LIMIT = 7.0
SWIGLU_ALPHA = 1.702
RMS_EPS = 1e-5
LN_EPS = 1e-5

LANES = 128
SUBLANES = 8
VMEM_LIMIT = 56 * 1024 * 1024

TM = 256
SUPER_ROWS = 1024
ROW_CHUNK = 256
FF_TILE = 256
N_FF_TILES = D_FF // FF_TILE
MAX_SUPER = (N_TOK * TOP_K + N_EXPERTS * (SUPER_ROWS - 1)) // SUPER_ROWS
TD = 256
TC = 256
DISPATCH_UNROLL = 8

_F32 = jnp.float32
_BF16 = jnp.bfloat16


def _dot(a, b):
    return jnp.dot(a, b, preferred_element_type=_F32)


def _gelu_exact(x):
    return 0.5 * x * (1.0 + lax.erf(x * (2.0 ** -0.5)))


def _rms_rows(x, gain):
    return x * lax.rsqrt(jnp.mean(x * x, axis=-1, keepdims=True) + RMS_EPS) * gain


def _mixer_router_kernel(x_ref, lng_ref, win_ref, convw_ref, slg_ref, slb_ref, sw_ref, sb_ref,
                         gnc_ref, gns_ref, wout_ref, lnf_ref, wr_ref, br_ref,
                         h1_ref, xn2_ref, route_ref, gates_ref, counts_ref,
                         cbuf, carry, ybuf):
    i = pl.program_id(0)

    @pl.when(i % (SEQ // TM) == 0)
    def _():
        cbuf[0:SUBLANES, :] = jnp.zeros((SUBLANES, D_CONV), _F32)

    @pl.when(i == 0)
    def _():
        carry[...] = jnp.zeros_like(carry)

    x = x_ref[...]
    xn = _rms_rows(x, lng_ref[...]).astype(_BF16)

    b_gate = _dot(xn, win_ref[:, 0:D_CONV])
    c_gate = _dot(xn, win_ref[:, D_CONV:2 * D_CONV])
    hh = _dot(xn, win_ref[:, 2 * D_CONV:3 * D_CONV])
    ch = c_gate * hh
    cbuf[SUBLANES:SUBLANES + TM, :] = ch
    ch1 = cbuf[SUBLANES - 1:SUBLANES - 1 + TM, :]
    ch2 = cbuf[SUBLANES - 2:SUBLANES - 2 + TM, :]
    conv = convw_ref[0:1, :] * ch2 + convw_ref[1:2, :] * ch1 + convw_ref[2:3, :] * ch
    cbuf[0:SUBLANES, :] = cbuf[TM:TM + SUBLANES, :]
    y_conv = b_gate * conv
    for g in range(N_CONV_GROUPS):
        sl = slice(g * LANES, (g + 1) * LANES)
        blk = y_conv[:, sl]
        ms = jnp.mean(blk * blk, axis=-1, keepdims=True)
        ybuf[:, sl] = (blk * lax.rsqrt(ms + RMS_EPS) * gnc_ref[:, sl]).astype(_BF16)

    gu = _gelu_exact(_dot(xn, win_ref[:, 3 * D_CONV:3 * D_CONV + D_SGU]))
    gv = _gelu_exact(_dot(xn, win_ref[:, 3 * D_CONV + D_SGU:D_IN_PROJ]))
    row_c = lax.broadcasted_iota(jnp.int32, (CHUNK, CHUNK), 0)
    col_c = lax.broadcasted_iota(jnp.int32, (CHUNK, CHUNK), 1)
    causal = row_c >= col_c
    for h in range(N_SGU_HEADS):
        sl = slice(h * SGU_HEAD_DIM, (h + 1) * SGU_HEAD_DIM)
        vh = gv[:, sl]
        mu = jnp.mean(vh, axis=-1, keepdims=True)
        xc = vh - mu
        var = jnp.mean(xc * xc, axis=-1, keepdims=True)
        vn = (xc * lax.rsqrt(var + LN_EPS) * slg_ref[:, sl] + slb_ref[:, sl]).astype(_BF16)
        wm = jnp.where(causal, sw_ref[h], 0.0).astype(_BF16)
        for c in range(TM // CHUNK):
            rows = slice(c * CHUNK, (c + 1) * CHUNK)
            mixed = _dot(wm, vn[rows, :]) + sb_ref[:, sl]
            ys = gu[rows, sl] * mixed
            ms = jnp.mean(ys * ys, axis=-1, keepdims=True)
            ybuf[rows, D_CONV + h * SGU_HEAD_DIM:D_CONV + (h + 1) * SGU_HEAD_DIM] = (
                ys * lax.rsqrt(ms + RMS_EPS) * gns_ref[:, sl]).astype(_BF16)

    h1 = x + _dot(ybuf[...], wout_ref[...])
    h1_ref[...] = h1
    xn2 = _rms_rows(h1, lnf_ref[...])
    xn2_ref[...] = xn2

    x_hi = xn2.astype(_BF16)
    x_lo = (xn2 - x_hi.astype(_F32)).astype(_BF16)
    p = _dot(x_hi, wr_ref[...]) + _dot(x_lo, wr_ref[...])
    logits = p[:, :LANES] + p[:, LANES:] + br_ref[...]
    lane = lax.broadcasted_iota(jnp.int32, (TM, LANES), 1)
    lane_f = lane.astype(_F32)
    neg_inf = jnp.float32(-jnp.inf)
    logits = jnp.where(lane < N_EXPERTS, logits, neg_inf)

    vals, ids, onehots = [], [], []
    cur = logits
    for _ in range(TOP_K):
        m = jnp.max(cur, axis=-1, keepdims=True)
        idx = jnp.min(jnp.where(cur == m, lane_f, float(LANES)), axis=-1, keepdims=True)
        oh = lane_f == idx
        vals.append(m)
        ids.append(idx)
        onehots.append(oh)
        cur = jnp.where(oh, neg_inf, cur)
    exps = [jnp.exp(v - vals[0]) for v in vals]
    denom = exps[0] + exps[1] + exps[2] + exps[3]
    gates = [e / denom for e in exps]

    mask = (onehots[0] | onehots[1] | onehots[2] | onehots[3]).astype(_F32)
    row_t = lax.broadcasted_iota(jnp.int32, (TM, TM), 0)
    col_t = lax.broadcasted_iota(jnp.int32, (TM, TM), 1)
    strict_lower = (row_t > col_t).astype(_BF16)
    before = _dot(strict_lower, mask.astype(_BF16)) + carry[...]
    ranks = [jnp.sum(jnp.where(oh, before, 0.0), axis=-1, keepdims=True) for oh in onehots]
    carry[...] = carry[...] + jnp.sum(mask, axis=0, keepdims=True)
    counts_ref[...] = jnp.broadcast_to(carry[...], counts_ref.shape)

    route = jnp.zeros((TM, LANES), _F32)
    gate_out = jnp.zeros((TM, LANES), _F32)
    for k in range(TOP_K):
        route = jnp.where(lane == k, ids[k], route)
        route = jnp.where(lane == TOP_K + k, ranks[k], route)
        gate_out = jnp.where(lane == k, gates[k], gate_out)
    route_ref[...] = route.astype(jnp.int32)
    gates_ref[...] = gate_out


def _mixer_router(x2d, ln_mix_g, w_in_bf, conv_w, sgu_ln_g, sgu_ln_b, sgu_w, sgu_b_full,
                  gn_conv, gn_sgu, w_out_bf, ln_ffn_g, wr_split, br_pad):
    def full(a):
        return pl.BlockSpec(a.shape, lambda i: (0,) * a.ndim)

    row_blk = lambda w: pl.BlockSpec((TM, w), lambda i: (i, 0))
    ins = [x2d, ln_mix_g, w_in_bf, conv_w, sgu_ln_g, sgu_ln_b, sgu_w, sgu_b_full,
           gn_conv, gn_sgu, w_out_bf, ln_ffn_g, wr_split, br_pad]
    in_specs = [row_blk(D_MODEL)] + [full(a) for a in ins[1:]]
    return pl.pallas_call(
        _mixer_router_kernel,
        grid=(N_TOK // TM,),
        in_specs=in_specs,
        out_specs=[row_blk(D_MODEL), row_blk(D_MODEL), row_blk(LANES), row_blk(LANES),
                   pl.BlockSpec((SUBLANES, LANES), lambda i: (0, 0))],
        out_shape=[jax.ShapeDtypeStruct((N_TOK, D_MODEL), _F32),
                   jax.ShapeDtypeStruct((N_TOK, D_MODEL), _F32),
                   jax.ShapeDtypeStruct((N_TOK, LANES), jnp.int32),
                   jax.ShapeDtypeStruct((N_TOK, LANES), _F32),
                   jax.ShapeDtypeStruct((SUBLANES, LANES), _F32)],
        scratch_shapes=[pltpu.VMEM((TM + SUBLANES, D_CONV), _F32),
                        pltpu.VMEM((1, LANES), _F32),
                        pltpu.VMEM((TM, D_MODEL), _BF16)],
        compiler_params=pltpu.CompilerParams(dimension_semantics=("arbitrary",),
                                             vmem_limit_bytes=VMEM_LIMIT),
        name="mixer_router",
    )(*ins)


def _dispatch_kernel(eid_ref, rank_ref, rstart_ref, zrow_ref, zflag_ref,
                     x_ref, bufx_hbm, zbuf, zsem, sem):
    i = pl.program_id(0)

    @pl.when(i == 0)
    def _():
        zbuf[...] = jnp.zeros_like(zbuf)

        def zero_copy(e):
            z0 = pl.multiple_of(zrow_ref[e], ROW_CHUNK)
            return pltpu.make_async_copy(zbuf, bufx_hbm.at[pl.ds(z0, ROW_CHUNK)], zsem)

        for e in range(N_EXPERTS):
            @pl.when(zflag_ref[e] > 0)
            def _(e=e):
                zero_copy(e).start()
        for e in range(N_EXPERTS):
            @pl.when(zflag_ref[e] > 0)
            def _(e=e):
                zero_copy(e).wait()

    def row_copy(t8, u, k):
        t = pl.multiple_of(t8 * SUBLANES, SUBLANES) + u
        a = (i * TD + t) * TOP_K + k
        dst = rstart_ref[eid_ref[a]] + rank_ref[a]
        return pltpu.make_async_copy(x_ref.at[pl.ds(t, 1)], bufx_hbm.at[pl.ds(dst, 1)], sem)

    def issue(t8, carry):
        for u in range(SUBLANES):
            for k in range(TOP_K):
                row_copy(t8, u, k).start()
        return carry

    def drain(t8, carry):
        for u in range(SUBLANES):
            for k in range(TOP_K):
                row_copy(t8, u, k).wait()
        return carry

    lax.fori_loop(0, TD // SUBLANES, issue, 0)
    lax.fori_loop(0, TD // SUBLANES, drain, 0)


def _dispatch(eid, rank, row_start, zrow, zflag, xn2):
    return pl.pallas_call(
        _dispatch_kernel,
        grid_spec=pltpu.PrefetchScalarGridSpec(
            num_scalar_prefetch=5,
            grid=(N_TOK // TD,),
            in_specs=[pl.BlockSpec((TD, D_MODEL), lambda i, *_: (i, 0))],
            out_specs=pl.BlockSpec(memory_space=pl.ANY),
            scratch_shapes=[pltpu.VMEM((ROW_CHUNK, D_MODEL), _F32),
                            pltpu.SemaphoreType.DMA(()),
                            pltpu.SemaphoreType.DMA(())],
        ),
        out_shape=jax.ShapeDtypeStruct((MAX_SUPER * SUPER_ROWS, D_MODEL), _F32),
        compiler_params=pltpu.CompilerParams(dimension_semantics=("arbitrary",),
                                             vmem_limit_bytes=VMEM_LIMIT),
        name="dispatch",
    )(eid, rank, row_start, zrow, zflag, xn2)


def _expert_kernel(sup_e_ref, sup_blk_ref, nchunk_ref,
                   x_ref, wg_ref, wu_ref, bg_ref, bu_ref, wd_ref, bd_ref, out_ref,
                   wg_s, wu_s, wd_s):
    s = pl.program_id(0)
    j = pl.program_id(1)
    n_chunks = nchunk_ref[s]

    @pl.when(n_chunks > 0)
    def _():
        wg_s[...] = wg_ref[0].astype(_BF16)
        wu_s[...] = wu_ref[0].astype(_BF16)
        wd_s[...] = wd_ref[0].astype(_BF16)

        def partial_out(c):
            r0 = pl.multiple_of(c * ROW_CHUNK, ROW_CHUNK)
            xb = x_ref[pl.ds(r0, ROW_CHUNK), :].astype(_BF16)
            gate = _dot(xb, wg_s[...]) + bg_ref[0]
            up = _dot(xb, wu_s[...]) + bu_ref[0]
            gate = jnp.minimum(gate, SWIGLU_LIMIT)
            up = jnp.clip(up, -SWIGLU_LIMIT, SWIGLU_LIMIT)
            glu = gate * jax.nn.sigmoid(gate * SWIGLU_ALPHA)
            act = ((up + 1.0) * glu).astype(_BF16)
            return r0, _dot(act, wd_s[...])

        @pl.when(j == 0)
        def _():
            def body(c, carry):
                r0, o = partial_out(c)
                out_ref[pl.ds(r0, ROW_CHUNK), :] = o + bd_ref[0]
                return carry
            lax.fori_loop(0, n_chunks, body, 0)

        @pl.when(j > 0)
        def _():
            def body(c, carry):
                r0, o = partial_out(c)
                out_ref[pl.ds(r0, ROW_CHUNK), :] += o
                return carry
            lax.fori_loop(0, n_chunks, body, 0)


def _experts(sup_e, sup_blk, nchunk, buf_x, w_gate_up, b_gate_up3, w_down, b_down3):
    def ff_tile(j, nchunk, s):
        return jnp.where(nchunk[s] > 0, j, N_FF_TILES - 1)

    x_spec = pl.BlockSpec((SUPER_ROWS, D_MODEL), lambda s, j, se, sb, nc: (sb[s], 0))
    wg_spec = pl.BlockSpec((1, D_MODEL, FF_TILE),
                           lambda s, j, se, sb, nc: (se[s], 0, ff_tile(j, nc, s)))
    wu_spec = pl.BlockSpec((1, D_MODEL, FF_TILE),
                           lambda s, j, se, sb, nc: (se[s], 0, N_FF_TILES + ff_tile(j, nc, s)))
    bg_spec = pl.BlockSpec((1, 1, FF_TILE),
                           lambda s, j, se, sb, nc: (se[s], 0, ff_tile(j, nc, s)))
    bu_spec = pl.BlockSpec((1, 1, FF_TILE),
                           lambda s, j, se, sb, nc: (se[s], 0, N_FF_TILES + ff_tile(j, nc, s)))
    wd_spec = pl.BlockSpec((1, FF_TILE, D_MODEL),
                           lambda s, j, se, sb, nc: (se[s], ff_tile(j, nc, s), 0))
    bd_spec = pl.BlockSpec((1, 1, D_MODEL), lambda s, j, se, sb, nc: (se[s], 0, 0))
    out_spec = pl.BlockSpec((SUPER_ROWS, D_MODEL), lambda s, j, se, sb, nc: (sb[s], 0))
    return pl.pallas_call(
        _expert_kernel,
        grid_spec=pltpu.PrefetchScalarGridSpec(
            num_scalar_prefetch=3,
            grid=(MAX_SUPER, N_FF_TILES),
            in_specs=[x_spec, wg_spec, wu_spec, bg_spec, bu_spec, wd_spec, bd_spec],
            out_specs=out_spec,
            scratch_shapes=[pltpu.VMEM((D_MODEL, FF_TILE), _BF16),
                            pltpu.VMEM((D_MODEL, FF_TILE), _BF16),
                            pltpu.VMEM((FF_TILE, D_MODEL), _BF16)],
        ),
        out_shape=jax.ShapeDtypeStruct((MAX_SUPER * SUPER_ROWS, D_MODEL), _F32),
        compiler_params=pltpu.CompilerParams(dimension_semantics=("arbitrary", "arbitrary"),
                                             vmem_limit_bytes=VMEM_LIMIT),
        name="experts",
    )(sup_e, sup_blk, nchunk, buf_x, w_gate_up, w_gate_up, b_gate_up3, b_gate_up3, w_down, b_down3)


def _combine_kernel(eid_ref, rank_ref, rstart_ref,
                    h1_ref, gates_ref, lng_ref, y_hbm, out_ref, rows, sem):
    i = pl.program_id(0)
    n_steps = pl.num_programs(0)

    def row_copy(tile, slot, t8, u, k):
        t = pl.multiple_of(t8 * SUBLANES, SUBLANES) + u
        a = (tile * TC + t) * TOP_K + k
        src = rstart_ref[eid_ref[a]] + rank_ref[a]
        return pltpu.make_async_copy(y_hbm.at[pl.ds(src, 1)], rows.at[slot, k, pl.ds(t, 1)],
                                     sem.at[slot])

    def issue_tile(tile, slot):
        def body(t8, carry):
            for u in range(SUBLANES):
                for k in range(TOP_K):
                    row_copy(tile, slot, t8, u, k).start()
            return carry
        lax.fori_loop(0, TC // SUBLANES, body, 0)

    def wait_tile(tile, slot):
        def body(t8, carry):
            for u in range(SUBLANES):
                for k in range(TOP_K):
                    row_copy(tile, slot, t8, u, k).wait()
            return carry
        lax.fori_loop(0, TC // SUBLANES, body, 0)

    slot = i % 2

    @pl.when(i == 0)
    def _():
        issue_tile(0, 0)

    @pl.when(i + 1 < n_steps)
    def _():
        issue_tile(i + 1, 1 - slot)

    wait_tile(i, slot)

    g = gates_ref[...]
    acc = h1_ref[...]
    for k in range(TOP_K):
        acc = acc + rows[slot, k] * g[:, k:k + 1]
    out_ref[...] = _rms_rows(acc, lng_ref[...])


def _combine(eid, rank, row_start, h1, gates, ln_final_g, y_sorted):
    return pl.pallas_call(
        _combine_kernel,
        grid_spec=pltpu.PrefetchScalarGridSpec(
            num_scalar_prefetch=3,
            grid=(N_TOK // TC,),
            in_specs=[pl.BlockSpec((TC, D_MODEL), lambda i, *_: (i, 0)),
                      pl.BlockSpec((TC, LANES), lambda i, *_: (i, 0)),
                      pl.BlockSpec((1, D_MODEL), lambda i, *_: (0, 0)),
                      pl.BlockSpec(memory_space=pl.ANY)],
            out_specs=pl.BlockSpec((TC, D_MODEL), lambda i, *_: (i, 0)),
            scratch_shapes=[pltpu.VMEM((2, TOP_K, TC, D_MODEL), _F32),
                            pltpu.SemaphoreType.DMA((2,))],
        ),
        out_shape=jax.ShapeDtypeStruct((N_TOK, D_MODEL), _F32),
        compiler_params=pltpu.CompilerParams(dimension_semantics=("arbitrary",),
                                             vmem_limit_bytes=VMEM_LIMIT),
        name="combine",
    )(eid, rank, row_start, h1, gates, ln_final_g, y_sorted)


def kernel(x, ln_mix_g, w_in, conv_w, sgu_ln_g, sgu_ln_b, sgu_w, sgu_b, gn_conv, gn_sgu, w_out,
           ln_ffn_g, w_router, b_router, w_gate_up, b_gate_up, w_down, b_down, ln_final_g):
    row = lambda v: v.reshape(1, -1)
    x2d = x.reshape(N_TOK, D_MODEL)

    sgu_b_full = jnp.repeat(jnp.transpose(sgu_b), SGU_HEAD_DIM, axis=1)
    wr_hi = w_router.astype(_BF16)
    wr_lo = (w_router - wr_hi.astype(_F32)).astype(_BF16)
    pad = ((0, 0), (0, LANES - N_EXPERTS))
    wr_split = jnp.concatenate([jnp.pad(wr_hi, pad), jnp.pad(wr_lo, pad)], axis=1)
    br_pad = jnp.pad(b_router, (0, LANES - N_EXPERTS)).reshape(1, LANES)

    h1, xn2, route, gates, counts = _mixer_router(
        x2d, row(ln_mix_g), w_in.astype(_BF16), conv_w, row(sgu_ln_g), row(sgu_ln_b), sgu_w,
        sgu_b_full, row(gn_conv), row(gn_sgu), w_out.astype(_BF16), row(ln_ffn_g), wr_split, br_pad)

    sizes = counts[0, :N_EXPERTS].astype(jnp.int32)
    n_sup = (sizes + SUPER_ROWS - 1) // SUPER_ROWS
    sup_end = jnp.cumsum(n_sup)
    sup_start = sup_end - n_sup
    total_sup = sup_end[-1]
    row_start = (sup_start * SUPER_ROWS).astype(jnp.int32)
    sid = jnp.arange(MAX_SUPER, dtype=jnp.int32)
    sid_eff = jnp.minimum(sid, total_sup - 1)
    sup_e = jnp.minimum(jnp.searchsorted(sup_end, sid_eff, side="right"),
                        N_EXPERTS - 1).astype(jnp.int32)
    rows_left = sizes[sup_e] - (sid_eff - sup_start[sup_e]) * SUPER_ROWS
    valid = jnp.clip(rows_left, 0, SUPER_ROWS)
    nchunk = jnp.where(sid < total_sup, (valid + ROW_CHUNK - 1) // ROW_CHUNK, 0).astype(jnp.int32)
    zflag = (sizes > 0).astype(jnp.int32)
    zrow = (row_start + ((jnp.maximum(sizes, 1) - 1) // ROW_CHUNK) * ROW_CHUNK).astype(jnp.int32)

    eid = route[:, 0:TOP_K].reshape(-1)
    rank = route[:, TOP_K:2 * TOP_K].reshape(-1)

    buf_x = _dispatch(eid, rank, row_start, zrow, zflag, xn2)
    y_sorted = _experts(sup_e, sid_eff.astype(jnp.int32), nchunk, buf_x, w_gate_up,
                        b_gate_up.reshape(N_EXPERTS, 1, 2 * D_FF), w_down,
                        b_down.reshape(N_EXPERTS, 1, D_MODEL))
    out = _combine(eid, rank, row_start, h1, gates, row(ln_final_g), y_sorted)
    return out.reshape(BATCH, SEQ, D_MODEL)
```

```python
import functools

import jax
import jax.numpy as jnp
from jax import lax
from jax.experimental import pallas as pl
from jax.experimental.pallas import tpu as pltpu

D_MODEL = 2048
BATCH = 2
SEQ = 4096
N_TOK = BATCH * SEQ

D_CONV = 1024
N_CONV_GROUPS = 8
CONV_WIDTH = 3
D_SGU = 1024
N_SGU_HEADS = 8
SGU_HEAD_DIM = 128
CHUNK = 128
D_IN_PROJ = 3 * D_CONV + 2 * D_SGU

N_EXPERTS = 32
TOP_K = 4
D_FF = 2048
SWIGLU_LIMIT = 7.0
SWIGLU_ALPHA = 1.702
RMS_EPS = 1e-5
LN_EPS = 1e-5

LANES = 128
SUBLANES = 8
VMEM_LIMIT = 56 * 1024 * 1024

TM = 256
SUPER_ROWS = 1280
ROW_CHUNK = 256
FF_TILE = 256
N_FF_TILES = D_FF // FF_TILE
MAX_SUPER = (N_TOK * TOP_K + N_EXPERTS * (SUPER_ROWS - 1)) // SUPER_ROWS
TD = 256
TC = 256
DISPATCH_UNROLL = 8

_F32 = jnp.float32
_BF16 = jnp.bfloat16


def _dot(a, b):
    return jnp.dot(a, b, preferred_element_type=_F32)


def _gelu_exact(x):
    return 0.5 * x * (1.0 + lax.erf(x * (2.0 ** -0.5)))


def _rms_rows(x, gain):
    return x * lax.rsqrt(jnp.mean(x * x, axis=-1, keepdims=True) + RMS_EPS) * gain


def _mixer_router_kernel(x_ref, lng_ref, win_ref, convw_ref, slg_ref, slb_ref, sw_ref, sb_ref,
                         gnc_ref, gns_ref, wout_ref, lnf_ref, wr_ref, br_ref,
                         h1_ref, xn2_ref, route_ref, gates_ref, counts_ref,
                         cbuf, carry, ybuf):
    i = pl.program_id(0)

    @pl.when(i % (SEQ // TM) == 0)
    def _():
        cbuf[0:SUBLANES, :] = jnp.zeros((SUBLANES, D_CONV), _F32)

    @pl.when(i == 0)
    def _():
        carry[...] = jnp.zeros_like(carry)

    x = x_ref[...]
    xn = _rms_rows(x, lng_ref[...]).astype(_BF16)

    b_gate = _dot(xn, win_ref[:, 0:D_CONV])
    c_gate = _dot(xn, win_ref[:, D_CONV:2 * D_CONV])
    hh = _dot(xn, win_ref[:, 2 * D_CONV:3 * D_CONV])
    ch = c_gate * hh
    cbuf[SUBLANES:SUBLANES + TM, :] = ch
    ch1 = cbuf[SUBLANES - 1:SUBLANES - 1 + TM, :]
    ch2 = cbuf[SUBLANES - 2:SUBLANES - 2 + TM, :]
    conv = convw_ref[0:1, :] * ch2 + convw_ref[1:2, :] * ch1 + convw_ref[2:3, :] * ch
    cbuf[0:SUBLANES, :] = cbuf[TM:TM + SUBLANES, :]
    y_conv = b_gate * conv
    for g in range(N_CONV_GROUPS):
        sl = slice(g * LANES, (g + 1) * LANES)
        blk = y_conv[:, sl]
        ms = jnp.mean(blk * blk, axis=-1, keepdims=True)
        ybuf[:, sl] = (blk * lax.rsqrt(ms + RMS_EPS) * gnc_ref[:, sl]).astype(_BF16)

    gu = _gelu_exact(_dot(xn, win_ref[:, 3 * D_CONV:3 * D_CONV + D_SGU]))
    gv = _gelu_exact(_dot(xn, win_ref[:, 3 * D_CONV + D_SGU:D_IN_PROJ]))
    row_c = lax.broadcasted_iota(jnp.int32, (CHUNK, CHUNK), 0)
    col_c = lax.broadcasted_iota(jnp.int32, (CHUNK, CHUNK), 1)
    causal = row_c >= col_c
    for h in range(N_SGU_HEADS):
        sl = slice(h * SGU_HEAD_DIM, (h + 1) * SGU_HEAD_DIM)
        vh = gv[:, sl]
        mu = jnp.mean(vh, axis=-1, keepdims=True)
        xc = vh - mu
        var = jnp.mean(xc * xc, axis=-1, keepdims=True)
        vn = (xc * lax.rsqrt(var + LN_EPS) * slg_ref[:, sl] + slb_ref[:, sl]).astype(_BF16)
        wm = jnp.where(causal, sw_ref[h], 0.0).astype(_BF16)
        for c in range(TM // CHUNK):
            rows = slice(c * CHUNK, (c + 1) * CHUNK)
            mixed = _dot(wm, vn[rows, :]) + sb_ref[:, sl]
            ys = gu[rows, sl] * mixed
            ms = jnp.mean(ys * ys, axis=-1, keepdims=True)
            ybuf[rows, D_CONV + h * SGU_HEAD_DIM:D_CONV + (h + 1) * SGU_HEAD_DIM] = (
                ys * lax.rsqrt(ms + RMS_EPS) * gns_ref[:, sl]).astype(_BF16)

    h1 = x + _dot(ybuf[...], wout_ref[...])
    h1_ref[...] = h1
    xn2 = _rms_rows(h1, lnf_ref[...])
    xn2_ref[...] = xn2

    x_hi = xn2.astype(_BF16)
    x_lo = (xn2 - x_hi.astype(_F32)).astype(_BF16)
    p = _dot(x_hi, wr_ref[...]) + _dot(x_lo, wr_ref[...])
    logits = p[:, :LANES] + p[:, LANES:] + br_ref[...]
    lane = lax.broadcasted_iota(jnp.int32, (TM, LANES), 1)
    lane_f = lane.astype(_F32)
    neg_inf = jnp.float32(-jnp.inf)
    logits = jnp.where(lane < N_EXPERTS, logits, neg_inf)

    vals, ids, onehots = [], [], []
    cur = logits
    for _ in range(TOP_K):
        m = jnp.max(cur, axis=-1, keepdims=True)
        idx = jnp.min(jnp.where(cur == m, lane_f, float(LANES)), axis=-1, keepdims=True)
        oh = lane_f == idx
        vals.append(m)
        ids.append(idx)
        onehots.append(oh)
        cur = jnp.where(oh, neg_inf, cur)
    exps = [jnp.exp(v - vals[0]) for v in vals]
    denom = exps[0] + exps[1] + exps[2] + exps[3]
    gates = [e / denom for e in exps]

    mask = (onehots[0] | onehots[1] | onehots[2] | onehots[3]).astype(_F32)
    row_t = lax.broadcasted_iota(jnp.int32, (TM, TM), 0)
    col_t = lax.broadcasted_iota(jnp.int32, (TM, TM), 1)
    strict_lower = (row_t > col_t).astype(_BF16)
    before = _dot(strict_lower, mask.astype(_BF16)) + carry[...]
    ranks = [jnp.sum(jnp.where(oh, before, 0.0), axis=-1, keepdims=True) for oh in onehots]
    carry[...] = carry[...] + jnp.sum(mask, axis=0, keepdims=True)
    counts_ref[...] = jnp.broadcast_to(carry[...], counts_ref.shape)

    route = jnp.zeros((TM, LANES), _F32)
    gate_out = jnp.zeros((TM, LANES), _F32)
    for k in range(TOP_K):
        route = jnp.where(lane == k, ids[k], route)
        route = jnp.where(lane == TOP_K + k, ranks[k], route)
        gate_out = jnp.where(lane == k, gates[k], gate_out)
    route_ref[...] = route.astype(jnp.int32)
    gates_ref[...] = gate_out


def _mixer_router(x2d, ln_mix_g, w_in_bf, conv_w, sgu_ln_g, sgu_ln_b, sgu_w, sgu_b_full,
                  gn_conv, gn_sgu, w_out_bf, ln_ffn_g, wr_split, br_pad):
    def full(a):
        return pl.BlockSpec(a.shape, lambda i: (0,) * a.ndim)

    row_blk = lambda w: pl.BlockSpec((TM, w), lambda i: (i, 0))
    ins = [x2d, ln_mix_g, w_in_bf, conv_w, sgu_ln_g, sgu_ln_b, sgu_w, sgu_b_full,
           gn_conv, gn_sgu, w_out_bf, ln_ffn_g, wr_split, br_pad]
    in_specs = [row_blk(D_MODEL)] + [full(a) for a in ins[1:]]
    return pl.pallas_call(
        _mixer_router_kernel,
        grid=(N_TOK // TM,),
        in_specs=in_specs,
        out_specs=[row_blk(D_MODEL), row_blk(D_MODEL), row_blk(LANES), row_blk(LANES),
                   pl.BlockSpec((SUBLANES, LANES), lambda i: (0, 0))],
        out_shape=[jax.ShapeDtypeStruct((N_TOK, D_MODEL), _F32),
                   jax.ShapeDtypeStruct((N_TOK, D_MODEL), _F32),
                   jax.ShapeDtypeStruct((N_TOK, LANES), jnp.int32),
                   jax.ShapeDtypeStruct((N_TOK, LANES), _F32),
                   jax.ShapeDtypeStruct((SUBLANES, LANES), _F32)],
        scratch_shapes=[pltpu.VMEM((TM + SUBLANES, D_CONV), _F32),
                        pltpu.VMEM((1, LANES), _F32),
                        pltpu.VMEM((TM, D_MODEL), _BF16)],
        compiler_params=pltpu.CompilerParams(dimension_semantics=("arbitrary",),
                                             vmem_limit_bytes=VMEM_LIMIT),
        name="mixer_router",
    )(*ins)


def _dispatch_kernel(eid_ref, rank_ref, rstart_ref, zrow_ref, zflag_ref,
                     x_ref, bufx_hbm, zbuf, zsem, sem):
    i = pl.program_id(0)

    @pl.when(i == 0)
    def _():
        zbuf[...] = jnp.zeros_like(zbuf)

        def zero_copy(e):
            z0 = pl.multiple_of(zrow_ref[e], ROW_CHUNK)
            return pltpu.make_async_copy(zbuf, bufx_hbm.at[pl.ds(z0, ROW_CHUNK)], zsem)

        for e in range(N_EXPERTS):
            @pl.when(zflag_ref[e] > 0)
            def _(e=e):
                zero_copy(e).start()
        for e in range(N_EXPERTS):
            @pl.when(zflag_ref[e] > 0)
            def _(e=e):
                zero_copy(e).wait()

    def row_copy(t8, u, k):
        t = pl.multiple_of(t8 * SUBLANES, SUBLANES) + u
        a = (i * TD + t) * TOP_K + k
        dst = rstart_ref[eid_ref[a]] + rank_ref[a]
        return pltpu.make_async_copy(x_ref.at[pl.ds(t, 1)], bufx_hbm.at[pl.ds(dst, 1)], sem)

    def issue(t8, carry):
        for u in range(SUBLANES):
            for k in range(TOP_K):
                row_copy(t8, u, k).start()
        return carry

    def drain(t8, carry):
        for u in range(SUBLANES):
            for k in range(TOP_K):
                row_copy(t8, u, k).wait()
        return carry

    lax.fori_loop(0, TD // SUBLANES, issue, 0)
    lax.fori_loop(0, TD // SUBLANES, drain, 0)


def _dispatch(eid, rank, row_start, zrow, zflag, xn2):
    return pl.pallas_call(
        _dispatch_kernel,
        grid_spec=pltpu.PrefetchScalarGridSpec(
            num_scalar_prefetch=5,
            grid=(N_TOK // TD,),
            in_specs=[pl.BlockSpec((TD, D_MODEL), lambda i, *_: (i, 0))],
            out_specs=pl.BlockSpec(memory_space=pl.ANY),
            scratch_shapes=[pltpu.VMEM((ROW_CHUNK, D_MODEL), _F32),
                            pltpu.SemaphoreType.DMA(()),
                            pltpu.SemaphoreType.DMA(())],
        ),
        out_shape=jax.ShapeDtypeStruct((MAX_SUPER * SUPER_ROWS, D_MODEL), _F32),
        compiler_params=pltpu.CompilerParams(dimension_semantics=("arbitrary",),
                                             vmem_limit_bytes=VMEM_LIMIT),
        name="dispatch",
    )(eid, rank, row_start, zrow, zflag, xn2)


def _expert_kernel(sup_e_ref, sup_blk_ref, nchunk_ref,
                   x_hbm, wg_ref, wu_ref, bg_ref, bu_ref, wd_ref, bd_ref, out_ref,
                   x_stage, x_bf, wg_s, wu_s, wd_s, x_sem):
    s = pl.program_id(0)
    j = pl.program_id(1)
    n_chunks = nchunk_ref[s]
    first_row = sup_blk_ref[s] * SUPER_ROWS

    def x_copy(c, slot):
        r = pl.multiple_of(first_row + c * ROW_CHUNK, ROW_CHUNK)
        return pltpu.make_async_copy(x_hbm.at[pl.ds(r, ROW_CHUNK)], x_stage.at[slot],
                                     x_sem.at[slot])

    @pl.when((n_chunks > 0) & (j == 0))
    def _():
        x_copy(0, 0).start()

    @pl.when(n_chunks > 0)
    def _():
        wg_s[...] = wg_ref[0].astype(_BF16)
        wu_s[...] = wu_ref[0].astype(_BF16)
        wd_s[...] = wd_ref[0].astype(_BF16)

        def partial_out(c):
            r0 = pl.multiple_of(c * ROW_CHUNK, ROW_CHUNK)
            xb = x_bf[pl.ds(r0, ROW_CHUNK), :]
            gate = _dot(xb, wg_s[...]) + bg_ref[0]
            up = _dot(xb, wu_s[...]) + bu_ref[0]
            gate = jnp.minimum(gate, SWIGLU_LIMIT)
            up = jnp.clip(up, -SWIGLU_LIMIT, SWIGLU_LIMIT)
            glu = gate * jax.nn.sigmoid(gate * SWIGLU_ALPHA)
            act = ((up + 1.0) * glu).astype(_BF16)
            return r0, _dot(act, wd_s[...])

        @pl.when(j == 0)
        def _():
            def body(c, carry):
                slot = c % 2

                @pl.when(c + 1 < n_chunks)
                def _():
                    x_copy(c + 1, 1 - slot).start()

                x_copy(c, slot).wait()
                rows = pl.ds(pl.multiple_of(c * ROW_CHUNK, ROW_CHUNK), ROW_CHUNK)
                x_bf[rows, :] = x_stage[slot].astype(_BF16)
                r0, o = partial_out(c)
                out_ref[pl.ds(r0, ROW_CHUNK), :] = o + bd_ref[0]
                return carry
            lax.fori_loop(0, n_chunks, body, 0)

        @pl.when(j > 0)
        def _():
            def body(c, carry):
                r0, o = partial_out(c)
                out_ref[pl.ds(r0, ROW_CHUNK), :] += o
                return carry
            lax.fori_loop(0, n_chunks, body, 0)


def _experts(sup_e, sup_blk, nchunk, buf_x, w_gate_up, b_gate_up3, w_down, b_down3):
    def ff_tile(j, nchunk, s):
        return jnp.where(nchunk[s] > 0, j, N_FF_TILES - 1)

    x_spec = pl.BlockSpec(memory_space=pl.ANY)
    wg_spec = pl.BlockSpec((1, D_MODEL, FF_TILE),
                           lambda s, j, se, sb, nc: (se[s], 0, ff_tile(j, nc, s)))
    wu_spec = pl.BlockSpec((1, D_MODEL, FF_TILE),
                           lambda s, j, se, sb, nc: (se[s], 0, N_FF_TILES + ff_tile(j, nc, s)))
    bg_spec = pl.BlockSpec((1, 1, FF_TILE),
                           lambda s, j, se, sb, nc: (se[s], 0, ff_tile(j, nc, s)))
    bu_spec = pl.BlockSpec((1, 1, FF_TILE),
                           lambda s, j, se, sb, nc: (se[s], 0, N_FF_TILES + ff_tile(j, nc, s)))
    wd_spec = pl.BlockSpec((1, FF_TILE, D_MODEL),
                           lambda s, j, se, sb, nc: (se[s], ff_tile(j, nc, s), 0))
    bd_spec = pl.BlockSpec((1, 1, D_MODEL), lambda s, j, se, sb, nc: (se[s], 0, 0))
    out_spec = pl.BlockSpec((SUPER_ROWS, D_MODEL), lambda s, j, se, sb, nc: (sb[s], 0))
    return pl.pallas_call(
        _expert_kernel,
        grid_spec=pltpu.PrefetchScalarGridSpec(
            num_scalar_prefetch=3,
            grid=(MAX_SUPER, N_FF_TILES),
            in_specs=[x_spec, wg_spec, wu_spec, bg_spec, bu_spec, wd_spec, bd_spec],
            out_specs=out_spec,
            scratch_shapes=[pltpu.VMEM((2, ROW_CHUNK, D_MODEL), _F32),
                            pltpu.VMEM((SUPER_ROWS, D_MODEL), _BF16),
                            pltpu.VMEM((D_MODEL, FF_TILE), _BF16),
                            pltpu.VMEM((D_MODEL, FF_TILE), _BF16),
                            pltpu.VMEM((FF_TILE, D_MODEL), _BF16),
                            pltpu.SemaphoreType.DMA((2,))],
        ),
        out_shape=jax.ShapeDtypeStruct((MAX_SUPER * SUPER_ROWS, D_MODEL), _F32),
        compiler_params=pltpu.CompilerParams(dimension_semantics=("arbitrary", "arbitrary"),
                                             vmem_limit_bytes=VMEM_LIMIT),
        name="experts",
    )(sup_e, sup_blk, nchunk, buf_x, w_gate_up, w_gate_up, b_gate_up3, b_gate_up3, w_down, b_down3)


def _combine_kernel(eid_ref, rank_ref, rstart_ref,
                    h1_ref, gates_ref, lng_ref, y_hbm, out_ref, rows, sem):
    i = pl.program_id(0)
    n_steps = pl.num_programs(0)

    def row_copy(tile, slot, t8, u, k):
        t = pl.multiple_of(t8 * SUBLANES, SUBLANES) + u
        a = (tile * TC + t) * TOP_K + k
        src = rstart_ref[eid_ref[a]] + rank_ref[a]
        return pltpu.make_async_copy(y_hbm.at[pl.ds(src, 1)], rows.at[slot, k, pl.ds(t, 1)],
                                     sem.at[slot])

    def issue_tile(tile, slot):
        def body(t8, carry):
            for u in range(SUBLANES):
                for k in range(TOP_K):
                    row_copy(tile, slot, t8, u, k).start()
            return carry
        lax.fori_loop(0, TC // SUBLANES, body, 0)

    def wait_tile(tile, slot):
        def body(t8, carry):
            for u in range(SUBLANES):
                for k in range(TOP_K):
                    row_copy(tile, slot, t8, u, k).wait()
            return carry
        lax.fori_loop(0, TC // SUBLANES, body, 0)

    slot = i % 2

    @pl.when(i == 0)
    def _():
        issue_tile(0, 0)

    @pl.when(i + 1 < n_steps)
    def _():
        issue_tile(i + 1, 1 - slot)

    wait_tile(i, slot)

    g = gates_ref[...]
    acc = h1_ref[...]
    for k in range(TOP_K):
        acc = acc + rows[slot, k] * g[:, k:k + 1]
    out_ref[...] = _rms_rows(acc, lng_ref[...])


def _combine(eid, rank, row_start, h1, gates, ln_final_g, y_sorted):
    return pl.pallas_call(
        _combine_kernel,
        grid_spec=pltpu.PrefetchScalarGridSpec(
            num_scalar_prefetch=3,
            grid=(N_TOK // TC,),
            in_specs=[pl.BlockSpec((TC, D_MODEL), lambda i, *_: (i, 0)),
                      pl.BlockSpec((TC, LANES), lambda i, *_: (i, 0)),
                      pl.BlockSpec((1, D_MODEL), lambda i, *_: (0, 0)),
                      pl.BlockSpec(memory_space=pl.ANY)],
            out_specs=pl.BlockSpec((TC, D_MODEL), lambda i, *_: (i, 0)),
            scratch_shapes=[pltpu.VMEM((2, TOP_K, TC, D_MODEL), _F32),
                            pltpu.SemaphoreType.DMA((2,))],
        ),
        out_shape=jax.ShapeDtypeStruct((N_TOK, D_MODEL), _F32),
        compiler_params=pltpu.CompilerParams(dimension_semantics=("arbitrary",),
                                             vmem_limit_bytes=VMEM_LIMIT),
        name="combine",
    )(eid, rank, row_start, h1, gates, ln_final_g, y_sorted)


def kernel(x, ln_mix_g, w_in, conv_w, sgu_ln_g, sgu_ln_b, sgu_w, sgu_b, gn_conv, gn_sgu, w_out,
           ln_ffn_g, w_router, b_router, w_gate_up, b_gate_up, w_down, b_down, ln_final_g):
    row = lambda v: v.reshape(1, -1)
    x2d = x.reshape(N_TOK, D_MODEL)

    sgu_b_full = jnp.repeat(jnp.transpose(sgu_b), SGU_HEAD_DIM, axis=1)
    wr_hi = w_router.astype(_BF16)
    wr_lo = (w_router - wr_hi.astype(_F32)).astype(_BF16)
    pad = ((0, 0), (0, LANES - N_EXPERTS))
    wr_split = jnp.concatenate([jnp.pad(wr_hi, pad), jnp.pad(wr_lo, pad)], axis=1)
    br_pad = jnp.pad(b_router, (0, LANES - N_EXPERTS)).reshape(1, LANES)

    h1, xn2, route, gates, counts = _mixer_router(
        x2d, row(ln_mix_g), w_in.astype(_BF16), conv_w, row(sgu_ln_g), row(sgu_ln_b), sgu_w,
        sgu_b_full, row(gn_conv), row(gn_sgu), w_out.astype(_BF16), row(ln_ffn_g), wr_split, br_pad)

    sizes = counts[0, :N_EXPERTS].astype(jnp.int32)
    n_sup = (sizes + SUPER_ROWS - 1) // SUPER_ROWS
    sup_end = jnp.cumsum(n_sup)
    sup_start = sup_end - n_sup
    total_sup = sup_end[-1]
    row_start = (sup_start * SUPER_ROWS).astype(jnp.int32)
    sid = jnp.arange(MAX_SUPER, dtype=jnp.int32)
    sid_eff = jnp.minimum(sid, total_sup - 1)
    sup_e = jnp.minimum(jnp.searchsorted(sup_end, sid_eff, side="right"),
                        N_EXPERTS - 1).astype(jnp.int32)
    rows_left = sizes[sup_e] - (sid_eff - sup_start[sup_e]) * SUPER_ROWS
    valid = jnp.clip(rows_left, 0, SUPER_ROWS)
    nchunk = jnp.where(sid < total_sup, (valid + ROW_CHUNK - 1) // ROW_CHUNK, 0).astype(jnp.int32)
    zflag = (sizes > 0).astype(jnp.int32)
    zrow = (row_start + ((jnp.maximum(sizes, 1) - 1) // ROW_CHUNK) * ROW_CHUNK).astype(jnp.int32)

    eid = route[:, 0:TOP_K].reshape(-1)
    rank = route[:, TOP_K:2 * TOP_K].reshape(-1)

    buf_x = _dispatch(eid, rank, row_start, zrow, zflag, xn2)
    y_sorted = _experts(sup_e, sid_eff.astype(jnp.int32), nchunk, buf_x, w_gate_up,
                        b_gate_up.reshape(N_EXPERTS, 1, 2 * D_FF), w_down,
                        b_down.reshape(N_EXPERTS, 1, D_MODEL))
    out = _combine(eid, rank, row_start, h1, gates, row(ln_final_g), y_sorted)
    return out.reshape(BATCH, SEQ, D_MODEL)
```

```python
import functools

import jax
import jax.numpy as jnp
from jax import lax
from jax.experimental import pallas as pl
from jax.experimental.pallas import tpu as pltpu

D_MODEL = 2048
BATCH = 2
SEQ = 4096
N_TOK = BATCH * SEQ

D_CONV = 1024
N_CONV_GROUPS = 8
CONV_WIDTH = 3
D_SGU = 1024
N_SGU_HEADS = 8
SGU_HEAD_DIM = 128
CHUNK = 128
D_IN_PROJ = 3 * D_CONV + 2 * D_SGU

N_EXPERTS = 32
TOP_K = 4
D_FF = 2048
SWIGLU_LIMIT = 7.0
SWIGLU_ALPHA = 1.702
RMS_EPS = 1e-5
LN_EPS = 1e-5

LANES = 128
SUBLANES = 8
VMEM_LIMIT = 56 * 1024 * 1024

TM = 256
SUPER_ROWS = 1280
ROW_CHUNK = 256
FF_TILE = 256
N_FF_TILES = D_FF // FF_TILE
MAX_SUPER = (N_TOK * TOP_K + N_EXPERTS * (SUPER_ROWS - 1)) // SUPER_ROWS
X_AHEAD = 2
X_SLOTS = X_AHEAD + 1
TD = 256
TC = 256
DISPATCH_UNROLL = 8

_F32 = jnp.float32
_BF16 = jnp.bfloat16


def _dot(a, b):
    return jnp.dot(a, b, preferred_element_type=_F32)


def _gelu_exact(x):
    return 0.5 * x * (1.0 + lax.erf(x * (2.0 ** -0.5)))


def _rms_rows(x, gain):
    return x * lax.rsqrt(jnp.mean(x * x, axis=-1, keepdims=True) + RMS_EPS) * gain


def _mixer_router_kernel(x_ref, lng_ref, win_ref, convw_ref, slg_ref, slb_ref, sw_ref, sb_ref,
                         gnc_ref, gns_ref, wout_ref, lnf_ref, wr_ref, br_ref,
                         h1_ref, xn2_ref, route_ref, gates_ref, counts_ref,
                         cbuf, carry, ybuf):
    i = pl.program_id(0)

    @pl.when(i % (SEQ // TM) == 0)
    def _():
        cbuf[0:SUBLANES, :] = jnp.zeros((SUBLANES, D_CONV), _F32)

    @pl.when(i == 0)
    def _():
        carry[...] = jnp.zeros_like(carry)

    x = x_ref[...]
    xn = _rms_rows(x, lng_ref[...]).astype(_BF16)

    b_gate = _dot(xn, win_ref[:, 0:D_CONV])
    c_gate = _dot(xn, win_ref[:, D_CONV:2 * D_CONV])
    hh = _dot(xn, win_ref[:, 2 * D_CONV:3 * D_CONV])
    ch = c_gate * hh
    cbuf[SUBLANES:SUBLANES + TM, :] = ch
    ch1 = cbuf[SUBLANES - 1:SUBLANES - 1 + TM, :]
    ch2 = cbuf[SUBLANES - 2:SUBLANES - 2 + TM, :]
    conv = convw_ref[0:1, :] * ch2 + convw_ref[1:2, :] * ch1 + convw_ref[2:3, :] * ch
    cbuf[0:SUBLANES, :] = cbuf[TM:TM + SUBLANES, :]
    y_conv = b_gate * conv
    for g in range(N_CONV_GROUPS):
        sl = slice(g * LANES, (g + 1) * LANES)
        blk = y_conv[:, sl]
        ms = jnp.mean(blk * blk, axis=-1, keepdims=True)
        ybuf[:, sl] = (blk * lax.rsqrt(ms + RMS_EPS) * gnc_ref[:, sl]).astype(_BF16)

    gu = _gelu_exact(_dot(xn, win_ref[:, 3 * D_CONV:3 * D_CONV + D_SGU]))
    gv = _gelu_exact(_dot(xn, win_ref[:, 3 * D_CONV + D_SGU:D_IN_PROJ]))
    row_c = lax.broadcasted_iota(jnp.int32, (CHUNK, CHUNK), 0)
    col_c = lax.broadcasted_iota(jnp.int32, (CHUNK, CHUNK), 1)
    causal = row_c >= col_c
    for h in range(N_SGU_HEADS):
        sl = slice(h * SGU_HEAD_DIM, (h + 1) * SGU_HEAD_DIM)
        vh = gv[:, sl]
        mu = jnp.mean(vh, axis=-1, keepdims=True)
        xc = vh - mu
        var = jnp.mean(xc * xc, axis=-1, keepdims=True)
        vn = (xc * lax.rsqrt(var + LN_EPS) * slg_ref[:, sl] + slb_ref[:, sl]).astype(_BF16)
        wm = jnp.where(causal, sw_ref[h], 0.0).astype(_BF16)
        for c in range(TM // CHUNK):
            rows = slice(c * CHUNK, (c + 1) * CHUNK)
            mixed = _dot(wm, vn[rows, :]) + sb_ref[:, sl]
            ys = gu[rows, sl] * mixed
            ms = jnp.mean(ys * ys, axis=-1, keepdims=True)
            ybuf[rows, D_CONV + h * SGU_HEAD_DIM:D_CONV + (h + 1) * SGU_HEAD_DIM] = (
                ys * lax.rsqrt(ms + RMS_EPS) * gns_ref[:, sl]).astype(_BF16)

    h1 = x + _dot(ybuf[...], wout_ref[...])
    h1_ref[...] = h1
    xn2 = _rms_rows(h1, lnf_ref[...])
    xn2_ref[...] = xn2

    x_hi = xn2.astype(_BF16)
    x_lo = (xn2 - x_hi.astype(_F32)).astype(_BF16)
    p = _dot(x_hi, wr_ref[...]) + _dot(x_lo, wr_ref[...])
    logits = p[:, :LANES] + p[:, LANES:] + br_ref[...]
    lane = lax.broadcasted_iota(jnp.int32, (TM, LANES), 1)
    lane_f = lane.astype(_F32)
    neg_inf = jnp.float32(-jnp.inf)
    logits = jnp.where(lane < N_EXPERTS, logits, neg_inf)

    vals, ids, onehots = [], [], []
    cur = logits
    for _ in range(TOP_K):
        m = jnp.max(cur, axis=-1, keepdims=True)
        idx = jnp.min(jnp.where(cur == m, lane_f, float(LANES)), axis=-1, keepdims=True)
        oh = lane_f == idx
        vals.append(m)
        ids.append(idx)
        onehots.append(oh)
        cur = jnp.where(oh, neg_inf, cur)
    exps = [jnp.exp(v - vals[0]) for v in vals]
    denom = exps[0] + exps[1] + exps[2] + exps[3]
    gates = [e / denom for e in exps]

    mask = (onehots[0] | onehots[1] | onehots[2] | onehots[3]).astype(_F32)
    row_t = lax.broadcasted_iota(jnp.int32, (TM, TM), 0)
    col_t = lax.broadcasted_iota(jnp.int32, (TM, TM), 1)
    strict_lower = (row_t > col_t).astype(_BF16)
    before = _dot(strict_lower, mask.astype(_BF16)) + carry[...]
    ranks = [jnp.sum(jnp.where(oh, before, 0.0), axis=-1, keepdims=True) for oh in onehots]
    carry[...] = carry[...] + jnp.sum(mask, axis=0, keepdims=True)
    counts_ref[...] = jnp.broadcast_to(carry[...], counts_ref.shape)

    route = jnp.zeros((TM, LANES), _F32)
    gate_out = jnp.zeros((TM, LANES), _F32)
    for k in range(TOP_K):
        route = jnp.where(lane == k, ids[k], route)
        route = jnp.where(lane == TOP_K + k, ranks[k], route)
        gate_out = jnp.where(lane == k, gates[k], gate_out)
    route_ref[...] = route.astype(jnp.int32)
    gates_ref[...] = gate_out


def _mixer_router(x2d, ln_mix_g, w_in_bf, conv_w, sgu_ln_g, sgu_ln_b, sgu_w, sgu_b_full,
                  gn_conv, gn_sgu, w_out_bf, ln_ffn_g, wr_split, br_pad):
    def full(a):
        return pl.BlockSpec(a.shape, lambda i: (0,) * a.ndim)

    row_blk = lambda w: pl.BlockSpec((TM, w), lambda i: (i, 0))
    ins = [x2d, ln_mix_g, w_in_bf, conv_w, sgu_ln_g, sgu_ln_b, sgu_w, sgu_b_full,
           gn_conv, gn_sgu, w_out_bf, ln_ffn_g, wr_split, br_pad]
    in_specs = [row_blk(D_MODEL)] + [full(a) for a in ins[1:]]
    return pl.pallas_call(
        _mixer_router_kernel,
        grid=(N_TOK // TM,),
        in_specs=in_specs,
        out_specs=[row_blk(D_MODEL), row_blk(D_MODEL), row_blk(LANES), row_blk(LANES),
                   pl.BlockSpec((SUBLANES, LANES), lambda i: (0, 0))],
        out_shape=[jax.ShapeDtypeStruct((N_TOK, D_MODEL), _F32),
                   jax.ShapeDtypeStruct((N_TOK, D_MODEL), _F32),
                   jax.ShapeDtypeStruct((N_TOK, LANES), jnp.int32),
                   jax.ShapeDtypeStruct((N_TOK, LANES), _F32),
                   jax.ShapeDtypeStruct((SUBLANES, LANES), _F32)],
        scratch_shapes=[pltpu.VMEM((TM + SUBLANES, D_CONV), _F32),
                        pltpu.VMEM((1, LANES), _F32),
                        pltpu.VMEM((TM, D_MODEL), _BF16)],
        compiler_params=pltpu.CompilerParams(dimension_semantics=("arbitrary",),
                                             vmem_limit_bytes=VMEM_LIMIT),
        name="mixer_router",
    )(*ins)


def _dispatch_kernel(eid_ref, rank_ref, rstart_ref, zrow_ref, zflag_ref,
                     x_ref, bufx_hbm, zbuf, zsem, sem):
    i = pl.program_id(0)

    @pl.when(i == 0)
    def _():
        zbuf[...] = jnp.zeros_like(zbuf)

        def zero_copy(e):
            z0 = pl.multiple_of(zrow_ref[e], ROW_CHUNK)
            return pltpu.make_async_copy(zbuf, bufx_hbm.at[pl.ds(z0, ROW_CHUNK)], zsem)

        for e in range(N_EXPERTS):
            @pl.when(zflag_ref[e] > 0)
            def _(e=e):
                zero_copy(e).start()
        for e in range(N_EXPERTS):
            @pl.when(zflag_ref[e] > 0)
            def _(e=e):
                zero_copy(e).wait()

    def row_copy(t8, u, k):
        t = pl.multiple_of(t8 * SUBLANES, SUBLANES) + u
        a = (i * TD + t) * TOP_K + k
        dst = rstart_ref[eid_ref[a]] + rank_ref[a]
        return pltpu.make_async_copy(x_ref.at[pl.ds(t, 1)], bufx_hbm.at[pl.ds(dst, 1)], sem)

    def issue(t8, carry):
        for u in range(SUBLANES):
            for k in range(TOP_K):
                row_copy(t8, u, k).start()
        return carry

    def drain(t8, carry):
        for u in range(SUBLANES):
            for k in range(TOP_K):
                row_copy(t8, u, k).wait()
        return carry

    lax.fori_loop(0, TD // SUBLANES, issue, 0)
    lax.fori_loop(0, TD // SUBLANES, drain, 0)


def _dispatch(eid, rank, row_start, zrow, zflag, xn2):
    return pl.pallas_call(
        _dispatch_kernel,
        grid_spec=pltpu.PrefetchScalarGridSpec(
            num_scalar_prefetch=5,
            grid=(N_TOK // TD,),
            in_specs=[pl.BlockSpec((TD, D_MODEL), lambda i, *_: (i, 0))],
            out_specs=pl.BlockSpec(memory_space=pl.ANY),
            scratch_shapes=[pltpu.VMEM((ROW_CHUNK, D_MODEL), _F32),
                            pltpu.SemaphoreType.DMA(()),
                            pltpu.SemaphoreType.DMA(())],
        ),
        out_shape=jax.ShapeDtypeStruct((MAX_SUPER * SUPER_ROWS, D_MODEL), _F32),
        compiler_params=pltpu.CompilerParams(dimension_semantics=("arbitrary",),
                                             vmem_limit_bytes=VMEM_LIMIT),
        name="dispatch",
    )(eid, rank, row_start, zrow, zflag, xn2)


def _expert_kernel(sup_e_ref, sup_blk_ref, nchunk_ref,
                   x_hbm, wg_ref, wu_ref, bg_ref, bu_ref, wd_ref, bd_ref, out_ref,
                   x_stage, x_bf, x_sem):
    s = pl.program_id(0)
    j = pl.program_id(1)
    n_chunks = nchunk_ref[s]

    def x_copy(sup, c):
        r = pl.multiple_of(sup_blk_ref[sup] * SUPER_ROWS + c * ROW_CHUNK, ROW_CHUNK)
        slot = c % X_SLOTS
        return pltpu.make_async_copy(x_hbm.at[pl.ds(r, ROW_CHUNK)], x_stage.at[slot],
                                     x_sem.at[slot])

    def prefetch_head(sup):
        for c in range(X_AHEAD):
            @pl.when(c < nchunk_ref[sup])
            def _(c=c):
                x_copy(sup, c).start()

    @pl.when((s == 0) & (j == 0))
    def _():
        prefetch_head(0)

    @pl.when(n_chunks > 0)
    def _():
        def partial_out(c, n_rows):
            r0 = pl.multiple_of(c * ROW_CHUNK, ROW_CHUNK)
            xb = x_bf[pl.ds(r0, n_rows), :]
            gate = _dot(xb, wg_ref[0].astype(_BF16)) + bg_ref[0]
            up = _dot(xb, wu_ref[0].astype(_BF16)) + bu_ref[0]
            gate = jnp.minimum(gate, SWIGLU_LIMIT)
            up = jnp.clip(up, -SWIGLU_LIMIT, SWIGLU_LIMIT)
            glu = gate * jax.nn.sigmoid(gate * SWIGLU_ALPHA)
            act = ((up + 1.0) * glu).astype(_BF16)
            return pl.ds(r0, n_rows), _dot(act, wd_ref[0].astype(_BF16))

        @pl.when(j == 0)
        def _():
            def body(c, carry):
                @pl.when(c + X_AHEAD < n_chunks)
                def _():
                    x_copy(s, c + X_AHEAD).start()

                x_copy(s, c).wait()
                rows = pl.ds(pl.multiple_of(c * ROW_CHUNK, ROW_CHUNK), ROW_CHUNK)
                x_bf[rows, :] = x_stage[c % X_SLOTS].astype(_BF16)
                rows, o = partial_out(c, ROW_CHUNK)
                out_ref[rows, :] = o + bd_ref[0]
                return carry
            lax.fori_loop(0, n_chunks, body, 0)

        @pl.when(j > 0)
        def _():
            def accumulate(c, n_rows):
                rows, o = partial_out(c, n_rows)
                out_ref[rows, :] += o

            def pair(p, carry):
                accumulate(2 * p, 2 * ROW_CHUNK)
                return carry
            lax.fori_loop(0, n_chunks // 2, pair, 0)

            @pl.when(n_chunks % 2 == 1)
            def _():
                accumulate(n_chunks - 1, ROW_CHUNK)

    @pl.when((j == N_FF_TILES - 1) & (s + 1 < MAX_SUPER))
    def _():
        prefetch_head(jnp.minimum(s + 1, MAX_SUPER - 1))


def _experts(sup_e, sup_blk, nchunk, buf_x, w_gate_up, b_gate_up3, w_down, b_down3):
    def ff_tile(j, nchunk, s):
        return jnp.where(nchunk[s] > 0, j, N_FF_TILES - 1)

    x_spec = pl.BlockSpec(memory_space=pl.ANY)
    wg_spec = pl.BlockSpec((1, D_MODEL, FF_TILE),
                           lambda s, j, se, sb, nc: (se[s], 0, ff_tile(j, nc, s)))
    wu_spec = pl.BlockSpec((1, D_MODEL, FF_TILE),
                           lambda s, j, se, sb, nc: (se[s], 0, N_FF_TILES + ff_tile(j, nc, s)))
    bg_spec = pl.BlockSpec((1, 1, FF_TILE),
                           lambda s, j, se, sb, nc: (se[s], 0, ff_tile(j, nc, s)))
    bu_spec = pl.BlockSpec((1, 1, FF_TILE),
                           lambda s, j, se, sb, nc: (se[s], 0, N_FF_TILES + ff_tile(j, nc, s)))
    wd_spec = pl.BlockSpec((1, FF_TILE, D_MODEL),
                           lambda s, j, se, sb, nc: (se[s], ff_tile(j, nc, s), 0))
    bd_spec = pl.BlockSpec((1, 1, D_MODEL), lambda s, j, se, sb, nc: (se[s], 0, 0))
    out_spec = pl.BlockSpec((SUPER_ROWS, D_MODEL), lambda s, j, se, sb, nc: (sb[s], 0))
    return pl.pallas_call(
        _expert_kernel,
        grid_spec=pltpu.PrefetchScalarGridSpec(
            num_scalar_prefetch=3,
            grid=(MAX_SUPER, N_FF_TILES),
            in_specs=[x_spec, wg_spec, wu_spec, bg_spec, bu_spec, wd_spec, bd_spec],
            out_specs=out_spec,
            scratch_shapes=[pltpu.VMEM((X_SLOTS, ROW_CHUNK, D_MODEL), _F32),
                            pltpu.VMEM((SUPER_ROWS, D_MODEL), _BF16),
                            pltpu.SemaphoreType.DMA((X_SLOTS,))],
        ),
        out_shape=jax.ShapeDtypeStruct((MAX_SUPER * SUPER_ROWS, D_MODEL), _F32),
        compiler_params=pltpu.CompilerParams(dimension_semantics=("arbitrary", "arbitrary"),
                                             vmem_limit_bytes=VMEM_LIMIT),
        name="experts",
    )(sup_e, sup_blk, nchunk, buf_x, w_gate_up, w_gate_up, b_gate_up3, b_gate_up3, w_down, b_down3)


def _combine_kernel(eid_ref, rank_ref, rstart_ref,
                    h1_ref, gates_ref, lng_ref, y_hbm, out_ref, rows, sem):
    i = pl.program_id(0)
    n_steps = pl.num_programs(0)

    def row_copy(tile, slot, t8, u, k):
        t = pl.multiple_of(t8 * SUBLANES, SUBLANES) + u
        a = (tile * TC + t) * TOP_K + k
        src = rstart_ref[eid_ref[a]] + rank_ref[a]
        return pltpu.make_async_copy(y_hbm.at[pl.ds(src, 1)], rows.at[slot, k, pl.ds(t, 1)],
                                     sem.at[slot])

    def issue_tile(tile, slot):
        def body(t8, carry):
            for u in range(SUBLANES):
                for k in range(TOP_K):
                    row_copy(tile, slot, t8, u, k).start()
            return carry
        lax.fori_loop(0, TC // SUBLANES, body, 0)

    def wait_tile(tile, slot):
        def body(t8, carry):
            for u in range(SUBLANES):
                for k in range(TOP_K):
                    row_copy(tile, slot, t8, u, k).wait()
            return carry
        lax.fori_loop(0, TC // SUBLANES, body, 0)

    slot = i % 2

    @pl.when(i == 0)
    def _():
        issue_tile(0, 0)

    @pl.when(i + 1 < n_steps)
    def _():
        issue_tile(i + 1, 1 - slot)

    wait_tile(i, slot)

    g = gates_ref[...]
    acc = h1_ref[...]
    for k in range(TOP_K):
        acc = acc + rows[slot, k] * g[:, k:k + 1]
    out_ref[...] = _rms_rows(acc, lng_ref[...])


def _combine(eid, rank, row_start, h1, gates, ln_final_g, y_sorted):
    return pl.pallas_call(
        _combine_kernel,
        grid_spec=pltpu.PrefetchScalarGridSpec(
            num_scalar_prefetch=3,
            grid=(N_TOK // TC,),
            in_specs=[pl.BlockSpec((TC, D_MODEL), lambda i, *_: (i, 0)),
                      pl.BlockSpec((TC, LANES), lambda i, *_: (i, 0)),
                      pl.BlockSpec((1, D_MODEL), lambda i, *_: (0, 0)),
                      pl.BlockSpec(memory_space=pl.ANY)],
            out_specs=pl.BlockSpec((TC, D_MODEL), lambda i, *_: (i, 0)),
            scratch_shapes=[pltpu.VMEM((2, TOP_K, TC, D_MODEL), _F32),
                            pltpu.SemaphoreType.DMA((2,))],
        ),
        out_shape=jax.ShapeDtypeStruct((N_TOK, D_MODEL), _F32),
        compiler_params=pltpu.CompilerParams(dimension_semantics=("arbitrary",),
                                             vmem_limit_bytes=VMEM_LIMIT),
        name="combine",
    )(eid, rank, row_start, h1, gates, ln_final_g, y_sorted)


def kernel(x, ln_mix_g, w_in, conv_w, sgu_ln_g, sgu_ln_b, sgu_w, sgu_b, gn_conv, gn_sgu, w_out,
           ln_ffn_g, w_router, b_router, w_gate_up, b_gate_up, w_down, b_down, ln_final_g):
    row = lambda v: v.reshape(1, -1)
    x2d = x.reshape(N_TOK, D_MODEL)

    sgu_b_full = jnp.repeat(jnp.transpose(sgu_b), SGU_HEAD_DIM, axis=1)
    wr_hi = w_router.astype(_BF16)
    wr_lo = (w_router - wr_hi.astype(_F32)).astype(_BF16)
    pad = ((0, 0), (0, LANES - N_EXPERTS))
    wr_split = jnp.concatenate([jnp.pad(wr_hi, pad), jnp.pad(wr_lo, pad)], axis=1)
    br_pad = jnp.pad(b_router, (0, LANES - N_EXPERTS)).reshape(1, LANES)

    h1, xn2, route, gates, counts = _mixer_router(
        x2d, row(ln_mix_g), w_in.astype(_BF16), conv_w, row(sgu_ln_g), row(sgu_ln_b), sgu_w,
        sgu_b_full, row(gn_conv), row(gn_sgu), w_out.astype(_BF16), row(ln_ffn_g), wr_split, br_pad)

    sizes = counts[0, :N_EXPERTS].astype(jnp.int32)
    n_sup = (sizes + SUPER_ROWS - 1) // SUPER_ROWS
    sup_end = jnp.cumsum(n_sup)
    sup_start = sup_end - n_sup
    total_sup = sup_end[-1]
    row_start = (sup_start * SUPER_ROWS).astype(jnp.int32)
    sid = jnp.arange(MAX_SUPER, dtype=jnp.int32)
    sid_eff = jnp.minimum(sid, total_sup - 1)
    sup_e = jnp.minimum(jnp.searchsorted(sup_end, sid_eff, side="right"),
                        N_EXPERTS - 1).astype(jnp.int32)
    rows_left = sizes[sup_e] - (sid_eff - sup_start[sup_e]) * SUPER_ROWS
    valid = jnp.clip(rows_left, 0, SUPER_ROWS)
    nchunk = jnp.where(sid < total_sup, (valid + ROW_CHUNK - 1) // ROW_CHUNK, 0).astype(jnp.int32)
    zflag = (sizes > 0).astype(jnp.int32)
    zrow = (row_start + ((jnp.maximum(sizes, 1) - 1) // ROW_CHUNK) * ROW_CHUNK).astype(jnp.int32)

    eid = route[:, 0:TOP_K].reshape(-1)
    rank = route[:, TOP_K:2 * TOP_K].reshape(-1)

    buf_x = _dispatch(eid, rank, row_start, zrow, zflag, xn2)
    y_sorted = _experts(sup_e, sid_eff.astype(jnp.int32), nchunk, buf_x, w_gate_up,
                        b_gate_up.reshape(N_EXPERTS, 1, 2 * D_FF), w_down,
                        b_down.reshape(N_EXPERTS, 1, D_MODEL))
    out = _combine(eid, rank, row_start, h1, gates, row(ln_final_g), y_sorted)
    return out.reshape(BATCH, SEQ, D_MODEL)
```

```python
import functools

import jax
import jax.numpy as jnp
from jax import lax
from jax.experimental import pallas as pl
from jax.experimental.pallas import tpu as pltpu

D_MODEL = 2048
BATCH = 2
SEQ = 4096
N_TOK = BATCH * SEQ

D_CONV = 1024
N_CONV_GROUPS = 8
CONV_WIDTH = 3
D_SGU = 1024
N_SGU_HEADS = 8
SGU_HEAD_DIM = 128
CHUNK = 128
D_IN_PROJ = 3 * D_CONV + 2 * D_SGU

N_EXPERTS = 32
TOP_K = 4
D_FF = 2048
SWIGLU_LIMIT = 7.0
SWIGLU_ALPHA = 1.702
RMS_EPS = 1e-5
LN_EPS = 1e-5

LANES = 128
SUBLANES = 8
VMEM_LIMIT = 56 * 1024 * 1024

TM = 256
SUPER_ROWS = 1280
ROW_CHUNK = 256
FF_TILE = 256
N_FF_TILES = D_FF // FF_TILE
MAX_SUPER = (N_TOK * TOP_K + N_EXPERTS * (SUPER_ROWS - 1)) // SUPER_ROWS
CHUNKS_PER_SUPER = SUPER_ROWS // ROW_CHUNK
assert CHUNKS_PER_SUPER + 1 <= N_FF_TILES
X_SLOTS = 2
TD = 256
TC = 256
DISPATCH_UNROLL = 8

_F32 = jnp.float32
_BF16 = jnp.bfloat16


def _dot(a, b):
    return jnp.dot(a, b, preferred_element_type=_F32)


def _gelu_exact(x):
    return 0.5 * x * (1.0 + lax.erf(x * (2.0 ** -0.5)))


def _rms_rows(x, gain):
    return x * lax.rsqrt(jnp.mean(x * x, axis=-1, keepdims=True) + RMS_EPS) * gain


def _mixer_router_kernel(x_ref, lng_ref, win_ref, convw_ref, slg_ref, slb_ref, sw_ref, sb_ref,
                         gnc_ref, gns_ref, wout_ref, lnf_ref, wr_ref, br_ref,
                         h1_ref, xn2_ref, route_ref, gates_ref, counts_ref,
                         cbuf, carry, ybuf):
    i = pl.program_id(0)

    @pl.when(i % (SEQ // TM) == 0)
    def _():
        cbuf[0:SUBLANES, :] = jnp.zeros((SUBLANES, D_CONV), _F32)

    @pl.when(i == 0)
    def _():
        carry[...] = jnp.zeros_like(carry)

    x = x_ref[...]
    xn = _rms_rows(x, lng_ref[...]).astype(_BF16)

    b_gate = _dot(xn, win_ref[:, 0:D_CONV])
    c_gate = _dot(xn, win_ref[:, D_CONV:2 * D_CONV])
    hh = _dot(xn, win_ref[:, 2 * D_CONV:3 * D_CONV])
    ch = c_gate * hh
    cbuf[SUBLANES:SUBLANES + TM, :] = ch
    ch1 = cbuf[SUBLANES - 1:SUBLANES - 1 + TM, :]
    ch2 = cbuf[SUBLANES - 2:SUBLANES - 2 + TM, :]
    conv = convw_ref[0:1, :] * ch2 + convw_ref[1:2, :] * ch1 + convw_ref[2:3, :] * ch
    cbuf[0:SUBLANES, :] = cbuf[TM:TM + SUBLANES, :]
    y_conv = b_gate * conv
    for g in range(N_CONV_GROUPS):
        sl = slice(g * LANES, (g + 1) * LANES)
        blk = y_conv[:, sl]
        ms = jnp.mean(blk * blk, axis=-1, keepdims=True)
        ybuf[:, sl] = (blk * lax.rsqrt(ms + RMS_EPS) * gnc_ref[:, sl]).astype(_BF16)

    gu = _gelu_exact(_dot(xn, win_ref[:, 3 * D_CONV:3 * D_CONV + D_SGU]))
    gv = _gelu_exact(_dot(xn, win_ref[:, 3 * D_CONV + D_SGU:D_IN_PROJ]))
    row_c = lax.broadcasted_iota(jnp.int32, (CHUNK, CHUNK), 0)
    col_c = lax.broadcasted_iota(jnp.int32, (CHUNK, CHUNK), 1)
    causal = row_c >= col_c
    for h in range(N_SGU_HEADS):
        sl = slice(h * SGU_HEAD_DIM, (h + 1) * SGU_HEAD_DIM)
        vh = gv[:, sl]
        mu = jnp.mean(vh, axis=-1, keepdims=True)
        xc = vh - mu
        var = jnp.mean(xc * xc, axis=-1, keepdims=True)
        vn = (xc * lax.rsqrt(var + LN_EPS) * slg_ref[:, sl] + slb_ref[:, sl]).astype(_BF16)
        wm = jnp.where(causal, sw_ref[h], 0.0).astype(_BF16)
        for c in range(TM // CHUNK):
            rows = slice(c * CHUNK, (c + 1) * CHUNK)
            mixed = _dot(wm, vn[rows, :]) + sb_ref[:, sl]
            ys = gu[rows, sl] * mixed
            ms = jnp.mean(ys * ys, axis=-1, keepdims=True)
            ybuf[rows, D_CONV + h * SGU_HEAD_DIM:D_CONV + (h + 1) * SGU_HEAD_DIM] = (
                ys * lax.rsqrt(ms + RMS_EPS) * gns_ref[:, sl]).astype(_BF16)

    h1 = x + _dot(ybuf[...], wout_ref[...])
    h1_ref[...] = h1
    xn2 = _rms_rows(h1, lnf_ref[...])
    xn2_ref[...] = xn2

    x_hi = xn2.astype(_BF16)
    x_lo = (xn2 - x_hi.astype(_F32)).astype(_BF16)
    p = _dot(x_hi, wr_ref[...]) + _dot(x_lo, wr_ref[...])
    logits = p[:, :LANES] + p[:, LANES:] + br_ref[...]
    lane = lax.broadcasted_iota(jnp.int32, (TM, LANES), 1)
    lane_f = lane.astype(_F32)
    neg_inf = jnp.float32(-jnp.inf)
    logits = jnp.where(lane < N_EXPERTS, logits, neg_inf)

    vals, ids, onehots = [], [], []
    cur = logits
    for _ in range(TOP_K):
        m = jnp.max(cur, axis=-1, keepdims=True)
        idx = jnp.min(jnp.where(cur == m, lane_f, float(LANES)), axis=-1, keepdims=True)
        oh = lane_f == idx
        vals.append(m)
        ids.append(idx)
        onehots.append(oh)
        cur = jnp.where(oh, neg_inf, cur)
    exps = [jnp.exp(v - vals[0]) for v in vals]
    denom = exps[0] + exps[1] + exps[2] + exps[3]
    gates = [e / denom for e in exps]

    mask = (onehots[0] | onehots[1] | onehots[2] | onehots[3]).astype(_F32)
    row_t = lax.broadcasted_iota(jnp.int32, (TM, TM), 0)
    col_t = lax.broadcasted_iota(jnp.int32, (TM, TM), 1)
    strict_lower = (row_t > col_t).astype(_BF16)
    before = _dot(strict_lower, mask.astype(_BF16)) + carry[...]
    ranks = [jnp.sum(jnp.where(oh, before, 0.0), axis=-1, keepdims=True) for oh in onehots]
    carry[...] = carry[...] + jnp.sum(mask, axis=0, keepdims=True)
    counts_ref[...] = jnp.broadcast_to(carry[...], counts_ref.shape)

    route = jnp.zeros((TM, LANES), _F32)
    gate_out = jnp.zeros((TM, LANES), _F32)
    for k in range(TOP_K):
        route = jnp.where(lane == k, ids[k], route)
        route = jnp.where(lane == TOP_K + k, ranks[k], route)
        gate_out = jnp.where(lane == k, gates[k], gate_out)
    route_ref[...] = route.astype(jnp.int32)
    gates_ref[...] = gate_out


def _mixer_router(x2d, ln_mix_g, w_in_bf, conv_w, sgu_ln_g, sgu_ln_b, sgu_w, sgu_b_full,
                  gn_conv, gn_sgu, w_out_bf, ln_ffn_g, wr_split, br_pad):
    def full(a):
        return pl.BlockSpec(a.shape, lambda i: (0,) * a.ndim)

    row_blk = lambda w: pl.BlockSpec((TM, w), lambda i: (i, 0))
    ins = [x2d, ln_mix_g, w_in_bf, conv_w, sgu_ln_g, sgu_ln_b, sgu_w, sgu_b_full,
           gn_conv, gn_sgu, w_out_bf, ln_ffn_g, wr_split, br_pad]
    in_specs = [row_blk(D_MODEL)] + [full(a) for a in ins[1:]]
    return pl.pallas_call(
        _mixer_router_kernel,
        grid=(N_TOK // TM,),
        in_specs=in_specs,
        out_specs=[row_blk(D_MODEL), row_blk(D_MODEL), row_blk(LANES), row_blk(LANES),
                   pl.BlockSpec((SUBLANES, LANES), lambda i: (0, 0))],
        out_shape=[jax.ShapeDtypeStruct((N_TOK, D_MODEL), _F32),
                   jax.ShapeDtypeStruct((N_TOK, D_MODEL), _F32),
                   jax.ShapeDtypeStruct((N_TOK, LANES), jnp.int32),
                   jax.ShapeDtypeStruct((N_TOK, LANES), _F32),
                   jax.ShapeDtypeStruct((SUBLANES, LANES), _F32)],
        scratch_shapes=[pltpu.VMEM((TM + SUBLANES, D_CONV), _F32),
                        pltpu.VMEM((1, LANES), _F32),
                        pltpu.VMEM((TM, D_MODEL), _BF16)],
        compiler_params=pltpu.CompilerParams(dimension_semantics=("arbitrary",),
                                             vmem_limit_bytes=VMEM_LIMIT),
        name="mixer_router",
    )(*ins)


def _dispatch_kernel(eid_ref, rank_ref, rstart_ref, zrow_ref, zflag_ref,
                     x_ref, bufx_hbm, zbuf, zsem, sem):
    i = pl.program_id(0)

    @pl.when(i == 0)
    def _():
        zbuf[...] = jnp.zeros_like(zbuf)

        def zero_copy(e):
            z0 = pl.multiple_of(zrow_ref[e], ROW_CHUNK)
            return pltpu.make_async_copy(zbuf, bufx_hbm.at[pl.ds(z0, ROW_CHUNK)], zsem)

        for e in range(N_EXPERTS):
            @pl.when(zflag_ref[e] > 0)
            def _(e=e):
                zero_copy(e).start()
        for e in range(N_EXPERTS):
            @pl.when(zflag_ref[e] > 0)
            def _(e=e):
                zero_copy(e).wait()

    def row_copy(t8, u, k):
        t = pl.multiple_of(t8 * SUBLANES, SUBLANES) + u
        a = (i * TD + t) * TOP_K + k
        dst = rstart_ref[eid_ref[a]] + rank_ref[a]
        return pltpu.make_async_copy(x_ref.at[pl.ds(t, 1)], bufx_hbm.at[pl.ds(dst, 1)], sem)

    def issue(t8, carry):
        for u in range(SUBLANES):
            for k in range(TOP_K):
                row_copy(t8, u, k).start()
        return carry

    def drain(t8, carry):
        for u in range(SUBLANES):
            for k in range(TOP_K):
                row_copy(t8, u, k).wait()
        return carry

    lax.fori_loop(0, TD // SUBLANES, issue, 0)
    lax.fori_loop(0, TD // SUBLANES, drain, 0)


def _dispatch(eid, rank, row_start, zrow, zflag, xn2):
    return pl.pallas_call(
        _dispatch_kernel,
        grid_spec=pltpu.PrefetchScalarGridSpec(
            num_scalar_prefetch=5,
            grid=(N_TOK // TD,),
            in_specs=[pl.BlockSpec((TD, D_MODEL), lambda i, *_: (i, 0))],
            out_specs=pl.BlockSpec(memory_space=pl.ANY),
            scratch_shapes=[pltpu.VMEM((ROW_CHUNK, D_MODEL), _F32),
                            pltpu.SemaphoreType.DMA(()),
                            pltpu.SemaphoreType.DMA(())],
        ),
        out_shape=jax.ShapeDtypeStruct((MAX_SUPER * SUPER_ROWS, D_MODEL), _F32),
        compiler_params=pltpu.CompilerParams(dimension_semantics=("arbitrary",),
                                             vmem_limit_bytes=VMEM_LIMIT),
        name="dispatch",
    )(eid, rank, row_start, zrow, zflag, xn2)


def _expert_kernel(sup_e_ref, sup_blk_ref, nchunk_ref,
                   x_hbm, wg_ref, wu_ref, bg_ref, bu_ref, wd_ref, bd_ref, out_hbm,
                   x_stage, x_bf, acc, x_sem, o_sem):
    s = pl.program_id(0)
    j = pl.program_id(1)
    n_chunks = nchunk_ref[s]
    cur = s % 2

    def x_copy(sup, c):
        r = pl.multiple_of(sup_blk_ref[sup] * SUPER_ROWS + c * ROW_CHUNK, ROW_CHUNK)
        slot = c % X_SLOTS
        return pltpu.make_async_copy(x_hbm.at[pl.ds(r, ROW_CHUNK)], x_stage.at[slot],
                                     x_sem.at[slot])

    def x_finish(sup, c):
        x_copy(sup, c).wait()
        rows = pl.ds(pl.multiple_of(c * ROW_CHUNK, ROW_CHUNK), ROW_CHUNK)
        x_bf[sup % 2, rows, :] = x_stage[c % X_SLOTS].astype(_BF16)

    def out_copy(sup, c):
        r = pl.multiple_of(sup_blk_ref[sup] * SUPER_ROWS + c * ROW_CHUNK, ROW_CHUNK)
        rows = pl.ds(c * ROW_CHUNK, ROW_CHUNK)
        return pltpu.make_async_copy(acc.at[sup % 2, rows], out_hbm.at[pl.ds(r, ROW_CHUNK)],
                                     o_sem.at[sup % 2])

    def for_each_chunk(sup, fn):
        for c in range(CHUNKS_PER_SUPER):
            @pl.when(c < nchunk_ref[sup])
            def _(c=c):
                fn(sup, c)

    @pl.when((s == 0) & (j == 0))
    def _():
        def load(sup, c):
            x_copy(sup, c).start()
            x_finish(sup, c)
        for_each_chunk(0, load)

    @pl.when((s >= 2) & (j == 0))
    def _():
        for_each_chunk(jnp.maximum(s - 2, 0), lambda sup, c: out_copy(sup, c).wait())

    nxt = jnp.minimum(s + 1, MAX_SUPER - 1)
    n_next = jnp.where(s + 1 < MAX_SUPER, nchunk_ref[nxt], 0)

    @pl.when((j >= 1) & (j - 1 < n_next))
    def _():
        x_finish(nxt, j - 1)

    @pl.when(j < n_next)
    def _():
        x_copy(nxt, j).start()

    @pl.when(n_chunks > 0)
    def _():
        def partial_out(c, n_rows):
            rows = pl.ds(pl.multiple_of(c * ROW_CHUNK, ROW_CHUNK), n_rows)
            xb = x_bf[cur, rows, :]
            gate = _dot(xb, wg_ref[0].astype(_BF16)) + bg_ref[0]
            up = _dot(xb, wu_ref[0].astype(_BF16)) + bu_ref[0]
            gate = jnp.minimum(gate, SWIGLU_LIMIT)
            up = jnp.clip(up, -SWIGLU_LIMIT, SWIGLU_LIMIT)
            glu = gate * jax.nn.sigmoid(gate * SWIGLU_ALPHA)
            act = ((up + 1.0) * glu).astype(_BF16)
            return rows, _dot(act, wd_ref[0].astype(_BF16))

        def sweep(update):
            def pair(p, carry):
                update(2 * p, 2 * ROW_CHUNK)
                return carry
            lax.fori_loop(0, n_chunks // 2, pair, 0)

            @pl.when(n_chunks % 2 == 1)
            def _():
                update(n_chunks - 1, ROW_CHUNK)

        def init(c, n_rows):
            rows, o = partial_out(c, n_rows)
            acc[cur, rows, :] = o + bd_ref[0]

        def accumulate(c, n_rows):
            rows, o = partial_out(c, n_rows)
            acc[cur, rows, :] += o

        @pl.when(j == 0)
        def _():
            sweep(init)

        @pl.when(j > 0)
        def _():
            sweep(accumulate)

    @pl.when(j == N_FF_TILES - 1)
    def _():
        for_each_chunk(s, lambda sup, c: out_copy(sup, c).start())

    @pl.when((s == MAX_SUPER - 1) & (j == N_FF_TILES - 1))
    def _():
        for_each_chunk(MAX_SUPER - 2, lambda sup, c: out_copy(sup, c).wait())
        for_each_chunk(MAX_SUPER - 1, lambda sup, c: out_copy(sup, c).wait())


def _experts(sup_e, sup_blk, nchunk, buf_x, w_gate_up, b_gate_up3, w_down, b_down3):
    def ff_tile(j, nchunk, s):
        return jnp.where(nchunk[s] > 0, j, N_FF_TILES - 1)

    x_spec = pl.BlockSpec(memory_space=pl.ANY)
    wg_spec = pl.BlockSpec((1, D_MODEL, FF_TILE),
                           lambda s, j, se, sb, nc: (se[s], 0, ff_tile(j, nc, s)))
    wu_spec = pl.BlockSpec((1, D_MODEL, FF_TILE),
                           lambda s, j, se, sb, nc: (se[s], 0, N_FF_TILES + ff_tile(j, nc, s)))
    bg_spec = pl.BlockSpec((1, 1, FF_TILE),
                           lambda s, j, se, sb, nc: (se[s], 0, ff_tile(j, nc, s)))
    bu_spec = pl.BlockSpec((1, 1, FF_TILE),
                           lambda s, j, se, sb, nc: (se[s], 0, N_FF_TILES + ff_tile(j, nc, s)))
    wd_spec = pl.BlockSpec((1, FF_TILE, D_MODEL),
                           lambda s, j, se, sb, nc: (se[s], ff_tile(j, nc, s), 0))
    bd_spec = pl.BlockSpec((1, 1, D_MODEL), lambda s, j, se, sb, nc: (se[s], 0, 0))
    out_spec = pl.BlockSpec(memory_space=pl.ANY)
    return pl.pallas_call(
        _expert_kernel,
        grid_spec=pltpu.PrefetchScalarGridSpec(
            num_scalar_prefetch=3,
            grid=(MAX_SUPER, N_FF_TILES),
            in_specs=[x_spec, wg_spec, wu_spec, bg_spec, bu_spec, wd_spec, bd_spec],
            out_specs=out_spec,
            scratch_shapes=[pltpu.VMEM((X_SLOTS, ROW_CHUNK, D_MODEL), _F32),
                            pltpu.VMEM((2, SUPER_ROWS, D_MODEL), _BF16),
                            pltpu.VMEM((2, SUPER_ROWS, D_MODEL), _F32),
                            pltpu.SemaphoreType.DMA((X_SLOTS,)),
                            pltpu.SemaphoreType.DMA((2,))],
        ),
        out_shape=jax.ShapeDtypeStruct((MAX_SUPER * SUPER_ROWS, D_MODEL), _F32),
        compiler_params=pltpu.CompilerParams(dimension_semantics=("arbitrary", "arbitrary"),
                                             vmem_limit_bytes=VMEM_LIMIT),
        name="experts",
    )(sup_e, sup_blk, nchunk, buf_x, w_gate_up, w_gate_up, b_gate_up3, b_gate_up3, w_down, b_down3)


def _combine_kernel(eid_ref, rank_ref, rstart_ref,
                    h1_ref, gates_ref, lng_ref, y_hbm, out_ref, rows, sem):
    i = pl.program_id(0)
    n_steps = pl.num_programs(0)

    def row_copy(tile, slot, t8, u, k):
        t = pl.multiple_of(t8 * SUBLANES, SUBLANES) + u
        a = (tile * TC + t) * TOP_K + k
        src = rstart_ref[eid_ref[a]] + rank_ref[a]
        return pltpu.make_async_copy(y_hbm.at[pl.ds(src, 1)], rows.at[slot, k, pl.ds(t, 1)],
                                     sem.at[slot])

    def issue_tile(tile, slot):
        def body(t8, carry):
            for u in range(SUBLANES):
                for k in range(TOP_K):
                    row_copy(tile, slot, t8, u, k).start()
            return carry
        lax.fori_loop(0, TC // SUBLANES, body, 0)

    def wait_tile(tile, slot):
        def body(t8, carry):
            for u in range(SUBLANES):
                for k in range(TOP_K):
                    row_copy(tile, slot, t8, u, k).wait()
            return carry
        lax.fori_loop(0, TC // SUBLANES, body, 0)

    slot = i % 2

    @pl.when(i == 0)
    def _():
        issue_tile(0, 0)

    @pl.when(i + 1 < n_steps)
    def _():
        issue_tile(i + 1, 1 - slot)

    wait_tile(i, slot)

    g = gates_ref[...]
    acc = h1_ref[...]
    for k in range(TOP_K):
        acc = acc + rows[slot, k] * g[:, k:k + 1]
    out_ref[...] = _rms_rows(acc, lng_ref[...])


def _combine(eid, rank, row_start, h1, gates, ln_final_g, y_sorted):
    return pl.pallas_call(
        _combine_kernel,
        grid_spec=pltpu.PrefetchScalarGridSpec(
            num_scalar_prefetch=3,
            grid=(N_TOK // TC,),
            in_specs=[pl.BlockSpec((TC, D_MODEL), lambda i, *_: (i, 0)),
                      pl.BlockSpec((TC, LANES), lambda i, *_: (i, 0)),
                      pl.BlockSpec((1, D_MODEL), lambda i, *_: (0, 0)),
                      pl.BlockSpec(memory_space=pl.ANY)],
            out_specs=pl.BlockSpec((TC, D_MODEL), lambda i, *_: (i, 0)),
            scratch_shapes=[pltpu.VMEM((2, TOP_K, TC, D_MODEL), _F32),
                            pltpu.SemaphoreType.DMA((2,))],
        ),
        out_shape=jax.ShapeDtypeStruct((N_TOK, D_MODEL), _F32),
        compiler_params=pltpu.CompilerParams(dimension_semantics=("arbitrary",),
                                             vmem_limit_bytes=VMEM_LIMIT),
        name="combine",
    )(eid, rank, row_start, h1, gates, ln_final_g, y_sorted)


def kernel(x, ln_mix_g, w_in, conv_w, sgu_ln_g, sgu_ln_b, sgu_w, sgu_b, gn_conv, gn_sgu, w_out,
           ln_ffn_g, w_router, b_router, w_gate_up, b_gate_up, w_down, b_down, ln_final_g):
    row = lambda v: v.reshape(1, -1)
    x2d = x.reshape(N_TOK, D_MODEL)

    sgu_b_full = jnp.repeat(jnp.transpose(sgu_b), SGU_HEAD_DIM, axis=1)
    wr_hi = w_router.astype(_BF16)
    wr_lo = (w_router - wr_hi.astype(_F32)).astype(_BF16)
    pad = ((0, 0), (0, LANES - N_EXPERTS))
    wr_split = jnp.concatenate([jnp.pad(wr_hi, pad), jnp.pad(wr_lo, pad)], axis=1)
    br_pad = jnp.pad(b_router, (0, LANES - N_EXPERTS)).reshape(1, LANES)

    h1, xn2, route, gates, counts = _mixer_router(
        x2d, row(ln_mix_g), w_in.astype(_BF16), conv_w, row(sgu_ln_g), row(sgu_ln_b), sgu_w,
        sgu_b_full, row(gn_conv), row(gn_sgu), w_out.astype(_BF16), row(ln_ffn_g), wr_split, br_pad)

    sizes = counts[0, :N_EXPERTS].astype(jnp.int32)
    n_sup = (sizes + SUPER_ROWS - 1) // SUPER_ROWS
    sup_end = jnp.cumsum(n_sup)
    sup_start = sup_end - n_sup
    total_sup = sup_end[-1]
    row_start = (sup_start * SUPER_ROWS).astype(jnp.int32)
    sid = jnp.arange(MAX_SUPER, dtype=jnp.int32)
    sid_eff = jnp.minimum(sid, total_sup - 1)
    sup_e = jnp.minimum(jnp.searchsorted(sup_end, sid_eff, side="right"),
                        N_EXPERTS - 1).astype(jnp.int32)
    rows_left = sizes[sup_e] - (sid_eff - sup_start[sup_e]) * SUPER_ROWS
    valid = jnp.clip(rows_left, 0, SUPER_ROWS)
    nchunk = jnp.where(sid < total_sup, (valid + ROW_CHUNK - 1) // ROW_CHUNK, 0).astype(jnp.int32)
    zflag = (sizes > 0).astype(jnp.int32)
    zrow = (row_start + ((jnp.maximum(sizes, 1) - 1) // ROW_CHUNK) * ROW_CHUNK).astype(jnp.int32)

    eid = route[:, 0:TOP_K].reshape(-1)
    rank = route[:, TOP_K:2 * TOP_K].reshape(-1)

    buf_x = _dispatch(eid, rank, row_start, zrow, zflag, xn2)
    y_sorted = _experts(sup_e, sid_eff.astype(jnp.int32), nchunk, buf_x, w_gate_up,
                        b_gate_up.reshape(N_EXPERTS, 1, 2 * D_FF), w_down,
                        b_down.reshape(N_EXPERTS, 1, D_MODEL))
    out = _combine(eid, rank, row_start, h1, gates, row(ln_final_g), y_sorted)
    return out.reshape(BATCH, SEQ, D_MODEL)
```

```python
import functools

import jax
import jax.numpy as jnp
from jax import lax
from jax.experimental import pallas as pl
from jax.experimental.pallas import tpu as pltpu

D_MODEL = 2048
BATCH = 2
SEQ = 4096
N_TOK = BATCH * SEQ

D_CONV = 1024
N_CONV_GROUPS = 8
CONV_WIDTH = 3
D_SGU = 1024
N_SGU_HEADS = 8
SGU_HEAD_DIM = 128
CHUNK = 128
D_IN_PROJ = 3 * D_CONV + 2 * D_SGU

N_EXPERTS = 32
TOP_K = 4
D_FF = 2048
SWIGLU_LIMIT = 7.0
SWIGLU_ALPHA = 1.702
RMS_EPS = 1e-5
LN_EPS = 1e-5

LANES = 128
SUBLANES = 8
VMEM_LIMIT = 56 * 1024 * 1024

TM = 256
SUPER_ROWS = 1280
ROW_CHUNK = 256
FF_TILE = 256
N_FF_TILES = D_FF // FF_TILE
MAX_SUPER = (N_TOK * TOP_K + N_EXPERTS * (SUPER_ROWS - 1)) // SUPER_ROWS
CHUNKS_PER_SUPER = SUPER_ROWS // ROW_CHUNK
assert CHUNKS_PER_SUPER + 1 <= N_FF_TILES
X_SLOTS = 2
TD = 256
TC = 256

_F32 = jnp.float32
_BF16 = jnp.bfloat16


def _dot(a, b):
    return jnp.dot(a, b, preferred_element_type=_F32)


def _gelu_exact(x):
    return 0.5 * x * (1.0 + lax.erf(x * (2.0 ** -0.5)))


def _rms_rows(x, gain):
    return x * lax.rsqrt(jnp.mean(x * x, axis=-1, keepdims=True) + RMS_EPS) * gain


def _mixer_router_kernel(x_ref, lng_ref, win_ref, convw_ref, slg_ref, slb_ref, sw_ref, sb_ref,
                         gnc_ref, gns_ref, wout_ref, lnf_ref, wr_ref, br_ref,
                         h1_ref, xn2_ref, route_ref, gates_ref, counts_ref,
                         cbuf, carry, ybuf):
    i = pl.program_id(0)

    @pl.when(i % (SEQ // TM) == 0)
    def _():
        cbuf[0:SUBLANES, :] = jnp.zeros((SUBLANES, D_CONV), _F32)

    @pl.when(i == 0)
    def _():
        carry[...] = jnp.zeros_like(carry)

    x = x_ref[...]
    xn = _rms_rows(x, lng_ref[...]).astype(_BF16)

    b_gate = _dot(xn, win_ref[:, 0:D_CONV])
    c_gate = _dot(xn, win_ref[:, D_CONV:2 * D_CONV])
    hh = _dot(xn, win_ref[:, 2 * D_CONV:3 * D_CONV])
    ch = c_gate * hh
    cbuf[SUBLANES:SUBLANES + TM, :] = ch
    ch1 = cbuf[SUBLANES - 1:SUBLANES - 1 + TM, :]
    ch2 = cbuf[SUBLANES - 2:SUBLANES - 2 + TM, :]
    conv = convw_ref[0:1, :] * ch2 + convw_ref[1:2, :] * ch1 + convw_ref[2:3, :] * ch
    cbuf[0:SUBLANES, :] = cbuf[TM:TM + SUBLANES, :]
    y_conv = b_gate * conv
    for g in range(N_CONV_GROUPS):
        sl = slice(g * LANES, (g + 1) * LANES)
        blk = y_conv[:, sl]
        ms = jnp.mean(blk * blk, axis=-1, keepdims=True)
        ybuf[:, sl] = (blk * lax.rsqrt(ms + RMS_EPS) * gnc_ref[:, sl]).astype(_BF16)

    gu = _gelu_exact(_dot(xn, win_ref[:, 3 * D_CONV:3 * D_CONV + D_SGU]))
    gv = _gelu_exact(_dot(xn, win_ref[:, 3 * D_CONV + D_SGU:D_IN_PROJ]))
    row_c = lax.broadcasted_iota(jnp.int32, (CHUNK, CHUNK), 0)
    col_c = lax.broadcasted_iota(jnp.int32, (CHUNK, CHUNK), 1)
    causal = row_c >= col_c
    for h in range(N_SGU_HEADS):
        sl = slice(h * SGU_HEAD_DIM, (h + 1) * SGU_HEAD_DIM)
        vh = gv[:, sl]
        mu = jnp.mean(vh, axis=-1, keepdims=True)
        xc = vh - mu
        var = jnp.mean(xc * xc, axis=-1, keepdims=True)
        vn = (xc * lax.rsqrt(var + LN_EPS) * slg_ref[:, sl] + slb_ref[:, sl]).astype(_BF16)
        wm = jnp.where(causal, sw_ref[h], 0.0).astype(_BF16)
        for c in range(TM // CHUNK):
            rows = slice(c * CHUNK, (c + 1) * CHUNK)
            mixed = _dot(wm, vn[rows, :]) + sb_ref[:, sl]
            ys = gu[rows, sl] * mixed
            ms = jnp.mean(ys * ys, axis=-1, keepdims=True)
            ybuf[rows, D_CONV + h * SGU_HEAD_DIM:D_CONV + (h + 1) * SGU_HEAD_DIM] = (
                ys * lax.rsqrt(ms + RMS_EPS) * gns_ref[:, sl]).astype(_BF16)

    h1 = x + _dot(ybuf[...], wout_ref[...])
    h1_ref[...] = h1
    xn2 = _rms_rows(h1, lnf_ref[...])
    xn2_ref[...] = xn2

    x_hi = xn2.astype(_BF16)
    x_lo = (xn2 - x_hi.astype(_F32)).astype(_BF16)
    p = _dot(x_hi, wr_ref[...]) + _dot(x_lo, wr_ref[...])
    logits = p[:, :LANES] + p[:, LANES:] + br_ref[...]
    lane = lax.broadcasted_iota(jnp.int32, (TM, LANES), 1)
    lane_f = lane.astype(_F32)
    neg_inf = jnp.float32(-jnp.inf)
    logits = jnp.where(lane < N_EXPERTS, logits, neg_inf)

    vals, ids, onehots = [], [], []
    cur = logits
    for _ in range(TOP_K):
        m = jnp.max(cur, axis=-1, keepdims=True)
        idx = jnp.min(jnp.where(cur == m, lane_f, float(LANES)), axis=-1, keepdims=True)
        oh = lane_f == idx
        vals.append(m)
        ids.append(idx)
        onehots.append(oh)
        cur = jnp.where(oh, neg_inf, cur)
    exps = [jnp.exp(v - vals[0]) for v in vals]
    denom = exps[0] + exps[1] + exps[2] + exps[3]
    gates = [e / denom for e in exps]

    mask = (onehots[0] | onehots[1] | onehots[2] | onehots[3]).astype(_F32)
    row_t = lax.broadcasted_iota(jnp.int32, (TM, TM), 0)
    col_t = lax.broadcasted_iota(jnp.int32, (TM, TM), 1)
    strict_lower = (row_t > col_t).astype(_BF16)
    before = _dot(strict_lower, mask.astype(_BF16)) + carry[...]
    ranks = [jnp.sum(jnp.where(oh, before, 0.0), axis=-1, keepdims=True) for oh in onehots]
    carry[...] = carry[...] + jnp.sum(mask, axis=0, keepdims=True)
    counts_ref[...] = jnp.broadcast_to(carry[...], counts_ref.shape)

    route = jnp.zeros((TM, LANES), _F32)
    gate_out = jnp.zeros((TM, LANES), _F32)
    for k in range(TOP_K):
        route = jnp.where(lane == k, ids[k], route)
        route = jnp.where(lane == TOP_K + k, ranks[k], route)
        gate_out = jnp.where(lane == k, gates[k], gate_out)
    route_ref[...] = route.astype(jnp.int32)
    gates_ref[...] = gate_out


def _mixer_router(x2d, ln_mix_g, w_in_bf, conv_w, sgu_ln_g, sgu_ln_b, sgu_w, sgu_b_full,
                  gn_conv, gn_sgu, w_out_bf, ln_ffn_g, wr_split, br_pad):
    def full(a):
        return pl.BlockSpec(a.shape, lambda i: (0,) * a.ndim)

    row_blk = lambda w: pl.BlockSpec((TM, w), lambda i: (i, 0))
    ins = [x2d, ln_mix_g, w_in_bf, conv_w, sgu_ln_g, sgu_ln_b, sgu_w, sgu_b_full,
           gn_conv, gn_sgu, w_out_bf, ln_ffn_g, wr_split, br_pad]
    in_specs = [row_blk(D_MODEL)] + [full(a) for a in ins[1:]]
    return pl.pallas_call(
        _mixer_router_kernel,
        grid=(N_TOK // TM,),
        in_specs=in_specs,
        out_specs=[row_blk(D_MODEL), row_blk(D_MODEL), row_blk(LANES), row_blk(LANES),
                   pl.BlockSpec((SUBLANES, LANES), lambda i: (0, 0))],
        out_shape=[jax.ShapeDtypeStruct((N_TOK, D_MODEL), _F32),
                   jax.ShapeDtypeStruct((N_TOK, D_MODEL), _F32),
                   jax.ShapeDtypeStruct((N_TOK, LANES), jnp.int32),
                   jax.ShapeDtypeStruct((N_TOK, LANES), _F32),
                   jax.ShapeDtypeStruct((SUBLANES, LANES), _F32)],
        scratch_shapes=[pltpu.VMEM((TM + SUBLANES, D_CONV), _F32),
                        pltpu.VMEM((1, LANES), _F32),
                        pltpu.VMEM((TM, D_MODEL), _BF16)],
        compiler_params=pltpu.CompilerParams(dimension_semantics=("arbitrary",),
                                             vmem_limit_bytes=VMEM_LIMIT),
        name="mixer_router",
    )(*ins)


def _dispatch_kernel(dest_ref, zrow_ref, zflag_ref,
                     x_ref, bufx_hbm, zbuf, zsem, sem):
    i = pl.program_id(0)

    @pl.when(i == 0)
    def _():
        zbuf[...] = jnp.zeros_like(zbuf)

        def zero_copy(e):
            z0 = pl.multiple_of(zrow_ref[e], ROW_CHUNK)
            return pltpu.make_async_copy(zbuf, bufx_hbm.at[pl.ds(z0, ROW_CHUNK)], zsem)

        for e in range(N_EXPERTS):
            @pl.when(zflag_ref[e] > 0)
            def _(e=e):
                zero_copy(e).start()
        for e in range(N_EXPERTS):
            @pl.when(zflag_ref[e] > 0)
            def _(e=e):
                zero_copy(e).wait()

    def row_copy(t8, u, k):
        t = pl.multiple_of(t8 * SUBLANES, SUBLANES) + u
        dst = dest_ref[(i * TD + t) * TOP_K + k]
        return pltpu.make_async_copy(x_ref.at[pl.ds(t, 1)], bufx_hbm.at[pl.ds(dst, 1)], sem)

    def issue(t8, carry):
        for u in range(SUBLANES):
            for k in range(TOP_K):
                row_copy(t8, u, k).start()
        return carry

    def drain(t8, carry):
        for u in range(SUBLANES):
            for k in range(TOP_K):
                row_copy(t8, u, k).wait()
        return carry

    lax.fori_loop(0, TD // SUBLANES, issue, 0)
    lax.fori_loop(0, TD // SUBLANES, drain, 0)


def _dispatch(dest, zrow, zflag, xn2):
    return pl.pallas_call(
        _dispatch_kernel,
        grid_spec=pltpu.PrefetchScalarGridSpec(
            num_scalar_prefetch=3,
            grid=(N_TOK // TD,),
            in_specs=[pl.BlockSpec((TD, D_MODEL), lambda i, *_: (i, 0))],
            out_specs=pl.BlockSpec(memory_space=pl.ANY),
            scratch_shapes=[pltpu.VMEM((ROW_CHUNK, D_MODEL), _F32),
                            pltpu.SemaphoreType.DMA(()),
                            pltpu.SemaphoreType.DMA(())],
        ),
        out_shape=jax.ShapeDtypeStruct((MAX_SUPER * SUPER_ROWS, D_MODEL), _F32),
        compiler_params=pltpu.CompilerParams(dimension_semantics=("arbitrary",),
                                             vmem_limit_bytes=VMEM_LIMIT),
        name="dispatch",
    )(dest, zrow, zflag, xn2)


def _expert_kernel(sup_e_ref, sup_blk_ref, nchunk_ref,
                   x_hbm, wg_ref, wu_ref, bg_ref, bu_ref, wd_ref, bd_ref, out_hbm,
                   x_stage, x_bf, acc, x_sem, o_sem, *, max_super):
    s = pl.program_id(0)
    j = pl.program_id(1)
    n_chunks = nchunk_ref[s]
    cur = s % 2

    def x_copy(sup, c):
        r = pl.multiple_of(sup_blk_ref[sup] * SUPER_ROWS + c * ROW_CHUNK, ROW_CHUNK)
        slot = c % X_SLOTS
        return pltpu.make_async_copy(x_hbm.at[pl.ds(r, ROW_CHUNK)], x_stage.at[slot],
                                     x_sem.at[slot])

    def x_finish(sup, c):
        x_copy(sup, c).wait()
        rows = pl.ds(pl.multiple_of(c * ROW_CHUNK, ROW_CHUNK), ROW_CHUNK)
        x_bf[sup % 2, rows, :] = x_stage[c % X_SLOTS].astype(_BF16)

    def out_copy(sup, c):
        r = pl.multiple_of(sup_blk_ref[sup] * SUPER_ROWS + c * ROW_CHUNK, ROW_CHUNK)
        rows = pl.ds(pl.multiple_of(c * ROW_CHUNK, ROW_CHUNK), ROW_CHUNK)
        return pltpu.make_async_copy(acc.at[sup % 2, rows], out_hbm.at[pl.ds(r, ROW_CHUNK)],
                                     o_sem.at[sup % 2])

    def for_each_chunk(sup, fn):
        for c in range(CHUNKS_PER_SUPER):
            @pl.when(c < nchunk_ref[sup])
            def _(c=c):
                fn(sup, c)

    @pl.when((s == 0) & (j == 0))
    def _():
        def load(sup, c):
            x_copy(sup, c).start()
            x_finish(sup, c)
        for_each_chunk(0, load)

    @pl.when((s >= 2) & (j == 0))
    def _():
        for_each_chunk(jnp.maximum(s - 2, 0), lambda sup, c: out_copy(sup, c).wait())

    prev = jnp.maximum(s - 1, 0)

    @pl.when((s >= 1) & (j < nchunk_ref[prev]))
    def _():
        out_copy(prev, j).start()

    nxt = jnp.minimum(s + 1, max_super - 1)
    n_next = jnp.where(s + 1 < max_super, nchunk_ref[nxt], 0)

    @pl.when((j >= 1) & (j - 1 < n_next))
    def _():
        x_finish(nxt, j - 1)

    @pl.when(j < n_next)
    def _():
        x_copy(nxt, j).start()

    @pl.when(n_chunks > 0)
    def _():
        def partial_out(c, n_rows):
            rows = pl.ds(pl.multiple_of(c * ROW_CHUNK, ROW_CHUNK), n_rows)
            xb = x_bf[cur, rows, :]
            gate = _dot(xb, wg_ref[0].astype(_BF16)) + bg_ref[0]
            up = _dot(xb, wu_ref[0].astype(_BF16)) + bu_ref[0]
            gate = jnp.minimum(gate, SWIGLU_LIMIT)
            up = jnp.clip(up, -SWIGLU_LIMIT, SWIGLU_LIMIT)
            glu = gate * jax.nn.sigmoid(gate * SWIGLU_ALPHA)
            act = ((up + 1.0) * glu).astype(_BF16)
            return rows, _dot(act, wd_ref[0].astype(_BF16))

        def sweep(update):
            def pair(p, carry):
                update(2 * p, 2 * ROW_CHUNK)
                return carry
            lax.fori_loop(0, n_chunks // 2, pair, 0)

            @pl.when(n_chunks % 2 == 1)
            def _():
                update(n_chunks - 1, ROW_CHUNK)

        def init(c, n_rows):
            rows, o = partial_out(c, n_rows)
            acc[cur, rows, :] = o + bd_ref[0]

        def accumulate(c, n_rows):
            rows, o = partial_out(c, n_rows)
            acc[cur, rows, :] += o

        @pl.when(j == 0)
        def _():
            sweep(init)

        @pl.when(j > 0)
        def _():
            sweep(accumulate)

    @pl.when((s == max_super - 1) & (j == N_FF_TILES - 1))
    def _():
        for_each_chunk(max_super - 1, lambda sup, c: out_copy(sup, c).start())
        for_each_chunk(max_super - 2, lambda sup, c: out_copy(sup, c).wait())
        for_each_chunk(max_super - 1, lambda sup, c: out_copy(sup, c).wait())


def _experts(max_super, sup_e, sup_blk, nchunk, buf_x, w_gate_up, b_gate_up3, w_down, b_down3):
    def ff_tile(j, nchunk, s):
        return jnp.where(nchunk[s] > 0, j, N_FF_TILES - 1)

    x_spec = pl.BlockSpec(memory_space=pl.ANY)
    wg_spec = pl.BlockSpec((1, D_MODEL, FF_TILE),
                           lambda s, j, se, sb, nc: (se[s], 0, ff_tile(j, nc, s)))
    wu_spec = pl.BlockSpec((1, D_MODEL, FF_TILE),
                           lambda s, j, se, sb, nc: (se[s], 0, N_FF_TILES + ff_tile(j, nc, s)))
    bg_spec = pl.BlockSpec((1, 1, FF_TILE),
                           lambda s, j, se, sb, nc: (se[s], 0, ff_tile(j, nc, s)))
    bu_spec = pl.BlockSpec((1, 1, FF_TILE),
                           lambda s, j, se, sb, nc: (se[s], 0, N_FF_TILES + ff_tile(j, nc, s)))
    wd_spec = pl.BlockSpec((1, FF_TILE, D_MODEL),
                           lambda s, j, se, sb, nc: (se[s], ff_tile(j, nc, s), 0))
    bd_spec = pl.BlockSpec((1, 1, D_MODEL), lambda s, j, se, sb, nc: (se[s], 0, 0))
    out_spec = pl.BlockSpec(memory_space=pl.ANY)
    return pl.pallas_call(
        functools.partial(_expert_kernel, max_super=max_super),
        grid_spec=pltpu.PrefetchScalarGridSpec(
            num_scalar_prefetch=3,
            grid=(max_super, N_FF_TILES),
            in_specs=[x_spec, wg_spec, wu_spec, bg_spec, bu_spec, wd_spec, bd_spec],
            out_specs=out_spec,
            scratch_shapes=[pltpu.VMEM((X_SLOTS, ROW_CHUNK, D_MODEL), _F32),
                            pltpu.VMEM((2, SUPER_ROWS, D_MODEL), _BF16),
                            pltpu.VMEM((2, SUPER_ROWS, D_MODEL), _F32),
                            pltpu.SemaphoreType.DMA((X_SLOTS,)),
                            pltpu.SemaphoreType.DMA((2,))],
        ),
        out_shape=jax.ShapeDtypeStruct((MAX_SUPER * SUPER_ROWS, D_MODEL), _F32),
        compiler_params=pltpu.CompilerParams(dimension_semantics=("arbitrary", "arbitrary"),
                                             vmem_limit_bytes=VMEM_LIMIT),
        name="experts",
    )(sup_e, sup_blk, nchunk, buf_x, w_gate_up, w_gate_up, b_gate_up3, b_gate_up3, w_down, b_down3)


def _combine_kernel(dest_ref, h1_ref, gates_ref, lng_ref, y_hbm, out_ref, rows, sem):
    i = pl.program_id(0)
    n_steps = pl.num_programs(0)

    def row_copy(tile, slot, t8, u, k):
        t = pl.multiple_of(t8 * SUBLANES, SUBLANES) + u
        src = dest_ref[(tile * TC + t) * TOP_K + k]
        return pltpu.make_async_copy(y_hbm.at[pl.ds(src, 1)], rows.at[slot, k, pl.ds(t, 1)],
                                     sem.at[slot])

    def issue_tile(tile, slot):
        def body(t8, carry):
            for u in range(SUBLANES):
                for k in range(TOP_K):
                    row_copy(tile, slot, t8, u, k).start()
            return carry
        lax.fori_loop(0, TC // SUBLANES, body, 0)

    def wait_tile(tile, slot):
        def body(t8, carry):
            for u in range(SUBLANES):
                for k in range(TOP_K):
                    row_copy(tile, slot, t8, u, k).wait()
            return carry
        lax.fori_loop(0, TC // SUBLANES, body, 0)

    slot = i % 2

    @pl.when(i == 0)
    def _():
        issue_tile(0, 0)

    @pl.when(i + 1 < n_steps)
    def _():
        issue_tile(i + 1, 1 - slot)

    wait_tile(i, slot)

    g = gates_ref[...]
    acc = h1_ref[...]
    for k in range(TOP_K):
        acc = acc + rows[slot, k] * g[:, k:k + 1]
    out_ref[...] = _rms_rows(acc, lng_ref[...])


def _combine(dest, h1, gates, ln_final_g, y_sorted):
    return pl.pallas_call(
        _combine_kernel,
        grid_spec=pltpu.PrefetchScalarGridSpec(
            num_scalar_prefetch=1,
            grid=(N_TOK // TC,),
            in_specs=[pl.BlockSpec((TC, D_MODEL), lambda i, *_: (i, 0)),
                      pl.BlockSpec((TC, LANES), lambda i, *_: (i, 0)),
                      pl.BlockSpec((1, D_MODEL), lambda i, *_: (0, 0)),
                      pl.BlockSpec(memory_space=pl.ANY)],
            out_specs=pl.BlockSpec((TC, D_MODEL), lambda i, *_: (i, 0)),
            scratch_shapes=[pltpu.VMEM((2, TOP_K, TC, D_MODEL), _F32),
                            pltpu.SemaphoreType.DMA((2,))],
        ),
        out_shape=jax.ShapeDtypeStruct((N_TOK, D_MODEL), _F32),
        compiler_params=pltpu.CompilerParams(dimension_semantics=("arbitrary",),
                                             vmem_limit_bytes=VMEM_LIMIT),
        name="combine",
    )(dest, h1, gates, ln_final_g, y_sorted)


def kernel(x, ln_mix_g, w_in, conv_w, sgu_ln_g, sgu_ln_b, sgu_w, sgu_b, gn_conv, gn_sgu, w_out,
           ln_ffn_g, w_router, b_router, w_gate_up, b_gate_up, w_down, b_down, ln_final_g):
    row = lambda v: v.reshape(1, -1)
    x2d = x.reshape(N_TOK, D_MODEL)

    sgu_b_full = jnp.repeat(jnp.transpose(sgu_b), SGU_HEAD_DIM, axis=1)
    wr_hi = w_router.astype(_BF16)
    wr_lo = (w_router - wr_hi.astype(_F32)).astype(_BF16)
    pad = ((0, 0), (0, LANES - N_EXPERTS))
    wr_split = jnp.concatenate([jnp.pad(wr_hi, pad), jnp.pad(wr_lo, pad)], axis=1)
    br_pad = jnp.pad(b_router, (0, LANES - N_EXPERTS)).reshape(1, LANES)

    h1, xn2, route, gates, counts = _mixer_router(
        x2d, row(ln_mix_g), w_in.astype(_BF16), conv_w, row(sgu_ln_g), row(sgu_ln_b), sgu_w,
        sgu_b_full, row(gn_conv), row(gn_sgu), w_out.astype(_BF16), row(ln_ffn_g), wr_split, br_pad)

    sizes = counts[0, :N_EXPERTS].astype(jnp.int32)
    n_sup = (sizes + SUPER_ROWS - 1) // SUPER_ROWS
    sup_end = jnp.cumsum(n_sup)
    sup_start = sup_end - n_sup
    total_sup = sup_end[-1]
    row_start = (sup_start * SUPER_ROWS).astype(jnp.int32)
    sid = jnp.arange(MAX_SUPER, dtype=jnp.int32)
    sid_eff = jnp.minimum(sid, total_sup - 1)
    sup_e = jnp.minimum(jnp.searchsorted(sup_end, sid_eff, side="right"),
                        N_EXPERTS - 1).astype(jnp.int32)
    rows_left = sizes[sup_e] - (sid_eff - sup_start[sup_e]) * SUPER_ROWS
    valid = jnp.clip(rows_left, 0, SUPER_ROWS)
    nchunk = jnp.where(sid < total_sup, (valid + ROW_CHUNK - 1) // ROW_CHUNK, 0).astype(jnp.int32)
    zflag = (sizes > 0).astype(jnp.int32)
    zrow = (row_start + ((jnp.maximum(sizes, 1) - 1) // ROW_CHUNK) * ROW_CHUNK).astype(jnp.int32)
    dest = (row_start[route[:, 0:TOP_K]] + route[:, TOP_K:2 * TOP_K]).reshape(-1)

    buf_x = _dispatch(dest, zrow, zflag, xn2)
    expert_args = (sup_e, sid_eff.astype(jnp.int32), nchunk, buf_x, w_gate_up,
                   b_gate_up.reshape(N_EXPERTS, 1, 2 * D_FF), w_down,
                   b_down.reshape(N_EXPERTS, 1, D_MODEL))
    y_sorted = lax.cond(total_sup <= N_EXPERTS,
                        functools.partial(_experts, N_EXPERTS),
                        functools.partial(_experts, MAX_SUPER), *expert_args)
    out = _combine(dest, h1, gates, row(ln_final_g), y_sorted)
    return out.reshape(BATCH, SEQ, D_MODEL)
```

```python
import functools

import jax
import jax.numpy as jnp
from jax import lax
from jax.experimental import pallas as pl
from jax.experimental.pallas import tpu as pltpu

D_MODEL = 2048
BATCH = 2
SEQ = 4096
N_TOK = BATCH * SEQ

D_CONV = 1024
N_CONV_GROUPS = 8
CONV_WIDTH = 3
D_SGU = 1024
N_SGU_HEADS = 8
SGU_HEAD_DIM = 128
CHUNK = 128
D_IN_PROJ = 3 * D_CONV + 2 * D_SGU

N_EXPERTS = 32
TOP_K = 4
D_FF = 2048
SWIGLU_LIMIT = 7.0
SWIGLU_ALPHA = 1.702
RMS_EPS = 1e-5
LN_EPS = 1e-5

LANES = 128
SUBLANES = 8
VMEM_LIMIT = 56 * 1024 * 1024

TM = 256
SUPER_ROWS = 1280
ROW_CHUNK = 256
FF_TILE = 256
N_FF_TILES = D_FF // FF_TILE
MAX_SUPER = (N_TOK * TOP_K + N_EXPERTS * (SUPER_ROWS - 1)) // SUPER_ROWS
CHUNKS_PER_SUPER = SUPER_ROWS // ROW_CHUNK
assert CHUNKS_PER_SUPER + 1 <= N_FF_TILES
X_SLOTS = 2
TD = 256
TC = 256

_F32 = jnp.float32
_BF16 = jnp.bfloat16


def _dot(a, b):
    return jnp.dot(a, b, preferred_element_type=_F32)


def _gelu_exact(x):
    return 0.5 * x * (1.0 + lax.erf(x * (2.0 ** -0.5)))


def _rms_rows(x, gain):
    return x * lax.rsqrt(jnp.mean(x * x, axis=-1, keepdims=True) + RMS_EPS) * gain


def _mixer_router_kernel(x_ref, lng_ref, win_ref, convw_ref, slg_ref, slb_ref, sw_ref, sb_ref,
                         gnc_ref, gns_ref, wout_ref, lnf_ref, wr_ref, br_ref,
                         h1_ref, xn2_ref, route_ref, gates_ref, counts_ref,
                         cbuf, carry, ybuf):
    i = pl.program_id(0)

    @pl.when(i % (SEQ // TM) == 0)
    def _():
        cbuf[0:SUBLANES, :] = jnp.zeros((SUBLANES, D_CONV), _F32)

    @pl.when(i == 0)
    def _():
        carry[...] = jnp.zeros_like(carry)

    x = x_ref[...]
    xn = _rms_rows(x, lng_ref[...]).astype(_BF16)

    b_gate = _dot(xn, win_ref[:, 0:D_CONV])
    c_gate = _dot(xn, win_ref[:, D_CONV:2 * D_CONV])
    hh = _dot(xn, win_ref[:, 2 * D_CONV:3 * D_CONV])
    ch = c_gate * hh
    cbuf[SUBLANES:SUBLANES + TM, :] = ch
    ch1 = cbuf[SUBLANES - 1:SUBLANES - 1 + TM, :]
    ch2 = cbuf[SUBLANES - 2:SUBLANES - 2 + TM, :]
    conv = convw_ref[0:1, :] * ch2 + convw_ref[1:2, :] * ch1 + convw_ref[2:3, :] * ch
    cbuf[0:SUBLANES, :] = cbuf[TM:TM + SUBLANES, :]
    y_conv = b_gate * conv
    for g in range(N_CONV_GROUPS):
        sl = slice(g * LANES, (g + 1) * LANES)
        blk = y_conv[:, sl]
        ms = jnp.mean(blk * blk, axis=-1, keepdims=True)
        ybuf[:, sl] = (blk * lax.rsqrt(ms + RMS_EPS) * gnc_ref[:, sl]).astype(_BF16)

    gu = _gelu_exact(_dot(xn, win_ref[:, 3 * D_CONV:3 * D_CONV + D_SGU]))
    gv = _gelu_exact(_dot(xn, win_ref[:, 3 * D_CONV + D_SGU:D_IN_PROJ]))
    row_c = lax.broadcasted_iota(jnp.int32, (CHUNK, CHUNK), 0)
    col_c = lax.broadcasted_iota(jnp.int32, (CHUNK, CHUNK), 1)
    causal = row_c >= col_c
    for h in range(N_SGU_HEADS):
        sl = slice(h * SGU_HEAD_DIM, (h + 1) * SGU_HEAD_DIM)
        vh = gv[:, sl]
        mu = jnp.mean(vh, axis=-1, keepdims=True)
        xc = vh - mu
        var = jnp.mean(xc * xc, axis=-1, keepdims=True)
        vn = (xc * lax.rsqrt(var + LN_EPS) * slg_ref[:, sl] + slb_ref[:, sl]).astype(_BF16)
        wm = jnp.where(causal, sw_ref[h], 0.0).astype(_BF16)
        for c in range(TM // CHUNK):
            rows = slice(c * CHUNK, (c + 1) * CHUNK)
            mixed = _dot(wm, vn[rows, :]) + sb_ref[:, sl]
            ys = gu[rows, sl] * mixed
            ms = jnp.mean(ys * ys, axis=-1, keepdims=True)
            ybuf[rows, D_CONV + h * SGU_HEAD_DIM:D_CONV + (h + 1) * SGU_HEAD_DIM] = (
                ys * lax.rsqrt(ms + RMS_EPS) * gns_ref[:, sl]).astype(_BF16)

    h1 = x + _dot(ybuf[...], wout_ref[...])
    h1_ref[...] = h1
    xn2 = _rms_rows(h1, lnf_ref[...])
    xn2_ref[...] = xn2

    x_hi = xn2.astype(_BF16)
    x_lo = (xn2 - x_hi.astype(_F32)).astype(_BF16)
    p = _dot(x_hi, wr_ref[...]) + _dot(x_lo, wr_ref[...])
    logits = p[:, :LANES] + p[:, LANES:] + br_ref[...]
    lane = lax.broadcasted_iota(jnp.int32, (TM, LANES), 1)
    lane_f = lane.astype(_F32)
    neg_inf = jnp.float32(-jnp.inf)
    logits = jnp.where(lane < N_EXPERTS, logits, neg_inf)

    vals, ids, onehots = [], [], []
    cur = logits
    for _ in range(TOP_K):
        m = jnp.max(cur, axis=-1, keepdims=True)
        idx = jnp.min(jnp.where(cur == m, lane_f, float(LANES)), axis=-1, keepdims=True)
        oh = lane_f == idx
        vals.append(m)
        ids.append(idx)
        onehots.append(oh)
        cur = jnp.where(oh, neg_inf, cur)
    exps = [jnp.exp(v - vals[0]) for v in vals]
    denom = exps[0] + exps[1] + exps[2] + exps[3]
    gates = [e / denom for e in exps]

    mask = (onehots[0] | onehots[1] | onehots[2] | onehots[3]).astype(_F32)
    row_t = lax.broadcasted_iota(jnp.int32, (TM, TM), 0)
    col_t = lax.broadcasted_iota(jnp.int32, (TM, TM), 1)
    strict_lower = (row_t > col_t).astype(_BF16)
    before = _dot(strict_lower, mask.astype(_BF16)) + carry[...]
    ranks = [jnp.sum(jnp.where(oh, before, 0.0), axis=-1, keepdims=True) for oh in onehots]
    carry[...] = carry[...] + jnp.sum(mask, axis=0, keepdims=True)
    counts_ref[...] = jnp.broadcast_to(carry[...], counts_ref.shape)

    route = jnp.zeros((TM, LANES), _F32)
    gate_out = jnp.zeros((TM, LANES), _F32)
    for k in range(TOP_K):
        route = jnp.where(lane == k, ids[k], route)
        route = jnp.where(lane == TOP_K + k, ranks[k], route)
        gate_out = jnp.where(lane == k, gates[k], gate_out)
    route_ref[...] = route.astype(jnp.int32)
    gates_ref[...] = gate_out


def _mixer_router(x2d, ln_mix_g, w_in_bf, conv_w, sgu_ln_g, sgu_ln_b, sgu_w, sgu_b_full,
                  gn_conv, gn_sgu, w_out_bf, ln_ffn_g, wr_split, br_pad):
    def full(a):
        return pl.BlockSpec(a.shape, lambda i: (0,) * a.ndim)

    row_blk = lambda w: pl.BlockSpec((TM, w), lambda i: (i, 0))
    ins = [x2d, ln_mix_g, w_in_bf, conv_w, sgu_ln_g, sgu_ln_b, sgu_w, sgu_b_full,
           gn_conv, gn_sgu, w_out_bf, ln_ffn_g, wr_split, br_pad]
    in_specs = [row_blk(D_MODEL)] + [full(a) for a in ins[1:]]
    return pl.pallas_call(
        _mixer_router_kernel,
        grid=(N_TOK // TM,),
        in_specs=in_specs,
        out_specs=[row_blk(D_MODEL), row_blk(D_MODEL), row_blk(LANES), row_blk(LANES),
                   pl.BlockSpec((SUBLANES, LANES), lambda i: (0, 0))],
        out_shape=[jax.ShapeDtypeStruct((N_TOK, D_MODEL), _F32),
                   jax.ShapeDtypeStruct((N_TOK, D_MODEL), _F32),
                   jax.ShapeDtypeStruct((N_TOK, LANES), jnp.int32),
                   jax.ShapeDtypeStruct((N_TOK, LANES), _F32),
                   jax.ShapeDtypeStruct((SUBLANES, LANES), _F32)],
        scratch_shapes=[pltpu.VMEM((TM + SUBLANES, D_CONV), _F32),
                        pltpu.VMEM((1, LANES), _F32),
                        pltpu.VMEM((TM, D_MODEL), _BF16)],
        compiler_params=pltpu.CompilerParams(dimension_semantics=("arbitrary",),
                                             vmem_limit_bytes=VMEM_LIMIT),
        name="mixer_router",
    )(*ins)


def _dispatch_kernel(dest_ref, zrow_ref, zflag_ref,
                     x_ref, bufx_hbm, zbuf, zsem, sem):
    i = pl.program_id(0)

    @pl.when(i == 0)
    def _():
        zbuf[...] = jnp.zeros_like(zbuf)

        def zero_copy(e):
            z0 = pl.multiple_of(zrow_ref[e], ROW_CHUNK)
            return pltpu.make_async_copy(zbuf, bufx_hbm.at[pl.ds(z0, ROW_CHUNK)], zsem)

        for e in range(N_EXPERTS):
            @pl.when(zflag_ref[e] > 0)
            def _(e=e):
                zero_copy(e).start()
        for e in range(N_EXPERTS):
            @pl.when(zflag_ref[e] > 0)
            def _(e=e):
                zero_copy(e).wait()

    def row_copy(t8, u, k):
        t = pl.multiple_of(t8 * SUBLANES, SUBLANES) + u
        dst = dest_ref[(i * TD + t) * TOP_K + k]
        return pltpu.make_async_copy(x_ref.at[pl.ds(t, 1)], bufx_hbm.at[pl.ds(dst, 1)], sem)

    def issue(t8, carry):
        for u in range(SUBLANES):
            for k in range(TOP_K):
                row_copy(t8, u, k).start()
        return carry

    def drain(t8, carry):
        for u in range(SUBLANES):
            for k in range(TOP_K):
                row_copy(t8, u, k).wait()
        return carry

    lax.fori_loop(0, TD // SUBLANES, issue, 0)
    lax.fori_loop(0, TD // SUBLANES, drain, 0)


def _dispatch(dest, zrow, zflag, xn2):
    return pl.pallas_call(
        _dispatch_kernel,
        grid_spec=pltpu.PrefetchScalarGridSpec(
            num_scalar_prefetch=3,
            grid=(N_TOK // TD,),
            in_specs=[pl.BlockSpec((TD, D_MODEL), lambda i, *_: (i, 0))],
            out_specs=pl.BlockSpec(memory_space=pl.ANY),
            scratch_shapes=[pltpu.VMEM((ROW_CHUNK, D_MODEL), _F32),
                            pltpu.SemaphoreType.DMA(()),
                            pltpu.SemaphoreType.DMA(())],
        ),
        out_shape=jax.ShapeDtypeStruct((MAX_SUPER * SUPER_ROWS, D_MODEL), _F32),
        compiler_params=pltpu.CompilerParams(dimension_semantics=("arbitrary",),
                                             vmem_limit_bytes=VMEM_LIMIT),
        name="dispatch",
    )(dest, zrow, zflag, xn2)


def _expert_kernel(sup_e_ref, sup_blk_ref, nchunk_ref,
                   x_hbm, wg_ref, wu_ref, bg_ref, bu_ref, wd_ref, bd_ref, out_hbm,
                   x_stage, x_bf, acc, x_sem, o_sem, *, max_super):
    s = pl.program_id(0)
    j = pl.program_id(1)
    n_chunks = nchunk_ref[s]
    cur = s % 2

    def x_copy(sup, c):
        r = pl.multiple_of(sup_blk_ref[sup] * SUPER_ROWS + c * ROW_CHUNK, ROW_CHUNK)
        slot = c % X_SLOTS
        return pltpu.make_async_copy(x_hbm.at[pl.ds(r, ROW_CHUNK)], x_stage.at[slot],
                                     x_sem.at[slot])

    def x_finish(sup, c):
        x_copy(sup, c).wait()
        rows = pl.ds(pl.multiple_of(c * ROW_CHUNK, ROW_CHUNK), ROW_CHUNK)
        x_bf[sup % 2, rows, :] = x_stage[c % X_SLOTS].astype(_BF16)

    def out_copy(sup, c):
        r = pl.multiple_of(sup_blk_ref[sup] * SUPER_ROWS + c * ROW_CHUNK, ROW_CHUNK)
        rows = pl.ds(pl.multiple_of(c * ROW_CHUNK, ROW_CHUNK), ROW_CHUNK)
        return pltpu.make_async_copy(acc.at[sup % 2, rows], out_hbm.at[pl.ds(r, ROW_CHUNK)],
                                     o_sem.at[sup % 2])

    def for_each_chunk(sup, fn):
        for c in range(CHUNKS_PER_SUPER):
            @pl.when(c < nchunk_ref[sup])
            def _(c=c):
                fn(sup, c)

    @pl.when((s == 0) & (j == 0))
    def _():
        def load(sup, c):
            x_copy(sup, c).start()
            x_finish(sup, c)
        for_each_chunk(0, load)

    @pl.when((s >= 2) & (j == 0))
    def _():
        for_each_chunk(jnp.maximum(s - 2, 0), lambda sup, c: out_copy(sup, c).wait())

    prev = jnp.maximum(s - 1, 0)

    @pl.when((s >= 1) & (j < nchunk_ref[prev]))
    def _():
        out_copy(prev, j).start()

    nxt = jnp.minimum(s + 1, max_super - 1)
    n_next = jnp.where(s + 1 < max_super, nchunk_ref[nxt], 0)

    @pl.when((j >= 1) & (j - 1 < n_next))
    def _():
        x_finish(nxt, j - 1)

    @pl.when(j < n_next)
    def _():
        x_copy(nxt, j).start()

    @pl.when(n_chunks > 0)
    def _():
        def partial_out(n_rows):
            rows = pl.ds(0, n_rows)
            xb = x_bf[cur, rows, :]
            gate = _dot(xb, wg_ref[0].astype(_BF16)) + bg_ref[0]
            up = _dot(xb, wu_ref[0].astype(_BF16)) + bu_ref[0]
            gate = jnp.minimum(gate, SWIGLU_LIMIT)
            up = jnp.clip(up, -SWIGLU_LIMIT, SWIGLU_LIMIT)
            glu = gate * jax.nn.sigmoid(gate * SWIGLU_ALPHA)
            act = ((up + 1.0) * glu).astype(_BF16)
            return rows, _dot(act, wd_ref[0].astype(_BF16))

        def sweep(update):
            for n in range(1, CHUNKS_PER_SUPER + 1):
                @pl.when(n_chunks == n)
                def _(n=n):
                    update(n * ROW_CHUNK)

        def init(n_rows):
            rows, o = partial_out(n_rows)
            acc[cur, rows, :] = o + bd_ref[0]

        def accumulate(n_rows):
            rows, o = partial_out(n_rows)
            acc[cur, rows, :] += o

        @pl.when(j == 0)
        def _():
            sweep(init)

        @pl.when(j > 0)
        def _():
            sweep(accumulate)

    @pl.when((s == max_super - 1) & (j == N_FF_TILES - 1))
    def _():
        for_each_chunk(max_super - 1, lambda sup, c: out_copy(sup, c).start())
        for_each_chunk(max_super - 2, lambda sup, c: out_copy(sup, c).wait())
        for_each_chunk(max_super - 1, lambda sup, c: out_copy(sup, c).wait())


def _experts(max_super, sup_e, sup_blk, nchunk, buf_x, w_gate_up, b_gate_up3, w_down, b_down3):
    def ff_tile(j, nchunk, s):
        return jnp.where(nchunk[s] > 0, j, N_FF_TILES - 1)

    x_spec = pl.BlockSpec(memory_space=pl.ANY)
    wg_spec = pl.BlockSpec((1, D_MODEL, FF_TILE),
                           lambda s, j, se, sb, nc: (se[s], 0, ff_tile(j, nc, s)))
    wu_spec = pl.BlockSpec((1, D_MODEL, FF_TILE),
                           lambda s, j, se, sb, nc: (se[s], 0, N_FF_TILES + ff_tile(j, nc, s)))
    bg_spec = pl.BlockSpec((1, 1, FF_TILE),
                           lambda s, j, se, sb, nc: (se[s], 0, ff_tile(j, nc, s)))
    bu_spec = pl.BlockSpec((1, 1, FF_TILE),
                           lambda s, j, se, sb, nc: (se[s], 0, N_FF_TILES + ff_tile(j, nc, s)))
    wd_spec = pl.BlockSpec((1, FF_TILE, D_MODEL),
                           lambda s, j, se, sb, nc: (se[s], ff_tile(j, nc, s), 0))
    bd_spec = pl.BlockSpec((1, 1, D_MODEL), lambda s, j, se, sb, nc: (se[s], 0, 0))
    out_spec = pl.BlockSpec(memory_space=pl.ANY)
    return pl.pallas_call(
        functools.partial(_expert_kernel, max_super=max_super),
        grid_spec=pltpu.PrefetchScalarGridSpec(
            num_scalar_prefetch=3,
            grid=(max_super, N_FF_TILES),
            in_specs=[x_spec, wg_spec, wu_spec, bg_spec, bu_spec, wd_spec, bd_spec],
            out_specs=out_spec,
            scratch_shapes=[pltpu.VMEM((X_SLOTS, ROW_CHUNK, D_MODEL), _F32),
                            pltpu.VMEM((2, SUPER_ROWS, D_MODEL), _BF16),
                            pltpu.VMEM((2, SUPER_ROWS, D_MODEL), _F32),
                            pltpu.SemaphoreType.DMA((X_SLOTS,)),
                            pltpu.SemaphoreType.DMA((2,))],
        ),
        out_shape=jax.ShapeDtypeStruct((MAX_SUPER * SUPER_ROWS, D_MODEL), _F32),
        compiler_params=pltpu.CompilerParams(dimension_semantics=("arbitrary", "arbitrary"),
                                             vmem_limit_bytes=VMEM_LIMIT),
        name="experts",
    )(sup_e, sup_blk, nchunk, buf_x, w_gate_up, w_gate_up, b_gate_up3, b_gate_up3, w_down, b_down3)


def _combine_kernel(dest_ref, h1_ref, gates_ref, lng_ref, y_hbm, out_ref, rows, sem):
    i = pl.program_id(0)
    n_steps = pl.num_programs(0)

    def row_copy(tile, slot, t8, u, k):
        t = pl.multiple_of(t8 * SUBLANES, SUBLANES) + u
        src = dest_ref[(tile * TC + t) * TOP_K + k]
        return pltpu.make_async_copy(y_hbm.at[pl.ds(src, 1)], rows.at[slot, k, pl.ds(t, 1)],
                                     sem.at[slot])

    def issue_tile(tile, slot):
        def body(t8, carry):
            for u in range(SUBLANES):
                for k in range(TOP_K):
                    row_copy(tile, slot, t8, u, k).start()
            return carry
        lax.fori_loop(0, TC // SUBLANES, body, 0)

    def wait_tile(tile, slot):
        def body(t8, carry):
            for u in range(SUBLANES):
                for k in range(TOP_K):
                    row_copy(tile, slot, t8, u, k).wait()
            return carry
        lax.fori_loop(0, TC // SUBLANES, body, 0)

    slot = i % 2

    @pl.when(i == 0)
    def _():
        issue_tile(0, 0)

    @pl.when(i + 1 < n_steps)
    def _():
        issue_tile(i + 1, 1 - slot)

    wait_tile(i, slot)

    g = gates_ref[...]
    acc = h1_ref[...]
    for k in range(TOP_K):
        acc = acc + rows[slot, k] * g[:, k:k + 1]
    out_ref[...] = _rms_rows(acc, lng_ref[...])


def _combine(dest, h1, gates, ln_final_g, y_sorted):
    return pl.pallas_call(
        _combine_kernel,
        grid_spec=pltpu.PrefetchScalarGridSpec(
            num_scalar_prefetch=1,
            grid=(N_TOK // TC,),
            in_specs=[pl.BlockSpec((TC, D_MODEL), lambda i, *_: (i, 0)),
                      pl.BlockSpec((TC, LANES), lambda i, *_: (i, 0)),
                      pl.BlockSpec((1, D_MODEL), lambda i, *_: (0, 0)),
                      pl.BlockSpec(memory_space=pl.ANY)],
            out_specs=pl.BlockSpec((TC, D_MODEL), lambda i, *_: (i, 0)),
            scratch_shapes=[pltpu.VMEM((2, TOP_K, TC, D_MODEL), _F32),
                            pltpu.SemaphoreType.DMA((2,))],
        ),
        out_shape=jax.ShapeDtypeStruct((N_TOK, D_MODEL), _F32),
        compiler_params=pltpu.CompilerParams(dimension_semantics=("arbitrary",),
                                             vmem_limit_bytes=VMEM_LIMIT),
        name="combine",
    )(dest, h1, gates, ln_final_g, y_sorted)


def kernel(x, ln_mix_g, w_in, conv_w, sgu_ln_g, sgu_ln_b, sgu_w, sgu_b, gn_conv, gn_sgu, w_out,
           ln_ffn_g, w_router, b_router, w_gate_up, b_gate_up, w_down, b_down, ln_final_g):
    row = lambda v: v.reshape(1, -1)
    x2d = x.reshape(N_TOK, D_MODEL)

    sgu_b_full = jnp.repeat(jnp.transpose(sgu_b), SGU_HEAD_DIM, axis=1)
    wr_hi = w_router.astype(_BF16)
    wr_lo = (w_router - wr_hi.astype(_F32)).astype(_BF16)
    pad = ((0, 0), (0, LANES - N_EXPERTS))
    wr_split = jnp.concatenate([jnp.pad(wr_hi, pad), jnp.pad(wr_lo, pad)], axis=1)
    br_pad = jnp.pad(b_router, (0, LANES - N_EXPERTS)).reshape(1, LANES)

    h1, xn2, route, gates, counts = _mixer_router(
        x2d, row(ln_mix_g), w_in.astype(_BF16), conv_w, row(sgu_ln_g), row(sgu_ln_b), sgu_w,
        sgu_b_full, row(gn_conv), row(gn_sgu), w_out.astype(_BF16), row(ln_ffn_g), wr_split, br_pad)

    sizes = counts[0, :N_EXPERTS].astype(jnp.int32)
    n_sup = (sizes + SUPER_ROWS - 1) // SUPER_ROWS
    sup_end = jnp.cumsum(n_sup)
    sup_start = sup_end - n_sup
    total_sup = sup_end[-1]
    row_start = (sup_start * SUPER_ROWS).astype(jnp.int32)
    sid = jnp.arange(MAX_SUPER, dtype=jnp.int32)
    sid_eff = jnp.minimum(sid, total_sup - 1)
    sup_e = jnp.minimum(jnp.searchsorted(sup_end, sid_eff, side="right"),
                        N_EXPERTS - 1).astype(jnp.int32)
    rows_left = sizes[sup_e] - (sid_eff - sup_start[sup_e]) * SUPER_ROWS
    valid = jnp.clip(rows_left, 0, SUPER_ROWS)
    nchunk = jnp.where(sid < total_sup, (valid + ROW_CHUNK - 1) // ROW_CHUNK, 0).astype(jnp.int32)
    zflag = (sizes > 0).astype(jnp.int32)
    zrow = (row_start + ((jnp.maximum(sizes, 1) - 1) // ROW_CHUNK) * ROW_CHUNK).astype(jnp.int32)
    eid = route[:, 0:TOP_K]
    eid_start = jnp.sum(jnp.where(eid[..., None] == jnp.arange(N_EXPERTS, dtype=jnp.int32),
                                  row_start, 0), axis=-1)
    dest = (eid_start + route[:, TOP_K:2 * TOP_K]).reshape(-1)

    buf_x = _dispatch(dest, zrow, zflag, xn2)
    expert_args = (sup_e, sid_eff.astype(jnp.int32), nchunk, buf_x, w_gate_up,
                   b_gate_up.reshape(N_EXPERTS, 1, 2 * D_FF), w_down,
                   b_down.reshape(N_EXPERTS, 1, D_MODEL))
    y_sorted = lax.cond(total_sup <= N_EXPERTS,
                        functools.partial(_experts, N_EXPERTS),
                        functools.partial(_experts, MAX_SUPER), *expert_args)
    out = _combine(dest, h1, gates, row(ln_final_g), y_sorted)
    return out.reshape(BATCH, SEQ, D_MODEL)
```

```python
import functools

import jax
import jax.numpy as jnp
from jax import lax
from jax.experimental import pallas as pl
from jax.experimental.pallas import tpu as pltpu

D_MODEL = 2048
BATCH = 2
SEQ = 4096
N_TOK = BATCH * SEQ

D_CONV = 1024
N_CONV_GROUPS = 8
CONV_WIDTH = 3
D_SGU = 1024
N_SGU_HEADS = 8
SGU_HEAD_DIM = 128
CHUNK = 128
D_IN_PROJ = 3 * D_CONV + 2 * D_SGU

N_EXPERTS = 32
TOP_K = 4
D_FF = 2048
SWIGLU_LIMIT = 7.0
SWIGLU_ALPHA = 1.702
RMS_EPS = 1e-5
LN_EPS = 1e-5

LANES = 128
SUBLANES = 8
VMEM_LIMIT = 56 * 1024 * 1024

TM = 256
SUPER_ROWS = 1280
ROW_CHUNK = 256
FF_TILE = 256
N_FF_TILES = D_FF // FF_TILE
MAX_SUPER = (N_TOK * TOP_K + N_EXPERTS * (SUPER_ROWS - 1)) // SUPER_ROWS
CHUNKS_PER_SUPER = SUPER_ROWS // ROW_CHUNK
assert CHUNKS_PER_SUPER + 1 <= N_FF_TILES
X_SLOTS = 2
HALF_CHUNK = ROW_CHUNK // 2
FINE_FROM = 7


def _row_variants():
    variants = {}
    for n_half in range(1, SUPER_ROWS // HALF_CHUNK + 1):
        if n_half >= FINE_FROM:
            n_rows = n_half * HALF_CHUNK
        else:
            n_rows = -(-n_half // 2) * ROW_CHUNK
        lo, hi = variants.get(n_rows, (n_half, n_half))
        variants[n_rows] = (min(lo, n_half), max(hi, n_half))
    return variants


ROW_VARIANTS = _row_variants()
TD = 256
TC = 256

_F32 = jnp.float32
_BF16 = jnp.bfloat16


def _dot(a, b):
    return jnp.dot(a, b, preferred_element_type=_F32)


def _gelu_exact(x):
    return 0.5 * x * (1.0 + lax.erf(x * (2.0 ** -0.5)))


def _rms_rows(x, gain):
    return x * lax.rsqrt(jnp.mean(x * x, axis=-1, keepdims=True) + RMS_EPS) * gain


def _mixer_router_kernel(x_ref, lng_ref, win_ref, convw_ref, slg_ref, slb_ref, sw_ref, sb_ref,
                         gnc_ref, gns_ref, wout_ref, lnf_ref, wr_ref, br_ref,
                         h1_ref, xn2_ref, route_ref, gates_ref, counts_ref,
                         cbuf, carry, ybuf):
    i = pl.program_id(0)

    @pl.when(i % (SEQ // TM) == 0)
    def _():
        cbuf[0:SUBLANES, :] = jnp.zeros((SUBLANES, D_CONV), _F32)

    @pl.when(i == 0)
    def _():
        carry[...] = jnp.zeros_like(carry)

    x = x_ref[...]
    xn = _rms_rows(x, lng_ref[...]).astype(_BF16)

    b_gate = _dot(xn, win_ref[:, 0:D_CONV])
    c_gate = _dot(xn, win_ref[:, D_CONV:2 * D_CONV])
    hh = _dot(xn, win_ref[:, 2 * D_CONV:3 * D_CONV])
    ch = c_gate * hh
    cbuf[SUBLANES:SUBLANES + TM, :] = ch
    ch1 = cbuf[SUBLANES - 1:SUBLANES - 1 + TM, :]
    ch2 = cbuf[SUBLANES - 2:SUBLANES - 2 + TM, :]
    conv = convw_ref[0:1, :] * ch2 + convw_ref[1:2, :] * ch1 + convw_ref[2:3, :] * ch
    cbuf[0:SUBLANES, :] = cbuf[TM:TM + SUBLANES, :]
    y_conv = b_gate * conv
    for g in range(N_CONV_GROUPS):
        sl = slice(g * LANES, (g + 1) * LANES)
        blk = y_conv[:, sl]
        ms = jnp.mean(blk * blk, axis=-1, keepdims=True)
        ybuf[:, sl] = (blk * lax.rsqrt(ms + RMS_EPS) * gnc_ref[:, sl]).astype(_BF16)

    gu = _gelu_exact(_dot(xn, win_ref[:, 3 * D_CONV:3 * D_CONV + D_SGU]))
    gv = _gelu_exact(_dot(xn, win_ref[:, 3 * D_CONV + D_SGU:D_IN_PROJ]))
    row_c = lax.broadcasted_iota(jnp.int32, (CHUNK, CHUNK), 0)
    col_c = lax.broadcasted_iota(jnp.int32, (CHUNK, CHUNK), 1)
    causal = row_c >= col_c
    for h in range(N_SGU_HEADS):
        sl = slice(h * SGU_HEAD_DIM, (h + 1) * SGU_HEAD_DIM)
        vh = gv[:, sl]
        mu = jnp.mean(vh, axis=-1, keepdims=True)
        xc = vh - mu
        var = jnp.mean(xc * xc, axis=-1, keepdims=True)
        vn = (xc * lax.rsqrt(var + LN_EPS) * slg_ref[:, sl] + slb_ref[:, sl]).astype(_BF16)
        wm = jnp.where(causal, sw_ref[h], 0.0).astype(_BF16)
        for c in range(TM // CHUNK):
            rows = slice(c * CHUNK, (c + 1) * CHUNK)
            mixed = _dot(wm, vn[rows, :]) + sb_ref[:, sl]
            ys = gu[rows, sl] * mixed
            ms = jnp.mean(ys * ys, axis=-1, keepdims=True)
            ybuf[rows, D_CONV + h * SGU_HEAD_DIM:D_CONV + (h + 1) * SGU_HEAD_DIM] = (
                ys * lax.rsqrt(ms + RMS_EPS) * gns_ref[:, sl]).astype(_BF16)

    h1 = x + _dot(ybuf[...], wout_ref[...])
    h1_ref[...] = h1
    xn2 = _rms_rows(h1, lnf_ref[...])
    xn2_ref[...] = xn2

    x_hi = xn2.astype(_BF16)
    x_lo = (xn2 - x_hi.astype(_F32)).astype(_BF16)
    p = _dot(x_hi, wr_ref[...]) + _dot(x_lo, wr_ref[...])
    logits = p[:, :LANES] + p[:, LANES:] + br_ref[...]
    lane = lax.broadcasted_iota(jnp.int32, (TM, LANES), 1)
    lane_f = lane.astype(_F32)
    neg_inf = jnp.float32(-jnp.inf)
    logits = jnp.where(lane < N_EXPERTS, logits, neg_inf)

    vals, ids, onehots = [], [], []
    cur = logits
    for _ in range(TOP_K):
        m = jnp.max(cur, axis=-1, keepdims=True)
        idx = jnp.min(jnp.where(cur == m, lane_f, float(LANES)), axis=-1, keepdims=True)
        oh = lane_f == idx
        vals.append(m)
        ids.append(idx)
        onehots.append(oh)
        cur = jnp.where(oh, neg_inf, cur)
    exps = [jnp.exp(v - vals[0]) for v in vals]
    denom = exps[0] + exps[1] + exps[2] + exps[3]
    gates = [e / denom for e in exps]

    mask = (onehots[0] | onehots[1] | onehots[2] | onehots[3]).astype(_F32)
    row_t = lax.broadcasted_iota(jnp.int32, (TM, TM), 0)
    col_t = lax.broadcasted_iota(jnp.int32, (TM, TM), 1)
    strict_lower = (row_t > col_t).astype(_BF16)
    before = _dot(strict_lower, mask.astype(_BF16)) + carry[...]
    ranks = [jnp.sum(jnp.where(oh, before, 0.0), axis=-1, keepdims=True) for oh in onehots]
    carry[...] = carry[...] + jnp.sum(mask, axis=0, keepdims=True)
    counts_ref[...] = jnp.broadcast_to(carry[...], counts_ref.shape)

    route = jnp.zeros((TM, LANES), _F32)
    gate_out = jnp.zeros((TM, LANES), _F32)
    for k in range(TOP_K):
        route = jnp.where(lane == k, ids[k], route)
        route = jnp.where(lane == TOP_K + k, ranks[k], route)
        gate_out = jnp.where(lane == k, gates[k], gate_out)
    route_ref[...] = route.astype(jnp.int32)
    gates_ref[...] = gate_out


def _mixer_router(x2d, ln_mix_g, w_in_bf, conv_w, sgu_ln_g, sgu_ln_b, sgu_w, sgu_b_full,
                  gn_conv, gn_sgu, w_out_bf, ln_ffn_g, wr_split, br_pad):
    def full(a):
        return pl.BlockSpec(a.shape, lambda i: (0,) * a.ndim)

    row_blk = lambda w: pl.BlockSpec((TM, w), lambda i: (i, 0))
    ins = [x2d, ln_mix_g, w_in_bf, conv_w, sgu_ln_g, sgu_ln_b, sgu_w, sgu_b_full,
           gn_conv, gn_sgu, w_out_bf, ln_ffn_g, wr_split, br_pad]
    in_specs = [row_blk(D_MODEL)] + [full(a) for a in ins[1:]]
    return pl.pallas_call(
        _mixer_router_kernel,
        grid=(N_TOK // TM,),
        in_specs=in_specs,
        out_specs=[row_blk(D_MODEL), row_blk(D_MODEL), row_blk(LANES), row_blk(LANES),
                   pl.BlockSpec((SUBLANES, LANES), lambda i: (0, 0))],
        out_shape=[jax.ShapeDtypeStruct((N_TOK, D_MODEL), _F32),
                   jax.ShapeDtypeStruct((N_TOK, D_MODEL), _F32),
                   jax.ShapeDtypeStruct((N_TOK, LANES), jnp.int32),
                   jax.ShapeDtypeStruct((N_TOK, LANES), _F32),
                   jax.ShapeDtypeStruct((SUBLANES, LANES), _F32)],
        scratch_shapes=[pltpu.VMEM((TM + SUBLANES, D_CONV), _F32),
                        pltpu.VMEM((1, LANES), _F32),
                        pltpu.VMEM((TM, D_MODEL), _BF16)],
        compiler_params=pltpu.CompilerParams(dimension_semantics=("arbitrary",),
                                             vmem_limit_bytes=VMEM_LIMIT),
        name="mixer_router",
    )(*ins)


def _dispatch_kernel(dest_ref, zrow_ref, zflag_ref,
                     x_ref, bufx_hbm, zbuf, zsem, sem):
    i = pl.program_id(0)

    @pl.when(i == 0)
    def _():
        zbuf[...] = jnp.zeros_like(zbuf)

        def zero_copy(e):
            z0 = pl.multiple_of(zrow_ref[e], ROW_CHUNK)
            return pltpu.make_async_copy(zbuf, bufx_hbm.at[pl.ds(z0, ROW_CHUNK)], zsem)

        for e in range(N_EXPERTS):
            @pl.when(zflag_ref[e] > 0)
            def _(e=e):
                zero_copy(e).start()
        for e in range(N_EXPERTS):
            @pl.when(zflag_ref[e] > 0)
            def _(e=e):
                zero_copy(e).wait()

    def row_copy(t8, u, k):
        t = pl.multiple_of(t8 * SUBLANES, SUBLANES) + u
        dst = dest_ref[(i * TD + t) * TOP_K + k]
        return pltpu.make_async_copy(x_ref.at[pl.ds(t, 1)], bufx_hbm.at[pl.ds(dst, 1)], sem)

    def issue(t8, carry):
        for u in range(SUBLANES):
            for k in range(TOP_K):
                row_copy(t8, u, k).start()
        return carry

    def drain(t8, carry):
        for u in range(SUBLANES):
            for k in range(TOP_K):
                row_copy(t8, u, k).wait()
        return carry

    lax.fori_loop(0, TD // SUBLANES, issue, 0)
    lax.fori_loop(0, TD // SUBLANES, drain, 0)


def _dispatch(dest, zrow, zflag, xn2):
    return pl.pallas_call(
        _dispatch_kernel,
        grid_spec=pltpu.PrefetchScalarGridSpec(
            num_scalar_prefetch=3,
            grid=(N_TOK // TD,),
            in_specs=[pl.BlockSpec((TD, D_MODEL), lambda i, *_: (i, 0))],
            out_specs=pl.BlockSpec(memory_space=pl.ANY),
            scratch_shapes=[pltpu.VMEM((ROW_CHUNK, D_MODEL), _F32),
                            pltpu.SemaphoreType.DMA(()),
                            pltpu.SemaphoreType.DMA(())],
        ),
        out_shape=jax.ShapeDtypeStruct((MAX_SUPER * SUPER_ROWS, D_MODEL), _F32),
        compiler_params=pltpu.CompilerParams(dimension_semantics=("arbitrary",),
                                             vmem_limit_bytes=VMEM_LIMIT),
        name="dispatch",
    )(dest, zrow, zflag, xn2)


def _expert_kernel(sup_e_ref, sup_blk_ref, nchunk_ref, nhalf_ref,
                   x_hbm, wg_ref, wu_ref, bg_ref, bu_ref, wd_ref, bd_ref, out_hbm,
                   x_stage, x_bf, acc, x_sem, o_sem, *, max_super):
    s = pl.program_id(0)
    j = pl.program_id(1)
    n_chunks = nchunk_ref[s]
    cur = s % 2

    def x_copy(sup, c):
        r = pl.multiple_of(sup_blk_ref[sup] * SUPER_ROWS + c * ROW_CHUNK, ROW_CHUNK)
        slot = c % X_SLOTS
        return pltpu.make_async_copy(x_hbm.at[pl.ds(r, ROW_CHUNK)], x_stage.at[slot],
                                     x_sem.at[slot])

    def x_finish(sup, c):
        x_copy(sup, c).wait()
        rows = pl.ds(pl.multiple_of(c * ROW_CHUNK, ROW_CHUNK), ROW_CHUNK)
        x_bf[sup % 2, rows, :] = x_stage[c % X_SLOTS].astype(_BF16)

    def out_copy(sup, c):
        r = pl.multiple_of(sup_blk_ref[sup] * SUPER_ROWS + c * ROW_CHUNK, ROW_CHUNK)
        rows = pl.ds(pl.multiple_of(c * ROW_CHUNK, ROW_CHUNK), ROW_CHUNK)
        return pltpu.make_async_copy(acc.at[sup % 2, rows], out_hbm.at[pl.ds(r, ROW_CHUNK)],
                                     o_sem.at[sup % 2])

    def for_each_chunk(sup, fn):
        for c in range(CHUNKS_PER_SUPER):
            @pl.when(c < nchunk_ref[sup])
            def _(c=c):
                fn(sup, c)

    @pl.when((s == 0) & (j == 0))
    def _():
        def load(sup, c):
            x_copy(sup, c).start()
            x_finish(sup, c)
        for_each_chunk(0, load)

    @pl.when((s >= 2) & (j == 0))
    def _():
        for_each_chunk(jnp.maximum(s - 2, 0), lambda sup, c: out_copy(sup, c).wait())

    prev = jnp.maximum(s - 1, 0)

    @pl.when((s >= 1) & (j < nchunk_ref[prev]))
    def _():
        out_copy(prev, j).start()

    nxt = jnp.minimum(s + 1, max_super - 1)
    n_next = jnp.where(s + 1 < max_super, nchunk_ref[nxt], 0)

    @pl.when((j >= 1) & (j - 1 < n_next))
    def _():
        x_finish(nxt, j - 1)

    @pl.when(j < n_next)
    def _():
        x_copy(nxt, j).start()

    @pl.when(n_chunks > 0)
    def _():
        def partial_out(n_rows):
            rows = pl.ds(0, n_rows)
            xb = x_bf[cur, rows, :]
            gate = _dot(xb, wg_ref[0].astype(_BF16)) + bg_ref[0]
            up = _dot(xb, wu_ref[0].astype(_BF16)) + bu_ref[0]
            gate = jnp.minimum(gate, SWIGLU_LIMIT)
            up = jnp.clip(up, -SWIGLU_LIMIT, SWIGLU_LIMIT)
            glu = gate * jax.nn.sigmoid(gate * SWIGLU_ALPHA)
            act = ((up + 1.0) * glu).astype(_BF16)
            return rows, _dot(act, wd_ref[0].astype(_BF16))

        def sweep(update):
            n_half = nhalf_ref[s]
            for n_rows, (lo, hi) in ROW_VARIANTS.items():
                @pl.when((n_half >= lo) & (n_half <= hi))
                def _(n_rows=n_rows):
                    update(n_rows)

        def init(n_rows):
            rows, o = partial_out(n_rows)
            acc[cur, rows, :] = o + bd_ref[0]

        def accumulate(n_rows):
            rows, o = partial_out(n_rows)
            acc[cur, rows, :] += o

        @pl.when(j == 0)
        def _():
            sweep(init)

        @pl.when(j > 0)
        def _():
            sweep(accumulate)

    @pl.when((s == max_super - 1) & (j == N_FF_TILES - 1))
    def _():
        for_each_chunk(max_super - 1, lambda sup, c: out_copy(sup, c).start())
        for_each_chunk(max_super - 2, lambda sup, c: out_copy(sup, c).wait())
        for_each_chunk(max_super - 1, lambda sup, c: out_copy(sup, c).wait())


def _experts(max_super, sup_e, sup_blk, nchunk, nhalf, buf_x, w_gate_up, b_gate_up3, w_down, b_down3):
    def ff_tile(j, nchunk, s):
        return jnp.where(nchunk[s] > 0, j, N_FF_TILES - 1)

    x_spec = pl.BlockSpec(memory_space=pl.ANY)
    wg_spec = pl.BlockSpec((1, D_MODEL, FF_TILE),
                           lambda s, j, se, sb, nc, nh: (se[s], 0, ff_tile(j, nc, s)))
    wu_spec = pl.BlockSpec((1, D_MODEL, FF_TILE),
                           lambda s, j, se, sb, nc, nh: (se[s], 0, N_FF_TILES + ff_tile(j, nc, s)))
    bg_spec = pl.BlockSpec((1, 1, FF_TILE),
                           lambda s, j, se, sb, nc, nh: (se[s], 0, ff_tile(j, nc, s)))
    bu_spec = pl.BlockSpec((1, 1, FF_TILE),
                           lambda s, j, se, sb, nc, nh: (se[s], 0, N_FF_TILES + ff_tile(j, nc, s)))
    wd_spec = pl.BlockSpec((1, FF_TILE, D_MODEL),
                           lambda s, j, se, sb, nc, nh: (se[s], ff_tile(j, nc, s), 0))
    bd_spec = pl.BlockSpec((1, 1, D_MODEL), lambda s, j, se, sb, nc, nh: (se[s], 0, 0))
    out_spec = pl.BlockSpec(memory_space=pl.ANY)
    return pl.pallas_call(
        functools.partial(_expert_kernel, max_super=max_super),
        grid_spec=pltpu.PrefetchScalarGridSpec(
            num_scalar_prefetch=4,
            grid=(max_super, N_FF_TILES),
            in_specs=[x_spec, wg_spec, wu_spec, bg_spec, bu_spec, wd_spec, bd_spec],
            out_specs=out_spec,
            scratch_shapes=[pltpu.VMEM((X_SLOTS, ROW_CHUNK, D_MODEL), _F32),
                            pltpu.VMEM((2, SUPER_ROWS, D_MODEL), _BF16),
                            pltpu.VMEM((2, SUPER_ROWS, D_MODEL), _F32),
                            pltpu.SemaphoreType.DMA((X_SLOTS,)),
                            pltpu.SemaphoreType.DMA((2,))],
        ),
        out_shape=jax.ShapeDtypeStruct((MAX_SUPER * SUPER_ROWS, D_MODEL), _F32),
        compiler_params=pltpu.CompilerParams(dimension_semantics=("arbitrary", "arbitrary"),
                                             vmem_limit_bytes=VMEM_LIMIT),
        name="experts",
    )(sup_e, sup_blk, nchunk, nhalf, buf_x, w_gate_up, w_gate_up, b_gate_up3, b_gate_up3, w_down, b_down3)


def _combine_kernel(dest_ref, h1_ref, gates_ref, lng_ref, y_hbm, out_ref, rows, sem):
    i = pl.program_id(0)
    n_steps = pl.num_programs(0)

    def row_copy(tile, slot, t8, u, k):
        t = pl.multiple_of(t8 * SUBLANES, SUBLANES) + u
        src = dest_ref[(tile * TC + t) * TOP_K + k]
        return pltpu.make_async_copy(y_hbm.at[pl.ds(src, 1)], rows.at[slot, k, pl.ds(t, 1)],
                                     sem.at[slot])

    def issue_tile(tile, slot):
        def body(t8, carry):
            for u in range(SUBLANES):
                for k in range(TOP_K):
                    row_copy(tile, slot, t8, u, k).start()
            return carry
        lax.fori_loop(0, TC // SUBLANES, body, 0)

    def wait_tile(tile, slot):
        def body(t8, carry):
            for u in range(SUBLANES):
                for k in range(TOP_K):
                    row_copy(tile, slot, t8, u, k).wait()
            return carry
        lax.fori_loop(0, TC // SUBLANES, body, 0)

    slot = i % 2

    @pl.when(i == 0)
    def _():
        issue_tile(0, 0)

    @pl.when(i + 1 < n_steps)
    def _():
        issue_tile(i + 1, 1 - slot)

    wait_tile(i, slot)

    g = gates_ref[...]
    acc = h1_ref[...]
    for k in range(TOP_K):
        acc = acc + rows[slot, k] * g[:, k:k + 1]
    out_ref[...] = _rms_rows(acc, lng_ref[...])


def _combine(dest, h1, gates, ln_final_g, y_sorted):
    return pl.pallas_call(
        _combine_kernel,
        grid_spec=pltpu.PrefetchScalarGridSpec(
            num_scalar_prefetch=1,
            grid=(N_TOK // TC,),
            in_specs=[pl.BlockSpec((TC, D_MODEL), lambda i, *_: (i, 0)),
                      pl.BlockSpec((TC, LANES), lambda i, *_: (i, 0)),
                      pl.BlockSpec((1, D_MODEL), lambda i, *_: (0, 0)),
                      pl.BlockSpec(memory_space=pl.ANY)],
            out_specs=pl.BlockSpec((TC, D_MODEL), lambda i, *_: (i, 0)),
            scratch_shapes=[pltpu.VMEM((2, TOP_K, TC, D_MODEL), _F32),
                            pltpu.SemaphoreType.DMA((2,))],
        ),
        out_shape=jax.ShapeDtypeStruct((N_TOK, D_MODEL), _F32),
        compiler_params=pltpu.CompilerParams(dimension_semantics=("arbitrary",),
                                             vmem_limit_bytes=VMEM_LIMIT),
        name="combine",
    )(dest, h1, gates, ln_final_g, y_sorted)


def kernel(x, ln_mix_g, w_in, conv_w, sgu_ln_g, sgu_ln_b, sgu_w, sgu_b, gn_conv, gn_sgu, w_out,
           ln_ffn_g, w_router, b_router, w_gate_up, b_gate_up, w_down, b_down, ln_final_g):
    row = lambda v: v.reshape(1, -1)
    x2d = x.reshape(N_TOK, D_MODEL)

    sgu_b_full = jnp.repeat(jnp.transpose(sgu_b), SGU_HEAD_DIM, axis=1)
    wr_hi = w_router.astype(_BF16)
    wr_lo = (w_router - wr_hi.astype(_F32)).astype(_BF16)
    pad = ((0, 0), (0, LANES - N_EXPERTS))
    wr_split = jnp.concatenate([jnp.pad(wr_hi, pad), jnp.pad(wr_lo, pad)], axis=1)
    br_pad = jnp.pad(b_router, (0, LANES - N_EXPERTS)).reshape(1, LANES)

    h1, xn2, route, gates, counts = _mixer_router(
        x2d, row(ln_mix_g), w_in.astype(_BF16), conv_w, row(sgu_ln_g), row(sgu_ln_b), sgu_w,
        sgu_b_full, row(gn_conv), row(gn_sgu), w_out.astype(_BF16), row(ln_ffn_g), wr_split, br_pad)

    sizes = counts[0, :N_EXPERTS].astype(jnp.int32)
    n_sup = (sizes + SUPER_ROWS - 1) // SUPER_ROWS
    sup_end = jnp.cumsum(n_sup)
    sup_start = sup_end - n_sup
    total_sup = sup_end[-1]
    row_start = (sup_start * SUPER_ROWS).astype(jnp.int32)
    sid = jnp.arange(MAX_SUPER, dtype=jnp.int32)
    sid_eff = jnp.minimum(sid, total_sup - 1)
    sup_e = jnp.minimum(jnp.searchsorted(sup_end, sid_eff, side="right"),
                        N_EXPERTS - 1).astype(jnp.int32)
    rows_left = sizes[sup_e] - (sid_eff - sup_start[sup_e]) * SUPER_ROWS
    valid = jnp.clip(rows_left, 0, SUPER_ROWS)
    nchunk = jnp.where(sid < total_sup, (valid + ROW_CHUNK - 1) // ROW_CHUNK, 0).astype(jnp.int32)
    nhalf = jnp.where(sid < total_sup, (valid + HALF_CHUNK - 1) // HALF_CHUNK, 0).astype(jnp.int32)
    zflag = (sizes > 0).astype(jnp.int32)
    zrow = (row_start + ((jnp.maximum(sizes, 1) - 1) // ROW_CHUNK) * ROW_CHUNK).astype(jnp.int32)
    eid = route[:, 0:TOP_K]
    eid_start = jnp.sum(jnp.where(eid[..., None] == jnp.arange(N_EXPERTS, dtype=jnp.int32),
                                  row_start, 0), axis=-1)
    dest = (eid_start + route[:, TOP_K:2 * TOP_K]).reshape(-1)

    buf_x = _dispatch(dest, zrow, zflag, xn2)
    expert_args = (sup_e, sid_eff.astype(jnp.int32), nchunk, nhalf, buf_x, w_gate_up,
                   b_gate_up.reshape(N_EXPERTS, 1, 2 * D_FF), w_down,
                   b_down.reshape(N_EXPERTS, 1, D_MODEL))
    y_sorted = lax.cond(total_sup <= N_EXPERTS,
                        functools.partial(_experts, N_EXPERTS),
                        functools.partial(_experts, MAX_SUPER), *expert_args)
    out = _combine(dest, h1, gates, row(ln_final_g), y_sorted)
    return out.reshape(BATCH, SEQ, D_MODEL)
```

```python
import functools

import jax
import jax.numpy as jnp
from jax import lax
from jax.experimental import pallas as pl
from jax.experimental.pallas import tpu as pltpu

D_MODEL = 2048
BATCH = 2
SEQ = 4096
N_TOK = BATCH * SEQ

D_CONV = 1024
N_CONV_GROUPS = 8
CONV_WIDTH = 3
D_SGU = 1024
N_SGU_HEADS = 8
SGU_HEAD_DIM = 128
CHUNK = 128
D_IN_PROJ = 3 * D_CONV + 2 * D_SGU

N_EXPERTS = 32
TOP_K = 4
D_FF = 2048
SWIGLU_LIMIT = 7.0
SWIGLU_ALPHA = 1.702
RMS_EPS = 1e-5
LN_EPS = 1e-5

LANES = 128
SUBLANES = 8
VMEM_LIMIT = 60 * 1024 * 1024

TM = 256
SUPER_ROWS = 1280
ROW_CHUNK = 256
FF_TILE = 256
N_FF_TILES = D_FF // FF_TILE
MAX_SUPER = (N_TOK * TOP_K + N_EXPERTS * (SUPER_ROWS - 1)) // SUPER_ROWS
CHUNKS_PER_SUPER = SUPER_ROWS // ROW_CHUNK
assert CHUNKS_PER_SUPER + 1 <= N_FF_TILES
X_SLOTS = 2
HALF_CHUNK = ROW_CHUNK // 2
FINE_FROM = 7


def _row_variants():
    variants = {}
    for n_half in range(1, SUPER_ROWS // HALF_CHUNK + 1):
        if n_half >= FINE_FROM:
            n_rows = n_half * HALF_CHUNK
        else:
            n_rows = -(-n_half // 2) * ROW_CHUNK
        lo, hi = variants.get(n_rows, (n_half, n_half))
        variants[n_rows] = (min(lo, n_half), max(hi, n_half))
    return variants


ROW_VARIANTS = _row_variants()
TD = 256
TC = 256

_F32 = jnp.float32
_BF16 = jnp.bfloat16


def _dot(a, b):
    return jnp.dot(a, b, preferred_element_type=_F32)


def _gelu_exact(x):
    return 0.5 * x * (1.0 + lax.erf(x * (2.0 ** -0.5)))


def _rms_rows(x, gain):
    return x * lax.rsqrt(jnp.mean(x * x, axis=-1, keepdims=True) + RMS_EPS) * gain


def _mixer_router_kernel(x_ref, lng_ref, win_ref, convw_ref, slg_ref, slb_ref, sw_ref, sb_ref,
                         gnc_ref, gns_ref, wout_ref, lnf_ref, wr_ref, br_ref,
                         h1_ref, xn2_ref, route_ref, gates_ref, counts_ref,
                         cbuf, carry, ybuf):
    i = pl.program_id(0)

    @pl.when(i % (SEQ // TM) == 0)
    def _():
        cbuf[0:SUBLANES, :] = jnp.zeros((SUBLANES, D_CONV), _F32)

    @pl.when(i == 0)
    def _():
        carry[...] = jnp.zeros_like(carry)

    x = x_ref[...]
    xn = _rms_rows(x, lng_ref[...]).astype(_BF16)

    b_gate = _dot(xn, win_ref[:, 0:D_CONV])
    c_gate = _dot(xn, win_ref[:, D_CONV:2 * D_CONV])
    hh = _dot(xn, win_ref[:, 2 * D_CONV:3 * D_CONV])
    ch = c_gate * hh
    cbuf[SUBLANES:SUBLANES + TM, :] = ch
    ch1 = cbuf[SUBLANES - 1:SUBLANES - 1 + TM, :]
    ch2 = cbuf[SUBLANES - 2:SUBLANES - 2 + TM, :]
    conv = convw_ref[0:1, :] * ch2 + convw_ref[1:2, :] * ch1 + convw_ref[2:3, :] * ch
    cbuf[0:SUBLANES, :] = cbuf[TM:TM + SUBLANES, :]
    y_conv = b_gate * conv
    for g in range(N_CONV_GROUPS):
        sl = slice(g * LANES, (g + 1) * LANES)
        blk = y_conv[:, sl]
        ms = jnp.mean(blk * blk, axis=-1, keepdims=True)
        ybuf[:, sl] = (blk * lax.rsqrt(ms + RMS_EPS) * gnc_ref[:, sl]).astype(_BF16)

    gu = _gelu_exact(_dot(xn, win_ref[:, 3 * D_CONV:3 * D_CONV + D_SGU]))
    gv = _gelu_exact(_dot(xn, win_ref[:, 3 * D_CONV + D_SGU:D_IN_PROJ]))
    row_c = lax.broadcasted_iota(jnp.int32, (CHUNK, CHUNK), 0)
    col_c = lax.broadcasted_iota(jnp.int32, (CHUNK, CHUNK), 1)
    causal = row_c >= col_c
    for h in range(N_SGU_HEADS):
        sl = slice(h * SGU_HEAD_DIM, (h + 1) * SGU_HEAD_DIM)
        vh = gv[:, sl]
        mu = jnp.mean(vh, axis=-1, keepdims=True)
        xc = vh - mu
        var = jnp.mean(xc * xc, axis=-1, keepdims=True)
        vn = (xc * lax.rsqrt(var + LN_EPS) * slg_ref[:, sl] + slb_ref[:, sl]).astype(_BF16)
        wm = jnp.where(causal, sw_ref[h], 0.0).astype(_BF16)
        for c in range(TM // CHUNK):
            rows = slice(c * CHUNK, (c + 1) * CHUNK)
            mixed = _dot(wm, vn[rows, :]) + sb_ref[:, sl]
            ys = gu[rows, sl] * mixed
            ms = jnp.mean(ys * ys, axis=-1, keepdims=True)
            ybuf[rows, D_CONV + h * SGU_HEAD_DIM:D_CONV + (h + 1) * SGU_HEAD_DIM] = (
                ys * lax.rsqrt(ms + RMS_EPS) * gns_ref[:, sl]).astype(_BF16)

    h1 = x + _dot(ybuf[...], wout_ref[...])
    h1_ref[...] = h1
    xn2 = _rms_rows(h1, lnf_ref[...])
    xn2_ref[...] = xn2

    x_hi = xn2.astype(_BF16)
    x_lo = (xn2 - x_hi.astype(_F32)).astype(_BF16)
    p = _dot(x_hi, wr_ref[...]) + _dot(x_lo, wr_ref[...])
    logits = p[:, :LANES] + p[:, LANES:] + br_ref[...]
    lane = lax.broadcasted_iota(jnp.int32, (TM, LANES), 1)
    lane_f = lane.astype(_F32)
    neg_inf = jnp.float32(-jnp.inf)
    logits = jnp.where(lane < N_EXPERTS, logits, neg_inf)

    vals, ids, onehots = [], [], []
    cur = logits
    for _ in range(TOP_K):
        m = jnp.max(cur, axis=-1, keepdims=True)
        idx = jnp.min(jnp.where(cur == m, lane_f, float(LANES)), axis=-1, keepdims=True)
        oh = lane_f == idx
        vals.append(m)
        ids.append(idx)
        onehots.append(oh)
        cur = jnp.where(oh, neg_inf, cur)
    exps = [jnp.exp(v - vals[0]) for v in vals]
    denom = exps[0] + exps[1] + exps[2] + exps[3]
    gates = [e / denom for e in exps]

    mask = (onehots[0] | onehots[1] | onehots[2] | onehots[3]).astype(_F32)
    row_t = lax.broadcasted_iota(jnp.int32, (TM, TM), 0)
    col_t = lax.broadcasted_iota(jnp.int32, (TM, TM), 1)
    strict_lower = (row_t > col_t).astype(_BF16)
    before = _dot(strict_lower, mask.astype(_BF16)) + carry[...]
    ranks = [jnp.sum(jnp.where(oh, before, 0.0), axis=-1, keepdims=True) for oh in onehots]
    carry[...] = carry[...] + jnp.sum(mask, axis=0, keepdims=True)
    counts_ref[...] = jnp.broadcast_to(carry[...], counts_ref.shape)

    route = jnp.zeros((TM, LANES), _F32)
    gate_out = jnp.zeros((TM, LANES), _F32)
    for k in range(TOP_K):
        route = jnp.where(lane == k, ids[k], route)
        route = jnp.where(lane == TOP_K + k, ranks[k], route)
        gate_out = jnp.where(lane == k, gates[k], gate_out)
    route_ref[...] = route.astype(jnp.int32)
    gates_ref[...] = gate_out


def _mixer_router(x2d, ln_mix_g, w_in_bf, conv_w, sgu_ln_g, sgu_ln_b, sgu_w, sgu_b_full,
                  gn_conv, gn_sgu, w_out_bf, ln_ffn_g, wr_split, br_pad):
    def full(a):
        return pl.BlockSpec(a.shape, lambda i: (0,) * a.ndim)

    row_blk = lambda w: pl.BlockSpec((TM, w), lambda i: (i, 0))
    ins = [x2d, ln_mix_g, w_in_bf, conv_w, sgu_ln_g, sgu_ln_b, sgu_w, sgu_b_full,
           gn_conv, gn_sgu, w_out_bf, ln_ffn_g, wr_split, br_pad]
    in_specs = [row_blk(D_MODEL)] + [full(a) for a in ins[1:]]
    return pl.pallas_call(
        _mixer_router_kernel,
        grid=(N_TOK // TM,),
        in_specs=in_specs,
        out_specs=[row_blk(D_MODEL), row_blk(D_MODEL), row_blk(LANES), row_blk(LANES),
                   pl.BlockSpec((SUBLANES, LANES), lambda i: (0, 0))],
        out_shape=[jax.ShapeDtypeStruct((N_TOK, D_MODEL), _F32),
                   jax.ShapeDtypeStruct((N_TOK, D_MODEL), _F32),
                   jax.ShapeDtypeStruct((N_TOK, LANES), jnp.int32),
                   jax.ShapeDtypeStruct((N_TOK, LANES), _F32),
                   jax.ShapeDtypeStruct((SUBLANES, LANES), _F32)],
        scratch_shapes=[pltpu.VMEM((TM + SUBLANES, D_CONV), _F32),
                        pltpu.VMEM((1, LANES), _F32),
                        pltpu.VMEM((TM, D_MODEL), _BF16)],
        compiler_params=pltpu.CompilerParams(dimension_semantics=("arbitrary",),
                                             vmem_limit_bytes=VMEM_LIMIT),
        name="mixer_router",
    )(*ins)


def _dispatch_kernel(dest_ref, zrow_ref, zflag_ref,
                     x_ref, bufx_hbm, zbuf, zsem, sem):
    i = pl.program_id(0)

    @pl.when(i == 0)
    def _():
        zbuf[...] = jnp.zeros_like(zbuf)

        def zero_copy(e):
            z0 = pl.multiple_of(zrow_ref[e], ROW_CHUNK)
            return pltpu.make_async_copy(zbuf, bufx_hbm.at[pl.ds(z0, ROW_CHUNK)], zsem)

        for e in range(N_EXPERTS):
            @pl.when(zflag_ref[e] > 0)
            def _(e=e):
                zero_copy(e).start()
        for e in range(N_EXPERTS):
            @pl.when(zflag_ref[e] > 0)
            def _(e=e):
                zero_copy(e).wait()

    def row_copy(t8, u, k):
        t = pl.multiple_of(t8 * SUBLANES, SUBLANES) + u
        dst = dest_ref[(i * TD + t) * TOP_K + k]
        return pltpu.make_async_copy(x_ref.at[pl.ds(t, 1)], bufx_hbm.at[pl.ds(dst, 1)], sem)

    def issue(t8, carry):
        for u in range(SUBLANES):
            for k in range(TOP_K):
                row_copy(t8, u, k).start()
        return carry

    def drain(t8, carry):
        for u in range(SUBLANES):
            for k in range(TOP_K):
                row_copy(t8, u, k).wait()
        return carry

    lax.fori_loop(0, TD // SUBLANES, issue, 0)
    lax.fori_loop(0, TD // SUBLANES, drain, 0)


def _dispatch(dest, zrow, zflag, xn2):
    return pl.pallas_call(
        _dispatch_kernel,
        grid_spec=pltpu.PrefetchScalarGridSpec(
            num_scalar_prefetch=3,
            grid=(N_TOK // TD,),
            in_specs=[pl.BlockSpec((TD, D_MODEL), lambda i, *_: (i, 0))],
            out_specs=pl.BlockSpec(memory_space=pl.ANY),
            scratch_shapes=[pltpu.VMEM((ROW_CHUNK, D_MODEL), _F32),
                            pltpu.SemaphoreType.DMA(()),
                            pltpu.SemaphoreType.DMA(())],
        ),
        out_shape=jax.ShapeDtypeStruct((MAX_SUPER * SUPER_ROWS, D_MODEL), _F32),
        compiler_params=pltpu.CompilerParams(dimension_semantics=("arbitrary",),
                                             vmem_limit_bytes=VMEM_LIMIT),
        name="dispatch",
    )(dest, zrow, zflag, xn2)


def _expert_kernel(sup_e_ref, sup_blk_ref, nchunk_ref, nhalf_ref,
                   x_hbm, wg_ref, wu_ref, bg_ref, bu_ref, wd_ref, bd_ref, out_hbm,
                   x_f32, acc, x_sem, o_sem, *, max_super):
    s = pl.program_id(0)
    j = pl.program_id(1)
    n_chunks = nchunk_ref[s]
    cur = s % 2

    def x_copy(sup, c):
        r = pl.multiple_of(sup_blk_ref[sup] * SUPER_ROWS + c * ROW_CHUNK, ROW_CHUNK)
        rows = pl.ds(pl.multiple_of(c * ROW_CHUNK, ROW_CHUNK), ROW_CHUNK)
        return pltpu.make_async_copy(x_hbm.at[pl.ds(r, ROW_CHUNK)], x_f32.at[sup % 2, rows],
                                     x_sem.at[c % X_SLOTS])

    def x_finish(sup, c):
        x_copy(sup, c).wait()

    def out_copy(sup, c):
        r = pl.multiple_of(sup_blk_ref[sup] * SUPER_ROWS + c * ROW_CHUNK, ROW_CHUNK)
        rows = pl.ds(pl.multiple_of(c * ROW_CHUNK, ROW_CHUNK), ROW_CHUNK)
        return pltpu.make_async_copy(acc.at[sup % 2, rows], out_hbm.at[pl.ds(r, ROW_CHUNK)],
                                     o_sem.at[sup % 2])

    def for_each_chunk(sup, fn):
        for c in range(CHUNKS_PER_SUPER):
            @pl.when(c < nchunk_ref[sup])
            def _(c=c):
                fn(sup, c)

    @pl.when((s == 0) & (j == 0))
    def _():
        def load(sup, c):
            x_copy(sup, c).start()
            x_finish(sup, c)
        for_each_chunk(0, load)

    @pl.when((s >= 2) & (j == 0))
    def _():
        for_each_chunk(jnp.maximum(s - 2, 0), lambda sup, c: out_copy(sup, c).wait())

    prev = jnp.maximum(s - 1, 0)

    @pl.when((s >= 1) & (j < nchunk_ref[prev]))
    def _():
        out_copy(prev, j).start()

    nxt = jnp.minimum(s + 1, max_super - 1)
    n_next = jnp.where(s + 1 < max_super, nchunk_ref[nxt], 0)

    @pl.when((j >= 1) & (j - 1 < n_next))
    def _():
        x_finish(nxt, j - 1)

    @pl.when(j < n_next)
    def _():
        x_copy(nxt, j).start()

    @pl.when(n_chunks > 0)
    def _():
        def partial_out(n_rows):
            rows = pl.ds(0, n_rows)
            xb = x_f32[cur, rows, :].astype(_BF16)
            gate = _dot(xb, wg_ref[0].astype(_BF16)) + bg_ref[0]
            up = _dot(xb, wu_ref[0].astype(_BF16)) + bu_ref[0]
            gate = jnp.minimum(gate, SWIGLU_LIMIT)
            up = jnp.clip(up, -SWIGLU_LIMIT, SWIGLU_LIMIT)
            glu = gate * jax.nn.sigmoid(gate * SWIGLU_ALPHA)
            act = ((up + 1.0) * glu).astype(_BF16)
            return rows, _dot(act, wd_ref[0].astype(_BF16))

        def sweep(update):
            n_half = nhalf_ref[s]
            for n_rows, (lo, hi) in ROW_VARIANTS.items():
                @pl.when((n_half >= lo) & (n_half <= hi))
                def _(n_rows=n_rows):
                    update(n_rows)

        def init(n_rows):
            rows, o = partial_out(n_rows)
            acc[cur, rows, :] = o + bd_ref[0]

        def accumulate(n_rows):
            rows, o = partial_out(n_rows)
            acc[cur, rows, :] += o

        @pl.when(j == 0)
        def _():
            sweep(init)

        @pl.when(j > 0)
        def _():
            sweep(accumulate)

    @pl.when((s == max_super - 1) & (j == N_FF_TILES - 1))
    def _():
        for_each_chunk(max_super - 1, lambda sup, c: out_copy(sup, c).start())
        for_each_chunk(max_super - 2, lambda sup, c: out_copy(sup, c).wait())
        for_each_chunk(max_super - 1, lambda sup, c: out_copy(sup, c).wait())


def _experts(max_super, sup_e, sup_blk, nchunk, nhalf, buf_x, w_gate_up, b_gate_up3, w_down, b_down3):
    def ff_tile(j, nchunk, s):
        return jnp.where(nchunk[s] > 0, j, N_FF_TILES - 1)

    x_spec = pl.BlockSpec(memory_space=pl.ANY)
    wg_spec = pl.BlockSpec((1, D_MODEL, FF_TILE),
                           lambda s, j, se, sb, nc, nh: (se[s], 0, ff_tile(j, nc, s)))
    wu_spec = pl.BlockSpec((1, D_MODEL, FF_TILE),
                           lambda s, j, se, sb, nc, nh: (se[s], 0, N_FF_TILES + ff_tile(j, nc, s)))
    bg_spec = pl.BlockSpec((1, 1, FF_TILE),
                           lambda s, j, se, sb, nc, nh: (se[s], 0, ff_tile(j, nc, s)))
    bu_spec = pl.BlockSpec((1, 1, FF_TILE),
                           lambda s, j, se, sb, nc, nh: (se[s], 0, N_FF_TILES + ff_tile(j, nc, s)))
    wd_spec = pl.BlockSpec((1, FF_TILE, D_MODEL),
                           lambda s, j, se, sb, nc, nh: (se[s], ff_tile(j, nc, s), 0))
    bd_spec = pl.BlockSpec((1, 1, D_MODEL), lambda s, j, se, sb, nc, nh: (se[s], 0, 0))
    out_spec = pl.BlockSpec(memory_space=pl.ANY)
    return pl.pallas_call(
        functools.partial(_expert_kernel, max_super=max_super),
        grid_spec=pltpu.PrefetchScalarGridSpec(
            num_scalar_prefetch=4,
            grid=(max_super, N_FF_TILES),
            in_specs=[x_spec, wg_spec, wu_spec, bg_spec, bu_spec, wd_spec, bd_spec],
            out_specs=out_spec,
            scratch_shapes=[pltpu.VMEM((2, SUPER_ROWS, D_MODEL), _F32),
                            pltpu.VMEM((2, SUPER_ROWS, D_MODEL), _F32),
                            pltpu.SemaphoreType.DMA((X_SLOTS,)),
                            pltpu.SemaphoreType.DMA((2,))],
        ),
        out_shape=jax.ShapeDtypeStruct((MAX_SUPER * SUPER_ROWS, D_MODEL), _F32),
        compiler_params=pltpu.CompilerParams(dimension_semantics=("arbitrary", "arbitrary"),
                                             vmem_limit_bytes=VMEM_LIMIT),
        name="experts",
    )(sup_e, sup_blk, nchunk, nhalf, buf_x, w_gate_up, w_gate_up, b_gate_up3, b_gate_up3, w_down, b_down3)


def _combine_kernel(dest_ref, h1_ref, gates_ref, lng_ref, y_hbm, out_ref, rows, sem):
    i = pl.program_id(0)
    n_steps = pl.num_programs(0)

    def row_copy(tile, slot, t8, u, k):
        t = pl.multiple_of(t8 * SUBLANES, SUBLANES) + u
        src = dest_ref[(tile * TC + t) * TOP_K + k]
        return pltpu.make_async_copy(y_hbm.at[pl.ds(src, 1)], rows.at[slot, k, pl.ds(t, 1)],
                                     sem.at[slot])

    def issue_tile(tile, slot):
        def body(t8, carry):
            for u in range(SUBLANES):
                for k in range(TOP_K):
                    row_copy(tile, slot, t8, u, k).start()
            return carry
        lax.fori_loop(0, TC // SUBLANES, body, 0)

    def wait_tile(tile, slot):
        def body(t8, carry):
            for u in range(SUBLANES):
                for k in range(TOP_K):
                    row_copy(tile, slot, t8, u, k).wait()
            return carry
        lax.fori_loop(0, TC // SUBLANES, body, 0)

    slot = i % 2

    @pl.when(i == 0)
    def _():
        issue_tile(0, 0)

    @pl.when(i + 1 < n_steps)
    def _():
        issue_tile(i + 1, 1 - slot)

    wait_tile(i, slot)

    g = gates_ref[...]
    acc = h1_ref[...]
    for k in range(TOP_K):
        acc = acc + rows[slot, k] * g[:, k:k + 1]
    out_ref[...] = _rms_rows(acc, lng_ref[...])


def _combine(dest, h1, gates, ln_final_g, y_sorted):
    return pl.pallas_call(
        _combine_kernel,
        grid_spec=pltpu.PrefetchScalarGridSpec(
            num_scalar_prefetch=1,
            grid=(N_TOK // TC,),
            in_specs=[pl.BlockSpec((TC, D_MODEL), lambda i, *_: (i, 0)),
                      pl.BlockSpec((TC, LANES), lambda i, *_: (i, 0)),
                      pl.BlockSpec((1, D_MODEL), lambda i, *_: (0, 0)),
                      pl.BlockSpec(memory_space=pl.ANY)],
            out_specs=pl.BlockSpec((TC, D_MODEL), lambda i, *_: (i, 0)),
            scratch_shapes=[pltpu.VMEM((2, TOP_K, TC, D_MODEL), _F32),
                            pltpu.SemaphoreType.DMA((2,))],
        ),
        out_shape=jax.ShapeDtypeStruct((N_TOK, D_MODEL), _F32),
        compiler_params=pltpu.CompilerParams(dimension_semantics=("arbitrary",),
                                             vmem_limit_bytes=VMEM_LIMIT),
        name="combine",
    )(dest, h1, gates, ln_final_g, y_sorted)


def kernel(x, ln_mix_g, w_in, conv_w, sgu_ln_g, sgu_ln_b, sgu_w, sgu_b, gn_conv, gn_sgu, w_out,
           ln_ffn_g, w_router, b_router, w_gate_up, b_gate_up, w_down, b_down, ln_final_g):
    row = lambda v: v.reshape(1, -1)
    x2d = x.reshape(N_TOK, D_MODEL)

    sgu_b_full = jnp.repeat(jnp.transpose(sgu_b), SGU_HEAD_DIM, axis=1)
    wr_hi = w_router.astype(_BF16)
    wr_lo = (w_router - wr_hi.astype(_F32)).astype(_BF16)
    pad = ((0, 0), (0, LANES - N_EXPERTS))
    wr_split = jnp.concatenate([jnp.pad(wr_hi, pad), jnp.pad(wr_lo, pad)], axis=1)
    br_pad = jnp.pad(b_router, (0, LANES - N_EXPERTS)).reshape(1, LANES)

    h1, xn2, route, gates, counts = _mixer_router(
        x2d, row(ln_mix_g), w_in.astype(_BF16), conv_w, row(sgu_ln_g), row(sgu_ln_b), sgu_w,
        sgu_b_full, row(gn_conv), row(gn_sgu), w_out.astype(_BF16), row(ln_ffn_g), wr_split, br_pad)

    sizes = counts[0, :N_EXPERTS].astype(jnp.int32)
    n_sup = (sizes + SUPER_ROWS - 1) // SUPER_ROWS
    sup_end = jnp.cumsum(n_sup)
    sup_start = sup_end - n_sup
    total_sup = sup_end[-1]
    row_start = (sup_start * SUPER_ROWS).astype(jnp.int32)
    sid = jnp.arange(MAX_SUPER, dtype=jnp.int32)
    sid_eff = jnp.minimum(sid, total_sup - 1)
    sup_e = jnp.minimum(jnp.searchsorted(sup_end, sid_eff, side="right"),
                        N_EXPERTS - 1).astype(jnp.int32)
    rows_left = sizes[sup_e] - (sid_eff - sup_start[sup_e]) * SUPER_ROWS
    valid = jnp.clip(rows_left, 0, SUPER_ROWS)
    nchunk = jnp.where(sid < total_sup, (valid + ROW_CHUNK - 1) // ROW_CHUNK, 0).astype(jnp.int32)
    nhalf = jnp.where(sid < total_sup, (valid + HALF_CHUNK - 1) // HALF_CHUNK, 0).astype(jnp.int32)
    zflag = (sizes > 0).astype(jnp.int32)
    zrow = (row_start + ((jnp.maximum(sizes, 1) - 1) // ROW_CHUNK) * ROW_CHUNK).astype(jnp.int32)
    eid = route[:, 0:TOP_K]
    eid_start = jnp.sum(jnp.where(eid[..., None] == jnp.arange(N_EXPERTS, dtype=jnp.int32),
                                  row_start, 0), axis=-1)
    dest = (eid_start + route[:, TOP_K:2 * TOP_K]).reshape(-1)

    buf_x = _dispatch(dest, zrow, zflag, xn2)
    expert_args = (sup_e, sid_eff.astype(jnp.int32), nchunk, nhalf, buf_x, w_gate_up,
                   b_gate_up.reshape(N_EXPERTS, 1, 2 * D_FF), w_down,
                   b_down.reshape(N_EXPERTS, 1, D_MODEL))
    y_sorted = lax.cond(total_sup <= N_EXPERTS,
                        functools.partial(_experts, N_EXPERTS),
                        functools.partial(_experts, MAX_SUPER), *expert_args)
    out = _combine(dest, h1, gates, row(ln_final_g), y_sorted)
    return out.reshape(BATCH, SEQ, D_MODEL)
```

```python
import functools

import jax
import jax.numpy as jnp
from jax import lax
from jax.experimental import pallas as pl
from jax.experimental.pallas import tpu as pltpu

D_MODEL = 2048
BATCH = 2
SEQ = 4096
N_TOK = BATCH * SEQ

D_CONV = 1024
N_CONV_GROUPS = 8
CONV_WIDTH = 3
D_SGU = 1024
N_SGU_HEADS = 8
SGU_HEAD_DIM = 128
CHUNK = 128
D_IN_PROJ = 3 * D_CONV + 2 * D_SGU

N_EXPERTS = 32
TOP_K = 4
D_FF = 2048
SWIGLU_LIMIT = 7.0
SWIGLU_ALPHA = 1.702
RMS_EPS = 1e-5
LN_EPS = 1e-5

LANES = 128
SUBLANES = 8
VMEM_LIMIT = 56 * 1024 * 1024

TM = 256
SUPER_ROWS = 1280
ROW_CHUNK = 256
FF_TILE = 256
N_FF_TILES = D_FF // FF_TILE
MAX_SUPER = (N_TOK * TOP_K + N_EXPERTS * (SUPER_ROWS - 1)) // SUPER_ROWS
CHUNKS_PER_SUPER = SUPER_ROWS // ROW_CHUNK
assert CHUNKS_PER_SUPER + 1 <= N_FF_TILES
X_SLOTS = 2
HALF_CHUNK = ROW_CHUNK // 2
FINE_FROM = 7


def _row_variants():
    variants = {}
    for n_half in range(1, SUPER_ROWS // HALF_CHUNK + 1):
        if n_half >= FINE_FROM:
            n_rows = n_half * HALF_CHUNK
        else:
            n_rows = -(-n_half // 2) * ROW_CHUNK
        lo, hi = variants.get(n_rows, (n_half, n_half))
        variants[n_rows] = (min(lo, n_half), max(hi, n_half))
    return variants


ROW_VARIANTS = _row_variants()
TD = 256
IN_SLOTS = 3
assert N_TOK // TD >= 2
TC = 256

_F32 = jnp.float32
_BF16 = jnp.bfloat16


def _dot(a, b):
    return jnp.dot(a, b, preferred_element_type=_F32)


def _gelu_exact(x):
    return 0.5 * x * (1.0 + lax.erf(x * (2.0 ** -0.5)))


def _rms_rows(x, gain):
    return x * lax.rsqrt(jnp.mean(x * x, axis=-1, keepdims=True) + RMS_EPS) * gain


def _mixer_router_kernel(x_ref, lng_ref, win_ref, convw_ref, slg_ref, slb_ref, sw_ref, sb_ref,
                         gnc_ref, gns_ref, wout_ref, lnf_ref, wr_ref, br_ref,
                         h1_ref, xn2_ref, route_ref, gates_ref, counts_ref,
                         cbuf, carry, ybuf):
    i = pl.program_id(0)

    @pl.when(i % (SEQ // TM) == 0)
    def _():
        cbuf[0:SUBLANES, :] = jnp.zeros((SUBLANES, D_CONV), _F32)

    @pl.when(i == 0)
    def _():
        carry[...] = jnp.zeros_like(carry)

    x = x_ref[...]
    xn = _rms_rows(x, lng_ref[...]).astype(_BF16)

    b_gate = _dot(xn, win_ref[:, 0:D_CONV])
    c_gate = _dot(xn, win_ref[:, D_CONV:2 * D_CONV])
    hh = _dot(xn, win_ref[:, 2 * D_CONV:3 * D_CONV])
    ch = c_gate * hh
    cbuf[SUBLANES:SUBLANES + TM, :] = ch
    ch1 = cbuf[SUBLANES - 1:SUBLANES - 1 + TM, :]
    ch2 = cbuf[SUBLANES - 2:SUBLANES - 2 + TM, :]
    conv = convw_ref[0:1, :] * ch2 + convw_ref[1:2, :] * ch1 + convw_ref[2:3, :] * ch
    cbuf[0:SUBLANES, :] = cbuf[TM:TM + SUBLANES, :]
    y_conv = b_gate * conv
    for g in range(N_CONV_GROUPS):
        sl = slice(g * LANES, (g + 1) * LANES)
        blk = y_conv[:, sl]
        ms = jnp.mean(blk * blk, axis=-1, keepdims=True)
        ybuf[:, sl] = (blk * lax.rsqrt(ms + RMS_EPS) * gnc_ref[:, sl]).astype(_BF16)

    gu = _gelu_exact(_dot(xn, win_ref[:, 3 * D_CONV:3 * D_CONV + D_SGU]))
    gv = _gelu_exact(_dot(xn, win_ref[:, 3 * D_CONV + D_SGU:D_IN_PROJ]))
    row_c = lax.broadcasted_iota(jnp.int32, (CHUNK, CHUNK), 0)
    col_c = lax.broadcasted_iota(jnp.int32, (CHUNK, CHUNK), 1)
    causal = row_c >= col_c
    for h in range(N_SGU_HEADS):
        sl = slice(h * SGU_HEAD_DIM, (h + 1) * SGU_HEAD_DIM)
        vh = gv[:, sl]
        mu = jnp.mean(vh, axis=-1, keepdims=True)
        xc = vh - mu
        var = jnp.mean(xc * xc, axis=-1, keepdims=True)
        vn = (xc * lax.rsqrt(var + LN_EPS) * slg_ref[:, sl] + slb_ref[:, sl]).astype(_BF16)
        wm = jnp.where(causal, sw_ref[h], 0.0).astype(_BF16)
        for c in range(TM // CHUNK):
            rows = slice(c * CHUNK, (c + 1) * CHUNK)
            mixed = _dot(wm, vn[rows, :]) + sb_ref[:, sl]
            ys = gu[rows, sl] * mixed
            ms = jnp.mean(ys * ys, axis=-1, keepdims=True)
            ybuf[rows, D_CONV + h * SGU_HEAD_DIM:D_CONV + (h + 1) * SGU_HEAD_DIM] = (
                ys * lax.rsqrt(ms + RMS_EPS) * gns_ref[:, sl]).astype(_BF16)

    h1 = x + _dot(ybuf[...], wout_ref[...])
    h1_ref[...] = h1
    xn2 = _rms_rows(h1, lnf_ref[...])
    xn2_ref[...] = xn2

    x_hi = xn2.astype(_BF16)
    x_lo = (xn2 - x_hi.astype(_F32)).astype(_BF16)
    p = _dot(x_hi, wr_ref[...]) + _dot(x_lo, wr_ref[...])
    logits = p[:, :LANES] + p[:, LANES:] + br_ref[...]
    lane = lax.broadcasted_iota(jnp.int32, (TM, LANES), 1)
    lane_f = lane.astype(_F32)
    neg_inf = jnp.float32(-jnp.inf)
    logits = jnp.where(lane < N_EXPERTS, logits, neg_inf)

    vals, ids, onehots = [], [], []
    cur = logits
    for _ in range(TOP_K):
        m = jnp.max(cur, axis=-1, keepdims=True)
        idx = jnp.min(jnp.where(cur == m, lane_f, float(LANES)), axis=-1, keepdims=True)
        oh = lane_f == idx
        vals.append(m)
        ids.append(idx)
        onehots.append(oh)
        cur = jnp.where(oh, neg_inf, cur)
    exps = [jnp.exp(v - vals[0]) for v in vals]
    denom = exps[0] + exps[1] + exps[2] + exps[3]
    gates = [e / denom for e in exps]

    mask = (onehots[0] | onehots[1] | onehots[2] | onehots[3]).astype(_F32)
    row_t = lax.broadcasted_iota(jnp.int32, (TM, TM), 0)
    col_t = lax.broadcasted_iota(jnp.int32, (TM, TM), 1)
    strict_lower = (row_t > col_t).astype(_BF16)
    before = _dot(strict_lower, mask.astype(_BF16)) + carry[...]
    ranks = [jnp.sum(jnp.where(oh, before, 0.0), axis=-1, keepdims=True) for oh in onehots]
    carry[...] = carry[...] + jnp.sum(mask, axis=0, keepdims=True)
    counts_ref[...] = jnp.broadcast_to(carry[...], counts_ref.shape)

    route = jnp.zeros((TM, LANES), _F32)
    gate_out = jnp.zeros((TM, LANES), _F32)
    for k in range(TOP_K):
        route = jnp.where(lane == k, ids[k], route)
        route = jnp.where(lane == TOP_K + k, ranks[k], route)
        gate_out = jnp.where(lane == k, gates[k], gate_out)
    route_ref[...] = route.astype(jnp.int32)
    gates_ref[...] = gate_out


def _mixer_router(x2d, ln_mix_g, w_in_bf, conv_w, sgu_ln_g, sgu_ln_b, sgu_w, sgu_b_full,
                  gn_conv, gn_sgu, w_out_bf, ln_ffn_g, wr_split, br_pad):
    def full(a):
        return pl.BlockSpec(a.shape, lambda i: (0,) * a.ndim)

    row_blk = lambda w: pl.BlockSpec((TM, w), lambda i: (i, 0))
    ins = [x2d, ln_mix_g, w_in_bf, conv_w, sgu_ln_g, sgu_ln_b, sgu_w, sgu_b_full,
           gn_conv, gn_sgu, w_out_bf, ln_ffn_g, wr_split, br_pad]
    in_specs = [row_blk(D_MODEL)] + [full(a) for a in ins[1:]]
    return pl.pallas_call(
        _mixer_router_kernel,
        grid=(N_TOK // TM,),
        in_specs=in_specs,
        out_specs=[row_blk(D_MODEL), row_blk(D_MODEL), row_blk(LANES), row_blk(LANES),
                   pl.BlockSpec((SUBLANES, LANES), lambda i: (0, 0))],
        out_shape=[jax.ShapeDtypeStruct((N_TOK, D_MODEL), _F32),
                   jax.ShapeDtypeStruct((N_TOK, D_MODEL), _F32),
                   jax.ShapeDtypeStruct((N_TOK, LANES), jnp.int32),
                   jax.ShapeDtypeStruct((N_TOK, LANES), _F32),
                   jax.ShapeDtypeStruct((SUBLANES, LANES), _F32)],
        scratch_shapes=[pltpu.VMEM((TM + SUBLANES, D_CONV), _F32),
                        pltpu.VMEM((1, LANES), _F32),
                        pltpu.VMEM((TM, D_MODEL), _BF16)],
        compiler_params=pltpu.CompilerParams(dimension_semantics=("arbitrary",),
                                             vmem_limit_bytes=VMEM_LIMIT),
        name="mixer_router",
    )(*ins)


def _dispatch_kernel(dest_ref, zrow_ref, zflag_ref,
                     x_hbm, bufx_hbm, x_in, zbuf, zsem, in_sem, out_sem):
    i = pl.program_id(0)
    n_steps = pl.num_programs(0)

    @pl.when(i == 0)
    def _():
        zbuf[...] = jnp.zeros_like(zbuf)

        def zero_copy(e):
            z0 = pl.multiple_of(zrow_ref[e], ROW_CHUNK)
            return pltpu.make_async_copy(zbuf, bufx_hbm.at[pl.ds(z0, ROW_CHUNK)], zsem)

        for e in range(N_EXPERTS):
            @pl.when(zflag_ref[e] > 0)
            def _(e=e):
                zero_copy(e).start()
        for e in range(N_EXPERTS):
            @pl.when(zflag_ref[e] > 0)
            def _(e=e):
                zero_copy(e).wait()

    def fetch(blk, slot):
        r = pl.multiple_of(blk * TD, TD)
        return pltpu.make_async_copy(x_hbm.at[pl.ds(r, TD)], x_in.at[slot], in_sem.at[slot])

    def row_copy(blk, slot, t8, u, k):
        t = pl.multiple_of(t8 * SUBLANES, SUBLANES) + u
        dst = dest_ref[(blk * TD + t) * TOP_K + k]
        return pltpu.make_async_copy(x_in.at[slot, pl.ds(t, 1)], bufx_hbm.at[pl.ds(dst, 1)],
                                     out_sem.at[blk % 2])

    def scatter(blk, slot, start):
        def body(t8, carry):
            for u in range(SUBLANES):
                for k in range(TOP_K):
                    copy = row_copy(blk, slot, t8, u, k)
                    copy.start() if start else copy.wait()
            return carry
        lax.fori_loop(0, TD // SUBLANES, body, 0)

    def in_slot_of(blk, fn):
        for q in range(IN_SLOTS):
            @pl.when(blk % IN_SLOTS == q)
            def _(q=q):
                fn(q)

    @pl.when(i == 0)
    def _():
        fetch(0, 0).start()
        fetch(1, 1).start()

    in_slot_of(i, lambda q: fetch(i, q).wait())
    in_slot_of(i, lambda q: scatter(i, q, True))

    @pl.when(i >= 1)
    def _():
        in_slot_of(i - 1, lambda q: scatter(i - 1, q, False))

    @pl.when(i + 2 < n_steps)
    def _():
        in_slot_of(i + 2, lambda q: fetch(i + 2, q).start())

    @pl.when(i == n_steps - 1)
    def _():
        in_slot_of(i, lambda q: scatter(i, q, False))


def _dispatch(dest, zrow, zflag, xn2):
    return pl.pallas_call(
        _dispatch_kernel,
        grid_spec=pltpu.PrefetchScalarGridSpec(
            num_scalar_prefetch=3,
            grid=(N_TOK // TD,),
            in_specs=[pl.BlockSpec(memory_space=pl.ANY)],
            out_specs=pl.BlockSpec(memory_space=pl.ANY),
            scratch_shapes=[pltpu.VMEM((IN_SLOTS, TD, D_MODEL), _F32),
                            pltpu.VMEM((ROW_CHUNK, D_MODEL), _F32),
                            pltpu.SemaphoreType.DMA(()),
                            pltpu.SemaphoreType.DMA((IN_SLOTS,)),
                            pltpu.SemaphoreType.DMA((2,))],
        ),
        out_shape=jax.ShapeDtypeStruct((MAX_SUPER * SUPER_ROWS, D_MODEL), _F32),
        compiler_params=pltpu.CompilerParams(dimension_semantics=("arbitrary",),
                                             vmem_limit_bytes=VMEM_LIMIT),
        name="dispatch",
    )(dest, zrow, zflag, xn2)


def _expert_kernel(sup_e_ref, sup_blk_ref, nchunk_ref, nhalf_ref,
                   x_hbm, wg_ref, wu_ref, bg_ref, bu_ref, wd_ref, bd_ref, out_hbm,
                   x_stage, x_bf, acc, x_sem, o_sem, *, max_super):
    s = pl.program_id(0)
    j = pl.program_id(1)
    n_chunks = nchunk_ref[s]
    cur = s % 2

    def x_copy(sup, c):
        r = pl.multiple_of(sup_blk_ref[sup] * SUPER_ROWS + c * ROW_CHUNK, ROW_CHUNK)
        slot = c % X_SLOTS
        return pltpu.make_async_copy(x_hbm.at[pl.ds(r, ROW_CHUNK)], x_stage.at[slot],
                                     x_sem.at[slot])

    def x_finish(sup, c):
        x_copy(sup, c).wait()
        rows = pl.ds(pl.multiple_of(c * ROW_CHUNK, ROW_CHUNK), ROW_CHUNK)
        x_bf[sup % 2, rows, :] = x_stage[c % X_SLOTS].astype(_BF16)

    def out_copy(sup, c):
        r = pl.multiple_of(sup_blk_ref[sup] * SUPER_ROWS + c * ROW_CHUNK, ROW_CHUNK)
        rows = pl.ds(pl.multiple_of(c * ROW_CHUNK, ROW_CHUNK), ROW_CHUNK)
        return pltpu.make_async_copy(acc.at[sup % 2, rows], out_hbm.at[pl.ds(r, ROW_CHUNK)],
                                     o_sem.at[sup % 2])

    def for_each_chunk(sup, fn):
        for c in range(CHUNKS_PER_SUPER):
            @pl.when(c < nchunk_ref[sup])
            def _(c=c):
                fn(sup, c)

    @pl.when((s == 0) & (j == 0))
    def _():
        def load(sup, c):
            x_copy(sup, c).start()
            x_finish(sup, c)
        for_each_chunk(0, load)

    @pl.when((s >= 2) & (j == 0))
    def _():
        for_each_chunk(jnp.maximum(s - 2, 0), lambda sup, c: out_copy(sup, c).wait())

    prev = jnp.maximum(s - 1, 0)

    @pl.when((s >= 1) & (j < nchunk_ref[prev]))
    def _():
        out_copy(prev, j).start()

    nxt = jnp.minimum(s + 1, max_super - 1)
    n_next = jnp.where(s + 1 < max_super, nchunk_ref[nxt], 0)

    @pl.when((j >= 1) & (j - 1 < n_next))
    def _():
        x_finish(nxt, j - 1)

    @pl.when(j < n_next)
    def _():
        x_copy(nxt, j).start()

    @pl.when(n_chunks > 0)
    def _():
        def partial_out(n_rows):
            rows = pl.ds(0, n_rows)
            xb = x_bf[cur, rows, :]
            gate = _dot(xb, wg_ref[0].astype(_BF16)) + bg_ref[0]
            up = _dot(xb, wu_ref[0].astype(_BF16)) + bu_ref[0]
            gate = jnp.minimum(gate, SWIGLU_LIMIT)
            up = jnp.clip(up, -SWIGLU_LIMIT, SWIGLU_LIMIT)
            glu = gate * jax.nn.sigmoid(gate * SWIGLU_ALPHA)
            act = ((up + 1.0) * glu).astype(_BF16)
            return rows, _dot(act, wd_ref[0].astype(_BF16))

        def sweep(update):
            n_half = nhalf_ref[s]
            for n_rows, (lo, hi) in ROW_VARIANTS.items():
                @pl.when((n_half >= lo) & (n_half <= hi))
                def _(n_rows=n_rows):
                    update(n_rows)

        def init(n_rows):
            rows, o = partial_out(n_rows)
            acc[cur, rows, :] = o + bd_ref[0]

        def accumulate(n_rows):
            rows, o = partial_out(n_rows)
            acc[cur, rows, :] += o

        @pl.when(j == 0)
        def _():
            sweep(init)

        @pl.when(j > 0)
        def _():
            sweep(accumulate)

    @pl.when((s == max_super - 1) & (j == N_FF_TILES - 1))
    def _():
        for_each_chunk(max_super - 1, lambda sup, c: out_copy(sup, c).start())
        for_each_chunk(max_super - 2, lambda sup, c: out_copy(sup, c).wait())
        for_each_chunk(max_super - 1, lambda sup, c: out_copy(sup, c).wait())


def _experts(max_super, sup_e, sup_blk, nchunk, nhalf, buf_x, w_gate_up, b_gate_up3, w_down, b_down3):
    def ff_tile(j, nchunk, s):
        return jnp.where(nchunk[s] > 0, j, N_FF_TILES - 1)

    x_spec = pl.BlockSpec(memory_space=pl.ANY)
    wg_spec = pl.BlockSpec((1, D_MODEL, FF_TILE),
                           lambda s, j, se, sb, nc, nh: (se[s], 0, ff_tile(j, nc, s)))
    wu_spec = pl.BlockSpec((1, D_MODEL, FF_TILE),
                           lambda s, j, se, sb, nc, nh: (se[s], 0, N_FF_TILES + ff_tile(j, nc, s)))
    bg_spec = pl.BlockSpec((1, 1, FF_TILE),
                           lambda s, j, se, sb, nc, nh: (se[s], 0, ff_tile(j, nc, s)))
    bu_spec = pl.BlockSpec((1, 1, FF_TILE),
                           lambda s, j, se, sb, nc, nh: (se[s], 0, N_FF_TILES + ff_tile(j, nc, s)))
    wd_spec = pl.BlockSpec((1, FF_TILE, D_MODEL),
                           lambda s, j, se, sb, nc, nh: (se[s], ff_tile(j, nc, s), 0))
    bd_spec = pl.BlockSpec((1, 1, D_MODEL), lambda s, j, se, sb, nc, nh: (se[s], 0, 0))
    out_spec = pl.BlockSpec(memory_space=pl.ANY)
    return pl.pallas_call(
        functools.partial(_expert_kernel, max_super=max_super),
        grid_spec=pltpu.PrefetchScalarGridSpec(
            num_scalar_prefetch=4,
            grid=(max_super, N_FF_TILES),
            in_specs=[x_spec, wg_spec, wu_spec, bg_spec, bu_spec, wd_spec, bd_spec],
            out_specs=out_spec,
            scratch_shapes=[pltpu.VMEM((X_SLOTS, ROW_CHUNK, D_MODEL), _F32),
                            pltpu.VMEM((2, SUPER_ROWS, D_MODEL), _BF16),
                            pltpu.VMEM((2, SUPER_ROWS, D_MODEL), _F32),
                            pltpu.SemaphoreType.DMA((X_SLOTS,)),
                            pltpu.SemaphoreType.DMA((2,))],
        ),
        out_shape=jax.ShapeDtypeStruct((MAX_SUPER * SUPER_ROWS, D_MODEL), _F32),
        compiler_params=pltpu.CompilerParams(dimension_semantics=("arbitrary", "arbitrary"),
                                             vmem_limit_bytes=VMEM_LIMIT),
        name="experts",
    )(sup_e, sup_blk, nchunk, nhalf, buf_x, w_gate_up, w_gate_up, b_gate_up3, b_gate_up3, w_down, b_down3)


def _combine_kernel(dest_ref, h1_ref, gates_ref, lng_ref, y_hbm, out_ref, rows, sem):
    i = pl.program_id(0)
    n_steps = pl.num_programs(0)

    def row_copy(tile, slot, t8, u, k):
        t = pl.multiple_of(t8 * SUBLANES, SUBLANES) + u
        src = dest_ref[(tile * TC + t) * TOP_K + k]
        return pltpu.make_async_copy(y_hbm.at[pl.ds(src, 1)], rows.at[slot, k, pl.ds(t, 1)],
                                     sem.at[slot])

    def issue_tile(tile, slot):
        def body(t8, carry):
            for u in range(SUBLANES):
                for k in range(TOP_K):
                    row_copy(tile, slot, t8, u, k).start()
            return carry
        lax.fori_loop(0, TC // SUBLANES, body, 0)

    def wait_tile(tile, slot):
        def body(t8, carry):
            for u in range(SUBLANES):
                for k in range(TOP_K):
                    row_copy(tile, slot, t8, u, k).wait()
            return carry
        lax.fori_loop(0, TC // SUBLANES, body, 0)

    @pl.when(i == 0)
    def _():
        issue_tile(0, 0)

    for slot in range(2):
        @pl.when(i % 2 == slot)
        def _(slot=slot):
            @pl.when(i + 1 < n_steps)
            def _():
                issue_tile(i + 1, 1 - slot)

            wait_tile(i, slot)
            g = gates_ref[...]
            acc = h1_ref[...]
            for k in range(TOP_K):
                acc = acc + rows[slot, k] * g[:, k:k + 1]
            out_ref[...] = _rms_rows(acc, lng_ref[...])


def _combine(dest, h1, gates, ln_final_g, y_sorted):
    return pl.pallas_call(
        _combine_kernel,
        grid_spec=pltpu.PrefetchScalarGridSpec(
            num_scalar_prefetch=1,
            grid=(N_TOK // TC,),
            in_specs=[pl.BlockSpec((TC, D_MODEL), lambda i, *_: (i, 0)),
                      pl.BlockSpec((TC, LANES), lambda i, *_: (i, 0)),
                      pl.BlockSpec((1, D_MODEL), lambda i, *_: (0, 0)),
                      pl.BlockSpec(memory_space=pl.ANY)],
            out_specs=pl.BlockSpec((TC, D_MODEL), lambda i, *_: (i, 0)),
            scratch_shapes=[pltpu.VMEM((2, TOP_K, TC, D_MODEL), _F32),
                            pltpu.SemaphoreType.DMA((2,))],
        ),
        out_shape=jax.ShapeDtypeStruct((N_TOK, D_MODEL), _F32),
        compiler_params=pltpu.CompilerParams(dimension_semantics=("arbitrary",),
                                             vmem_limit_bytes=VMEM_LIMIT),
        name="combine",
    )(dest, h1, gates, ln_final_g, y_sorted)


def kernel(x, ln_mix_g, w_in, conv_w, sgu_ln_g, sgu_ln_b, sgu_w, sgu_b, gn_conv, gn_sgu, w_out,
           ln_ffn_g, w_router, b_router, w_gate_up, b_gate_up, w_down, b_down, ln_final_g):
    row = lambda v: v.reshape(1, -1)
    x2d = x.reshape(N_TOK, D_MODEL)

    sgu_b_full = jnp.repeat(jnp.transpose(sgu_b), SGU_HEAD_DIM, axis=1)
    wr_hi = w_router.astype(_BF16)
    wr_lo = (w_router - wr_hi.astype(_F32)).astype(_BF16)
    pad = ((0, 0), (0, LANES - N_EXPERTS))
    wr_split = jnp.concatenate([jnp.pad(wr_hi, pad), jnp.pad(wr_lo, pad)], axis=1)
    br_pad = jnp.pad(b_router, (0, LANES - N_EXPERTS)).reshape(1, LANES)

    h1, xn2, route, gates, counts = _mixer_router(
        x2d, row(ln_mix_g), w_in.astype(_BF16), conv_w, row(sgu_ln_g), row(sgu_ln_b), sgu_w,
        sgu_b_full, row(gn_conv), row(gn_sgu), w_out.astype(_BF16), row(ln_ffn_g), wr_split, br_pad)

    sizes = counts[0, :N_EXPERTS].astype(jnp.int32)
    n_sup = (sizes + SUPER_ROWS - 1) // SUPER_ROWS
    sup_end = jnp.cumsum(n_sup)
    sup_start = sup_end - n_sup
    total_sup = sup_end[-1]
    row_start = (sup_start * SUPER_ROWS).astype(jnp.int32)
    sid = jnp.arange(MAX_SUPER, dtype=jnp.int32)
    sid_eff = jnp.minimum(sid, total_sup - 1)
    sup_e = jnp.minimum(jnp.searchsorted(sup_end, sid_eff, side="right"),
                        N_EXPERTS - 1).astype(jnp.int32)
    rows_left = sizes[sup_e] - (sid_eff - sup_start[sup_e]) * SUPER_ROWS
    valid = jnp.clip(rows_left, 0, SUPER_ROWS)
    nchunk = jnp.where(sid < total_sup, (valid + ROW_CHUNK - 1) // ROW_CHUNK, 0).astype(jnp.int32)
    nhalf = jnp.where(sid < total_sup, (valid + HALF_CHUNK - 1) // HALF_CHUNK, 0).astype(jnp.int32)
    zflag = (sizes > 0).astype(jnp.int32)
    zrow = (row_start + ((jnp.maximum(sizes, 1) - 1) // ROW_CHUNK) * ROW_CHUNK).astype(jnp.int32)
    eid = route[:, 0:TOP_K]
    eid_start = jnp.sum(jnp.where(eid[..., None] == jnp.arange(N_EXPERTS, dtype=jnp.int32),
                                  row_start, 0), axis=-1)
    dest = (eid_start + route[:, TOP_K:2 * TOP_K]).reshape(-1)

    buf_x = _dispatch(dest, zrow, zflag, xn2)
    expert_args = (sup_e, sid_eff.astype(jnp.int32), nchunk, nhalf, buf_x, w_gate_up,
                   b_gate_up.reshape(N_EXPERTS, 1, 2 * D_FF), w_down,
                   b_down.reshape(N_EXPERTS, 1, D_MODEL))
    y_sorted = lax.cond(total_sup <= N_EXPERTS,
                        functools.partial(_experts, N_EXPERTS),
                        functools.partial(_experts, MAX_SUPER), *expert_args)
    out = _combine(dest, h1, gates, row(ln_final_g), y_sorted)
    return out.reshape(BATCH, SEQ, D_MODEL)
```

```python
import functools

import jax
import jax.numpy as jnp
from jax import lax
from jax.experimental import pallas as pl
from jax.experimental.pallas import tpu as pltpu

D_MODEL = 2048
BATCH = 2
SEQ = 4096
N_TOK = BATCH * SEQ

D_CONV = 1024
N_CONV_GROUPS = 8
CONV_WIDTH = 3
D_SGU = 1024
N_SGU_HEADS = 8
SGU_HEAD_DIM = 128
CHUNK = 128
D_IN_PROJ = 3 * D_CONV + 2 * D_SGU

N_EXPERTS = 32
TOP_K = 4
D_FF = 2048
SWIGLU_LIMIT = 7.0
SWIGLU_ALPHA = 1.702
RMS_EPS = 1e-5
LN_EPS = 1e-5

LANES = 128
SUBLANES = 8
VMEM_LIMIT = 56 * 1024 * 1024

TM = 256
SUPER_ROWS = 1536
ROW_CHUNK = 256
FF_TILE = 256
N_FF_TILES = D_FF // FF_TILE
MAX_SUPER = (N_TOK * TOP_K + N_EXPERTS * (SUPER_ROWS - 1)) // SUPER_ROWS
SHORT_GRID = N_EXPERTS + 4
CHUNKS_PER_SUPER = SUPER_ROWS // ROW_CHUNK
assert CHUNKS_PER_SUPER + 1 <= N_FF_TILES
X_SLOTS = 2
HALF_CHUNK = ROW_CHUNK // 2
FINE_FROM = 7


def _row_variants():
    variants = {}
    for n_half in range(1, SUPER_ROWS // HALF_CHUNK + 1):
        if n_half >= FINE_FROM:
            n_rows = n_half * HALF_CHUNK
        else:
            n_rows = -(-n_half // 2) * ROW_CHUNK
        lo, hi = variants.get(n_rows, (n_half, n_half))
        variants[n_rows] = (min(lo, n_half), max(hi, n_half))
    return variants


ROW_VARIANTS = _row_variants()
TD = 256
IN_SLOTS = 3
assert N_TOK // TD >= 2
TC = 256

_F32 = jnp.float32
_BF16 = jnp.bfloat16


def _dot(a, b):
    return jnp.dot(a, b, preferred_element_type=_F32)


def _gelu_exact(x):
    return 0.5 * x * (1.0 + lax.erf(x * (2.0 ** -0.5)))


def _rms_rows(x, gain):
    return x * lax.rsqrt(jnp.mean(x * x, axis=-1, keepdims=True) + RMS_EPS) * gain


def _mixer_router_kernel(x_ref, lng_ref, win_ref, convw_ref, slg_ref, slb_ref, sw_ref, sb_ref,
                         gnc_ref, gns_ref, wout_ref, lnf_ref, wr_ref, br_ref,
                         h1_ref, xn2_ref, route_ref, gates_ref, counts_ref,
                         cbuf, carry, ybuf):
    i = pl.program_id(0)

    @pl.when(i % (SEQ // TM) == 0)
    def _():
        cbuf[0:SUBLANES, :] = jnp.zeros((SUBLANES, D_CONV), _F32)

    @pl.when(i == 0)
    def _():
        carry[...] = jnp.zeros_like(carry)

    x = x_ref[...]
    xn = _rms_rows(x, lng_ref[...]).astype(_BF16)

    b_gate = _dot(xn, win_ref[:, 0:D_CONV])
    c_gate = _dot(xn, win_ref[:, D_CONV:2 * D_CONV])
    hh = _dot(xn, win_ref[:, 2 * D_CONV:3 * D_CONV])
    ch = c_gate * hh
    cbuf[SUBLANES:SUBLANES + TM, :] = ch
    ch1 = cbuf[SUBLANES - 1:SUBLANES - 1 + TM, :]
    ch2 = cbuf[SUBLANES - 2:SUBLANES - 2 + TM, :]
    conv = convw_ref[0:1, :] * ch2 + convw_ref[1:2, :] * ch1 + convw_ref[2:3, :] * ch
    cbuf[0:SUBLANES, :] = cbuf[TM:TM + SUBLANES, :]
    y_conv = b_gate * conv
    for g in range(N_CONV_GROUPS):
        sl = slice(g * LANES, (g + 1) * LANES)
        blk = y_conv[:, sl]
        ms = jnp.mean(blk * blk, axis=-1, keepdims=True)
        ybuf[:, sl] = (blk * lax.rsqrt(ms + RMS_EPS) * gnc_ref[:, sl]).astype(_BF16)

    gu = _gelu_exact(_dot(xn, win_ref[:, 3 * D_CONV:3 * D_CONV + D_SGU]))
    gv = _gelu_exact(_dot(xn, win_ref[:, 3 * D_CONV + D_SGU:D_IN_PROJ]))
    row_c = lax.broadcasted_iota(jnp.int32, (CHUNK, CHUNK), 0)
    col_c = lax.broadcasted_iota(jnp.int32, (CHUNK, CHUNK), 1)
    causal = row_c >= col_c
    for h in range(N_SGU_HEADS):
        sl = slice(h * SGU_HEAD_DIM, (h + 1) * SGU_HEAD_DIM)
        vh = gv[:, sl]
        mu = jnp.mean(vh, axis=-1, keepdims=True)
        xc = vh - mu
        var = jnp.mean(xc * xc, axis=-1, keepdims=True)
        vn = (xc * lax.rsqrt(var + LN_EPS) * slg_ref[:, sl] + slb_ref[:, sl]).astype(_BF16)
        wm = jnp.where(causal, sw_ref[h], 0.0).astype(_BF16)
        for c in range(TM // CHUNK):
            rows = slice(c * CHUNK, (c + 1) * CHUNK)
            mixed = _dot(wm, vn[rows, :]) + sb_ref[:, sl]
            ys = gu[rows, sl] * mixed
            ms = jnp.mean(ys * ys, axis=-1, keepdims=True)
            ybuf[rows, D_CONV + h * SGU_HEAD_DIM:D_CONV + (h + 1) * SGU_HEAD_DIM] = (
                ys * lax.rsqrt(ms + RMS_EPS) * gns_ref[:, sl]).astype(_BF16)

    h1 = x + _dot(ybuf[...], wout_ref[...])
    h1_ref[...] = h1
    xn2 = _rms_rows(h1, lnf_ref[...])
    xn2_ref[...] = xn2

    x_hi = xn2.astype(_BF16)
    x_lo = (xn2 - x_hi.astype(_F32)).astype(_BF16)
    p = _dot(x_hi, wr_ref[...]) + _dot(x_lo, wr_ref[...])
    logits = p[:, :LANES] + p[:, LANES:] + br_ref[...]
    lane = lax.broadcasted_iota(jnp.int32, (TM, LANES), 1)
    lane_f = lane.astype(_F32)
    neg_inf = jnp.float32(-jnp.inf)
    logits = jnp.where(lane < N_EXPERTS, logits, neg_inf)

    vals, ids, onehots = [], [], []
    cur = logits
    for _ in range(TOP_K):
        m = jnp.max(cur, axis=-1, keepdims=True)
        idx = jnp.min(jnp.where(cur == m, lane_f, float(LANES)), axis=-1, keepdims=True)
        oh = lane_f == idx
        vals.append(m)
        ids.append(idx)
        onehots.append(oh)
        cur = jnp.where(oh, neg_inf, cur)
    exps = [jnp.exp(v - vals[0]) for v in vals]
    denom = exps[0] + exps[1] + exps[2] + exps[3]
    gates = [e / denom for e in exps]

    mask = (onehots[0] | onehots[1] | onehots[2] | onehots[3]).astype(_F32)
    row_t = lax.broadcasted_iota(jnp.int32, (TM, TM), 0)
    col_t = lax.broadcasted_iota(jnp.int32, (TM, TM), 1)
    strict_lower = (row_t > col_t).astype(_BF16)
    before = _dot(strict_lower, mask.astype(_BF16)) + carry[...]
    ranks = [jnp.sum(jnp.where(oh, before, 0.0), axis=-1, keepdims=True) for oh in onehots]
    carry[...] = carry[...] + jnp.sum(mask, axis=0, keepdims=True)
    counts_ref[...] = jnp.broadcast_to(carry[...], counts_ref.shape)

    route = jnp.zeros((TM, LANES), _F32)
    gate_out = jnp.zeros((TM, LANES), _F32)
    for k in range(TOP_K):
        route = jnp.where(lane == k, ids[k], route)
        route = jnp.where(lane == TOP_K + k, ranks[k], route)
        gate_out = jnp.where(lane == k, gates[k], gate_out)
    route_ref[...] = route.astype(jnp.int32)
    gates_ref[...] = gate_out


def _mixer_router(x2d, ln_mix_g, w_in_bf, conv_w, sgu_ln_g, sgu_ln_b, sgu_w, sgu_b_full,
                  gn_conv, gn_sgu, w_out_bf, ln_ffn_g, wr_split, br_pad):
    def full(a):
        return pl.BlockSpec(a.shape, lambda i: (0,) * a.ndim)

    row_blk = lambda w: pl.BlockSpec((TM, w), lambda i: (i, 0))
    ins = [x2d, ln_mix_g, w_in_bf, conv_w, sgu_ln_g, sgu_ln_b, sgu_w, sgu_b_full,
           gn_conv, gn_sgu, w_out_bf, ln_ffn_g, wr_split, br_pad]
    in_specs = [row_blk(D_MODEL)] + [full(a) for a in ins[1:]]
    return pl.pallas_call(
        _mixer_router_kernel,
        grid=(N_TOK // TM,),
        in_specs=in_specs,
        out_specs=[row_blk(D_MODEL), row_blk(D_MODEL), row_blk(LANES), row_blk(LANES),
                   pl.BlockSpec((SUBLANES, LANES), lambda i: (0, 0))],
        out_shape=[jax.ShapeDtypeStruct((N_TOK, D_MODEL), _F32),
                   jax.ShapeDtypeStruct((N_TOK, D_MODEL), _F32),
                   jax.ShapeDtypeStruct((N_TOK, LANES), jnp.int32),
                   jax.ShapeDtypeStruct((N_TOK, LANES), _F32),
                   jax.ShapeDtypeStruct((SUBLANES, LANES), _F32)],
        scratch_shapes=[pltpu.VMEM((TM + SUBLANES, D_CONV), _F32),
                        pltpu.VMEM((1, LANES), _F32),
                        pltpu.VMEM((TM, D_MODEL), _BF16)],
        compiler_params=pltpu.CompilerParams(dimension_semantics=("arbitrary",),
                                             vmem_limit_bytes=VMEM_LIMIT),
        name="mixer_router",
    )(*ins)


def _dispatch_kernel(dest_ref, zrow_ref, zflag_ref,
                     x_hbm, bufx_hbm, x_in, zbuf, zsem, in_sem, out_sem):
    i = pl.program_id(0)
    n_steps = pl.num_programs(0)

    @pl.when(i == 0)
    def _():
        zbuf[...] = jnp.zeros_like(zbuf)

        def zero_copy(e):
            z0 = pl.multiple_of(zrow_ref[e], ROW_CHUNK)
            return pltpu.make_async_copy(zbuf, bufx_hbm.at[pl.ds(z0, ROW_CHUNK)], zsem)

        for e in range(N_EXPERTS):
            @pl.when(zflag_ref[e] > 0)
            def _(e=e):
                zero_copy(e).start()
        for e in range(N_EXPERTS):
            @pl.when(zflag_ref[e] > 0)
            def _(e=e):
                zero_copy(e).wait()

    def fetch(blk, slot):
        r = pl.multiple_of(blk * TD, TD)
        return pltpu.make_async_copy(x_hbm.at[pl.ds(r, TD)], x_in.at[slot], in_sem.at[slot])

    def row_copy(blk, slot, t8, u, k):
        t = pl.multiple_of(t8 * SUBLANES, SUBLANES) + u
        dst = dest_ref[(blk * TD + t) * TOP_K + k]
        return pltpu.make_async_copy(x_in.at[slot, pl.ds(t, 1)], bufx_hbm.at[pl.ds(dst, 1)],
                                     out_sem.at[blk % 2])

    def scatter(blk, slot, start):
        def body(t8, carry):
            for u in range(SUBLANES):
                for k in range(TOP_K):
                    copy = row_copy(blk, slot, t8, u, k)
                    copy.start() if start else copy.wait()
            return carry
        lax.fori_loop(0, TD // SUBLANES, body, 0)

    def in_slot_of(blk, fn):
        for q in range(IN_SLOTS):
            @pl.when(blk % IN_SLOTS == q)
            def _(q=q):
                fn(q)

    @pl.when(i == 0)
    def _():
        fetch(0, 0).start()
        fetch(1, 1).start()

    in_slot_of(i, lambda q: fetch(i, q).wait())
    in_slot_of(i, lambda q: scatter(i, q, True))

    @pl.when(i >= 1)
    def _():
        in_slot_of(i - 1, lambda q: scatter(i - 1, q, False))

    @pl.when(i + 2 < n_steps)
    def _():
        in_slot_of(i + 2, lambda q: fetch(i + 2, q).start())

    @pl.when(i == n_steps - 1)
    def _():
        in_slot_of(i, lambda q: scatter(i, q, False))


def _dispatch(dest, zrow, zflag, xn2):
    return pl.pallas_call(
        _dispatch_kernel,
        grid_spec=pltpu.PrefetchScalarGridSpec(
            num_scalar_prefetch=3,
            grid=(N_TOK // TD,),
            in_specs=[pl.BlockSpec(memory_space=pl.ANY)],
            out_specs=pl.BlockSpec(memory_space=pl.ANY),
            scratch_shapes=[pltpu.VMEM((IN_SLOTS, TD, D_MODEL), _F32),
                            pltpu.VMEM((ROW_CHUNK, D_MODEL), _F32),
                            pltpu.SemaphoreType.DMA(()),
                            pltpu.SemaphoreType.DMA((IN_SLOTS,)),
                            pltpu.SemaphoreType.DMA((2,))],
        ),
        out_shape=jax.ShapeDtypeStruct((MAX_SUPER * SUPER_ROWS, D_MODEL), _F32),
        compiler_params=pltpu.CompilerParams(dimension_semantics=("arbitrary",),
                                             vmem_limit_bytes=VMEM_LIMIT),
        name="dispatch",
    )(dest, zrow, zflag, xn2)


def _expert_kernel(sup_e_ref, sup_blk_ref, nchunk_ref, nhalf_ref,
                   x_hbm, wg_ref, wu_ref, bg_ref, bu_ref, wd_ref, bd_ref, out_hbm,
                   x_stage, x_bf, acc, x_sem, o_sem, *, max_super):
    s = pl.program_id(0)
    j = pl.program_id(1)
    n_chunks = nchunk_ref[s]
    cur = s % 2

    def x_copy(sup, c):
        r = pl.multiple_of(sup_blk_ref[sup] * SUPER_ROWS + c * ROW_CHUNK, ROW_CHUNK)
        slot = c % X_SLOTS
        return pltpu.make_async_copy(x_hbm.at[pl.ds(r, ROW_CHUNK)], x_stage.at[slot],
                                     x_sem.at[slot])

    def x_finish(sup, c):
        x_copy(sup, c).wait()
        rows = pl.ds(pl.multiple_of(c * ROW_CHUNK, ROW_CHUNK), ROW_CHUNK)
        x_bf[sup % 2, rows, :] = x_stage[c % X_SLOTS].astype(_BF16)

    def out_copy(sup, c):
        r = pl.multiple_of(sup_blk_ref[sup] * SUPER_ROWS + c * ROW_CHUNK, ROW_CHUNK)
        rows = pl.ds(pl.multiple_of(c * ROW_CHUNK, ROW_CHUNK), ROW_CHUNK)
        return pltpu.make_async_copy(acc.at[sup % 2, rows], out_hbm.at[pl.ds(r, ROW_CHUNK)],
                                     o_sem.at[sup % 2])

    def for_each_chunk(sup, fn):
        for c in range(CHUNKS_PER_SUPER):
            @pl.when(c < nchunk_ref[sup])
            def _(c=c):
                fn(sup, c)

    @pl.when((s == 0) & (j == 0))
    def _():
        def load(sup, c):
            x_copy(sup, c).start()
            x_finish(sup, c)
        for_each_chunk(0, load)

    @pl.when((s >= 2) & (j == 0))
    def _():
        for_each_chunk(jnp.maximum(s - 2, 0), lambda sup, c: out_copy(sup, c).wait())

    prev = jnp.maximum(s - 1, 0)

    @pl.when((s >= 1) & (j < nchunk_ref[prev]))
    def _():
        out_copy(prev, j).start()

    nxt = jnp.minimum(s + 1, max_super - 1)
    n_next = jnp.where(s + 1 < max_super, nchunk_ref[nxt], 0)

    @pl.when((j >= 1) & (j - 1 < n_next))
    def _():
        x_finish(nxt, j - 1)

    @pl.when(j < n_next)
    def _():
        x_copy(nxt, j).start()

    @pl.when(n_chunks > 0)
    def _():
        def partial_out(n_rows):
            rows = pl.ds(0, n_rows)
            xb = x_bf[cur, rows, :]
            gate = _dot(xb, wg_ref[0].astype(_BF16)) + bg_ref[0]
            up = _dot(xb, wu_ref[0].astype(_BF16)) + bu_ref[0]
            gate = jnp.minimum(gate, SWIGLU_LIMIT)
            up = jnp.clip(up, -SWIGLU_LIMIT, SWIGLU_LIMIT)
            glu = gate * jax.nn.sigmoid(gate * SWIGLU_ALPHA)
            act = ((up + 1.0) * glu).astype(_BF16)
            return rows, _dot(act, wd_ref[0].astype(_BF16))

        def sweep(update):
            n_half = nhalf_ref[s]
            for n_rows, (lo, hi) in ROW_VARIANTS.items():
                @pl.when((n_half >= lo) & (n_half <= hi))
                def _(n_rows=n_rows):
                    update(n_rows)

        def init(n_rows):
            rows, o = partial_out(n_rows)
            acc[cur, rows, :] = o + bd_ref[0]

        def accumulate(n_rows):
            rows, o = partial_out(n_rows)
            acc[cur, rows, :] += o

        @pl.when(j == 0)
        def _():
            sweep(init)

        @pl.when(j > 0)
        def _():
            sweep(accumulate)

    @pl.when((s == max_super - 1) & (j == N_FF_TILES - 1))
    def _():
        for_each_chunk(max_super - 1, lambda sup, c: out_copy(sup, c).start())
        for_each_chunk(max_super - 2, lambda sup, c: out_copy(sup, c).wait())
        for_each_chunk(max_super - 1, lambda sup, c: out_copy(sup, c).wait())


def _experts(max_super, sup_e, sup_blk, nchunk, nhalf, buf_x, w_gate_up, b_gate_up3, w_down, b_down3):
    def ff_tile(j, nchunk, s):
        return jnp.where(nchunk[s] > 0, j, N_FF_TILES - 1)

    x_spec = pl.BlockSpec(memory_space=pl.ANY)
    wg_spec = pl.BlockSpec((1, D_MODEL, FF_TILE),
                           lambda s, j, se, sb, nc, nh: (se[s], 0, ff_tile(j, nc, s)))
    wu_spec = pl.BlockSpec((1, D_MODEL, FF_TILE),
                           lambda s, j, se, sb, nc, nh: (se[s], 0, N_FF_TILES + ff_tile(j, nc, s)))
    bg_spec = pl.BlockSpec((1, 1, FF_TILE),
                           lambda s, j, se, sb, nc, nh: (se[s], 0, ff_tile(j, nc, s)))
    bu_spec = pl.BlockSpec((1, 1, FF_TILE),
                           lambda s, j, se, sb, nc, nh: (se[s], 0, N_FF_TILES + ff_tile(j, nc, s)))
    wd_spec = pl.BlockSpec((1, FF_TILE, D_MODEL),
                           lambda s, j, se, sb, nc, nh: (se[s], ff_tile(j, nc, s), 0))
    bd_spec = pl.BlockSpec((1, 1, D_MODEL), lambda s, j, se, sb, nc, nh: (se[s], 0, 0))
    out_spec = pl.BlockSpec(memory_space=pl.ANY)
    return pl.pallas_call(
        functools.partial(_expert_kernel, max_super=max_super),
        grid_spec=pltpu.PrefetchScalarGridSpec(
            num_scalar_prefetch=4,
            grid=(max_super, N_FF_TILES),
            in_specs=[x_spec, wg_spec, wu_spec, bg_spec, bu_spec, wd_spec, bd_spec],
            out_specs=out_spec,
            scratch_shapes=[pltpu.VMEM((X_SLOTS, ROW_CHUNK, D_MODEL), _F32),
                            pltpu.VMEM((2, SUPER_ROWS, D_MODEL), _BF16),
                            pltpu.VMEM((2, SUPER_ROWS, D_MODEL), _F32),
                            pltpu.SemaphoreType.DMA((X_SLOTS,)),
                            pltpu.SemaphoreType.DMA((2,))],
        ),
        out_shape=jax.ShapeDtypeStruct((MAX_SUPER * SUPER_ROWS, D_MODEL), _F32),
        compiler_params=pltpu.CompilerParams(dimension_semantics=("arbitrary", "arbitrary"),
                                             vmem_limit_bytes=VMEM_LIMIT),
        name="experts",
    )(sup_e, sup_blk, nchunk, nhalf, buf_x, w_gate_up, w_gate_up, b_gate_up3, b_gate_up3, w_down, b_down3)


def _combine_kernel(dest_ref, h1_ref, gates_ref, lng_ref, y_hbm, out_ref, rows, sem):
    i = pl.program_id(0)
    n_steps = pl.num_programs(0)

    def row_copy(tile, slot, t8, u, k):
        t = pl.multiple_of(t8 * SUBLANES, SUBLANES) + u
        src = dest_ref[(tile * TC + t) * TOP_K + k]
        return pltpu.make_async_copy(y_hbm.at[pl.ds(src, 1)], rows.at[slot, k, pl.ds(t, 1)],
                                     sem.at[slot])

    def issue_tile(tile, slot):
        def body(t8, carry):
            for u in range(SUBLANES):
                for k in range(TOP_K):
                    row_copy(tile, slot, t8, u, k).start()
            return carry
        lax.fori_loop(0, TC // SUBLANES, body, 0)

    def wait_tile(tile, slot):
        def body(t8, carry):
            for u in range(SUBLANES):
                for k in range(TOP_K):
                    row_copy(tile, slot, t8, u, k).wait()
            return carry
        lax.fori_loop(0, TC // SUBLANES, body, 0)

    @pl.when(i == 0)
    def _():
        issue_tile(0, 0)

    for slot in range(2):
        @pl.when(i % 2 == slot)
        def _(slot=slot):
            @pl.when(i + 1 < n_steps)
            def _():
                issue_tile(i + 1, 1 - slot)

            wait_tile(i, slot)
            g = gates_ref[...]
            acc = h1_ref[...]
            for k in range(TOP_K):
                acc = acc + rows[slot, k] * g[:, k:k + 1]
            out_ref[...] = _rms_rows(acc, lng_ref[...])


def _combine(dest, h1, gates, ln_final_g, y_sorted):
    return pl.pallas_call(
        _combine_kernel,
        grid_spec=pltpu.PrefetchScalarGridSpec(
            num_scalar_prefetch=1,
            grid=(N_TOK // TC,),
            in_specs=[pl.BlockSpec((TC, D_MODEL), lambda i, *_: (i, 0)),
                      pl.BlockSpec((TC, LANES), lambda i, *_: (i, 0)),
                      pl.BlockSpec((1, D_MODEL), lambda i, *_: (0, 0)),
                      pl.BlockSpec(memory_space=pl.ANY)],
            out_specs=pl.BlockSpec((TC, D_MODEL), lambda i, *_: (i, 0)),
            scratch_shapes=[pltpu.VMEM((2, TOP_K, TC, D_MODEL), _F32),
                            pltpu.SemaphoreType.DMA((2,))],
        ),
        out_shape=jax.ShapeDtypeStruct((N_TOK, D_MODEL), _F32),
        compiler_params=pltpu.CompilerParams(dimension_semantics=("arbitrary",),
                                             vmem_limit_bytes=VMEM_LIMIT),
        name="combine",
    )(dest, h1, gates, ln_final_g, y_sorted)


def kernel(x, ln_mix_g, w_in, conv_w, sgu_ln_g, sgu_ln_b, sgu_w, sgu_b, gn_conv, gn_sgu, w_out,
           ln_ffn_g, w_router, b_router, w_gate_up, b_gate_up, w_down, b_down, ln_final_g):
    row = lambda v: v.reshape(1, -1)
    x2d = x.reshape(N_TOK, D_MODEL)

    sgu_b_full = jnp.repeat(jnp.transpose(sgu_b), SGU_HEAD_DIM, axis=1)
    wr_hi = w_router.astype(_BF16)
    wr_lo = (w_router - wr_hi.astype(_F32)).astype(_BF16)
    pad = ((0, 0), (0, LANES - N_EXPERTS))
    wr_split = jnp.concatenate([jnp.pad(wr_hi, pad), jnp.pad(wr_lo, pad)], axis=1)
    br_pad = jnp.pad(b_router, (0, LANES - N_EXPERTS)).reshape(1, LANES)

    h1, xn2, route, gates, counts = _mixer_router(
        x2d, row(ln_mix_g), w_in.astype(_BF16), conv_w, row(sgu_ln_g), row(sgu_ln_b), sgu_w,
        sgu_b_full, row(gn_conv), row(gn_sgu), w_out.astype(_BF16), row(ln_ffn_g), wr_split, br_pad)

    sizes = counts[0, :N_EXPERTS].astype(jnp.int32)
    n_sup = (sizes + SUPER_ROWS - 1) // SUPER_ROWS
    sup_end = jnp.cumsum(n_sup)
    sup_start = sup_end - n_sup
    total_sup = sup_end[-1]
    row_start = (sup_start * SUPER_ROWS).astype(jnp.int32)
    sid = jnp.arange(MAX_SUPER, dtype=jnp.int32)
    sid_eff = jnp.minimum(sid, total_sup - 1)
    sup_e = jnp.minimum(jnp.searchsorted(sup_end, sid_eff, side="right"),
                        N_EXPERTS - 1).astype(jnp.int32)
    rows_left = sizes[sup_e] - (sid_eff - sup_start[sup_e]) * SUPER_ROWS
    valid = jnp.clip(rows_left, 0, SUPER_ROWS)
    nchunk = jnp.where(sid < total_sup, (valid + ROW_CHUNK - 1) // ROW_CHUNK, 0).astype(jnp.int32)
    nhalf = jnp.where(sid < total_sup, (valid + HALF_CHUNK - 1) // HALF_CHUNK, 0).astype(jnp.int32)
    zflag = (sizes > 0).astype(jnp.int32)
    zrow = (row_start + ((jnp.maximum(sizes, 1) - 1) // ROW_CHUNK) * ROW_CHUNK).astype(jnp.int32)
    eid = route[:, 0:TOP_K]
    eid_start = jnp.sum(jnp.where(eid[..., None] == jnp.arange(N_EXPERTS, dtype=jnp.int32),
                                  row_start, 0), axis=-1)
    dest = (eid_start + route[:, TOP_K:2 * TOP_K]).reshape(-1)

    buf_x = _dispatch(dest, zrow, zflag, xn2)
    expert_args = (sup_e, sid_eff.astype(jnp.int32), nchunk, nhalf, buf_x, w_gate_up,
                   b_gate_up.reshape(N_EXPERTS, 1, 2 * D_FF), w_down,
                   b_down.reshape(N_EXPERTS, 1, D_MODEL))
    y_sorted = lax.cond(total_sup <= SHORT_GRID,
                        functools.partial(_experts, SHORT_GRID),
                        functools.partial(_experts, MAX_SUPER), *expert_args)
    out = _combine(dest, h1, gates, row(ln_final_g), y_sorted)
    return out.reshape(BATCH, SEQ, D_MODEL)
```

```python
import functools

import jax
import jax.numpy as jnp
from jax import lax
from jax.experimental import pallas as pl
from jax.experimental.pallas import tpu as pltpu

D_MODEL = 2048
BATCH = 2
SEQ = 4096
N_TOK = BATCH * SEQ

D_CONV = 1024
N_CONV_GROUPS = 8
CONV_WIDTH = 3
D_SGU = 1024
N_SGU_HEADS = 8
SGU_HEAD_DIM = 128
CHUNK = 128
D_IN_PROJ = 3 * D_CONV + 2 * D_SGU

N_EXPERTS = 32
TOP_K = 4
D_FF = 2048
SWIGLU_LIMIT = 7.0
SWIGLU_ALPHA = 1.702
RMS_EPS = 1e-5
LN_EPS = 1e-5

LANES = 128
SUBLANES = 8
VMEM_LIMIT = 56 * 1024 * 1024

TM = 256
SUPER_ROWS = 1280
ROW_CHUNK = 256
FF_TILE = 256
N_FF_TILES = D_FF // FF_TILE
MAX_SUPER = (N_TOK * TOP_K + N_EXPERTS * (SUPER_ROWS - 1)) // SUPER_ROWS
SHORT_GRID = N_EXPERTS + 8
CHUNKS_PER_SUPER = SUPER_ROWS // ROW_CHUNK
assert CHUNKS_PER_SUPER + 1 <= N_FF_TILES
X_SLOTS = 2
HALF_CHUNK = ROW_CHUNK // 2
FINE_FROM = 7


def _row_variants():
    variants = {}
    for n_half in range(1, SUPER_ROWS // HALF_CHUNK + 1):
        if n_half >= FINE_FROM:
            n_rows = n_half * HALF_CHUNK
        else:
            n_rows = -(-n_half // 2) * ROW_CHUNK
        lo, hi = variants.get(n_rows, (n_half, n_half))
        variants[n_rows] = (min(lo, n_half), max(hi, n_half))
    return variants


ROW_VARIANTS = _row_variants()
TD = 256
IN_SLOTS = 3
assert N_TOK // TD >= 2
TC = 256

_F32 = jnp.float32
_BF16 = jnp.bfloat16


def _dot(a, b):
    return jnp.dot(a, b, preferred_element_type=_F32)


def _gelu_exact(x):
    return 0.5 * x * (1.0 + lax.erf(x * (2.0 ** -0.5)))


def _rms_rows(x, gain):
    return x * lax.rsqrt(jnp.mean(x * x, axis=-1, keepdims=True) + RMS_EPS) * gain


def _mixer_router_kernel(x_ref, lng_ref, win_ref, convw_ref, slg_ref, slb_ref, sw_ref, sb_ref,
                         gnc_ref, gns_ref, wout_ref, lnf_ref, wr_ref, br_ref,
                         h1_ref, xn2_ref, route_ref, gates_ref, counts_ref,
                         cbuf, carry, ybuf):
    i = pl.program_id(0)

    @pl.when(i % (SEQ // TM) == 0)
    def _():
        cbuf[0:SUBLANES, :] = jnp.zeros((SUBLANES, D_CONV), _F32)

    @pl.when(i == 0)
    def _():
        carry[...] = jnp.zeros_like(carry)

    x = x_ref[...]
    xn = _rms_rows(x, lng_ref[...]).astype(_BF16)

    b_gate = _dot(xn, win_ref[:, 0:D_CONV])
    c_gate = _dot(xn, win_ref[:, D_CONV:2 * D_CONV])
    hh = _dot(xn, win_ref[:, 2 * D_CONV:3 * D_CONV])
    ch = c_gate * hh
    cbuf[SUBLANES:SUBLANES + TM, :] = ch
    ch1 = cbuf[SUBLANES - 1:SUBLANES - 1 + TM, :]
    ch2 = cbuf[SUBLANES - 2:SUBLANES - 2 + TM, :]
    conv = convw_ref[0:1, :] * ch2 + convw_ref[1:2, :] * ch1 + convw_ref[2:3, :] * ch
    cbuf[0:SUBLANES, :] = cbuf[TM:TM + SUBLANES, :]
    y_conv = b_gate * conv
    for g in range(N_CONV_GROUPS):
        sl = slice(g * LANES, (g + 1) * LANES)
        blk = y_conv[:, sl]
        ms = jnp.mean(blk * blk, axis=-1, keepdims=True)
        ybuf[:, sl] = (blk * lax.rsqrt(ms + RMS_EPS) * gnc_ref[:, sl]).astype(_BF16)

    gu = _gelu_exact(_dot(xn, win_ref[:, 3 * D_CONV:3 * D_CONV + D_SGU]))
    gv = _gelu_exact(_dot(xn, win_ref[:, 3 * D_CONV + D_SGU:D_IN_PROJ]))
    row_c = lax.broadcasted_iota(jnp.int32, (CHUNK, CHUNK), 0)
    col_c = lax.broadcasted_iota(jnp.int32, (CHUNK, CHUNK), 1)
    causal = row_c >= col_c
    for h in range(N_SGU_HEADS):
        sl = slice(h * SGU_HEAD_DIM, (h + 1) * SGU_HEAD_DIM)
        vh = gv[:, sl]
        mu = jnp.mean(vh, axis=-1, keepdims=True)
        xc = vh - mu
        var = jnp.mean(xc * xc, axis=-1, keepdims=True)
        vn = (xc * lax.rsqrt(var + LN_EPS) * slg_ref[:, sl] + slb_ref[:, sl]).astype(_BF16)
        wm = jnp.where(causal, sw_ref[h], 0.0).astype(_BF16)
        for c in range(TM // CHUNK):
            rows = slice(c * CHUNK, (c + 1) * CHUNK)
            mixed = _dot(wm, vn[rows, :]) + sb_ref[:, sl]
            ys = gu[rows, sl] * mixed
            ms = jnp.mean(ys * ys, axis=-1, keepdims=True)
            ybuf[rows, D_CONV + h * SGU_HEAD_DIM:D_CONV + (h + 1) * SGU_HEAD_DIM] = (
                ys * lax.rsqrt(ms + RMS_EPS) * gns_ref[:, sl]).astype(_BF16)

    h1 = x + _dot(ybuf[...], wout_ref[...])
    h1_ref[...] = h1
    xn2 = _rms_rows(h1, lnf_ref[...])
    xn2_ref[...] = xn2

    x_hi = xn2.astype(_BF16)
    x_lo = (xn2 - x_hi.astype(_F32)).astype(_BF16)
    p = _dot(x_hi, wr_ref[...]) + _dot(x_lo, wr_ref[...])
    logits = p[:, :LANES] + p[:, LANES:] + br_ref[...]
    lane = lax.broadcasted_iota(jnp.int32, (TM, LANES), 1)
    lane_f = lane.astype(_F32)
    neg_inf = jnp.float32(-jnp.inf)
    logits = jnp.where(lane < N_EXPERTS, logits, neg_inf)

    vals, ids, onehots = [], [], []
    cur = logits
    for _ in range(TOP_K):
        m = jnp.max(cur, axis=-1, keepdims=True)
        idx = jnp.min(jnp.where(cur == m, lane_f, float(LANES)), axis=-1, keepdims=True)
        oh = lane_f == idx
        vals.append(m)
        ids.append(idx)
        onehots.append(oh)
        cur = jnp.where(oh, neg_inf, cur)
    exps = [jnp.exp(v - vals[0]) for v in vals]
    denom = exps[0] + exps[1] + exps[2] + exps[3]
    gates = [e / denom for e in exps]

    mask = (onehots[0] | onehots[1] | onehots[2] | onehots[3]).astype(_F32)
    row_t = lax.broadcasted_iota(jnp.int32, (TM, TM), 0)
    col_t = lax.broadcasted_iota(jnp.int32, (TM, TM), 1)
    strict_lower = (row_t > col_t).astype(_BF16)
    before = _dot(strict_lower, mask.astype(_BF16)) + carry[...]
    ranks = [jnp.sum(jnp.where(oh, before, 0.0), axis=-1, keepdims=True) for oh in onehots]
    carry[...] = carry[...] + jnp.sum(mask, axis=0, keepdims=True)
    counts_ref[...] = jnp.broadcast_to(carry[...], counts_ref.shape)

    route = jnp.zeros((TM, LANES), _F32)
    gate_out = jnp.zeros((TM, LANES), _F32)
    for k in range(TOP_K):
        route = jnp.where(lane == k, ids[k], route)
        route = jnp.where(lane == TOP_K + k, ranks[k], route)
        gate_out = jnp.where(lane == k, gates[k], gate_out)
    route_ref[...] = route.astype(jnp.int32)
    gates_ref[...] = gate_out


def _mixer_router(x2d, ln_mix_g, w_in_bf, conv_w, sgu_ln_g, sgu_ln_b, sgu_w, sgu_b_full,
                  gn_conv, gn_sgu, w_out_bf, ln_ffn_g, wr_split, br_pad):
    def full(a):
        return pl.BlockSpec(a.shape, lambda i: (0,) * a.ndim)

    row_blk = lambda w: pl.BlockSpec((TM, w), lambda i: (i, 0))
    ins = [x2d, ln_mix_g, w_in_bf, conv_w, sgu_ln_g, sgu_ln_b, sgu_w, sgu_b_full,
           gn_conv, gn_sgu, w_out_bf, ln_ffn_g, wr_split, br_pad]
    in_specs = [row_blk(D_MODEL)] + [full(a) for a in ins[1:]]
    return pl.pallas_call(
        _mixer_router_kernel,
        grid=(N_TOK // TM,),
        in_specs=in_specs,
        out_specs=[row_blk(D_MODEL), row_blk(D_MODEL), row_blk(LANES), row_blk(LANES),
                   pl.BlockSpec((SUBLANES, LANES), lambda i: (0, 0))],
        out_shape=[jax.ShapeDtypeStruct((N_TOK, D_MODEL), _F32),
                   jax.ShapeDtypeStruct((N_TOK, D_MODEL), _F32),
                   jax.ShapeDtypeStruct((N_TOK, LANES), jnp.int32),
                   jax.ShapeDtypeStruct((N_TOK, LANES), _F32),
                   jax.ShapeDtypeStruct((SUBLANES, LANES), _F32)],
        scratch_shapes=[pltpu.VMEM((TM + SUBLANES, D_CONV), _F32),
                        pltpu.VMEM((1, LANES), _F32),
                        pltpu.VMEM((TM, D_MODEL), _BF16)],
        compiler_params=pltpu.CompilerParams(dimension_semantics=("arbitrary",),
                                             vmem_limit_bytes=VMEM_LIMIT),
        name="mixer_router",
    )(*ins)


def _dispatch_kernel(dest_ref, zrow_ref, zflag_ref,
                     x_hbm, bufx_hbm, x_in, zbuf, zsem, in_sem, out_sem):
    i = pl.program_id(0)
    n_steps = pl.num_programs(0)

    @pl.when(i == 0)
    def _():
        zbuf[...] = jnp.zeros_like(zbuf)

        def zero_copy(e):
            z0 = pl.multiple_of(zrow_ref[e], ROW_CHUNK)
            return pltpu.make_async_copy(zbuf, bufx_hbm.at[pl.ds(z0, ROW_CHUNK)], zsem)

        for e in range(N_EXPERTS):
            @pl.when(zflag_ref[e] > 0)
            def _(e=e):
                zero_copy(e).start()
        for e in range(N_EXPERTS):
            @pl.when(zflag_ref[e] > 0)
            def _(e=e):
                zero_copy(e).wait()

    def fetch(blk, slot):
        r = pl.multiple_of(blk * TD, TD)
        return pltpu.make_async_copy(x_hbm.at[pl.ds(r, TD)], x_in.at[slot], in_sem.at[slot])

    def row_copy(blk, slot, t8, u, k):
        t = pl.multiple_of(t8 * SUBLANES, SUBLANES) + u
        dst = dest_ref[(blk * TD + t) * TOP_K + k]
        return pltpu.make_async_copy(x_in.at[slot, pl.ds(t, 1)], bufx_hbm.at[pl.ds(dst, 1)],
                                     out_sem.at[blk % 2])

    def scatter(blk, slot, start):
        def body(t8, carry):
            for u in range(SUBLANES):
                for k in range(TOP_K):
                    copy = row_copy(blk, slot, t8, u, k)
                    copy.start() if start else copy.wait()
            return carry
        lax.fori_loop(0, TD // SUBLANES, body, 0)

    def in_slot_of(blk, fn):
        for q in range(IN_SLOTS):
            @pl.when(blk % IN_SLOTS == q)
            def _(q=q):
                fn(q)

    @pl.when(i == 0)
    def _():
        fetch(0, 0).start()
        fetch(1, 1).start()

    in_slot_of(i, lambda q: fetch(i, q).wait())
    in_slot_of(i, lambda q: scatter(i, q, True))

    @pl.when(i >= 1)
    def _():
        in_slot_of(i - 1, lambda q: scatter(i - 1, q, False))

    @pl.when(i + 2 < n_steps)
    def _():
        in_slot_of(i + 2, lambda q: fetch(i + 2, q).start())

    @pl.when(i == n_steps - 1)
    def _():
        in_slot_of(i, lambda q: scatter(i, q, False))


def _dispatch(dest, zrow, zflag, xn2):
    return pl.pallas_call(
        _dispatch_kernel,
        grid_spec=pltpu.PrefetchScalarGridSpec(
            num_scalar_prefetch=3,
            grid=(N_TOK // TD,),
            in_specs=[pl.BlockSpec(memory_space=pl.ANY)],
            out_specs=pl.BlockSpec(memory_space=pl.ANY),
            scratch_shapes=[pltpu.VMEM((IN_SLOTS, TD, D_MODEL), _F32),
                            pltpu.VMEM((ROW_CHUNK, D_MODEL), _F32),
                            pltpu.SemaphoreType.DMA(()),
                            pltpu.SemaphoreType.DMA((IN_SLOTS,)),
                            pltpu.SemaphoreType.DMA((2,))],
        ),
        out_shape=jax.ShapeDtypeStruct((MAX_SUPER * SUPER_ROWS, D_MODEL), _F32),
        compiler_params=pltpu.CompilerParams(dimension_semantics=("arbitrary",),
                                             vmem_limit_bytes=VMEM_LIMIT),
        name="dispatch",
    )(dest, zrow, zflag, xn2)


def _expert_kernel(sup_e_ref, sup_blk_ref, nchunk_ref, nhalf_ref,
                   x_hbm, wg_ref, wu_ref, bg_ref, bu_ref, wd_ref, bd_ref, out_hbm,
                   x_stage, x_bf, acc, x_sem, o_sem, *, max_super):
    s = pl.program_id(0)
    j = pl.program_id(1)
    n_chunks = nchunk_ref[s]
    cur = s % 2

    def x_copy(sup, c):
        r = pl.multiple_of(sup_blk_ref[sup] * SUPER_ROWS + c * ROW_CHUNK, ROW_CHUNK)
        slot = c % X_SLOTS
        return pltpu.make_async_copy(x_hbm.at[pl.ds(r, ROW_CHUNK)], x_stage.at[slot],
                                     x_sem.at[slot])

    def x_finish(sup, c):
        x_copy(sup, c).wait()
        rows = pl.ds(pl.multiple_of(c * ROW_CHUNK, ROW_CHUNK), ROW_CHUNK)
        x_bf[sup % 2, rows, :] = x_stage[c % X_SLOTS].astype(_BF16)

    def out_copy(sup, c):
        r = pl.multiple_of(sup_blk_ref[sup] * SUPER_ROWS + c * ROW_CHUNK, ROW_CHUNK)
        rows = pl.ds(pl.multiple_of(c * ROW_CHUNK, ROW_CHUNK), ROW_CHUNK)
        return pltpu.make_async_copy(acc.at[sup % 2, rows], out_hbm.at[pl.ds(r, ROW_CHUNK)],
                                     o_sem.at[sup % 2])

    def for_each_chunk(sup, fn):
        for c in range(CHUNKS_PER_SUPER):
            @pl.when(c < nchunk_ref[sup])
            def _(c=c):
                fn(sup, c)

    @pl.when((s == 0) & (j == 0))
    def _():
        def load(sup, c):
            x_copy(sup, c).start()
            x_finish(sup, c)
        for_each_chunk(0, load)

    @pl.when((s >= 2) & (j == 0))
    def _():
        for_each_chunk(jnp.maximum(s - 2, 0), lambda sup, c: out_copy(sup, c).wait())

    prev = jnp.maximum(s - 1, 0)

    @pl.when((s >= 1) & (j < nchunk_ref[prev]))
    def _():
        out_copy(prev, j).start()

    nxt = jnp.minimum(s + 1, max_super - 1)
    n_next = jnp.where(s + 1 < max_super, nchunk_ref[nxt], 0)

    @pl.when((j >= 1) & (j - 1 < n_next))
    def _():
        x_finish(nxt, j - 1)

    @pl.when(j < n_next)
    def _():
        x_copy(nxt, j).start()

    @pl.when(n_chunks > 0)
    def _():
        def partial_out(n_rows):
            rows = pl.ds(0, n_rows)
            xb = x_bf[cur, rows, :]
            gate = _dot(xb, wg_ref[0].astype(_BF16)) + bg_ref[0]
            up = _dot(xb, wu_ref[0].astype(_BF16)) + bu_ref[0]
            gate = jnp.minimum(gate, SWIGLU_LIMIT)
            up = jnp.clip(up, -SWIGLU_LIMIT, SWIGLU_LIMIT)
            glu = gate * jax.nn.sigmoid(gate * SWIGLU_ALPHA)
            act = ((up + 1.0) * glu).astype(_BF16)
            return rows, _dot(act, wd_ref[0].astype(_BF16))

        def sweep(update):
            n_half = nhalf_ref[s]
            for n_rows, (lo, hi) in ROW_VARIANTS.items():
                @pl.when((n_half >= lo) & (n_half <= hi))
                def _(n_rows=n_rows):
                    update(n_rows)

        def init(n_rows):
            rows, o = partial_out(n_rows)
            acc[cur, rows, :] = o + bd_ref[0]

        def accumulate(n_rows):
            rows, o = partial_out(n_rows)
            acc[cur, rows, :] += o

        @pl.when(j == 0)
        def _():
            sweep(init)

        @pl.when(j > 0)
        def _():
            sweep(accumulate)

    @pl.when((s == max_super - 1) & (j == N_FF_TILES - 1))
    def _():
        for_each_chunk(max_super - 1, lambda sup, c: out_copy(sup, c).start())
        for_each_chunk(max_super - 2, lambda sup, c: out_copy(sup, c).wait())
        for_each_chunk(max_super - 1, lambda sup, c: out_copy(sup, c).wait())


def _experts(max_super, sup_e, sup_blk, nchunk, nhalf, buf_x, w_gate_up, b_gate_up3, w_down, b_down3):
    def ff_tile(j, nchunk, s):
        return jnp.where(nchunk[s] > 0, j, N_FF_TILES - 1)

    x_spec = pl.BlockSpec(memory_space=pl.ANY)
    wg_spec = pl.BlockSpec((1, D_MODEL, FF_TILE),
                           lambda s, j, se, sb, nc, nh: (se[s], 0, ff_tile(j, nc, s)))
    wu_spec = pl.BlockSpec((1, D_MODEL, FF_TILE),
                           lambda s, j, se, sb, nc, nh: (se[s], 0, N_FF_TILES + ff_tile(j, nc, s)))
    bg_spec = pl.BlockSpec((1, 1, FF_TILE),
                           lambda s, j, se, sb, nc, nh: (se[s], 0, ff_tile(j, nc, s)))
    bu_spec = pl.BlockSpec((1, 1, FF_TILE),
                           lambda s, j, se, sb, nc, nh: (se[s], 0, N_FF_TILES + ff_tile(j, nc, s)))
    wd_spec = pl.BlockSpec((1, FF_TILE, D_MODEL),
                           lambda s, j, se, sb, nc, nh: (se[s], ff_tile(j, nc, s), 0))
    bd_spec = pl.BlockSpec((1, 1, D_MODEL), lambda s, j, se, sb, nc, nh: (se[s], 0, 0))
    out_spec = pl.BlockSpec(memory_space=pl.ANY)
    return pl.pallas_call(
        functools.partial(_expert_kernel, max_super=max_super),
        grid_spec=pltpu.PrefetchScalarGridSpec(
            num_scalar_prefetch=4,
            grid=(max_super, N_FF_TILES),
            in_specs=[x_spec, wg_spec, wu_spec, bg_spec, bu_spec, wd_spec, bd_spec],
            out_specs=out_spec,
            scratch_shapes=[pltpu.VMEM((X_SLOTS, ROW_CHUNK, D_MODEL), _F32),
                            pltpu.VMEM((2, SUPER_ROWS, D_MODEL), _BF16),
                            pltpu.VMEM((2, SUPER_ROWS, D_MODEL), _F32),
                            pltpu.SemaphoreType.DMA((X_SLOTS,)),
                            pltpu.SemaphoreType.DMA((2,))],
        ),
        out_shape=jax.ShapeDtypeStruct((MAX_SUPER * SUPER_ROWS, D_MODEL), _F32),
        compiler_params=pltpu.CompilerParams(dimension_semantics=("arbitrary", "arbitrary"),
                                             vmem_limit_bytes=VMEM_LIMIT),
        name="experts",
    )(sup_e, sup_blk, nchunk, nhalf, buf_x, w_gate_up, w_gate_up, b_gate_up3, b_gate_up3, w_down, b_down3)


def _combine_kernel(dest_ref, h1_ref, gates_ref, lng_ref, y_hbm, out_ref, rows, sem):
    i = pl.program_id(0)
    n_steps = pl.num_programs(0)

    def row_copy(tile, slot, t8, u, k):
        t = pl.multiple_of(t8 * SUBLANES, SUBLANES) + u
        src = dest_ref[(tile * TC + t) * TOP_K + k]
        return pltpu.make_async_copy(y_hbm.at[pl.ds(src, 1)], rows.at[slot, k, pl.ds(t, 1)],
                                     sem.at[slot])

    def issue_tile(tile, slot):
        def body(t8, carry):
            for u in range(SUBLANES):
                for k in range(TOP_K):
                    row_copy(tile, slot, t8, u, k).start()
            return carry
        lax.fori_loop(0, TC // SUBLANES, body, 0)

    def wait_tile(tile, slot):
        def body(t8, carry):
            for u in range(SUBLANES):
                for k in range(TOP_K):
                    row_copy(tile, slot, t8, u, k).wait()
            return carry
        lax.fori_loop(0, TC // SUBLANES, body, 0)

    @pl.when(i == 0)
    def _():
        issue_tile(0, 0)

    for slot in range(2):
        @pl.when(i % 2 == slot)
        def _(slot=slot):
            @pl.when(i + 1 < n_steps)
            def _():
                issue_tile(i + 1, 1 - slot)

            wait_tile(i, slot)
            g = gates_ref[...]
            acc = h1_ref[...]
            for k in range(TOP_K):
                acc = acc + rows[slot, k] * g[:, k:k + 1]
            out_ref[...] = _rms_rows(acc, lng_ref[...])


def _combine(dest, h1, gates, ln_final_g, y_sorted):
    return pl.pallas_call(
        _combine_kernel,
        grid_spec=pltpu.PrefetchScalarGridSpec(
            num_scalar_prefetch=1,
            grid=(N_TOK // TC,),
            in_specs=[pl.BlockSpec((TC, D_MODEL), lambda i, *_: (i, 0)),
                      pl.BlockSpec((TC, LANES), lambda i, *_: (i, 0)),
                      pl.BlockSpec((1, D_MODEL), lambda i, *_: (0, 0)),
                      pl.BlockSpec(memory_space=pl.ANY)],
            out_specs=pl.BlockSpec((TC, D_MODEL), lambda i, *_: (i, 0)),
            scratch_shapes=[pltpu.VMEM((2, TOP_K, TC, D_MODEL), _F32),
                            pltpu.SemaphoreType.DMA((2,))],
        ),
        out_shape=jax.ShapeDtypeStruct((N_TOK, D_MODEL), _F32),
        compiler_params=pltpu.CompilerParams(dimension_semantics=("arbitrary",),
                                             vmem_limit_bytes=VMEM_LIMIT),
        name="combine",
    )(dest, h1, gates, ln_final_g, y_sorted)


def kernel(x, ln_mix_g, w_in, conv_w, sgu_ln_g, sgu_ln_b, sgu_w, sgu_b, gn_conv, gn_sgu, w_out,
           ln_ffn_g, w_router, b_router, w_gate_up, b_gate_up, w_down, b_down, ln_final_g):
    row = lambda v: v.reshape(1, -1)
    x2d = x.reshape(N_TOK, D_MODEL)

    sgu_b_full = jnp.repeat(jnp.transpose(sgu_b), SGU_HEAD_DIM, axis=1)
    wr_hi = w_router.astype(_BF16)
    wr_lo = (w_router - wr_hi.astype(_F32)).astype(_BF16)
    pad = ((0, 0), (0, LANES - N_EXPERTS))
    wr_split = jnp.concatenate([jnp.pad(wr_hi, pad), jnp.pad(wr_lo, pad)], axis=1)
    br_pad = jnp.pad(b_router, (0, LANES - N_EXPERTS)).reshape(1, LANES)

    h1, xn2, route, gates, counts = _mixer_router(
        x2d, row(ln_mix_g), w_in.astype(_BF16), conv_w, row(sgu_ln_g), row(sgu_ln_b), sgu_w,
        sgu_b_full, row(gn_conv), row(gn_sgu), w_out.astype(_BF16), row(ln_ffn_g), wr_split, br_pad)

    sizes = counts[0, :N_EXPERTS].astype(jnp.int32)
    n_sup = (sizes + SUPER_ROWS - 1) // SUPER_ROWS
    sup_end = jnp.cumsum(n_sup)
    sup_start = sup_end - n_sup
    total_sup = sup_end[-1]
    row_start = (sup_start * SUPER_ROWS).astype(jnp.int32)
    sid = jnp.arange(MAX_SUPER, dtype=jnp.int32)
    sid_eff = jnp.minimum(sid, total_sup - 1)
    sup_e = jnp.minimum(jnp.sum((sid_eff[:, None] >= sup_end[None, :]).astype(jnp.int32), axis=1),
                        N_EXPERTS - 1)
    of_expert = sup_e[:, None] == jnp.arange(N_EXPERTS, dtype=jnp.int32)[None, :]
    rows_left = jnp.sum(jnp.where(of_expert, sizes - (sid_eff[:, None] - sup_start) * SUPER_ROWS, 0),
                        axis=1)
    valid = jnp.clip(rows_left, 0, SUPER_ROWS)
    nchunk = jnp.where(sid < total_sup, (valid + ROW_CHUNK - 1) // ROW_CHUNK, 0).astype(jnp.int32)
    nhalf = jnp.where(sid < total_sup, (valid + HALF_CHUNK - 1) // HALF_CHUNK, 0).astype(jnp.int32)
    zflag = (sizes > 0).astype(jnp.int32)
    zrow = (row_start + ((jnp.maximum(sizes, 1) - 1) // ROW_CHUNK) * ROW_CHUNK).astype(jnp.int32)
    eid = route[:, 0:TOP_K]
    eid_start = jnp.sum(jnp.where(eid[..., None] == jnp.arange(N_EXPERTS, dtype=jnp.int32),
                                  row_start, 0), axis=-1)
    dest = (eid_start + route[:, TOP_K:2 * TOP_K]).reshape(-1)

    buf_x = _dispatch(dest, zrow, zflag, xn2)
    expert_args = (sup_e, sid_eff.astype(jnp.int32), nchunk, nhalf, buf_x, w_gate_up,
                   b_gate_up.reshape(N_EXPERTS, 1, 2 * D_FF), w_down,
                   b_down.reshape(N_EXPERTS, 1, D_MODEL))
    y_sorted = lax.cond(total_sup <= SHORT_GRID,
                        functools.partial(_experts, SHORT_GRID),
                        functools.partial(_experts, MAX_SUPER), *expert_args)
    out = _combine(dest, h1, gates, row(ln_final_g), y_sorted)
    return out.reshape(BATCH, SEQ, D_MODEL)
```

```python
import functools

import jax
import jax.numpy as jnp
from jax import lax
from jax.experimental import pallas as pl
from jax.experimental.pallas import tpu as pltpu

D_MODEL = 2048
BATCH = 2
SEQ = 4096
N_TOK = BATCH * SEQ

D_CONV = 1024
N_CONV_GROUPS = 8
CONV_WIDTH = 3
D_SGU = 1024
N_SGU_HEADS = 8
SGU_HEAD_DIM = 128
CHUNK = 128
D_IN_PROJ = 3 * D_CONV + 2 * D_SGU

N_EXPERTS = 32
TOP_K = 4
D_FF = 2048
SWIGLU_LIMIT = 7.0
SWIGLU_ALPHA = 1.702
RMS_EPS = 1e-5
LN_EPS = 1e-5

LANES = 128
SUBLANES = 8
VMEM_LIMIT = 56 * 1024 * 1024

TM = 256
SUPER_ROWS = 1280
ROW_CHUNK = 256
FF_TILE = 256
N_FF_TILES = D_FF // FF_TILE
MAX_SUPER = (N_TOK * TOP_K + N_EXPERTS * (SUPER_ROWS - 1)) // SUPER_ROWS
SHORT_GRID = N_EXPERTS + 8
CHUNKS_PER_SUPER = SUPER_ROWS // ROW_CHUNK
assert CHUNKS_PER_SUPER + 1 <= N_FF_TILES
X_SLOTS = 2
HALF_CHUNK = ROW_CHUNK // 2
FINE_FROM = 7


def _row_variants():
    variants = {}
    for n_half in range(1, SUPER_ROWS // HALF_CHUNK + 1):
        if n_half >= FINE_FROM:
            n_rows = n_half * HALF_CHUNK
        else:
            n_rows = -(-n_half // 2) * ROW_CHUNK
        lo, hi = variants.get(n_rows, (n_half, n_half))
        variants[n_rows] = (min(lo, n_half), max(hi, n_half))
    return variants


ROW_VARIANTS = _row_variants()
TD = 256
IN_SLOTS = 3
assert N_TOK // TD >= 2
TC = 256

_F32 = jnp.float32
_BF16 = jnp.bfloat16


def _dot(a, b):
    return jnp.dot(a, b, preferred_element_type=_F32)


def _gelu_exact(x):
    return 0.5 * x * (1.0 + lax.erf(x * (2.0 ** -0.5)))


def _rms_rows(x, gain):
    return x * lax.rsqrt(jnp.mean(x * x, axis=-1, keepdims=True) + RMS_EPS) * gain


def _mixer_router_kernel(x_ref, lng_ref, win_ref, convw_ref, slg_ref, slb_ref, sw_ref, sb_ref,
                         gnc_ref, gns_ref, wout_ref, lnf_ref, wr_ref, br_ref,
                         h1_ref, xn2_ref, route_ref, gates_ref, counts_ref,
                         cbuf, carry, ybuf):
    i = pl.program_id(0)

    @pl.when(i % (SEQ // TM) == 0)
    def _():
        cbuf[0:SUBLANES, :] = jnp.zeros((SUBLANES, D_CONV), _F32)

    @pl.when(i == 0)
    def _():
        carry[...] = jnp.zeros_like(carry)

    x = x_ref[...]
    xn = _rms_rows(x, lng_ref[...]).astype(_BF16)

    b_gate = _dot(xn, win_ref[:, 0:D_CONV])
    c_gate = _dot(xn, win_ref[:, D_CONV:2 * D_CONV])
    hh = _dot(xn, win_ref[:, 2 * D_CONV:3 * D_CONV])
    ch = c_gate * hh
    cbuf[SUBLANES:SUBLANES + TM, :] = ch
    ch1 = cbuf[SUBLANES - 1:SUBLANES - 1 + TM, :]
    ch2 = cbuf[SUBLANES - 2:SUBLANES - 2 + TM, :]
    conv = convw_ref[0:1, :] * ch2 + convw_ref[1:2, :] * ch1 + convw_ref[2:3, :] * ch
    cbuf[0:SUBLANES, :] = cbuf[TM:TM + SUBLANES, :]
    y_conv = b_gate * conv
    for g in range(N_CONV_GROUPS):
        sl = slice(g * LANES, (g + 1) * LANES)
        blk = y_conv[:, sl]
        ms = jnp.mean(blk * blk, axis=-1, keepdims=True)
        ybuf[:, sl] = (blk * lax.rsqrt(ms + RMS_EPS) * gnc_ref[:, sl]).astype(_BF16)

    gu = _gelu_exact(_dot(xn, win_ref[:, 3 * D_CONV:3 * D_CONV + D_SGU]))
    gv = _gelu_exact(_dot(xn, win_ref[:, 3 * D_CONV + D_SGU:D_IN_PROJ]))
    row_c = lax.broadcasted_iota(jnp.int32, (CHUNK, CHUNK), 0)
    col_c = lax.broadcasted_iota(jnp.int32, (CHUNK, CHUNK), 1)
    causal = row_c >= col_c
    for h in range(N_SGU_HEADS):
        sl = slice(h * SGU_HEAD_DIM, (h + 1) * SGU_HEAD_DIM)
        vh = gv[:, sl]
        mu = jnp.mean(vh, axis=-1, keepdims=True)
        xc = vh - mu
        var = jnp.mean(xc * xc, axis=-1, keepdims=True)
        vn = (xc * lax.rsqrt(var + LN_EPS) * slg_ref[:, sl] + slb_ref[:, sl]).astype(_BF16)
        wm = jnp.where(causal, sw_ref[h], 0.0).astype(_BF16)
        for c in range(TM // CHUNK):
            rows = slice(c * CHUNK, (c + 1) * CHUNK)
            mixed = _dot(wm, vn[rows, :]) + sb_ref[:, sl]
            ys = gu[rows, sl] * mixed
            ms = jnp.mean(ys * ys, axis=-1, keepdims=True)
            ybuf[rows, D_CONV + h * SGU_HEAD_DIM:D_CONV + (h + 1) * SGU_HEAD_DIM] = (
                ys * lax.rsqrt(ms + RMS_EPS) * gns_ref[:, sl]).astype(_BF16)

    h1 = x + _dot(ybuf[...], wout_ref[...])
    h1_ref[...] = h1
    xn2 = _rms_rows(h1, lnf_ref[...])
    xn2_ref[...] = xn2

    x_hi = xn2.astype(_BF16)
    x_lo = (xn2 - x_hi.astype(_F32)).astype(_BF16)
    p = _dot(x_hi, wr_ref[...]) + _dot(x_lo, wr_ref[...])
    logits = p[:, :LANES] + p[:, LANES:] + br_ref[...]
    cur = jnp.transpose(logits)[0:N_EXPERTS, :]
    expert = lax.broadcasted_iota(jnp.int32, (N_EXPERTS, TM), 0).astype(_F32)
    neg_inf = jnp.float32(-jnp.inf)

    vals, ids, onehots = [], [], []
    for _ in range(TOP_K):
        m = jnp.max(cur, axis=0, keepdims=True)
        idx = jnp.min(jnp.where(cur == m, expert, float(N_EXPERTS)), axis=0, keepdims=True)
        oh = expert == idx
        vals.append(m)
        ids.append(idx)
        onehots.append(oh)
        cur = jnp.where(oh, neg_inf, cur)
    exps = [jnp.exp(v - vals[0]) for v in vals]
    denom = exps[0] + exps[1] + exps[2] + exps[3]
    gates = [e / denom for e in exps]

    mask = (onehots[0] | onehots[1] | onehots[2] | onehots[3]).astype(_F32)
    row_t = lax.broadcasted_iota(jnp.int32, (TM, TM), 0)
    col_t = lax.broadcasted_iota(jnp.int32, (TM, TM), 1)
    earlier = (row_t < col_t).astype(_BF16)
    before = _dot(mask.astype(_BF16), earlier) + carry[:, 0:1]
    ranks = [jnp.sum(jnp.where(oh, before, 0.0), axis=0, keepdims=True) for oh in onehots]
    carry[...] = carry[...] + jnp.sum(mask, axis=1, keepdims=True)
    counts_ref[...] = carry[...]

    out_row = lax.broadcasted_iota(jnp.int32, (2 * TOP_K, TM), 0)
    route = jnp.zeros((2 * TOP_K, TM), _F32)
    gate_out = jnp.zeros((2 * TOP_K, TM), _F32)
    for k in range(TOP_K):
        route = jnp.where(out_row == k, ids[k], route)
        route = jnp.where(out_row == TOP_K + k, ranks[k], route)
        gate_out = jnp.where(out_row == k, gates[k], gate_out)
    route_ref[...] = route.astype(jnp.int32)
    gates_ref[...] = gate_out


def _mixer_router(x2d, ln_mix_g, w_in_bf, conv_w, sgu_ln_g, sgu_ln_b, sgu_w, sgu_b_full,
                  gn_conv, gn_sgu, w_out_bf, ln_ffn_g, wr_split, br_pad):
    def full(a):
        return pl.BlockSpec(a.shape, lambda i: (0,) * a.ndim)

    row_blk = lambda w: pl.BlockSpec((TM, w), lambda i: (i, 0))
    col_blk = pl.BlockSpec((2 * TOP_K, TM), lambda i: (0, i))
    ins = [x2d, ln_mix_g, w_in_bf, conv_w, sgu_ln_g, sgu_ln_b, sgu_w, sgu_b_full,
           gn_conv, gn_sgu, w_out_bf, ln_ffn_g, wr_split, br_pad]
    in_specs = [row_blk(D_MODEL)] + [full(a) for a in ins[1:]]
    return pl.pallas_call(
        _mixer_router_kernel,
        grid=(N_TOK // TM,),
        in_specs=in_specs,
        out_specs=[row_blk(D_MODEL), row_blk(D_MODEL), col_blk, col_blk,
                   pl.BlockSpec((N_EXPERTS, LANES), lambda i: (0, 0))],
        out_shape=[jax.ShapeDtypeStruct((N_TOK, D_MODEL), _F32),
                   jax.ShapeDtypeStruct((N_TOK, D_MODEL), _F32),
                   jax.ShapeDtypeStruct((2 * TOP_K, N_TOK), jnp.int32),
                   jax.ShapeDtypeStruct((2 * TOP_K, N_TOK), _F32),
                   jax.ShapeDtypeStruct((N_EXPERTS, LANES), _F32)],
        scratch_shapes=[pltpu.VMEM((TM + SUBLANES, D_CONV), _F32),
                        pltpu.VMEM((N_EXPERTS, LANES), _F32),
                        pltpu.VMEM((TM, D_MODEL), _BF16)],
        compiler_params=pltpu.CompilerParams(dimension_semantics=("arbitrary",),
                                             vmem_limit_bytes=VMEM_LIMIT),
        name="mixer_router",
    )(*ins)


def _dispatch_kernel(dest_ref, zrow_ref, zflag_ref,
                     x_hbm, bufx_hbm, x_in, zbuf, zsem, in_sem, out_sem):
    i = pl.program_id(0)
    n_steps = pl.num_programs(0)

    @pl.when(i == 0)
    def _():
        zbuf[...] = jnp.zeros_like(zbuf)

        def zero_copy(e):
            z0 = pl.multiple_of(zrow_ref[e], ROW_CHUNK)
            return pltpu.make_async_copy(zbuf, bufx_hbm.at[pl.ds(z0, ROW_CHUNK)], zsem)

        for e in range(N_EXPERTS):
            @pl.when(zflag_ref[e] > 0)
            def _(e=e):
                zero_copy(e).start()
        for e in range(N_EXPERTS):
            @pl.when(zflag_ref[e] > 0)
            def _(e=e):
                zero_copy(e).wait()

    def fetch(blk, slot):
        r = pl.multiple_of(blk * TD, TD)
        return pltpu.make_async_copy(x_hbm.at[pl.ds(r, TD)], x_in.at[slot], in_sem.at[slot])

    def row_copy(blk, slot, t8, u, k):
        t = pl.multiple_of(t8 * SUBLANES, SUBLANES) + u
        dst = dest_ref[(blk * TD + t) * TOP_K + k]
        return pltpu.make_async_copy(x_in.at[slot, pl.ds(t, 1)], bufx_hbm.at[pl.ds(dst, 1)],
                                     out_sem.at[blk % 2])

    def scatter(blk, slot, start):
        def body(t8, carry):
            for u in range(SUBLANES):
                for k in range(TOP_K):
                    copy = row_copy(blk, slot, t8, u, k)
                    copy.start() if start else copy.wait()
            return carry
        lax.fori_loop(0, TD // SUBLANES, body, 0)

    def in_slot_of(blk, fn):
        for q in range(IN_SLOTS):
            @pl.when(blk % IN_SLOTS == q)
            def _(q=q):
                fn(q)

    @pl.when(i == 0)
    def _():
        fetch(0, 0).start()
        fetch(1, 1).start()

    in_slot_of(i, lambda q: fetch(i, q).wait())
    in_slot_of(i, lambda q: scatter(i, q, True))

    @pl.when(i >= 1)
    def _():
        in_slot_of(i - 1, lambda q: scatter(i - 1, q, False))

    @pl.when(i + 2 < n_steps)
    def _():
        in_slot_of(i + 2, lambda q: fetch(i + 2, q).start())

    @pl.when(i == n_steps - 1)
    def _():
        in_slot_of(i, lambda q: scatter(i, q, False))


def _dispatch(dest, zrow, zflag, xn2):
    return pl.pallas_call(
        _dispatch_kernel,
        grid_spec=pltpu.PrefetchScalarGridSpec(
            num_scalar_prefetch=3,
            grid=(N_TOK // TD,),
            in_specs=[pl.BlockSpec(memory_space=pl.ANY)],
            out_specs=pl.BlockSpec(memory_space=pl.ANY),
            scratch_shapes=[pltpu.VMEM((IN_SLOTS, TD, D_MODEL), _F32),
                            pltpu.VMEM((ROW_CHUNK, D_MODEL), _F32),
                            pltpu.SemaphoreType.DMA(()),
                            pltpu.SemaphoreType.DMA((IN_SLOTS,)),
                            pltpu.SemaphoreType.DMA((2,))],
        ),
        out_shape=jax.ShapeDtypeStruct((MAX_SUPER * SUPER_ROWS, D_MODEL), _F32),
        compiler_params=pltpu.CompilerParams(dimension_semantics=("arbitrary",),
                                             vmem_limit_bytes=VMEM_LIMIT),
        name="dispatch",
    )(dest, zrow, zflag, xn2)


def _expert_kernel(sup_e_ref, sup_blk_ref, nchunk_ref, nhalf_ref,
                   x_hbm, wg_ref, wu_ref, bg_ref, bu_ref, wd_ref, bd_ref, out_hbm,
                   x_stage, x_bf, acc, x_sem, o_sem, *, max_super):
    s = pl.program_id(0)
    j = pl.program_id(1)
    n_chunks = nchunk_ref[s]
    cur = s % 2

    def x_copy(sup, c):
        r = pl.multiple_of(sup_blk_ref[sup] * SUPER_ROWS + c * ROW_CHUNK, ROW_CHUNK)
        slot = c % X_SLOTS
        return pltpu.make_async_copy(x_hbm.at[pl.ds(r, ROW_CHUNK)], x_stage.at[slot],
                                     x_sem.at[slot])

    def x_finish(sup, c):
        x_copy(sup, c).wait()
        rows = pl.ds(pl.multiple_of(c * ROW_CHUNK, ROW_CHUNK), ROW_CHUNK)
        x_bf[sup % 2, rows, :] = x_stage[c % X_SLOTS].astype(_BF16)

    def out_copy(sup, c):
        r = pl.multiple_of(sup_blk_ref[sup] * SUPER_ROWS + c * ROW_CHUNK, ROW_CHUNK)
        rows = pl.ds(pl.multiple_of(c * ROW_CHUNK, ROW_CHUNK), ROW_CHUNK)
        return pltpu.make_async_copy(acc.at[sup % 2, rows], out_hbm.at[pl.ds(r, ROW_CHUNK)],
                                     o_sem.at[sup % 2])

    def for_each_chunk(sup, fn):
        for c in range(CHUNKS_PER_SUPER):
            @pl.when(c < nchunk_ref[sup])
            def _(c=c):
                fn(sup, c)

    @pl.when((s == 0) & (j == 0))
    def _():
        def load(sup, c):
            x_copy(sup, c).start()
            x_finish(sup, c)
        for_each_chunk(0, load)

    @pl.when((s >= 2) & (j == 0))
    def _():
        for_each_chunk(jnp.maximum(s - 2, 0), lambda sup, c: out_copy(sup, c).wait())

    prev = jnp.maximum(s - 1, 0)

    @pl.when((s >= 1) & (j < nchunk_ref[prev]))
    def _():
        out_copy(prev, j).start()

    nxt = jnp.minimum(s + 1, max_super - 1)
    n_next = jnp.where(s + 1 < max_super, nchunk_ref[nxt], 0)

    @pl.when((j >= 1) & (j - 1 < n_next))
    def _():
        x_finish(nxt, j - 1)

    @pl.when(j < n_next)
    def _():
        x_copy(nxt, j).start()

    @pl.when(n_chunks > 0)
    def _():
        def partial_out(n_rows):
            rows = pl.ds(0, n_rows)
            xb = x_bf[cur, rows, :]
            gate = _dot(xb, wg_ref[0].astype(_BF16)) + bg_ref[0]
            up = _dot(xb, wu_ref[0].astype(_BF16)) + bu_ref[0]
            gate = jnp.minimum(gate, SWIGLU_LIMIT)
            up = jnp.clip(up, -SWIGLU_LIMIT, SWIGLU_LIMIT)
            glu = gate * jax.nn.sigmoid(gate * SWIGLU_ALPHA)
            act = ((up + 1.0) * glu).astype(_BF16)
            return rows, _dot(act, wd_ref[0].astype(_BF16))

        def sweep(update):
            n_half = nhalf_ref[s]
            for n_rows, (lo, hi) in ROW_VARIANTS.items():
                @pl.when((n_half >= lo) & (n_half <= hi))
                def _(n_rows=n_rows):
                    update(n_rows)

        def init(n_rows):
            rows, o = partial_out(n_rows)
            acc[cur, rows, :] = o + bd_ref[0]

        def accumulate(n_rows):
            rows, o = partial_out(n_rows)
            acc[cur, rows, :] += o

        @pl.when(j == 0)
        def _():
            sweep(init)

        @pl.when(j > 0)
        def _():
            sweep(accumulate)

    @pl.when((s == max_super - 1) & (j == N_FF_TILES - 1))
    def _():
        for_each_chunk(max_super - 1, lambda sup, c: out_copy(sup, c).start())
        for_each_chunk(max_super - 2, lambda sup, c: out_copy(sup, c).wait())
        for_each_chunk(max_super - 1, lambda sup, c: out_copy(sup, c).wait())


def _experts(max_super, sup_e, sup_blk, nchunk, nhalf, buf_x, w_gate_up, b_gate_up3, w_down, b_down3):
    def ff_tile(j, nchunk, s):
        return jnp.where(nchunk[s] > 0, j, N_FF_TILES - 1)

    x_spec = pl.BlockSpec(memory_space=pl.ANY)
    wg_spec = pl.BlockSpec((1, D_MODEL, FF_TILE),
                           lambda s, j, se, sb, nc, nh: (se[s], 0, ff_tile(j, nc, s)))
    wu_spec = pl.BlockSpec((1, D_MODEL, FF_TILE),
                           lambda s, j, se, sb, nc, nh: (se[s], 0, N_FF_TILES + ff_tile(j, nc, s)))
    bg_spec = pl.BlockSpec((1, 1, FF_TILE),
                           lambda s, j, se, sb, nc, nh: (se[s], 0, ff_tile(j, nc, s)))
    bu_spec = pl.BlockSpec((1, 1, FF_TILE),
                           lambda s, j, se, sb, nc, nh: (se[s], 0, N_FF_TILES + ff_tile(j, nc, s)))
    wd_spec = pl.BlockSpec((1, FF_TILE, D_MODEL),
                           lambda s, j, se, sb, nc, nh: (se[s], ff_tile(j, nc, s), 0))
    bd_spec = pl.BlockSpec((1, 1, D_MODEL), lambda s, j, se, sb, nc, nh: (se[s], 0, 0))
    out_spec = pl.BlockSpec(memory_space=pl.ANY)
    return pl.pallas_call(
        functools.partial(_expert_kernel, max_super=max_super),
        grid_spec=pltpu.PrefetchScalarGridSpec(
            num_scalar_prefetch=4,
            grid=(max_super, N_FF_TILES),
            in_specs=[x_spec, wg_spec, wu_spec, bg_spec, bu_spec, wd_spec, bd_spec],
            out_specs=out_spec,
            scratch_shapes=[pltpu.VMEM((X_SLOTS, ROW_CHUNK, D_MODEL), _F32),
                            pltpu.VMEM((2, SUPER_ROWS, D_MODEL), _BF16),
                            pltpu.VMEM((2, SUPER_ROWS, D_MODEL), _F32),
                            pltpu.SemaphoreType.DMA((X_SLOTS,)),
                            pltpu.SemaphoreType.DMA((2,))],
        ),
        out_shape=jax.ShapeDtypeStruct((MAX_SUPER * SUPER_ROWS, D_MODEL), _F32),
        compiler_params=pltpu.CompilerParams(dimension_semantics=("arbitrary", "arbitrary"),
                                             vmem_limit_bytes=VMEM_LIMIT),
        name="experts",
    )(sup_e, sup_blk, nchunk, nhalf, buf_x, w_gate_up, w_gate_up, b_gate_up3, b_gate_up3, w_down, b_down3)


def _combine_kernel(dest_ref, h1_ref, gates_ref, lng_ref, y_hbm, out_ref, rows, sem):
    i = pl.program_id(0)
    n_steps = pl.num_programs(0)

    def row_copy(tile, slot, t8, u, k):
        t = pl.multiple_of(t8 * SUBLANES, SUBLANES) + u
        src = dest_ref[(tile * TC + t) * TOP_K + k]
        return pltpu.make_async_copy(y_hbm.at[pl.ds(src, 1)], rows.at[slot, k, pl.ds(t, 1)],
                                     sem.at[slot])

    def issue_tile(tile, slot):
        def body(t8, carry):
            for u in range(SUBLANES):
                for k in range(TOP_K):
                    row_copy(tile, slot, t8, u, k).start()
            return carry
        lax.fori_loop(0, TC // SUBLANES, body, 0)

    def wait_tile(tile, slot):
        def body(t8, carry):
            for u in range(SUBLANES):
                for k in range(TOP_K):
                    row_copy(tile, slot, t8, u, k).wait()
            return carry
        lax.fori_loop(0, TC // SUBLANES, body, 0)

    @pl.when(i == 0)
    def _():
        issue_tile(0, 0)

    for slot in range(2):
        @pl.when(i % 2 == slot)
        def _(slot=slot):
            @pl.when(i + 1 < n_steps)
            def _():
                issue_tile(i + 1, 1 - slot)

            wait_tile(i, slot)
            g = gates_ref[...]
            acc = h1_ref[...]
            for k in range(TOP_K):
                acc = acc + rows[slot, k] * g[:, k:k + 1]
            out_ref[...] = _rms_rows(acc, lng_ref[...])


def _combine(dest, h1, gates, ln_final_g, y_sorted):
    return pl.pallas_call(
        _combine_kernel,
        grid_spec=pltpu.PrefetchScalarGridSpec(
            num_scalar_prefetch=1,
            grid=(N_TOK // TC,),
            in_specs=[pl.BlockSpec((TC, D_MODEL), lambda i, *_: (i, 0)),
                      pl.BlockSpec((TC, TOP_K), lambda i, *_: (i, 0)),
                      pl.BlockSpec((1, D_MODEL), lambda i, *_: (0, 0)),
                      pl.BlockSpec(memory_space=pl.ANY)],
            out_specs=pl.BlockSpec((TC, D_MODEL), lambda i, *_: (i, 0)),
            scratch_shapes=[pltpu.VMEM((2, TOP_K, TC, D_MODEL), _F32),
                            pltpu.SemaphoreType.DMA((2,))],
        ),
        out_shape=jax.ShapeDtypeStruct((N_TOK, D_MODEL), _F32),
        compiler_params=pltpu.CompilerParams(dimension_semantics=("arbitrary",),
                                             vmem_limit_bytes=VMEM_LIMIT),
        name="combine",
    )(dest, h1, gates, ln_final_g, y_sorted)


def kernel(x, ln_mix_g, w_in, conv_w, sgu_ln_g, sgu_ln_b, sgu_w, sgu_b, gn_conv, gn_sgu, w_out,
           ln_ffn_g, w_router, b_router, w_gate_up, b_gate_up, w_down, b_down, ln_final_g):
    row = lambda v: v.reshape(1, -1)
    x2d = x.reshape(N_TOK, D_MODEL)

    sgu_b_full = jnp.repeat(jnp.transpose(sgu_b), SGU_HEAD_DIM, axis=1)
    wr_hi = w_router.astype(_BF16)
    wr_lo = (w_router - wr_hi.astype(_F32)).astype(_BF16)
    pad = ((0, 0), (0, LANES - N_EXPERTS))
    wr_split = jnp.concatenate([jnp.pad(wr_hi, pad), jnp.pad(wr_lo, pad)], axis=1)
    br_pad = jnp.pad(b_router, (0, LANES - N_EXPERTS)).reshape(1, LANES)

    h1, xn2, route, gates, counts = _mixer_router(
        x2d, row(ln_mix_g), w_in.astype(_BF16), conv_w, row(sgu_ln_g), row(sgu_ln_b), sgu_w,
        sgu_b_full, row(gn_conv), row(gn_sgu), w_out.astype(_BF16), row(ln_ffn_g), wr_split, br_pad)

    sizes = counts[:, 0].astype(jnp.int32)
    n_sup = (sizes + SUPER_ROWS - 1) // SUPER_ROWS
    sup_end = jnp.cumsum(n_sup)
    sup_start = sup_end - n_sup
    total_sup = sup_end[-1]
    row_start = (sup_start * SUPER_ROWS).astype(jnp.int32)
    sid = jnp.arange(MAX_SUPER, dtype=jnp.int32)
    sid_eff = jnp.minimum(sid, total_sup - 1)
    sup_e = jnp.minimum(jnp.sum((sid_eff[:, None] >= sup_end[None, :]).astype(jnp.int32), axis=1),
                        N_EXPERTS - 1)
    of_expert = sup_e[:, None] == jnp.arange(N_EXPERTS, dtype=jnp.int32)[None, :]
    rows_left = jnp.sum(jnp.where(of_expert, sizes - (sid_eff[:, None] - sup_start) * SUPER_ROWS, 0),
                        axis=1)
    valid = jnp.clip(rows_left, 0, SUPER_ROWS)
    nchunk = jnp.where(sid < total_sup, (valid + ROW_CHUNK - 1) // ROW_CHUNK, 0).astype(jnp.int32)
    nhalf = jnp.where(sid < total_sup, (valid + HALF_CHUNK - 1) // HALF_CHUNK, 0).astype(jnp.int32)
    zflag = (sizes > 0).astype(jnp.int32)
    zrow = (row_start + ((jnp.maximum(sizes, 1) - 1) // ROW_CHUNK) * ROW_CHUNK).astype(jnp.int32)
    eid = route[0:TOP_K]
    eid_start = jnp.sum(jnp.where(eid[None] == jnp.arange(N_EXPERTS, dtype=jnp.int32)[:, None, None],
                                  row_start[:, None, None], 0), axis=0)
    dest = jnp.transpose(eid_start + route[TOP_K:2 * TOP_K]).reshape(-1)
    gates = jnp.transpose(gates[0:TOP_K])

    buf_x = _dispatch(dest, zrow, zflag, xn2)
    expert_args = (sup_e, sid_eff.astype(jnp.int32), nchunk, nhalf, buf_x, w_gate_up,
                   b_gate_up.reshape(N_EXPERTS, 1, 2 * D_FF), w_down,
                   b_down.reshape(N_EXPERTS, 1, D_MODEL))
    y_sorted = lax.cond(total_sup <= SHORT_GRID,
                        functools.partial(_experts, SHORT_GRID),
                        functools.partial(_experts, MAX_SUPER), *expert_args)
    out = _combine(dest, h1, gates, row(ln_final_g), y_sorted)
    return out.reshape(BATCH, SEQ, D_MODEL)
```

```python
import functools

import jax
import jax.numpy as jnp
from jax import lax
from jax.experimental import pallas as pl
from jax.experimental.pallas import tpu as pltpu

D_MODEL = 2048
BATCH = 2
SEQ = 4096
N_TOK = BATCH * SEQ

D_CONV = 1024
N_CONV_GROUPS = 8
CONV_WIDTH = 3
D_SGU = 1024
N_SGU_HEADS = 8
SGU_HEAD_DIM = 128
CHUNK = 128
D_IN_PROJ = 3 * D_CONV + 2 * D_SGU

N_EXPERTS = 32
TOP_K = 4
D_FF = 2048
SWIGLU_LIMIT = 7.0
SWIGLU_ALPHA = 1.702
RMS_EPS = 1e-5
LN_EPS = 1e-5

LANES = 128
SUBLANES = 8
VMEM_LIMIT = 56 * 1024 * 1024

TM = 256
SUPER_ROWS = 1280
ROW_CHUNK = 256
FF_TILE = 256
N_FF_TILES = D_FF // FF_TILE
MAX_SUPER = (N_TOK * TOP_K + N_EXPERTS * (SUPER_ROWS - 1)) // SUPER_ROWS
CHUNKS_PER_SUPER = SUPER_ROWS // ROW_CHUNK
assert CHUNKS_PER_SUPER + 1 <= N_FF_TILES
X_SLOTS = 2
HALF_CHUNK = ROW_CHUNK // 2
FINE_FROM = 7


def _row_variants():
    variants = {}
    for n_half in range(1, SUPER_ROWS // HALF_CHUNK + 1):
        if n_half >= FINE_FROM:
            n_rows = n_half * HALF_CHUNK
        else:
            n_rows = -(-n_half // 2) * ROW_CHUNK
        lo, hi = variants.get(n_rows, (n_half, n_half))
        variants[n_rows] = (min(lo, n_half), max(hi, n_half))
    return variants


ROW_VARIANTS = _row_variants()
TD = 256
IN_SLOTS = 3
assert N_TOK // TD >= 2
TC = 256

_F32 = jnp.float32
_BF16 = jnp.bfloat16


def _dot(a, b):
    return jnp.dot(a, b, preferred_element_type=_F32)


def _gelu_exact(x):
    return 0.5 * x * (1.0 + lax.erf(x * (2.0 ** -0.5)))


def _rms_rows(x, gain):
    return x * lax.rsqrt(jnp.mean(x * x, axis=-1, keepdims=True) + RMS_EPS) * gain


def _mixer_router_kernel(x_ref, lng_ref, win_ref, convw_ref, slg_ref, slb_ref, sw_ref, sb_ref,
                         gnc_ref, gns_ref, wout_ref, lnf_ref, wr_ref, br_ref,
                         h1_ref, xn2_ref, route_ref, gates_ref, counts_ref,
                         cbuf, carry, ybuf):
    i = pl.program_id(0)

    @pl.when(i % (SEQ // TM) == 0)
    def _():
        cbuf[0:SUBLANES, :] = jnp.zeros((SUBLANES, D_CONV), _F32)

    @pl.when(i == 0)
    def _():
        carry[...] = jnp.zeros_like(carry)

    x = x_ref[...]
    xn = _rms_rows(x, lng_ref[...]).astype(_BF16)

    b_gate = _dot(xn, win_ref[:, 0:D_CONV])
    c_gate = _dot(xn, win_ref[:, D_CONV:2 * D_CONV])
    hh = _dot(xn, win_ref[:, 2 * D_CONV:3 * D_CONV])
    ch = c_gate * hh
    cbuf[SUBLANES:SUBLANES + TM, :] = ch
    ch1 = cbuf[SUBLANES - 1:SUBLANES - 1 + TM, :]
    ch2 = cbuf[SUBLANES - 2:SUBLANES - 2 + TM, :]
    conv = convw_ref[0:1, :] * ch2 + convw_ref[1:2, :] * ch1 + convw_ref[2:3, :] * ch
    cbuf[0:SUBLANES, :] = cbuf[TM:TM + SUBLANES, :]
    y_conv = b_gate * conv
    for g in range(N_CONV_GROUPS):
        sl = slice(g * LANES, (g + 1) * LANES)
        blk = y_conv[:, sl]
        ms = jnp.mean(blk * blk, axis=-1, keepdims=True)
        ybuf[:, sl] = (blk * lax.rsqrt(ms + RMS_EPS) * gnc_ref[:, sl]).astype(_BF16)

    gu = _gelu_exact(_dot(xn, win_ref[:, 3 * D_CONV:3 * D_CONV + D_SGU]))
    gv = _gelu_exact(_dot(xn, win_ref[:, 3 * D_CONV + D_SGU:D_IN_PROJ]))
    row_c = lax.broadcasted_iota(jnp.int32, (CHUNK, CHUNK), 0)
    col_c = lax.broadcasted_iota(jnp.int32, (CHUNK, CHUNK), 1)
    causal = row_c >= col_c
    for h in range(N_SGU_HEADS):
        sl = slice(h * SGU_HEAD_DIM, (h + 1) * SGU_HEAD_DIM)
        vh = gv[:, sl]
        mu = jnp.mean(vh, axis=-1, keepdims=True)
        xc = vh - mu
        var = jnp.mean(xc * xc, axis=-1, keepdims=True)
        vn = (xc * lax.rsqrt(var + LN_EPS) * slg_ref[:, sl] + slb_ref[:, sl]).astype(_BF16)
        wm = jnp.where(causal, sw_ref[h], 0.0).astype(_BF16)
        for c in range(TM // CHUNK):
            rows = slice(c * CHUNK, (c + 1) * CHUNK)
            mixed = _dot(wm, vn[rows, :]) + sb_ref[:, sl]
            ys = gu[rows, sl] * mixed
            ms = jnp.mean(ys * ys, axis=-1, keepdims=True)
            ybuf[rows, D_CONV + h * SGU_HEAD_DIM:D_CONV + (h + 1) * SGU_HEAD_DIM] = (
                ys * lax.rsqrt(ms + RMS_EPS) * gns_ref[:, sl]).astype(_BF16)

    h1 = x + _dot(ybuf[...], wout_ref[...])
    h1_ref[...] = h1
    xn2 = _rms_rows(h1, lnf_ref[...])
    xn2_ref[...] = xn2

    x_hi = xn2.astype(_BF16)
    x_lo = (xn2 - x_hi.astype(_F32)).astype(_BF16)
    p = _dot(x_hi, wr_ref[...]) + _dot(x_lo, wr_ref[...])
    logits = p[:, :LANES] + p[:, LANES:] + br_ref[...]
    cur = jnp.transpose(logits)[0:N_EXPERTS, :]
    expert = lax.broadcasted_iota(jnp.int32, (N_EXPERTS, TM), 0).astype(_F32)
    neg_inf = jnp.float32(-jnp.inf)

    vals, ids, onehots = [], [], []
    for _ in range(TOP_K):
        m = jnp.max(cur, axis=0, keepdims=True)
        idx = jnp.min(jnp.where(cur == m, expert, float(N_EXPERTS)), axis=0, keepdims=True)
        oh = expert == idx
        vals.append(m)
        ids.append(idx)
        onehots.append(oh)
        cur = jnp.where(oh, neg_inf, cur)
    exps = [jnp.exp(v - vals[0]) for v in vals]
    denom = exps[0] + exps[1] + exps[2] + exps[3]
    gates = [e / denom for e in exps]

    mask = (onehots[0] | onehots[1] | onehots[2] | onehots[3]).astype(_F32)
    row_t = lax.broadcasted_iota(jnp.int32, (TM, TM), 0)
    col_t = lax.broadcasted_iota(jnp.int32, (TM, TM), 1)
    earlier = (row_t < col_t).astype(_BF16)
    before = _dot(mask.astype(_BF16), earlier) + carry[:, 0:1]
    ranks = [jnp.sum(jnp.where(oh, before, 0.0), axis=0, keepdims=True) for oh in onehots]
    carry[...] = carry[...] + jnp.sum(mask, axis=1, keepdims=True)
    counts_ref[...] = carry[...]

    out_row = lax.broadcasted_iota(jnp.int32, (2 * TOP_K, TM), 0)
    route = jnp.zeros((2 * TOP_K, TM), _F32)
    gate_out = jnp.zeros((2 * TOP_K, TM), _F32)
    for k in range(TOP_K):
        route = jnp.where(out_row == k, ids[k], route)
        route = jnp.where(out_row == TOP_K + k, ranks[k], route)
        gate_out = jnp.where(out_row == k, gates[k], gate_out)
    route_ref[...] = route.astype(jnp.int32)
    gates_ref[...] = gate_out


def _mixer_router(x2d, ln_mix_g, w_in_bf, conv_w, sgu_ln_g, sgu_ln_b, sgu_w, sgu_b_full,
                  gn_conv, gn_sgu, w_out_bf, ln_ffn_g, wr_split, br_pad):
    def full(a):
        return pl.BlockSpec(a.shape, lambda i: (0,) * a.ndim)

    row_blk = lambda w: pl.BlockSpec((TM, w), lambda i: (i, 0))
    col_blk = pl.BlockSpec((2 * TOP_K, TM), lambda i: (0, i))
    ins = [x2d, ln_mix_g, w_in_bf, conv_w, sgu_ln_g, sgu_ln_b, sgu_w, sgu_b_full,
           gn_conv, gn_sgu, w_out_bf, ln_ffn_g, wr_split, br_pad]
    in_specs = [row_blk(D_MODEL)] + [full(a) for a in ins[1:]]
    return pl.pallas_call(
        _mixer_router_kernel,
        grid=(N_TOK // TM,),
        in_specs=in_specs,
        out_specs=[row_blk(D_MODEL), row_blk(D_MODEL), col_blk, col_blk,
                   pl.BlockSpec((N_EXPERTS, LANES), lambda i: (0, 0))],
        out_shape=[jax.ShapeDtypeStruct((N_TOK, D_MODEL), _F32),
                   jax.ShapeDtypeStruct((N_TOK, D_MODEL), _F32),
                   jax.ShapeDtypeStruct((2 * TOP_K, N_TOK), jnp.int32),
                   jax.ShapeDtypeStruct((2 * TOP_K, N_TOK), _F32),
                   jax.ShapeDtypeStruct((N_EXPERTS, LANES), _F32)],
        scratch_shapes=[pltpu.VMEM((TM + SUBLANES, D_CONV), _F32),
                        pltpu.VMEM((N_EXPERTS, LANES), _F32),
                        pltpu.VMEM((TM, D_MODEL), _BF16)],
        compiler_params=pltpu.CompilerParams(dimension_semantics=("arbitrary",),
                                             vmem_limit_bytes=VMEM_LIMIT),
        name="mixer_router",
    )(*ins)


def _dispatch_kernel(dest_ref, zrow_ref, zflag_ref,
                     x_hbm, bufx_hbm, x_in, zbuf, zsem, in_sem, out_sem):
    i = pl.program_id(0)
    n_steps = pl.num_programs(0)

    @pl.when(i == 0)
    def _():
        zbuf[...] = jnp.zeros_like(zbuf)

        def zero_copy(e):
            z0 = pl.multiple_of(zrow_ref[e], ROW_CHUNK)
            return pltpu.make_async_copy(zbuf, bufx_hbm.at[pl.ds(z0, ROW_CHUNK)], zsem)

        for e in range(N_EXPERTS):
            @pl.when(zflag_ref[e] > 0)
            def _(e=e):
                zero_copy(e).start()
        for e in range(N_EXPERTS):
            @pl.when(zflag_ref[e] > 0)
            def _(e=e):
                zero_copy(e).wait()

    def fetch(blk, slot):
        r = pl.multiple_of(blk * TD, TD)
        return pltpu.make_async_copy(x_hbm.at[pl.ds(r, TD)], x_in.at[slot], in_sem.at[slot])

    def row_copy(blk, slot, t8, u, k):
        t = pl.multiple_of(t8 * SUBLANES, SUBLANES) + u
        dst = dest_ref[(blk * TD + t) * TOP_K + k]
        return pltpu.make_async_copy(x_in.at[slot, pl.ds(t, 1)], bufx_hbm.at[pl.ds(dst, 1)],
                                     out_sem.at[blk % 2])

    def scatter(blk, slot, start):
        def body(t8, carry):
            for u in range(SUBLANES):
                for k in range(TOP_K):
                    copy = row_copy(blk, slot, t8, u, k)
                    copy.start() if start else copy.wait()
            return carry
        lax.fori_loop(0, TD // SUBLANES, body, 0)

    def in_slot_of(blk, fn):
        for q in range(IN_SLOTS):
            @pl.when(blk % IN_SLOTS == q)
            def _(q=q):
                fn(q)

    @pl.when(i == 0)
    def _():
        fetch(0, 0).start()
        fetch(1, 1).start()

    in_slot_of(i, lambda q: fetch(i, q).wait())
    in_slot_of(i, lambda q: scatter(i, q, True))

    @pl.when(i >= 1)
    def _():
        in_slot_of(i - 1, lambda q: scatter(i - 1, q, False))

    @pl.when(i + 2 < n_steps)
    def _():
        in_slot_of(i + 2, lambda q: fetch(i + 2, q).start())

    @pl.when(i == n_steps - 1)
    def _():
        in_slot_of(i, lambda q: scatter(i, q, False))


def _dispatch(dest, zrow, zflag, xn2):
    return pl.pallas_call(
        _dispatch_kernel,
        grid_spec=pltpu.PrefetchScalarGridSpec(
            num_scalar_prefetch=3,
            grid=(N_TOK // TD,),
            in_specs=[pl.BlockSpec(memory_space=pl.ANY)],
            out_specs=pl.BlockSpec(memory_space=pl.ANY),
            scratch_shapes=[pltpu.VMEM((IN_SLOTS, TD, D_MODEL), _F32),
                            pltpu.VMEM((ROW_CHUNK, D_MODEL), _F32),
                            pltpu.SemaphoreType.DMA(()),
                            pltpu.SemaphoreType.DMA((IN_SLOTS,)),
                            pltpu.SemaphoreType.DMA((2,))],
        ),
        out_shape=jax.ShapeDtypeStruct((MAX_SUPER * SUPER_ROWS, D_MODEL), _F32),
        compiler_params=pltpu.CompilerParams(dimension_semantics=("arbitrary",),
                                             vmem_limit_bytes=VMEM_LIMIT),
        name="dispatch",
    )(dest, zrow, zflag, xn2)


def _expert_kernel(sup_e_ref, sup_blk_ref, nchunk_ref, nhalf_ref,
                   x_hbm, wg_ref, wu_ref, bg_ref, bu_ref, wd_ref, bd_ref, out_hbm,
                   x_stage, x_bf, acc, x_sem, o_sem):
    s = pl.program_id(0)
    j = pl.program_id(1)
    n_super = pl.num_programs(0)
    n_chunks = nchunk_ref[s]
    cur = s % 2

    def x_copy(sup, c):
        r = pl.multiple_of(sup_blk_ref[sup] * SUPER_ROWS + c * ROW_CHUNK, ROW_CHUNK)
        slot = c % X_SLOTS
        return pltpu.make_async_copy(x_hbm.at[pl.ds(r, ROW_CHUNK)], x_stage.at[slot],
                                     x_sem.at[slot])

    def x_finish(sup, c):
        x_copy(sup, c).wait()
        rows = pl.ds(pl.multiple_of(c * ROW_CHUNK, ROW_CHUNK), ROW_CHUNK)
        x_bf[sup % 2, rows, :] = x_stage[c % X_SLOTS].astype(_BF16)

    def out_copy(sup, c):
        r = pl.multiple_of(sup_blk_ref[sup] * SUPER_ROWS + c * ROW_CHUNK, ROW_CHUNK)
        rows = pl.ds(pl.multiple_of(c * ROW_CHUNK, ROW_CHUNK), ROW_CHUNK)
        return pltpu.make_async_copy(acc.at[sup % 2, rows], out_hbm.at[pl.ds(r, ROW_CHUNK)],
                                     o_sem.at[sup % 2])

    def for_each_chunk(sup, fn):
        for c in range(CHUNKS_PER_SUPER):
            @pl.when(c < nchunk_ref[sup])
            def _(c=c):
                fn(sup, c)

    @pl.when((s == 0) & (j == 0))
    def _():
        def load(sup, c):
            x_copy(sup, c).start()
            x_finish(sup, c)
        for_each_chunk(0, load)

    @pl.when((s >= 2) & (j == 0))
    def _():
        for_each_chunk(jnp.maximum(s - 2, 0), lambda sup, c: out_copy(sup, c).wait())

    prev = jnp.maximum(s - 1, 0)

    @pl.when((s >= 1) & (j < nchunk_ref[prev]))
    def _():
        out_copy(prev, j).start()

    nxt = jnp.minimum(s + 1, n_super - 1)
    n_next = jnp.where(s + 1 < n_super, nchunk_ref[nxt], 0)

    @pl.when((j >= 1) & (j - 1 < n_next))
    def _():
        x_finish(nxt, j - 1)

    @pl.when(j < n_next)
    def _():
        x_copy(nxt, j).start()

    @pl.when(n_chunks > 0)
    def _():
        def partial_out(n_rows):
            rows = pl.ds(0, n_rows)
            xb = x_bf[cur, rows, :]
            gate = _dot(xb, wg_ref[0].astype(_BF16)) + bg_ref[0]
            up = _dot(xb, wu_ref[0].astype(_BF16)) + bu_ref[0]
            gate = jnp.minimum(gate, SWIGLU_LIMIT)
            up = jnp.clip(up, -SWIGLU_LIMIT, SWIGLU_LIMIT)
            glu = gate * jax.nn.sigmoid(gate * SWIGLU_ALPHA)
            act = ((up + 1.0) * glu).astype(_BF16)
            return rows, _dot(act, wd_ref[0].astype(_BF16))

        def sweep(update):
            n_half = nhalf_ref[s]
            for n_rows, (lo, hi) in ROW_VARIANTS.items():
                @pl.when((n_half >= lo) & (n_half <= hi))
                def _(n_rows=n_rows):
                    update(n_rows)

        def init(n_rows):
            rows, o = partial_out(n_rows)
            acc[cur, rows, :] = o + bd_ref[0]

        def accumulate(n_rows):
            rows, o = partial_out(n_rows)
            acc[cur, rows, :] += o

        @pl.when(j == 0)
        def _():
            sweep(init)

        @pl.when(j > 0)
        def _():
            sweep(accumulate)

    @pl.when((s == n_super - 1) & (j == N_FF_TILES - 1))
    def _():
        for_each_chunk(s, lambda sup, c: out_copy(sup, c).start())

        @pl.when(s >= 1)
        def _():
            for_each_chunk(jnp.maximum(s - 1, 0), lambda sup, c: out_copy(sup, c).wait())

        for_each_chunk(s, lambda sup, c: out_copy(sup, c).wait())


def _experts(n_super, sup_e, sup_blk, nchunk, nhalf, buf_x, w_gate_up, b_gate_up3, w_down, b_down3):
    def ff_tile(j, nchunk, s):
        return jnp.where(nchunk[s] > 0, j, N_FF_TILES - 1)

    x_spec = pl.BlockSpec(memory_space=pl.ANY)
    wg_spec = pl.BlockSpec((1, D_MODEL, FF_TILE),
                           lambda s, j, se, sb, nc, nh: (se[s], 0, ff_tile(j, nc, s)))
    wu_spec = pl.BlockSpec((1, D_MODEL, FF_TILE),
                           lambda s, j, se, sb, nc, nh: (se[s], 0, N_FF_TILES + ff_tile(j, nc, s)))
    bg_spec = pl.BlockSpec((1, 1, FF_TILE),
                           lambda s, j, se, sb, nc, nh: (se[s], 0, ff_tile(j, nc, s)))
    bu_spec = pl.BlockSpec((1, 1, FF_TILE),
                           lambda s, j, se, sb, nc, nh: (se[s], 0, N_FF_TILES + ff_tile(j, nc, s)))
    wd_spec = pl.BlockSpec((1, FF_TILE, D_MODEL),
                           lambda s, j, se, sb, nc, nh: (se[s], ff_tile(j, nc, s), 0))
    bd_spec = pl.BlockSpec((1, 1, D_MODEL), lambda s, j, se, sb, nc, nh: (se[s], 0, 0))
    out_spec = pl.BlockSpec(memory_space=pl.ANY)
    return pl.pallas_call(
        _expert_kernel,
        grid_spec=pltpu.PrefetchScalarGridSpec(
            num_scalar_prefetch=4,
            grid=(n_super, N_FF_TILES),
            in_specs=[x_spec, wg_spec, wu_spec, bg_spec, bu_spec, wd_spec, bd_spec],
            out_specs=out_spec,
            scratch_shapes=[pltpu.VMEM((X_SLOTS, ROW_CHUNK, D_MODEL), _F32),
                            pltpu.VMEM((2, SUPER_ROWS, D_MODEL), _BF16),
                            pltpu.VMEM((2, SUPER_ROWS, D_MODEL), _F32),
                            pltpu.SemaphoreType.DMA((X_SLOTS,)),
                            pltpu.SemaphoreType.DMA((2,))],
        ),
        out_shape=jax.ShapeDtypeStruct((MAX_SUPER * SUPER_ROWS, D_MODEL), _F32),
        compiler_params=pltpu.CompilerParams(dimension_semantics=("arbitrary", "arbitrary"),
                                             vmem_limit_bytes=VMEM_LIMIT),
        name="experts",
    )(sup_e, sup_blk, nchunk, nhalf, buf_x, w_gate_up, w_gate_up, b_gate_up3, b_gate_up3, w_down, b_down3)


def _combine_kernel(dest_ref, h1_ref, gates_ref, lng_ref, y_hbm, out_ref, rows, sem):
    i = pl.program_id(0)
    n_steps = pl.num_programs(0)

    def row_copy(tile, slot, t8, u, k):
        t = pl.multiple_of(t8 * SUBLANES, SUBLANES) + u
        src = dest_ref[(tile * TC + t) * TOP_K + k]
        return pltpu.make_async_copy(y_hbm.at[pl.ds(src, 1)], rows.at[slot, k, pl.ds(t, 1)],
                                     sem.at[slot])

    def issue_tile(tile, slot):
        def body(t8, carry):
            for u in range(SUBLANES):
                for k in range(TOP_K):
                    row_copy(tile, slot, t8, u, k).start()
            return carry
        lax.fori_loop(0, TC // SUBLANES, body, 0)

    def wait_tile(tile, slot):
        def body(t8, carry):
            for u in range(SUBLANES):
                for k in range(TOP_K):
                    row_copy(tile, slot, t8, u, k).wait()
            return carry
        lax.fori_loop(0, TC // SUBLANES, body, 0)

    @pl.when(i == 0)
    def _():
        issue_tile(0, 0)

    for slot in range(2):
        @pl.when(i % 2 == slot)
        def _(slot=slot):
            @pl.when(i + 1 < n_steps)
            def _():
                issue_tile(i + 1, 1 - slot)

            wait_tile(i, slot)
            g = gates_ref[...]
            acc = h1_ref[...]
            for k in range(TOP_K):
                acc = acc + rows[slot, k] * g[:, k:k + 1]
            out_ref[...] = _rms_rows(acc, lng_ref[...])


def _combine(dest, h1, gates, ln_final_g, y_sorted):
    return pl.pallas_call(
        _combine_kernel,
        grid_spec=pltpu.PrefetchScalarGridSpec(
            num_scalar_prefetch=1,
            grid=(N_TOK // TC,),
            in_specs=[pl.BlockSpec((TC, D_MODEL), lambda i, *_: (i, 0)),
                      pl.BlockSpec((TC, TOP_K), lambda i, *_: (i, 0)),
                      pl.BlockSpec((1, D_MODEL), lambda i, *_: (0, 0)),
                      pl.BlockSpec(memory_space=pl.ANY)],
            out_specs=pl.BlockSpec((TC, D_MODEL), lambda i, *_: (i, 0)),
            scratch_shapes=[pltpu.VMEM((2, TOP_K, TC, D_MODEL), _F32),
                            pltpu.SemaphoreType.DMA((2,))],
        ),
        out_shape=jax.ShapeDtypeStruct((N_TOK, D_MODEL), _F32),
        compiler_params=pltpu.CompilerParams(dimension_semantics=("arbitrary",),
                                             vmem_limit_bytes=VMEM_LIMIT),
        name="combine",
    )(dest, h1, gates, ln_final_g, y_sorted)


def kernel(x, ln_mix_g, w_in, conv_w, sgu_ln_g, sgu_ln_b, sgu_w, sgu_b, gn_conv, gn_sgu, w_out,
           ln_ffn_g, w_router, b_router, w_gate_up, b_gate_up, w_down, b_down, ln_final_g):
    row = lambda v: v.reshape(1, -1)
    x2d = x.reshape(N_TOK, D_MODEL)

    sgu_b_full = jnp.repeat(jnp.transpose(sgu_b), SGU_HEAD_DIM, axis=1)
    wr_hi = w_router.astype(_BF16)
    wr_lo = (w_router - wr_hi.astype(_F32)).astype(_BF16)
    pad = ((0, 0), (0, LANES - N_EXPERTS))
    wr_split = jnp.concatenate([jnp.pad(wr_hi, pad), jnp.pad(wr_lo, pad)], axis=1)
    br_pad = jnp.pad(b_router, (0, LANES - N_EXPERTS)).reshape(1, LANES)

    h1, xn2, route, gates, counts = _mixer_router(
        x2d, row(ln_mix_g), w_in.astype(_BF16), conv_w, row(sgu_ln_g), row(sgu_ln_b), sgu_w,
        sgu_b_full, row(gn_conv), row(gn_sgu), w_out.astype(_BF16), row(ln_ffn_g), wr_split, br_pad)

    sizes = counts[:, 0].astype(jnp.int32)
    n_sup = (sizes + SUPER_ROWS - 1) // SUPER_ROWS
    sup_end = jnp.cumsum(n_sup)
    sup_start = sup_end - n_sup
    total_sup = sup_end[-1]
    row_start = (sup_start * SUPER_ROWS).astype(jnp.int32)
    sid = jnp.arange(MAX_SUPER, dtype=jnp.int32)
    sid_eff = jnp.minimum(sid, total_sup - 1)
    sup_e = jnp.minimum(jnp.sum((sid_eff[:, None] >= sup_end[None, :]).astype(jnp.int32), axis=1),
                        N_EXPERTS - 1)
    of_expert = sup_e[:, None] == jnp.arange(N_EXPERTS, dtype=jnp.int32)[None, :]
    rows_left = jnp.sum(jnp.where(of_expert, sizes - (sid_eff[:, None] - sup_start) * SUPER_ROWS, 0),
                        axis=1)
    valid = jnp.clip(rows_left, 0, SUPER_ROWS)
    nchunk = jnp.where(sid < total_sup, (valid + ROW_CHUNK - 1) // ROW_CHUNK, 0).astype(jnp.int32)
    nhalf = jnp.where(sid < total_sup, (valid + HALF_CHUNK - 1) // HALF_CHUNK, 0).astype(jnp.int32)
    zflag = (sizes > 0).astype(jnp.int32)
    zrow = (row_start + ((jnp.maximum(sizes, 1) - 1) // ROW_CHUNK) * ROW_CHUNK).astype(jnp.int32)
    eid = route[0:TOP_K]
    eid_start = jnp.sum(jnp.where(eid[None] == jnp.arange(N_EXPERTS, dtype=jnp.int32)[:, None, None],
                                  row_start[:, None, None], 0), axis=0)
    dest = jnp.transpose(eid_start + route[TOP_K:2 * TOP_K]).reshape(-1)
    gates = jnp.transpose(gates[0:TOP_K])

    buf_x = _dispatch(dest, zrow, zflag, xn2)
    y_sorted = _experts(total_sup, sup_e, sid_eff.astype(jnp.int32), nchunk, nhalf, buf_x, w_gate_up,
                        b_gate_up.reshape(N_EXPERTS, 1, 2 * D_FF), w_down,
                        b_down.reshape(N_EXPERTS, 1, D_MODEL))
    out = _combine(dest, h1, gates, row(ln_final_g), y_sorted)
    return out.reshape(BATCH, SEQ, D_MODEL)
```

```python
import jax
import jax.numpy as jnp
from jax import lax
from jax.experimental import pallas as pl
from jax.experimental.pallas import tpu as pltpu

D_MODEL = 2048
BATCH = 2
SEQ = 4096
N_TOK = BATCH * SEQ

D_CONV = 1024
N_CONV_GROUPS = 8
CONV_WIDTH = 3
D_SGU = 1024
N_SGU_HEADS = 8
SGU_HEAD_DIM = 128
CHUNK = 128
D_IN_PROJ = 3 * D_CONV + 2 * D_SGU

N_EXPERTS = 32
TOP_K = 4
D_FF = 2048
SWIGLU_LIMIT = 7.0
SWIGLU_ALPHA = 1.702
RMS_EPS = 1e-5
LN_EPS = 1e-5

LANES = 128
SUBLANES = 8
VMEM_LIMIT = 56 * 1024 * 1024

TM = 256
SUPER_ROWS = 1280
ROW_CHUNK = 256
FF_TILE = 256
N_FF_TILES = D_FF // FF_TILE
MAX_SUPER = (N_TOK * TOP_K + N_EXPERTS * (SUPER_ROWS - 1)) // SUPER_ROWS
CHUNKS_PER_SUPER = SUPER_ROWS // ROW_CHUNK
assert CHUNKS_PER_SUPER + 1 <= N_FF_TILES
X_SLOTS = 2
HALF_CHUNK = ROW_CHUNK // 2
FINE_FROM = 7


def _row_variants():
    variants = {}
    for n_half in range(1, SUPER_ROWS // HALF_CHUNK + 1):
        if n_half >= FINE_FROM:
            n_rows = n_half * HALF_CHUNK
        else:
            n_rows = -(-n_half // 2) * ROW_CHUNK
        lo, hi = variants.get(n_rows, (n_half, n_half))
        variants[n_rows] = (min(lo, n_half), max(hi, n_half))
    return variants


ROW_VARIANTS = _row_variants()
TD = 256
IN_SLOTS = 3
assert N_TOK // TD >= 2
TC = 256

_F32 = jnp.float32
_BF16 = jnp.bfloat16


def _dot(a, b):
    return jnp.dot(a, b, preferred_element_type=_F32)


def _gelu_exact(x):
    return 0.5 * x * (1.0 + lax.erf(x * (2.0 ** -0.5)))


def _rms_rows(x, gain):
    return x * lax.rsqrt(jnp.mean(x * x, axis=-1, keepdims=True) + RMS_EPS) * gain


def _mixer_router_kernel(x_ref, lng_ref, win_ref, convw_ref, slg_ref, slb_ref, sw_ref, sb_ref,
                         gnc_ref, gns_ref, wout_ref, lnf_ref, wr_ref, br_ref,
                         h1_ref, xn2_ref, route_ref, gates_ref, counts_ref,
                         cbuf, carry, ybuf):
    i = pl.program_id(0)

    @pl.when(i % (SEQ // TM) == 0)
    def _():
        cbuf[0:SUBLANES, :] = jnp.zeros((SUBLANES, D_CONV), _F32)

    @pl.when(i == 0)
    def _():
        carry[...] = jnp.zeros_like(carry)

    x = x_ref[...]
    xn = _rms_rows(x, lng_ref[...]).astype(_BF16)

    b_gate = _dot(xn, win_ref[:, 0:D_CONV])
    c_gate = _dot(xn, win_ref[:, D_CONV:2 * D_CONV])
    hh = _dot(xn, win_ref[:, 2 * D_CONV:3 * D_CONV])
    ch = c_gate * hh
    cbuf[SUBLANES:SUBLANES + TM, :] = ch
    ch1 = cbuf[SUBLANES - 1:SUBLANES - 1 + TM, :]
    ch2 = cbuf[SUBLANES - 2:SUBLANES - 2 + TM, :]
    conv = convw_ref[0:1, :] * ch2 + convw_ref[1:2, :] * ch1 + convw_ref[2:3, :] * ch
    cbuf[0:SUBLANES, :] = cbuf[TM:TM + SUBLANES, :]
    y_conv = b_gate * conv
    for g in range(N_CONV_GROUPS):
        sl = slice(g * LANES, (g + 1) * LANES)
        blk = y_conv[:, sl]
        ms = jnp.mean(blk * blk, axis=-1, keepdims=True)
        ybuf[:, sl] = (blk * lax.rsqrt(ms + RMS_EPS) * gnc_ref[:, sl]).astype(_BF16)

    gu = _gelu_exact(_dot(xn, win_ref[:, 3 * D_CONV:3 * D_CONV + D_SGU]))
    gv = _gelu_exact(_dot(xn, win_ref[:, 3 * D_CONV + D_SGU:D_IN_PROJ]))
    row_c = lax.broadcasted_iota(jnp.int32, (CHUNK, CHUNK), 0)
    col_c = lax.broadcasted_iota(jnp.int32, (CHUNK, CHUNK), 1)
    causal = row_c >= col_c
    for h in range(N_SGU_HEADS):
        sl = slice(h * SGU_HEAD_DIM, (h + 1) * SGU_HEAD_DIM)
        vh = gv[:, sl]
        mu = jnp.mean(vh, axis=-1, keepdims=True)
        xc = vh - mu
        var = jnp.mean(xc * xc, axis=-1, keepdims=True)
        vn = (xc * lax.rsqrt(var + LN_EPS) * slg_ref[:, sl] + slb_ref[:, sl]).astype(_BF16)
        wm = jnp.where(causal, sw_ref[h], 0.0).astype(_BF16)
        for c in range(TM // CHUNK):
            rows = slice(c * CHUNK, (c + 1) * CHUNK)
            mixed = _dot(wm, vn[rows, :]) + sb_ref[:, sl]
            ys = gu[rows, sl] * mixed
            ms = jnp.mean(ys * ys, axis=-1, keepdims=True)
            ybuf[rows, D_CONV + h * SGU_HEAD_DIM:D_CONV + (h + 1) * SGU_HEAD_DIM] = (
                ys * lax.rsqrt(ms + RMS_EPS) * gns_ref[:, sl]).astype(_BF16)

    h1 = x + _dot(ybuf[...], wout_ref[...])
    h1_ref[...] = h1
    xn2 = _rms_rows(h1, lnf_ref[...])
    xn2_ref[...] = xn2

    x_hi = xn2.astype(_BF16)
    x_lo = (xn2 - x_hi.astype(_F32)).astype(_BF16)
    p = _dot(x_hi, wr_ref[...]) + _dot(x_lo, wr_ref[...])
    logits = p[:, :LANES] + p[:, LANES:] + br_ref[...]
    cur = jnp.transpose(logits)[0:N_EXPERTS, :]
    expert = lax.broadcasted_iota(jnp.int32, (N_EXPERTS, TM), 0).astype(_F32)
    neg_inf = jnp.float32(-jnp.inf)

    vals, ids, onehots = [], [], []
    for _ in range(TOP_K):
        m = jnp.max(cur, axis=0, keepdims=True)
        idx = jnp.min(jnp.where(cur == m, expert, float(N_EXPERTS)), axis=0, keepdims=True)
        oh = expert == idx
        vals.append(m)
        ids.append(idx)
        onehots.append(oh)
        cur = jnp.where(oh, neg_inf, cur)
    exps = [jnp.exp(v - vals[0]) for v in vals]
    denom = exps[0] + exps[1] + exps[2] + exps[3]
    gates = [e / denom for e in exps]

    mask = (onehots[0] | onehots[1] | onehots[2] | onehots[3]).astype(_F32)
    row_t = lax.broadcasted_iota(jnp.int32, (TM, TM), 0)
    col_t = lax.broadcasted_iota(jnp.int32, (TM, TM), 1)
    earlier = (row_t < col_t).astype(_BF16)
    before = _dot(mask.astype(_BF16), earlier) + carry[:, 0:1]
    ranks = [jnp.sum(jnp.where(oh, before, 0.0), axis=0, keepdims=True) for oh in onehots]
    carry[...] = carry[...] + jnp.sum(mask, axis=1, keepdims=True)
    counts_ref[...] = carry[...]

    out_row = lax.broadcasted_iota(jnp.int32, (2 * TOP_K, TM), 0)
    route = jnp.zeros((2 * TOP_K, TM), _F32)
    gate_out = jnp.zeros((2 * TOP_K, TM), _F32)
    for k in range(TOP_K):
        route = jnp.where(out_row == k, ids[k], route)
        route = jnp.where(out_row == TOP_K + k, ranks[k], route)
        gate_out = jnp.where(out_row == k, gates[k], gate_out)
    route_ref[...] = route.astype(jnp.int32)
    gates_ref[...] = gate_out


def _mixer_router(x2d, ln_mix_g, w_in_bf, conv_w, sgu_ln_g, sgu_ln_b, sgu_w, sgu_b_full,
                  gn_conv, gn_sgu, w_out_bf, ln_ffn_g, wr_split, br_pad):
    def full(a):
        return pl.BlockSpec(a.shape, lambda i: (0,) * a.ndim)

    row_blk = lambda w: pl.BlockSpec((TM, w), lambda i: (i, 0))
    col_blk = pl.BlockSpec((2 * TOP_K, TM), lambda i: (0, i))
    ins = [x2d, ln_mix_g, w_in_bf, conv_w, sgu_ln_g, sgu_ln_b, sgu_w, sgu_b_full,
           gn_conv, gn_sgu, w_out_bf, ln_ffn_g, wr_split, br_pad]
    in_specs = [row_blk(D_MODEL)] + [full(a) for a in ins[1:]]
    return pl.pallas_call(
        _mixer_router_kernel,
        grid=(N_TOK // TM,),
        in_specs=in_specs,
        out_specs=[row_blk(D_MODEL), row_blk(D_MODEL), col_blk, col_blk,
                   pl.BlockSpec((N_EXPERTS, LANES), lambda i: (0, 0))],
        out_shape=[jax.ShapeDtypeStruct((N_TOK, D_MODEL), _F32),
                   jax.ShapeDtypeStruct((N_TOK, D_MODEL), _F32),
                   jax.ShapeDtypeStruct((2 * TOP_K, N_TOK), jnp.int32),
                   jax.ShapeDtypeStruct((2 * TOP_K, N_TOK), _F32),
                   jax.ShapeDtypeStruct((N_EXPERTS, LANES), _F32)],
        scratch_shapes=[pltpu.VMEM((TM + SUBLANES, D_CONV), _F32),
                        pltpu.VMEM((N_EXPERTS, LANES), _F32),
                        pltpu.VMEM((TM, D_MODEL), _BF16)],
        compiler_params=pltpu.CompilerParams(dimension_semantics=("arbitrary",),
                                             vmem_limit_bytes=VMEM_LIMIT),
        name="mixer_router",
    )(*ins)


def _dispatch_kernel(dest_ref, zrow_ref, zflag_ref,
                     x_hbm, bufx_hbm, x_in, zbuf, zsem, in_sem, out_sem):
    i = pl.program_id(0)
    n_steps = pl.num_programs(0)

    @pl.when(i == 0)
    def _():
        zbuf[...] = jnp.zeros_like(zbuf)

        def zero_copy(e):
            z0 = pl.multiple_of(zrow_ref[e], ROW_CHUNK)
            return pltpu.make_async_copy(zbuf, bufx_hbm.at[pl.ds(z0, ROW_CHUNK)], zsem)

        for e in range(N_EXPERTS):
            @pl.when(zflag_ref[e] > 0)
            def _(e=e):
                zero_copy(e).start()
        for e in range(N_EXPERTS):
            @pl.when(zflag_ref[e] > 0)
            def _(e=e):
                zero_copy(e).wait()

    def fetch(blk, slot):
        r = pl.multiple_of(blk * TD, TD)
        return pltpu.make_async_copy(x_hbm.at[pl.ds(r, TD)], x_in.at[slot], in_sem.at[slot])

    def row_copy(blk, slot, t8, u, k):
        t = pl.multiple_of(t8 * SUBLANES, SUBLANES) + u
        dst = dest_ref[(blk * TD + t) * TOP_K + k]
        return pltpu.make_async_copy(x_in.at[slot, pl.ds(t, 1)], bufx_hbm.at[pl.ds(dst, 1)],
                                     out_sem.at[blk % 2])

    def scatter(blk, slot, start):
        def body(t8, carry):
            for u in range(SUBLANES):
                for k in range(TOP_K):
                    copy = row_copy(blk, slot, t8, u, k)
                    copy.start() if start else copy.wait()
            return carry
        lax.fori_loop(0, TD // SUBLANES, body, 0)

    def in_slot_of(blk, fn):
        for q in range(IN_SLOTS):
            @pl.when(blk % IN_SLOTS == q)
            def _(q=q):
                fn(q)

    @pl.when(i == 0)
    def _():
        fetch(0, 0).start()
        fetch(1, 1).start()

    in_slot_of(i, lambda q: fetch(i, q).wait())
    in_slot_of(i, lambda q: scatter(i, q, True))

    @pl.when(i >= 1)
    def _():
        in_slot_of(i - 1, lambda q: scatter(i - 1, q, False))

    @pl.when(i + 2 < n_steps)
    def _():
        in_slot_of(i + 2, lambda q: fetch(i + 2, q).start())

    @pl.when(i == n_steps - 1)
    def _():
        in_slot_of(i, lambda q: scatter(i, q, False))


def _dispatch(dest, zrow, zflag, xn2):
    return pl.pallas_call(
        _dispatch_kernel,
        grid_spec=pltpu.PrefetchScalarGridSpec(
            num_scalar_prefetch=3,
            grid=(N_TOK // TD,),
            in_specs=[pl.BlockSpec(memory_space=pl.ANY)],
            out_specs=pl.BlockSpec(memory_space=pl.ANY),
            scratch_shapes=[pltpu.VMEM((IN_SLOTS, TD, D_MODEL), _F32),
                            pltpu.VMEM((ROW_CHUNK, D_MODEL), _F32),
                            pltpu.SemaphoreType.DMA(()),
                            pltpu.SemaphoreType.DMA((IN_SLOTS,)),
                            pltpu.SemaphoreType.DMA((2,))],
        ),
        out_shape=jax.ShapeDtypeStruct((MAX_SUPER * SUPER_ROWS, D_MODEL), _F32),
        compiler_params=pltpu.CompilerParams(dimension_semantics=("arbitrary",),
                                             vmem_limit_bytes=VMEM_LIMIT),
        name="dispatch",
    )(dest, zrow, zflag, xn2)


def _expert_kernel(sup_e_ref, sup_blk_ref, nchunk_ref, nhalf_ref,
                   x_hbm, wg_ref, wu_ref, bgu_ref, wd_ref, bd_ref, out_hbm,
                   x_stage, x_bf, acc, x_sem, o_sem):
    s = pl.program_id(0)
    j = pl.program_id(1)
    n_super = pl.num_programs(0)
    n_chunks = nchunk_ref[s]
    cur = s % 2

    def x_copy(sup, c):
        r = pl.multiple_of(sup_blk_ref[sup] * SUPER_ROWS + c * ROW_CHUNK, ROW_CHUNK)
        slot = c % X_SLOTS
        return pltpu.make_async_copy(x_hbm.at[pl.ds(r, ROW_CHUNK)], x_stage.at[slot],
                                     x_sem.at[slot])

    def x_finish(sup, c):
        x_copy(sup, c).wait()
        rows = pl.ds(pl.multiple_of(c * ROW_CHUNK, ROW_CHUNK), ROW_CHUNK)
        x_bf[sup % 2, rows, :] = x_stage[c % X_SLOTS].astype(_BF16)

    def out_copy(sup, c):
        r = pl.multiple_of(sup_blk_ref[sup] * SUPER_ROWS + c * ROW_CHUNK, ROW_CHUNK)
        rows = pl.ds(pl.multiple_of(c * ROW_CHUNK, ROW_CHUNK), ROW_CHUNK)
        return pltpu.make_async_copy(acc.at[sup % 2, rows], out_hbm.at[pl.ds(r, ROW_CHUNK)],
                                     o_sem.at[sup % 2])

    def for_each_chunk(sup, fn):
        for c in range(CHUNKS_PER_SUPER):
            @pl.when(c < nchunk_ref[sup])
            def _(c=c):
                fn(sup, c)

    @pl.when((s == 0) & (j == 0))
    def _():
        def load(sup, c):
            x_copy(sup, c).start()
            x_finish(sup, c)
        for_each_chunk(0, load)

    @pl.when((s >= 2) & (j == 0))
    def _():
        for_each_chunk(jnp.maximum(s - 2, 0), lambda sup, c: out_copy(sup, c).wait())

    prev = jnp.maximum(s - 1, 0)

    @pl.when((s >= 1) & (j < nchunk_ref[prev]))
    def _():
        out_copy(prev, j).start()

    nxt = jnp.minimum(s + 1, n_super - 1)
    n_next = jnp.where(s + 1 < n_super, nchunk_ref[nxt], 0)

    @pl.when((j >= 1) & (j - 1 < n_next))
    def _():
        x_finish(nxt, j - 1)

    @pl.when(j < n_next)
    def _():
        x_copy(nxt, j).start()

    @pl.when(n_chunks > 0)
    def _():
        def partial_out(n_rows):
            rows = pl.ds(0, n_rows)
            xb = x_bf[cur, rows, :]
            gate = _dot(xb, wg_ref[0].astype(_BF16)) + bgu_ref[0, pl.ds(j, 1), :]
            up = _dot(xb, wu_ref[0].astype(_BF16)) + bgu_ref[0, pl.ds(N_FF_TILES + j, 1), :]
            gate = jnp.minimum(gate, SWIGLU_LIMIT)
            up = jnp.clip(up, -SWIGLU_LIMIT, SWIGLU_LIMIT)
            glu = gate * jax.nn.sigmoid(gate * SWIGLU_ALPHA)
            act = ((up + 1.0) * glu).astype(_BF16)
            return rows, _dot(act, wd_ref[0].astype(_BF16))

        def sweep(update):
            n_half = nhalf_ref[s]
            for n_rows, (lo, hi) in ROW_VARIANTS.items():
                @pl.when((n_half >= lo) & (n_half <= hi))
                def _(n_rows=n_rows):
                    update(n_rows)

        def init(n_rows):
            rows, o = partial_out(n_rows)
            acc[cur, rows, :] = o + bd_ref[0]

        def accumulate(n_rows):
            rows, o = partial_out(n_rows)
            acc[cur, rows, :] += o

        @pl.when(j == 0)
        def _():
            sweep(init)

        @pl.when(j > 0)
        def _():
            sweep(accumulate)

    @pl.when((s == n_super - 1) & (j == N_FF_TILES - 1))
    def _():
        for_each_chunk(s, lambda sup, c: out_copy(sup, c).start())

        @pl.when(s >= 1)
        def _():
            for_each_chunk(jnp.maximum(s - 1, 0), lambda sup, c: out_copy(sup, c).wait())

        for_each_chunk(s, lambda sup, c: out_copy(sup, c).wait())


def _experts(n_super, sup_e, sup_blk, nchunk, nhalf, buf_x, w_gate_up, b_gate_up3, w_down, b_down3):
    x_spec = pl.BlockSpec(memory_space=pl.ANY)
    wg_spec = pl.BlockSpec((1, D_MODEL, FF_TILE), lambda s, j, se, sb, nc, nh: (se[s], 0, j))
    wu_spec = pl.BlockSpec((1, D_MODEL, FF_TILE),
                           lambda s, j, se, sb, nc, nh: (se[s], 0, N_FF_TILES + j))
    bgu_spec = pl.BlockSpec((1, 2 * N_FF_TILES, FF_TILE), lambda s, j, se, sb, nc, nh: (se[s], 0, 0))
    wd_spec = pl.BlockSpec((1, FF_TILE, D_MODEL), lambda s, j, se, sb, nc, nh: (se[s], j, 0))
    bd_spec = pl.BlockSpec((1, 1, D_MODEL), lambda s, j, se, sb, nc, nh: (se[s], 0, 0))
    out_spec = pl.BlockSpec(memory_space=pl.ANY)
    return pl.pallas_call(
        _expert_kernel,
        grid_spec=pltpu.PrefetchScalarGridSpec(
            num_scalar_prefetch=4,
            grid=(n_super, N_FF_TILES),
            in_specs=[x_spec, wg_spec, wu_spec, bgu_spec, wd_spec, bd_spec],
            out_specs=out_spec,
            scratch_shapes=[pltpu.VMEM((X_SLOTS, ROW_CHUNK, D_MODEL), _F32),
                            pltpu.VMEM((2, SUPER_ROWS, D_MODEL), _BF16),
                            pltpu.VMEM((2, SUPER_ROWS, D_MODEL), _F32),
                            pltpu.SemaphoreType.DMA((X_SLOTS,)),
                            pltpu.SemaphoreType.DMA((2,))],
        ),
        out_shape=jax.ShapeDtypeStruct((MAX_SUPER * SUPER_ROWS, D_MODEL), _F32),
        compiler_params=pltpu.CompilerParams(dimension_semantics=("arbitrary", "arbitrary"),
                                             vmem_limit_bytes=VMEM_LIMIT),
        name="experts",
    )(sup_e, sup_blk, nchunk, nhalf, buf_x, w_gate_up, w_gate_up, b_gate_up3, w_down, b_down3)


def _combine_kernel(dest_ref, h1_ref, gates_ref, lng_ref, y_hbm, out_ref, rows, sem):
    i = pl.program_id(0)
    n_steps = pl.num_programs(0)

    def row_copy(tile, slot, t8, u, k):
        t = pl.multiple_of(t8 * SUBLANES, SUBLANES) + u
        src = dest_ref[(tile * TC + t) * TOP_K + k]
        return pltpu.make_async_copy(y_hbm.at[pl.ds(src, 1)], rows.at[slot, k, pl.ds(t, 1)],
                                     sem.at[slot])

    def issue_tile(tile, slot):
        def body(t8, carry):
            for u in range(SUBLANES):
                for k in range(TOP_K):
                    row_copy(tile, slot, t8, u, k).start()
            return carry
        lax.fori_loop(0, TC // SUBLANES, body, 0)

    def wait_tile(tile, slot):
        def body(t8, carry):
            for u in range(SUBLANES):
                for k in range(TOP_K):
                    row_copy(tile, slot, t8, u, k).wait()
            return carry
        lax.fori_loop(0, TC // SUBLANES, body, 0)

    @pl.when(i == 0)
    def _():
        issue_tile(0, 0)

    for slot in range(2):
        @pl.when(i % 2 == slot)
        def _(slot=slot):
            @pl.when(i + 1 < n_steps)
            def _():
                issue_tile(i + 1, 1 - slot)

            wait_tile(i, slot)
            g = gates_ref[...]
            acc = h1_ref[...]
            for k in range(TOP_K):
                acc = acc + rows[slot, k] * g[:, k:k + 1]
            out_ref[...] = _rms_rows(acc, lng_ref[...])


def _combine(dest, h1, gates, ln_final_g, y_sorted):
    return pl.pallas_call(
        _combine_kernel,
        grid_spec=pltpu.PrefetchScalarGridSpec(
            num_scalar_prefetch=1,
            grid=(N_TOK // TC,),
            in_specs=[pl.BlockSpec((TC, D_MODEL), lambda i, *_: (i, 0)),
                      pl.BlockSpec((TC, TOP_K), lambda i, *_: (i, 0)),
                      pl.BlockSpec((1, D_MODEL), lambda i, *_: (0, 0)),
                      pl.BlockSpec(memory_space=pl.ANY)],
            out_specs=pl.BlockSpec((TC, D_MODEL), lambda i, *_: (i, 0)),
            scratch_shapes=[pltpu.VMEM((2, TOP_K, TC, D_MODEL), _F32),
                            pltpu.SemaphoreType.DMA((2,))],
        ),
        out_shape=jax.ShapeDtypeStruct((N_TOK, D_MODEL), _F32),
        compiler_params=pltpu.CompilerParams(dimension_semantics=("arbitrary",),
                                             vmem_limit_bytes=VMEM_LIMIT),
        name="combine",
    )(dest, h1, gates, ln_final_g, y_sorted)


def kernel(x, ln_mix_g, w_in, conv_w, sgu_ln_g, sgu_ln_b, sgu_w, sgu_b, gn_conv, gn_sgu, w_out,
           ln_ffn_g, w_router, b_router, w_gate_up, b_gate_up, w_down, b_down, ln_final_g):
    row = lambda v: v.reshape(1, -1)
    x2d = x.reshape(N_TOK, D_MODEL)

    sgu_b_full = jnp.repeat(jnp.transpose(sgu_b), SGU_HEAD_DIM, axis=1)
    wr_hi = w_router.astype(_BF16)
    wr_lo = (w_router - wr_hi.astype(_F32)).astype(_BF16)
    pad = ((0, 0), (0, LANES - N_EXPERTS))
    wr_split = jnp.concatenate([jnp.pad(wr_hi, pad), jnp.pad(wr_lo, pad)], axis=1)
    br_pad = jnp.pad(b_router, (0, LANES - N_EXPERTS)).reshape(1, LANES)

    h1, xn2, route, gates, counts = _mixer_router(
        x2d, row(ln_mix_g), w_in.astype(_BF16), conv_w, row(sgu_ln_g), row(sgu_ln_b), sgu_w,
        sgu_b_full, row(gn_conv), row(gn_sgu), w_out.astype(_BF16), row(ln_ffn_g), wr_split, br_pad)

    sizes = counts[:, 0].astype(jnp.int32)
    n_sup = (sizes + SUPER_ROWS - 1) // SUPER_ROWS
    sup_end = jnp.cumsum(n_sup)
    sup_start = sup_end - n_sup
    total_sup = sup_end[-1]
    row_start = (sup_start * SUPER_ROWS).astype(jnp.int32)
    sid = jnp.arange(MAX_SUPER, dtype=jnp.int32)
    sid_eff = jnp.minimum(sid, total_sup - 1)
    sup_e = jnp.minimum(jnp.sum((sid_eff[:, None] >= sup_end[None, :]).astype(jnp.int32), axis=1),
                        N_EXPERTS - 1)
    of_expert = sup_e[:, None] == jnp.arange(N_EXPERTS, dtype=jnp.int32)[None, :]
    rows_left = jnp.sum(jnp.where(of_expert, sizes - (sid_eff[:, None] - sup_start) * SUPER_ROWS, 0),
                        axis=1)
    valid = jnp.clip(rows_left, 0, SUPER_ROWS)
    nchunk = jnp.where(sid < total_sup, (valid + ROW_CHUNK - 1) // ROW_CHUNK, 0).astype(jnp.int32)
    nhalf = jnp.where(sid < total_sup, (valid + HALF_CHUNK - 1) // HALF_CHUNK, 0).astype(jnp.int32)
    zflag = (sizes > 0).astype(jnp.int32)
    zrow = (row_start + ((jnp.maximum(sizes, 1) - 1) // ROW_CHUNK) * ROW_CHUNK).astype(jnp.int32)
    eid = route[0:TOP_K]
    eid_start = jnp.sum(jnp.where(eid[None] == jnp.arange(N_EXPERTS, dtype=jnp.int32)[:, None, None],
                                  row_start[:, None, None], 0), axis=0)
    dest = jnp.transpose(eid_start + route[TOP_K:2 * TOP_K]).reshape(-1)
    gates = jnp.transpose(gates[0:TOP_K])

    buf_x = _dispatch(dest, zrow, zflag, xn2)
    y_sorted = _experts(total_sup, sup_e, sid_eff.astype(jnp.int32), nchunk, nhalf, buf_x, w_gate_up,
                        b_gate_up.reshape(N_EXPERTS, 2 * N_FF_TILES, FF_TILE), w_down,
                        b_down.reshape(N_EXPERTS, 1, D_MODEL))
    out = _combine(dest, h1, gates, row(ln_final_g), y_sorted)
    return out.reshape(BATCH, SEQ, D_MODEL)
```

```python
import jax
import jax.numpy as jnp
from jax import lax
from jax.experimental import pallas as pl
from jax.experimental.pallas import tpu as pltpu

D_MODEL = 2048
BATCH = 2
SEQ = 4096
N_TOK = BATCH * SEQ

D_CONV = 1024
N_CONV_GROUPS = 8
CONV_WIDTH = 3
D_SGU = 1024
N_SGU_HEADS = 8
SGU_HEAD_DIM = 128
CHUNK = 128
D_IN_PROJ = 3 * D_CONV + 2 * D_SGU

N_EXPERTS = 32
TOP_K = 4
D_FF = 2048
SWIGLU_LIMIT = 7.0
SWIGLU_ALPHA = 1.702
RMS_EPS = 1e-5
LN_EPS = 1e-5

LANES = 128
SUBLANES = 8
VMEM_LIMIT = 56 * 1024 * 1024

TM = 256
SUPER_ROWS = 1280
ROW_CHUNK = 256
FF_TILE = 256
N_FF_TILES = D_FF // FF_TILE
MAX_SUPER = (N_TOK * TOP_K + N_EXPERTS * (SUPER_ROWS - 1)) // SUPER_ROWS
CHUNKS_PER_SUPER = SUPER_ROWS // ROW_CHUNK
assert CHUNKS_PER_SUPER + 1 <= N_FF_TILES
X_SLOTS = 2
HALF_CHUNK = ROW_CHUNK // 2
TD = 256
IN_SLOTS = 3
assert N_TOK // TD >= 2
TC = 256

_F32 = jnp.float32
_BF16 = jnp.bfloat16


def _dot(a, b):
    return jnp.dot(a, b, preferred_element_type=_F32)


def _gelu_exact(x):
    return 0.5 * x * (1.0 + lax.erf(x * (2.0 ** -0.5)))


def _rms_rows(x, gain):
    return x * lax.rsqrt(jnp.mean(x * x, axis=-1, keepdims=True) + RMS_EPS) * gain


def _mixer_router_kernel(x_ref, lng_ref, win_ref, convw_ref, slg_ref, slb_ref, sw_ref, sb_ref,
                         gnc_ref, gns_ref, wout_ref, lnf_ref, wr_ref, br_ref,
                         h1_ref, xn2_ref, route_ref, gates_ref, counts_ref,
                         cbuf, carry, ybuf):
    i = pl.program_id(0)

    @pl.when(i % (SEQ // TM) == 0)
    def _():
        cbuf[0:SUBLANES, :] = jnp.zeros((SUBLANES, D_CONV), _F32)

    @pl.when(i == 0)
    def _():
        carry[...] = jnp.zeros_like(carry)

    x = x_ref[...]
    xn = _rms_rows(x, lng_ref[...]).astype(_BF16)

    b_gate = _dot(xn, win_ref[:, 0:D_CONV])
    c_gate = _dot(xn, win_ref[:, D_CONV:2 * D_CONV])
    hh = _dot(xn, win_ref[:, 2 * D_CONV:3 * D_CONV])
    ch = c_gate * hh
    cbuf[SUBLANES:SUBLANES + TM, :] = ch
    ch1 = cbuf[SUBLANES - 1:SUBLANES - 1 + TM, :]
    ch2 = cbuf[SUBLANES - 2:SUBLANES - 2 + TM, :]
    conv = convw_ref[0:1, :] * ch2 + convw_ref[1:2, :] * ch1 + convw_ref[2:3, :] * ch
    cbuf[0:SUBLANES, :] = cbuf[TM:TM + SUBLANES, :]
    y_conv = b_gate * conv
    for g in range(N_CONV_GROUPS):
        sl = slice(g * LANES, (g + 1) * LANES)
        blk = y_conv[:, sl]
        ms = jnp.mean(blk * blk, axis=-1, keepdims=True)
        ybuf[:, sl] = (blk * lax.rsqrt(ms + RMS_EPS) * gnc_ref[:, sl]).astype(_BF16)

    gu = _gelu_exact(_dot(xn, win_ref[:, 3 * D_CONV:3 * D_CONV + D_SGU]))
    gv = _gelu_exact(_dot(xn, win_ref[:, 3 * D_CONV + D_SGU:D_IN_PROJ]))
    row_c = lax.broadcasted_iota(jnp.int32, (CHUNK, CHUNK), 0)
    col_c = lax.broadcasted_iota(jnp.int32, (CHUNK, CHUNK), 1)
    causal = row_c >= col_c
    for h in range(N_SGU_HEADS):
        sl = slice(h * SGU_HEAD_DIM, (h + 1) * SGU_HEAD_DIM)
        vh = gv[:, sl]
        mu = jnp.mean(vh, axis=-1, keepdims=True)
        xc = vh - mu
        var = jnp.mean(xc * xc, axis=-1, keepdims=True)
        vn = (xc * lax.rsqrt(var + LN_EPS) * slg_ref[:, sl] + slb_ref[:, sl]).astype(_BF16)
        wm = jnp.where(causal, sw_ref[h], 0.0).astype(_BF16)
        for c in range(TM // CHUNK):
            rows = slice(c * CHUNK, (c + 1) * CHUNK)
            mixed = _dot(wm, vn[rows, :]) + sb_ref[:, sl]
            ys = gu[rows, sl] * mixed
            ms = jnp.mean(ys * ys, axis=-1, keepdims=True)
            ybuf[rows, D_CONV + h * SGU_HEAD_DIM:D_CONV + (h + 1) * SGU_HEAD_DIM] = (
                ys * lax.rsqrt(ms + RMS_EPS) * gns_ref[:, sl]).astype(_BF16)

    h1 = x + _dot(ybuf[...], wout_ref[...])
    h1_ref[...] = h1
    xn2 = _rms_rows(h1, lnf_ref[...])
    xn2_ref[...] = xn2

    x_hi = xn2.astype(_BF16)
    x_lo = (xn2 - x_hi.astype(_F32)).astype(_BF16)
    p = _dot(x_hi, wr_ref[...]) + _dot(x_lo, wr_ref[...])
    logits = p[:, :LANES] + p[:, LANES:] + br_ref[...]
    cur = jnp.transpose(logits)[0:N_EXPERTS, :]
    expert = lax.broadcasted_iota(jnp.int32, (N_EXPERTS, TM), 0).astype(_F32)
    neg_inf = jnp.float32(-jnp.inf)

    vals, ids, onehots = [], [], []
    for _ in range(TOP_K):
        m = jnp.max(cur, axis=0, keepdims=True)
        idx = jnp.min(jnp.where(cur == m, expert, float(N_EXPERTS)), axis=0, keepdims=True)
        oh = expert == idx
        vals.append(m)
        ids.append(idx)
        onehots.append(oh)
        cur = jnp.where(oh, neg_inf, cur)
    exps = [jnp.exp(v - vals[0]) for v in vals]
    denom = exps[0] + exps[1] + exps[2] + exps[3]
    gates = [e / denom for e in exps]

    mask = (onehots[0] | onehots[1] | onehots[2] | onehots[3]).astype(_F32)
    row_t = lax.broadcasted_iota(jnp.int32, (TM, TM), 0)
    col_t = lax.broadcasted_iota(jnp.int32, (TM, TM), 1)
    earlier = (row_t < col_t).astype(_BF16)
    before = _dot(mask.astype(_BF16), earlier) + carry[:, 0:1]
    ranks = [jnp.sum(jnp.where(oh, before, 0.0), axis=0, keepdims=True) for oh in onehots]
    carry[...] = carry[...] + jnp.sum(mask, axis=1, keepdims=True)
    counts_ref[...] = carry[...]

    out_row = lax.broadcasted_iota(jnp.int32, (2 * TOP_K, TM), 0)
    route = jnp.zeros((2 * TOP_K, TM), _F32)
    gate_out = jnp.zeros((2 * TOP_K, TM), _F32)
    for k in range(TOP_K):
        route = jnp.where(out_row == k, ids[k], route)
        route = jnp.where(out_row == TOP_K + k, ranks[k], route)
        gate_out = jnp.where(out_row == k, gates[k], gate_out)
    route_ref[...] = route.astype(jnp.int32)
    gates_ref[...] = gate_out


def _mixer_router(x2d, ln_mix_g, w_in_bf, conv_w, sgu_ln_g, sgu_ln_b, sgu_w, sgu_b_full,
                  gn_conv, gn_sgu, w_out_bf, ln_ffn_g, wr_split, br_pad):
    def full(a):
        return pl.BlockSpec(a.shape, lambda i: (0,) * a.ndim)

    row_blk = lambda w: pl.BlockSpec((TM, w), lambda i: (i, 0))
    col_blk = pl.BlockSpec((2 * TOP_K, TM), lambda i: (0, i))
    ins = [x2d, ln_mix_g, w_in_bf, conv_w, sgu_ln_g, sgu_ln_b, sgu_w, sgu_b_full,
           gn_conv, gn_sgu, w_out_bf, ln_ffn_g, wr_split, br_pad]
    in_specs = [row_blk(D_MODEL)] + [full(a) for a in ins[1:]]
    return pl.pallas_call(
        _mixer_router_kernel,
        grid=(N_TOK // TM,),
        in_specs=in_specs,
        out_specs=[row_blk(D_MODEL), row_blk(D_MODEL), col_blk, col_blk,
                   pl.BlockSpec((N_EXPERTS, LANES), lambda i: (0, 0))],
        out_shape=[jax.ShapeDtypeStruct((N_TOK, D_MODEL), _F32),
                   jax.ShapeDtypeStruct((N_TOK, D_MODEL), _F32),
                   jax.ShapeDtypeStruct((2 * TOP_K, N_TOK), jnp.int32),
                   jax.ShapeDtypeStruct((2 * TOP_K, N_TOK), _F32),
                   jax.ShapeDtypeStruct((N_EXPERTS, LANES), _F32)],
        scratch_shapes=[pltpu.VMEM((TM + SUBLANES, D_CONV), _F32),
                        pltpu.VMEM((N_EXPERTS, LANES), _F32),
                        pltpu.VMEM((TM, D_MODEL), _BF16)],
        compiler_params=pltpu.CompilerParams(dimension_semantics=("arbitrary",),
                                             vmem_limit_bytes=VMEM_LIMIT),
        name="mixer_router",
    )(*ins)


def _dispatch_kernel(dest_ref, pad_lo_ref, pad_hi_ref,
                     x_hbm, bufx_hbm, x_in, zbuf, zsem, in_sem, out_sem):
    i = pl.program_id(0)
    n_steps = pl.num_programs(0)

    def pad_rows(start):
        def per_expert(e, carry):
            def per_row(r, inner):
                copy = pltpu.make_async_copy(zbuf.at[pl.ds(0, 1)], bufx_hbm.at[pl.ds(r, 1)], zsem)
                copy.start() if start else copy.wait()
                return inner
            lax.fori_loop(pad_lo_ref[e], pad_hi_ref[e], per_row, 0)
            return carry
        lax.fori_loop(0, N_EXPERTS, per_expert, 0)

    def fetch(blk, slot):
        r = pl.multiple_of(blk * TD, TD)
        return pltpu.make_async_copy(x_hbm.at[pl.ds(r, TD)], x_in.at[slot], in_sem.at[slot])

    def row_copy(blk, slot, t8, u, k):
        t = pl.multiple_of(t8 * SUBLANES, SUBLANES) + u
        dst = dest_ref[(blk * TD + t) * TOP_K + k]
        return pltpu.make_async_copy(x_in.at[slot, pl.ds(t, 1)], bufx_hbm.at[pl.ds(dst, 1)],
                                     out_sem.at[blk % 2])

    def scatter(blk, slot, start):
        def body(t8, carry):
            for u in range(SUBLANES):
                for k in range(TOP_K):
                    copy = row_copy(blk, slot, t8, u, k)
                    copy.start() if start else copy.wait()
            return carry
        lax.fori_loop(0, TD // SUBLANES, body, 0)

    def in_slot_of(blk, fn):
        for q in range(IN_SLOTS):
            @pl.when(blk % IN_SLOTS == q)
            def _(q=q):
                fn(q)

    @pl.when(i == 0)
    def _():
        fetch(0, 0).start()
        fetch(1, 1).start()
        zbuf[...] = jnp.zeros_like(zbuf)
        pad_rows(True)

    in_slot_of(i, lambda q: fetch(i, q).wait())
    in_slot_of(i, lambda q: scatter(i, q, True))

    @pl.when(i == 0)
    def _():
        pad_rows(False)

    @pl.when(i >= 1)
    def _():
        in_slot_of(i - 1, lambda q: scatter(i - 1, q, False))

    @pl.when(i + 2 < n_steps)
    def _():
        in_slot_of(i + 2, lambda q: fetch(i + 2, q).start())

    @pl.when(i == n_steps - 1)
    def _():
        in_slot_of(i, lambda q: scatter(i, q, False))


def _dispatch(dest, pad_lo, pad_hi, xn2):
    return pl.pallas_call(
        _dispatch_kernel,
        grid_spec=pltpu.PrefetchScalarGridSpec(
            num_scalar_prefetch=3,
            grid=(N_TOK // TD,),
            in_specs=[pl.BlockSpec(memory_space=pl.ANY)],
            out_specs=pl.BlockSpec(memory_space=pl.ANY),
            scratch_shapes=[pltpu.VMEM((IN_SLOTS, TD, D_MODEL), _F32),
                            pltpu.VMEM((SUBLANES, D_MODEL), _F32),
                            pltpu.SemaphoreType.DMA(()),
                            pltpu.SemaphoreType.DMA((IN_SLOTS,)),
                            pltpu.SemaphoreType.DMA((2,))],
        ),
        out_shape=jax.ShapeDtypeStruct((MAX_SUPER * SUPER_ROWS, D_MODEL), _F32),
        compiler_params=pltpu.CompilerParams(dimension_semantics=("arbitrary",),
                                             vmem_limit_bytes=VMEM_LIMIT),
        name="dispatch",
    )(dest, pad_lo, pad_hi, xn2)


def _expert_kernel(sup_e_ref, sup_blk_ref, nchunk_ref, nhalf_ref,
                   x_hbm, wg_ref, wu_ref, bgu_ref, wd_ref, bd_ref, out_hbm,
                   x_stage, x_bf, acc, x_sem, o_sem):
    s = pl.program_id(0)
    j = pl.program_id(1)
    n_super = pl.num_programs(0)
    n_chunks = nchunk_ref[s]
    cur = s % 2

    def x_copy(sup, c):
        r = pl.multiple_of(sup_blk_ref[sup] * SUPER_ROWS + c * ROW_CHUNK, ROW_CHUNK)
        slot = c % X_SLOTS
        return pltpu.make_async_copy(x_hbm.at[pl.ds(r, ROW_CHUNK)], x_stage.at[slot],
                                     x_sem.at[slot])

    def x_finish(sup, c):
        x_copy(sup, c).wait()
        rows = pl.ds(pl.multiple_of(c * ROW_CHUNK, ROW_CHUNK), ROW_CHUNK)
        x_bf[sup % 2, rows, :] = x_stage[c % X_SLOTS].astype(_BF16)

    def out_copy(sup, c):
        r = pl.multiple_of(sup_blk_ref[sup] * SUPER_ROWS + c * ROW_CHUNK, ROW_CHUNK)
        rows = pl.ds(pl.multiple_of(c * ROW_CHUNK, ROW_CHUNK), ROW_CHUNK)
        return pltpu.make_async_copy(acc.at[sup % 2, rows], out_hbm.at[pl.ds(r, ROW_CHUNK)],
                                     o_sem.at[sup % 2])

    def for_each_chunk(sup, fn):
        for c in range(CHUNKS_PER_SUPER):
            @pl.when(c < nchunk_ref[sup])
            def _(c=c):
                fn(sup, c)

    @pl.when((s == 0) & (j == 0))
    def _():
        def load(sup, c):
            x_copy(sup, c).start()
            x_finish(sup, c)
        for_each_chunk(0, load)

    @pl.when((s >= 2) & (j == 0))
    def _():
        for_each_chunk(jnp.maximum(s - 2, 0), lambda sup, c: out_copy(sup, c).wait())

    prev = jnp.maximum(s - 1, 0)

    @pl.when((s >= 1) & (j < nchunk_ref[prev]))
    def _():
        out_copy(prev, j).start()

    nxt = jnp.minimum(s + 1, n_super - 1)
    n_next = jnp.where(s + 1 < n_super, nchunk_ref[nxt], 0)

    @pl.when((j >= 1) & (j - 1 < n_next))
    def _():
        x_finish(nxt, j - 1)

    @pl.when(j < n_next)
    def _():
        x_copy(nxt, j).start()

    @pl.when(n_chunks > 0)
    def _():
        def partial_out(n_rows):
            rows = pl.ds(0, n_rows)
            xb = x_bf[cur, rows, :]
            gate = _dot(xb, wg_ref[0].astype(_BF16)) + bgu_ref[0, pl.ds(j, 1), :]
            up = _dot(xb, wu_ref[0].astype(_BF16)) + bgu_ref[0, pl.ds(N_FF_TILES + j, 1), :]
            gate = jnp.minimum(gate, SWIGLU_LIMIT)
            up = jnp.clip(up, -SWIGLU_LIMIT, SWIGLU_LIMIT)
            glu = gate * jax.nn.sigmoid(gate * SWIGLU_ALPHA)
            act = ((up + 1.0) * glu).astype(_BF16)
            return rows, _dot(act, wd_ref[0].astype(_BF16))

        def sweep(update):
            n_half = nhalf_ref[s]
            for n in range(1, SUPER_ROWS // HALF_CHUNK + 1):
                @pl.when(n_half == n)
                def _(n=n):
                    update(n * HALF_CHUNK)

        def init(n_rows):
            rows, o = partial_out(n_rows)
            acc[cur, rows, :] = o + bd_ref[0]

        def accumulate(n_rows):
            rows, o = partial_out(n_rows)
            acc[cur, rows, :] += o

        @pl.when(j == 0)
        def _():
            sweep(init)

        @pl.when(j > 0)
        def _():
            sweep(accumulate)

    @pl.when((s == n_super - 1) & (j == N_FF_TILES - 1))
    def _():
        for_each_chunk(s, lambda sup, c: out_copy(sup, c).start())

        @pl.when(s >= 1)
        def _():
            for_each_chunk(jnp.maximum(s - 1, 0), lambda sup, c: out_copy(sup, c).wait())

        for_each_chunk(s, lambda sup, c: out_copy(sup, c).wait())


def _experts(n_super, sup_e, sup_blk, nchunk, nhalf, buf_x, w_gate_up, b_gate_up3, w_down, b_down3):
    x_spec = pl.BlockSpec(memory_space=pl.ANY)
    wg_spec = pl.BlockSpec((1, D_MODEL, FF_TILE), lambda s, j, se, sb, nc, nh: (se[s], 0, j))
    wu_spec = pl.BlockSpec((1, D_MODEL, FF_TILE),
                           lambda s, j, se, sb, nc, nh: (se[s], 0, N_FF_TILES + j))
    bgu_spec = pl.BlockSpec((1, 2 * N_FF_TILES, FF_TILE), lambda s, j, se, sb, nc, nh: (se[s], 0, 0))
    wd_spec = pl.BlockSpec((1, FF_TILE, D_MODEL), lambda s, j, se, sb, nc, nh: (se[s], j, 0))
    bd_spec = pl.BlockSpec((1, 1, D_MODEL), lambda s, j, se, sb, nc, nh: (se[s], 0, 0))
    out_spec = pl.BlockSpec(memory_space=pl.ANY)
    return pl.pallas_call(
        _expert_kernel,
        grid_spec=pltpu.PrefetchScalarGridSpec(
            num_scalar_prefetch=4,
            grid=(n_super, N_FF_TILES),
            in_specs=[x_spec, wg_spec, wu_spec, bgu_spec, wd_spec, bd_spec],
            out_specs=out_spec,
            scratch_shapes=[pltpu.VMEM((X_SLOTS, ROW_CHUNK, D_MODEL), _F32),
                            pltpu.VMEM((2, SUPER_ROWS, D_MODEL), _BF16),
                            pltpu.VMEM((2, SUPER_ROWS, D_MODEL), _F32),
                            pltpu.SemaphoreType.DMA((X_SLOTS,)),
                            pltpu.SemaphoreType.DMA((2,))],
        ),
        out_shape=jax.ShapeDtypeStruct((MAX_SUPER * SUPER_ROWS, D_MODEL), _F32),
        compiler_params=pltpu.CompilerParams(dimension_semantics=("arbitrary", "arbitrary"),
                                             vmem_limit_bytes=VMEM_LIMIT),
        name="experts",
    )(sup_e, sup_blk, nchunk, nhalf, buf_x, w_gate_up, w_gate_up, b_gate_up3, w_down, b_down3)


def _combine_kernel(dest_ref, h1_ref, gates_ref, lng_ref, y_hbm, out_ref, rows, sem):
    i = pl.program_id(0)
    n_steps = pl.num_programs(0)

    def row_copy(tile, slot, t8, u, k):
        t = pl.multiple_of(t8 * SUBLANES, SUBLANES) + u
        src = dest_ref[(tile * TC + t) * TOP_K + k]
        return pltpu.make_async_copy(y_hbm.at[pl.ds(src, 1)], rows.at[slot, k, pl.ds(t, 1)],
                                     sem.at[slot])

    def issue_tile(tile, slot):
        def body(t8, carry):
            for u in range(SUBLANES):
                for k in range(TOP_K):
                    row_copy(tile, slot, t8, u, k).start()
            return carry
        lax.fori_loop(0, TC // SUBLANES, body, 0)

    def wait_tile(tile, slot):
        def body(t8, carry):
            for u in range(SUBLANES):
                for k in range(TOP_K):
                    row_copy(tile, slot, t8, u, k).wait()
            return carry
        lax.fori_loop(0, TC // SUBLANES, body, 0)

    @pl.when(i == 0)
    def _():
        issue_tile(0, 0)

    for slot in range(2):
        @pl.when(i % 2 == slot)
        def _(slot=slot):
            @pl.when(i + 1 < n_steps)
            def _():
                issue_tile(i + 1, 1 - slot)

            wait_tile(i, slot)
            g = gates_ref[...]
            acc = h1_ref[...]
            for k in range(TOP_K):
                acc = acc + rows[slot, k] * g[:, k:k + 1]
            out_ref[...] = _rms_rows(acc, lng_ref[...])


def _combine(dest, h1, gates, ln_final_g, y_sorted):
    return pl.pallas_call(
        _combine_kernel,
        grid_spec=pltpu.PrefetchScalarGridSpec(
            num_scalar_prefetch=1,
            grid=(N_TOK // TC,),
            in_specs=[pl.BlockSpec((TC, D_MODEL), lambda i, *_: (i, 0)),
                      pl.BlockSpec((TC, TOP_K), lambda i, *_: (i, 0)),
                      pl.BlockSpec((1, D_MODEL), lambda i, *_: (0, 0)),
                      pl.BlockSpec(memory_space=pl.ANY)],
            out_specs=pl.BlockSpec((TC, D_MODEL), lambda i, *_: (i, 0)),
            scratch_shapes=[pltpu.VMEM((2, TOP_K, TC, D_MODEL), _F32),
                            pltpu.SemaphoreType.DMA((2,))],
        ),
        out_shape=jax.ShapeDtypeStruct((N_TOK, D_MODEL), _F32),
        compiler_params=pltpu.CompilerParams(dimension_semantics=("arbitrary",),
                                             vmem_limit_bytes=VMEM_LIMIT),
        name="combine",
    )(dest, h1, gates, ln_final_g, y_sorted)


def kernel(x, ln_mix_g, w_in, conv_w, sgu_ln_g, sgu_ln_b, sgu_w, sgu_b, gn_conv, gn_sgu, w_out,
           ln_ffn_g, w_router, b_router, w_gate_up, b_gate_up, w_down, b_down, ln_final_g):
    row = lambda v: v.reshape(1, -1)
    x2d = x.reshape(N_TOK, D_MODEL)

    sgu_b_full = jnp.repeat(jnp.transpose(sgu_b), SGU_HEAD_DIM, axis=1)
    wr_hi = w_router.astype(_BF16)
    wr_lo = (w_router - wr_hi.astype(_F32)).astype(_BF16)
    pad = ((0, 0), (0, LANES - N_EXPERTS))
    wr_split = jnp.concatenate([jnp.pad(wr_hi, pad), jnp.pad(wr_lo, pad)], axis=1)
    br_pad = jnp.pad(b_router, (0, LANES - N_EXPERTS)).reshape(1, LANES)

    h1, xn2, route, gates, counts = _mixer_router(
        x2d, row(ln_mix_g), w_in.astype(_BF16), conv_w, row(sgu_ln_g), row(sgu_ln_b), sgu_w,
        sgu_b_full, row(gn_conv), row(gn_sgu), w_out.astype(_BF16), row(ln_ffn_g), wr_split, br_pad)

    sizes = counts[:, 0].astype(jnp.int32)
    n_sup = (sizes + SUPER_ROWS - 1) // SUPER_ROWS
    sup_end = jnp.cumsum(n_sup)
    sup_start = sup_end - n_sup
    total_sup = sup_end[-1]
    row_start = (sup_start * SUPER_ROWS).astype(jnp.int32)
    sid = jnp.arange(MAX_SUPER, dtype=jnp.int32)
    sid_eff = jnp.minimum(sid, total_sup - 1)
    sup_e = jnp.minimum(jnp.sum((sid_eff[:, None] >= sup_end[None, :]).astype(jnp.int32), axis=1),
                        N_EXPERTS - 1)
    of_expert = sup_e[:, None] == jnp.arange(N_EXPERTS, dtype=jnp.int32)[None, :]
    rows_left = jnp.sum(jnp.where(of_expert, sizes - (sid_eff[:, None] - sup_start) * SUPER_ROWS, 0),
                        axis=1)
    valid = jnp.clip(rows_left, 0, SUPER_ROWS)
    nchunk = jnp.where(sid < total_sup, (valid + ROW_CHUNK - 1) // ROW_CHUNK, 0).astype(jnp.int32)
    nhalf = jnp.where(sid < total_sup, (valid + HALF_CHUNK - 1) // HALF_CHUNK, 0).astype(jnp.int32)
    pad_lo = row_start + sizes
    pad_hi = row_start + ((sizes + HALF_CHUNK - 1) // HALF_CHUNK) * HALF_CHUNK
    eid = route[0:TOP_K]
    eid_start = jnp.sum(jnp.where(eid[None] == jnp.arange(N_EXPERTS, dtype=jnp.int32)[:, None, None],
                                  row_start[:, None, None], 0), axis=0)
    dest = jnp.transpose(eid_start + route[TOP_K:2 * TOP_K]).reshape(-1)
    gates = jnp.transpose(gates[0:TOP_K])

    buf_x = _dispatch(dest, pad_lo, pad_hi, xn2)
    y_sorted = _experts(total_sup, sup_e, sid_eff.astype(jnp.int32), nchunk, nhalf, buf_x, w_gate_up,
                        b_gate_up.reshape(N_EXPERTS, 2 * N_FF_TILES, FF_TILE), w_down,
                        b_down.reshape(N_EXPERTS, 1, D_MODEL))
    out = _combine(dest, h1, gates, row(ln_final_g), y_sorted)
    return out.reshape(BATCH, SEQ, D_MODEL)
```

```python
import jax
import jax.numpy as jnp
from jax import lax
from jax.experimental import pallas as pl
from jax.experimental.pallas import tpu as pltpu

D_MODEL = 2048
BATCH = 2
SEQ = 4096
N_TOK = BATCH * SEQ

D_CONV = 1024
N_CONV_GROUPS = 8
CONV_WIDTH = 3
D_SGU = 1024
N_SGU_HEADS = 8
SGU_HEAD_DIM = 128
CHUNK = 128
D_IN_PROJ = 3 * D_CONV + 2 * D_SGU

N_EXPERTS = 32
TOP_K = 4
D_FF = 2048
SWIGLU_LIMIT = 7.0
SWIGLU_ALPHA = 1.702
RMS_EPS = 1e-5
LN_EPS = 1e-5

LANES = 128
SUBLANES = 8
VMEM_LIMIT = 56 * 1024 * 1024

TM = 256
SUPER_ROWS = 1280
ROW_CHUNK = 256
FF_TILE = 256
N_FF_TILES = D_FF // FF_TILE
MAX_SUPER = (N_TOK * TOP_K + N_EXPERTS * (SUPER_ROWS - 1)) // SUPER_ROWS
CHUNKS_PER_SUPER = SUPER_ROWS // ROW_CHUNK
assert CHUNKS_PER_SUPER + 1 <= N_FF_TILES
X_SLOTS = 2
HALF_CHUNK = ROW_CHUNK // 2
FINE_FROM = 7


def _row_variants():
    variants = {}
    for n_half in range(1, SUPER_ROWS // HALF_CHUNK + 1):
        if n_half >= FINE_FROM:
            n_rows = n_half * HALF_CHUNK
        else:
            n_rows = -(-n_half // 2) * ROW_CHUNK
        lo, hi = variants.get(n_rows, (n_half, n_half))
        variants[n_rows] = (min(lo, n_half), max(hi, n_half))
    return variants


ROW_VARIANTS = _row_variants()
TD = 256
IN_SLOTS = 3
assert N_TOK // TD >= 2
TC = 256

_F32 = jnp.float32
_BF16 = jnp.bfloat16


def _dot(a, b):
    return jnp.dot(a, b, preferred_element_type=_F32)


def _gelu_exact(x):
    return 0.5 * x * (1.0 + lax.erf(x * (2.0 ** -0.5)))


def _rms_rows(x, gain):
    return x * lax.rsqrt(jnp.mean(x * x, axis=-1, keepdims=True) + RMS_EPS) * gain


def _mixer_router_kernel(x_ref, lng_ref, win_ref, convw_ref, slg_ref, slb_ref, sw_ref, sb_ref,
                         gnc_ref, gns_ref, wout_ref, lnf_ref, wr_ref, br_ref,
                         h1_ref, xn2_ref, route_ref, gates_ref, counts_ref,
                         cbuf, carry, ybuf):
    i = pl.program_id(0)

    @pl.when(i % (SEQ // TM) == 0)
    def _():
        cbuf[0:SUBLANES, :] = jnp.zeros((SUBLANES, D_CONV), _F32)

    @pl.when(i == 0)
    def _():
        carry[...] = jnp.zeros_like(carry)

    x = x_ref[...]
    xn = _rms_rows(x, lng_ref[...]).astype(_BF16)

    b_gate = _dot(xn, win_ref[:, 0:D_CONV])
    c_gate = _dot(xn, win_ref[:, D_CONV:2 * D_CONV])
    hh = _dot(xn, win_ref[:, 2 * D_CONV:3 * D_CONV])
    ch = c_gate * hh
    cbuf[SUBLANES:SUBLANES + TM, :] = ch
    ch1 = cbuf[SUBLANES - 1:SUBLANES - 1 + TM, :]
    ch2 = cbuf[SUBLANES - 2:SUBLANES - 2 + TM, :]
    conv = convw_ref[0:1, :] * ch2 + convw_ref[1:2, :] * ch1 + convw_ref[2:3, :] * ch
    cbuf[0:SUBLANES, :] = cbuf[TM:TM + SUBLANES, :]
    y_conv = b_gate * conv
    for g in range(N_CONV_GROUPS):
        sl = slice(g * LANES, (g + 1) * LANES)
        blk = y_conv[:, sl]
        ms = jnp.mean(blk * blk, axis=-1, keepdims=True)
        ybuf[:, sl] = (blk * lax.rsqrt(ms + RMS_EPS) * gnc_ref[:, sl]).astype(_BF16)

    gu = _gelu_exact(_dot(xn, win_ref[:, 3 * D_CONV:3 * D_CONV + D_SGU]))
    gv = _gelu_exact(_dot(xn, win_ref[:, 3 * D_CONV + D_SGU:D_IN_PROJ]))
    row_c = lax.broadcasted_iota(jnp.int32, (CHUNK, CHUNK), 0)
    col_c = lax.broadcasted_iota(jnp.int32, (CHUNK, CHUNK), 1)
    causal = row_c >= col_c
    for h in range(N_SGU_HEADS):
        sl = slice(h * SGU_HEAD_DIM, (h + 1) * SGU_HEAD_DIM)
        vh = gv[:, sl]
        mu = jnp.mean(vh, axis=-1, keepdims=True)
        xc = vh - mu
        var = jnp.mean(xc * xc, axis=-1, keepdims=True)
        vn = (xc * lax.rsqrt(var + LN_EPS) * slg_ref[:, sl] + slb_ref[:, sl]).astype(_BF16)
        wm = jnp.where(causal, sw_ref[h], 0.0).astype(_BF16)
        for c in range(TM // CHUNK):
            rows = slice(c * CHUNK, (c + 1) * CHUNK)
            mixed = _dot(wm, vn[rows, :]) + sb_ref[:, sl]
            ys = gu[rows, sl] * mixed
            ms = jnp.mean(ys * ys, axis=-1, keepdims=True)
            ybuf[rows, D_CONV + h * SGU_HEAD_DIM:D_CONV + (h + 1) * SGU_HEAD_DIM] = (
                ys * lax.rsqrt(ms + RMS_EPS) * gns_ref[:, sl]).astype(_BF16)

    h1 = x + _dot(ybuf[...], wout_ref[...])
    h1_ref[...] = h1
    xn2 = _rms_rows(h1, lnf_ref[...])
    xn2_ref[...] = xn2

    x_hi = xn2.astype(_BF16)
    x_lo = (xn2 - x_hi.astype(_F32)).astype(_BF16)
    p = _dot(x_hi, wr_ref[...]) + _dot(x_lo, wr_ref[...])
    logits = p[:, :LANES] + p[:, LANES:] + br_ref[...]
    cur = jnp.transpose(logits)[0:N_EXPERTS, :]
    expert = lax.broadcasted_iota(jnp.int32, (N_EXPERTS, TM), 0).astype(_F32)
    neg_inf = jnp.float32(-jnp.inf)

    vals, ids, onehots = [], [], []
    for _ in range(TOP_K):
        m = jnp.max(cur, axis=0, keepdims=True)
        idx = jnp.min(jnp.where(cur == m, expert, float(N_EXPERTS)), axis=0, keepdims=True)
        oh = expert == idx
        vals.append(m)
        ids.append(idx)
        onehots.append(oh)
        cur = jnp.where(oh, neg_inf, cur)
    exps = [jnp.exp(v - vals[0]) for v in vals]
    denom = exps[0] + exps[1] + exps[2] + exps[3]
    gates = [e / denom for e in exps]

    mask = (onehots[0] | onehots[1] | onehots[2] | onehots[3]).astype(_F32)
    row_t = lax.broadcasted_iota(jnp.int32, (TM, TM), 0)
    col_t = lax.broadcasted_iota(jnp.int32, (TM, TM), 1)
    earlier = (row_t < col_t).astype(_BF16)
    before = _dot(mask.astype(_BF16), earlier) + carry[:, 0:1]
    ranks = [jnp.sum(jnp.where(oh, before, 0.0), axis=0, keepdims=True) for oh in onehots]
    carry[...] = carry[...] + jnp.sum(mask, axis=1, keepdims=True)
    counts_ref[...] = carry[...]

    out_row = lax.broadcasted_iota(jnp.int32, (2 * TOP_K, TM), 0)
    route = jnp.zeros((2 * TOP_K, TM), _F32)
    gate_out = jnp.zeros((2 * TOP_K, TM), _F32)
    for k in range(TOP_K):
        route = jnp.where(out_row == k, ids[k], route)
        route = jnp.where(out_row == TOP_K + k, ranks[k], route)
        gate_out = jnp.where(out_row == k, gates[k], gate_out)
    route_ref[...] = route.astype(jnp.int32)
    gates_ref[...] = gate_out


def _mixer_router(x2d, ln_mix_g, w_in_bf, conv_w, sgu_ln_g, sgu_ln_b, sgu_w, sgu_b_full,
                  gn_conv, gn_sgu, w_out_bf, ln_ffn_g, wr_split, br_pad):
    def full(a):
        return pl.BlockSpec(a.shape, lambda i: (0,) * a.ndim)

    row_blk = lambda w: pl.BlockSpec((TM, w), lambda i: (i, 0))
    col_blk = pl.BlockSpec((2 * TOP_K, TM), lambda i: (0, i))
    ins = [x2d, ln_mix_g, w_in_bf, conv_w, sgu_ln_g, sgu_ln_b, sgu_w, sgu_b_full,
           gn_conv, gn_sgu, w_out_bf, ln_ffn_g, wr_split, br_pad]
    in_specs = [row_blk(D_MODEL)] + [full(a) for a in ins[1:]]
    return pl.pallas_call(
        _mixer_router_kernel,
        grid=(N_TOK // TM,),
        in_specs=in_specs,
        out_specs=[row_blk(D_MODEL), row_blk(D_MODEL), col_blk, col_blk,
                   pl.BlockSpec((N_EXPERTS, LANES), lambda i: (0, 0))],
        out_shape=[jax.ShapeDtypeStruct((N_TOK, D_MODEL), _F32),
                   jax.ShapeDtypeStruct((N_TOK, D_MODEL), _F32),
                   jax.ShapeDtypeStruct((2 * TOP_K, N_TOK), jnp.int32),
                   jax.ShapeDtypeStruct((2 * TOP_K, N_TOK), _F32),
                   jax.ShapeDtypeStruct((N_EXPERTS, LANES), _F32)],
        scratch_shapes=[pltpu.VMEM((TM + SUBLANES, D_CONV), _F32),
                        pltpu.VMEM((N_EXPERTS, LANES), _F32),
                        pltpu.VMEM((TM, D_MODEL), _BF16)],
        compiler_params=pltpu.CompilerParams(dimension_semantics=("arbitrary",),
                                             vmem_limit_bytes=VMEM_LIMIT),
        name="mixer_router",
    )(*ins)


def _dispatch_kernel(dest_ref, x_hbm, bufx_hbm, x_in, in_sem, out_sem):
    i = pl.program_id(0)
    n_steps = pl.num_programs(0)

    def fetch(blk, slot):
        r = pl.multiple_of(blk * TD, TD)
        return pltpu.make_async_copy(x_hbm.at[pl.ds(r, TD)], x_in.at[slot], in_sem.at[slot])

    def row_copy(blk, slot, t8, u, k):
        t = pl.multiple_of(t8 * SUBLANES, SUBLANES) + u
        dst = dest_ref[(blk * TD + t) * TOP_K + k]
        return pltpu.make_async_copy(x_in.at[slot, pl.ds(t, 1)], bufx_hbm.at[pl.ds(dst, 1)],
                                     out_sem.at[blk % 2])

    def scatter(blk, slot, start):
        def body(t8, carry):
            for u in range(SUBLANES):
                for k in range(TOP_K):
                    copy = row_copy(blk, slot, t8, u, k)
                    copy.start() if start else copy.wait()
            return carry
        lax.fori_loop(0, TD // SUBLANES, body, 0)

    def in_slot_of(blk, fn):
        for q in range(IN_SLOTS):
            @pl.when(blk % IN_SLOTS == q)
            def _(q=q):
                fn(q)

    @pl.when(i == 0)
    def _():
        fetch(0, 0).start()
        fetch(1, 1).start()

    in_slot_of(i, lambda q: fetch(i, q).wait())
    in_slot_of(i, lambda q: scatter(i, q, True))

    @pl.when(i >= 1)
    def _():
        in_slot_of(i - 1, lambda q: scatter(i - 1, q, False))

    @pl.when(i + 2 < n_steps)
    def _():
        in_slot_of(i + 2, lambda q: fetch(i + 2, q).start())

    @pl.when(i == n_steps - 1)
    def _():
        in_slot_of(i, lambda q: scatter(i, q, False))


def _dispatch(dest, xn2):
    return pl.pallas_call(
        _dispatch_kernel,
        grid_spec=pltpu.PrefetchScalarGridSpec(
            num_scalar_prefetch=1,
            grid=(N_TOK // TD,),
            in_specs=[pl.BlockSpec(memory_space=pl.ANY)],
            out_specs=pl.BlockSpec(memory_space=pl.ANY),
            scratch_shapes=[pltpu.VMEM((IN_SLOTS, TD, D_MODEL), _F32),
                            pltpu.SemaphoreType.DMA((IN_SLOTS,)),
                            pltpu.SemaphoreType.DMA((2,))],
        ),
        out_shape=jax.ShapeDtypeStruct((MAX_SUPER * SUPER_ROWS, D_MODEL), _F32),
        compiler_params=pltpu.CompilerParams(dimension_semantics=("arbitrary",),
                                             vmem_limit_bytes=VMEM_LIMIT),
        name="dispatch",
    )(dest, xn2)


def _expert_kernel(sup_e_ref, sup_blk_ref, nchunk_ref, nhalf_ref, valid_ref,
                   x_hbm, wg_ref, wu_ref, bgu_ref, wd_ref, bd_ref, out_hbm,
                   x_stage, x_bf, acc, x_sem, o_sem):
    s = pl.program_id(0)
    j = pl.program_id(1)
    n_super = pl.num_programs(0)
    n_chunks = nchunk_ref[s]
    cur = s % 2

    def x_copy(sup, c):
        r = pl.multiple_of(sup_blk_ref[sup] * SUPER_ROWS + c * ROW_CHUNK, ROW_CHUNK)
        slot = c % X_SLOTS
        return pltpu.make_async_copy(x_hbm.at[pl.ds(r, ROW_CHUNK)], x_stage.at[slot],
                                     x_sem.at[slot])

    def keep_tokens(sup, c, chunk):
        row = c * ROW_CHUNK + lax.broadcasted_iota(jnp.int32, (ROW_CHUNK, 1), 0)
        return jnp.where(row < valid_ref[sup], chunk, 0.0).astype(_BF16)

    def x_finish(sup, c):
        x_copy(sup, c).wait()
        rows = pl.ds(pl.multiple_of(c * ROW_CHUNK, ROW_CHUNK), ROW_CHUNK)
        x_bf[sup % 2, rows, :] = keep_tokens(sup, c, x_stage[c % X_SLOTS])

    def first_copy(c):
        rows = pl.ds(c * ROW_CHUNK, ROW_CHUNK)
        r = pl.multiple_of(sup_blk_ref[0] * SUPER_ROWS + c * ROW_CHUNK, ROW_CHUNK)
        return pltpu.make_async_copy(x_hbm.at[pl.ds(r, ROW_CHUNK)], acc.at[1, rows], x_sem.at[0])

    def out_copy(sup, c):
        r = pl.multiple_of(sup_blk_ref[sup] * SUPER_ROWS + c * ROW_CHUNK, ROW_CHUNK)
        rows = pl.ds(pl.multiple_of(c * ROW_CHUNK, ROW_CHUNK), ROW_CHUNK)
        return pltpu.make_async_copy(acc.at[sup % 2, rows], out_hbm.at[pl.ds(r, ROW_CHUNK)],
                                     o_sem.at[sup % 2])

    def for_each_chunk(sup, fn):
        for c in range(CHUNKS_PER_SUPER):
            @pl.when(c < nchunk_ref[sup])
            def _(c=c):
                fn(sup, c)

    @pl.when((s == 0) & (j == 0))
    def _():
        for_each_chunk(0, lambda sup, c: first_copy(c).start())

        def finish(sup, c):
            first_copy(c).wait()
            rows = pl.ds(c * ROW_CHUNK, ROW_CHUNK)
            x_bf[0, rows, :] = keep_tokens(0, c, acc[1, rows, :])
        for_each_chunk(0, finish)

    @pl.when((s >= 2) & (j == 0))
    def _():
        for_each_chunk(jnp.maximum(s - 2, 0), lambda sup, c: out_copy(sup, c).wait())

    prev = jnp.maximum(s - 1, 0)

    @pl.when((s >= 1) & (j < nchunk_ref[prev]))
    def _():
        out_copy(prev, j).start()

    nxt = jnp.minimum(s + 1, n_super - 1)
    n_next = jnp.where(s + 1 < n_super, nchunk_ref[nxt], 0)

    @pl.when((j >= 1) & (j - 1 < n_next))
    def _():
        x_finish(nxt, j - 1)

    @pl.when(j < n_next)
    def _():
        x_copy(nxt, j).start()

    @pl.when(n_chunks > 0)
    def _():
        def partial_out(n_rows):
            rows = pl.ds(0, n_rows)
            xb = x_bf[cur, rows, :]
            gate = _dot(xb, wg_ref[0].astype(_BF16)) + bgu_ref[0, pl.ds(j, 1), :]
            up = _dot(xb, wu_ref[0].astype(_BF16)) + bgu_ref[0, pl.ds(N_FF_TILES + j, 1), :]
            gate = jnp.minimum(gate, SWIGLU_LIMIT)
            up = jnp.clip(up, -SWIGLU_LIMIT, SWIGLU_LIMIT)
            glu = gate * jax.nn.sigmoid(gate * SWIGLU_ALPHA)
            act = ((up + 1.0) * glu).astype(_BF16)
            return rows, _dot(act, wd_ref[0].astype(_BF16))

        def sweep(update):
            n_half = nhalf_ref[s]
            for n_rows, (lo, hi) in ROW_VARIANTS.items():
                @pl.when((n_half >= lo) & (n_half <= hi))
                def _(n_rows=n_rows):
                    update(n_rows)

        def init(n_rows):
            rows, o = partial_out(n_rows)
            acc[cur, rows, :] = o + bd_ref[0]

        def accumulate(n_rows):
            rows, o = partial_out(n_rows)
            acc[cur, rows, :] += o

        @pl.when(j == 0)
        def _():
            sweep(init)

        @pl.when(j > 0)
        def _():
            sweep(accumulate)

    @pl.when((s == n_super - 1) & (j == N_FF_TILES - 1))
    def _():
        for_each_chunk(s, lambda sup, c: out_copy(sup, c).start())

        @pl.when(s >= 1)
        def _():
            for_each_chunk(jnp.maximum(s - 1, 0), lambda sup, c: out_copy(sup, c).wait())

        for_each_chunk(s, lambda sup, c: out_copy(sup, c).wait())


def _experts(n_super, sup_e, sup_blk, nchunk, nhalf, valid, buf_x, w_gate_up, b_gate_up3, w_down, b_down3):
    x_spec = pl.BlockSpec(memory_space=pl.ANY)
    wg_spec = pl.BlockSpec((1, D_MODEL, FF_TILE), lambda s, j, se, sb, nc, nh, va: (se[s], 0, j))
    wu_spec = pl.BlockSpec((1, D_MODEL, FF_TILE),
                           lambda s, j, se, sb, nc, nh, va: (se[s], 0, N_FF_TILES + j))
    bgu_spec = pl.BlockSpec((1, 2 * N_FF_TILES, FF_TILE), lambda s, j, se, sb, nc, nh, va: (se[s], 0, 0))
    wd_spec = pl.BlockSpec((1, FF_TILE, D_MODEL), lambda s, j, se, sb, nc, nh, va: (se[s], j, 0))
    bd_spec = pl.BlockSpec((1, 1, D_MODEL), lambda s, j, se, sb, nc, nh, va: (se[s], 0, 0))
    out_spec = pl.BlockSpec(memory_space=pl.ANY)
    return pl.pallas_call(
        _expert_kernel,
        grid_spec=pltpu.PrefetchScalarGridSpec(
            num_scalar_prefetch=5,
            grid=(n_super, N_FF_TILES),
            in_specs=[x_spec, wg_spec, wu_spec, bgu_spec, wd_spec, bd_spec],
            out_specs=out_spec,
            scratch_shapes=[pltpu.VMEM((X_SLOTS, ROW_CHUNK, D_MODEL), _F32),
                            pltpu.VMEM((2, SUPER_ROWS, D_MODEL), _BF16),
                            pltpu.VMEM((2, SUPER_ROWS, D_MODEL), _F32),
                            pltpu.SemaphoreType.DMA((X_SLOTS,)),
                            pltpu.SemaphoreType.DMA((2,))],
        ),
        out_shape=jax.ShapeDtypeStruct((MAX_SUPER * SUPER_ROWS, D_MODEL), _F32),
        compiler_params=pltpu.CompilerParams(dimension_semantics=("arbitrary", "arbitrary"),
                                             vmem_limit_bytes=VMEM_LIMIT),
        name="experts",
    )(sup_e, sup_blk, nchunk, nhalf, valid, buf_x, w_gate_up, w_gate_up, b_gate_up3, w_down, b_down3)


def _combine_kernel(dest_ref, h1_ref, gates_ref, lng_ref, y_hbm, out_ref, rows, sem):
    i = pl.program_id(0)
    n_steps = pl.num_programs(0)

    def row_copy(tile, slot, t8, u, k):
        t = pl.multiple_of(t8 * SUBLANES, SUBLANES) + u
        src = dest_ref[(tile * TC + t) * TOP_K + k]
        return pltpu.make_async_copy(y_hbm.at[pl.ds(src, 1)], rows.at[slot, k, pl.ds(t, 1)],
                                     sem.at[slot])

    def issue_tile(tile, slot):
        def body(t8, carry):
            for u in range(SUBLANES):
                for k in range(TOP_K):
                    row_copy(tile, slot, t8, u, k).start()
            return carry
        lax.fori_loop(0, TC // SUBLANES, body, 0)

    def wait_tile(tile, slot):
        def body(t8, carry):
            for u in range(SUBLANES):
                for k in range(TOP_K):
                    row_copy(tile, slot, t8, u, k).wait()
            return carry
        lax.fori_loop(0, TC // SUBLANES, body, 0)

    @pl.when(i == 0)
    def _():
        issue_tile(0, 0)

    for slot in range(2):
        @pl.when(i % 2 == slot)
        def _(slot=slot):
            @pl.when(i + 1 < n_steps)
            def _():
                issue_tile(i + 1, 1 - slot)

            wait_tile(i, slot)
            g = gates_ref[...]
            acc = h1_ref[...]
            for k in range(TOP_K):
                acc = acc + rows[slot, k] * g[:, k:k + 1]
            out_ref[...] = _rms_rows(acc, lng_ref[...])


def _combine(dest, h1, gates, ln_final_g, y_sorted):
    return pl.pallas_call(
        _combine_kernel,
        grid_spec=pltpu.PrefetchScalarGridSpec(
            num_scalar_prefetch=1,
            grid=(N_TOK // TC,),
            in_specs=[pl.BlockSpec((TC, D_MODEL), lambda i, *_: (i, 0)),
                      pl.BlockSpec((TC, TOP_K), lambda i, *_: (i, 0)),
                      pl.BlockSpec((1, D_MODEL), lambda i, *_: (0, 0)),
                      pl.BlockSpec(memory_space=pl.ANY)],
            out_specs=pl.BlockSpec((TC, D_MODEL), lambda i, *_: (i, 0)),
            scratch_shapes=[pltpu.VMEM((2, TOP_K, TC, D_MODEL), _F32),
                            pltpu.SemaphoreType.DMA((2,))],
        ),
        out_shape=jax.ShapeDtypeStruct((N_TOK, D_MODEL), _F32),
        compiler_params=pltpu.CompilerParams(dimension_semantics=("arbitrary",),
                                             vmem_limit_bytes=VMEM_LIMIT),
        name="combine",
    )(dest, h1, gates, ln_final_g, y_sorted)


def kernel(x, ln_mix_g, w_in, conv_w, sgu_ln_g, sgu_ln_b, sgu_w, sgu_b, gn_conv, gn_sgu, w_out,
           ln_ffn_g, w_router, b_router, w_gate_up, b_gate_up, w_down, b_down, ln_final_g):
    row = lambda v: v.reshape(1, -1)
    x2d = x.reshape(N_TOK, D_MODEL)

    sgu_b_full = jnp.repeat(jnp.transpose(sgu_b), SGU_HEAD_DIM, axis=1)
    wr_hi = w_router.astype(_BF16)
    wr_lo = (w_router - wr_hi.astype(_F32)).astype(_BF16)
    pad = ((0, 0), (0, LANES - N_EXPERTS))
    wr_split = jnp.concatenate([jnp.pad(wr_hi, pad), jnp.pad(wr_lo, pad)], axis=1)
    br_pad = jnp.pad(b_router, (0, LANES - N_EXPERTS)).reshape(1, LANES)

    h1, xn2, route, gates, counts = _mixer_router(
        x2d, row(ln_mix_g), w_in.astype(_BF16), conv_w, row(sgu_ln_g), row(sgu_ln_b), sgu_w,
        sgu_b_full, row(gn_conv), row(gn_sgu), w_out.astype(_BF16), row(ln_ffn_g), wr_split, br_pad)

    sizes = counts[:, 0].astype(jnp.int32)
    n_sup = (sizes + SUPER_ROWS - 1) // SUPER_ROWS
    sup_end = jnp.cumsum(n_sup)
    sup_start = sup_end - n_sup
    total_sup = sup_end[-1]
    row_start = (sup_start * SUPER_ROWS).astype(jnp.int32)
    sid = jnp.arange(MAX_SUPER, dtype=jnp.int32)
    sid_eff = jnp.minimum(sid, total_sup - 1)
    sup_e = jnp.minimum(jnp.sum((sid_eff[:, None] >= sup_end[None, :]).astype(jnp.int32), axis=1),
                        N_EXPERTS - 1)
    of_expert = sup_e[:, None] == jnp.arange(N_EXPERTS, dtype=jnp.int32)[None, :]
    rows_left = jnp.sum(jnp.where(of_expert, sizes - (sid_eff[:, None] - sup_start) * SUPER_ROWS, 0),
                        axis=1)
    valid = jnp.clip(rows_left, 0, SUPER_ROWS)
    nchunk = jnp.where(sid < total_sup, (valid + ROW_CHUNK - 1) // ROW_CHUNK, 0).astype(jnp.int32)
    nhalf = jnp.where(sid < total_sup, (valid + HALF_CHUNK - 1) // HALF_CHUNK, 0).astype(jnp.int32)
    eid = route[0:TOP_K]
    eid_start = jnp.sum(jnp.where(eid[None] == jnp.arange(N_EXPERTS, dtype=jnp.int32)[:, None, None],
                                  row_start[:, None, None], 0), axis=0)
    dest = jnp.transpose(eid_start + route[TOP_K:2 * TOP_K]).reshape(-1)
    gates = jnp.transpose(gates[0:TOP_K])

    buf_x = _dispatch(dest, xn2)
    y_sorted = _experts(total_sup, sup_e, sid_eff.astype(jnp.int32), nchunk, nhalf,
                        valid.astype(jnp.int32), buf_x, w_gate_up,
                        b_gate_up.reshape(N_EXPERTS, 2 * N_FF_TILES, FF_TILE), w_down,
                        b_down.reshape(N_EXPERTS, 1, D_MODEL))
    out = _combine(dest, h1, gates, row(ln_final_g), y_sorted)
    return out.reshape(BATCH, SEQ, D_MODEL)
```

```python
import jax
import jax.numpy as jnp
from jax import lax
from jax.experimental import pallas as pl
from jax.experimental.pallas import tpu as pltpu

D_MODEL = 2048
BATCH = 2
SEQ = 4096
N_TOK = BATCH * SEQ

D_CONV = 1024
N_CONV_GROUPS = 8
CONV_WIDTH = 3
D_SGU = 1024
N_SGU_HEADS = 8
SGU_HEAD_DIM = 128
CHUNK = 128
D_IN_PROJ = 3 * D_CONV + 2 * D_SGU

N_EXPERTS = 32
TOP_K = 4
D_FF = 2048
SWIGLU_LIMIT = 7.0
SWIGLU_ALPHA = 1.702
RMS_EPS = 1e-5
LN_EPS = 1e-5

LANES = 128
SUBLANES = 8
VMEM_LIMIT = 56 * 1024 * 1024

TM = 256
SUPER_ROWS = 1280
ROW_CHUNK = 256
FF_TILE = 256
N_FF_TILES = D_FF // FF_TILE
MAX_SUPER = (N_TOK * TOP_K + N_EXPERTS * (SUPER_ROWS - 1)) // SUPER_ROWS
CHUNKS_PER_SUPER = SUPER_ROWS // ROW_CHUNK
assert CHUNKS_PER_SUPER + 1 <= N_FF_TILES
X_SLOTS = 2
HALF_CHUNK = ROW_CHUNK // 2
TD = 256
IN_SLOTS = 3
assert N_TOK // TD >= 2
TC = 256

_F32 = jnp.float32
_BF16 = jnp.bfloat16


def _dot(a, b):
    return jnp.dot(a, b, preferred_element_type=_F32)


def _gelu_exact(x):
    return 0.5 * x * (1.0 + lax.erf(x * (2.0 ** -0.5)))


def _rms_rows(x, gain):
    return x * lax.rsqrt(jnp.mean(x * x, axis=-1, keepdims=True) + RMS_EPS) * gain


def _mixer_router_kernel(x_ref, lng_ref, win_ref, convw_ref, slg_ref, slb_ref, sw_ref, sb_ref,
                         gnc_ref, gns_ref, wout_ref, lnf_ref, wr_ref, br_ref,
                         h1_ref, xn2_ref, route_ref, gates_ref, counts_ref,
                         cbuf, carry, ybuf):
    i = pl.program_id(0)

    @pl.when(i % (SEQ // TM) == 0)
    def _():
        cbuf[0:SUBLANES, :] = jnp.zeros((SUBLANES, D_CONV), _F32)

    @pl.when(i == 0)
    def _():
        carry[...] = jnp.zeros_like(carry)

    x = x_ref[...]
    xn = _rms_rows(x, lng_ref[...]).astype(_BF16)

    b_gate = _dot(xn, win_ref[:, 0:D_CONV])
    c_gate = _dot(xn, win_ref[:, D_CONV:2 * D_CONV])
    hh = _dot(xn, win_ref[:, 2 * D_CONV:3 * D_CONV])
    ch = c_gate * hh
    cbuf[SUBLANES:SUBLANES + TM, :] = ch
    ch1 = cbuf[SUBLANES - 1:SUBLANES - 1 + TM, :]
    ch2 = cbuf[SUBLANES - 2:SUBLANES - 2 + TM, :]
    conv = convw_ref[0:1, :] * ch2 + convw_ref[1:2, :] * ch1 + convw_ref[2:3, :] * ch
    cbuf[0:SUBLANES, :] = cbuf[TM:TM + SUBLANES, :]
    y_conv = b_gate * conv
    for g in range(N_CONV_GROUPS):
        sl = slice(g * LANES, (g + 1) * LANES)
        blk = y_conv[:, sl]
        ms = jnp.mean(blk * blk, axis=-1, keepdims=True)
        ybuf[:, sl] = (blk * lax.rsqrt(ms + RMS_EPS) * gnc_ref[:, sl]).astype(_BF16)

    gu = _gelu_exact(_dot(xn, win_ref[:, 3 * D_CONV:3 * D_CONV + D_SGU]))
    gv = _gelu_exact(_dot(xn, win_ref[:, 3 * D_CONV + D_SGU:D_IN_PROJ]))
    row_c = lax.broadcasted_iota(jnp.int32, (CHUNK, CHUNK), 0)
    col_c = lax.broadcasted_iota(jnp.int32, (CHUNK, CHUNK), 1)
    causal = row_c >= col_c
    for h in range(N_SGU_HEADS):
        sl = slice(h * SGU_HEAD_DIM, (h + 1) * SGU_HEAD_DIM)
        vh = gv[:, sl]
        mu = jnp.mean(vh, axis=-1, keepdims=True)
        xc = vh - mu
        var = jnp.mean(xc * xc, axis=-1, keepdims=True)
        vn = (xc * lax.rsqrt(var + LN_EPS) * slg_ref[:, sl] + slb_ref[:, sl]).astype(_BF16)
        wm = jnp.where(causal, sw_ref[h], 0.0).astype(_BF16)
        for c in range(TM // CHUNK):
            rows = slice(c * CHUNK, (c + 1) * CHUNK)
            mixed = _dot(wm, vn[rows, :]) + sb_ref[:, sl]
            ys = gu[rows, sl] * mixed
            ms = jnp.mean(ys * ys, axis=-1, keepdims=True)
            ybuf[rows, D_CONV + h * SGU_HEAD_DIM:D_CONV + (h + 1) * SGU_HEAD_DIM] = (
                ys * lax.rsqrt(ms + RMS_EPS) * gns_ref[:, sl]).astype(_BF16)

    h1 = x + _dot(ybuf[...], wout_ref[...])
    h1_ref[...] = h1
    xn2 = _rms_rows(h1, lnf_ref[...])
    xn2_ref[...] = xn2

    x_hi = xn2.astype(_BF16)
    x_lo = (xn2 - x_hi.astype(_F32)).astype(_BF16)
    p = _dot(x_hi, wr_ref[...]) + _dot(x_lo, wr_ref[...])
    logits = p[:, :LANES] + p[:, LANES:] + br_ref[...]
    cur = jnp.transpose(logits)[0:N_EXPERTS, :]
    expert = lax.broadcasted_iota(jnp.int32, (N_EXPERTS, TM), 0).astype(_F32)
    neg_inf = jnp.float32(-jnp.inf)

    vals, ids, onehots = [], [], []
    for _ in range(TOP_K):
        m = jnp.max(cur, axis=0, keepdims=True)
        idx = jnp.min(jnp.where(cur == m, expert, float(N_EXPERTS)), axis=0, keepdims=True)
        oh = expert == idx
        vals.append(m)
        ids.append(idx)
        onehots.append(oh)
        cur = jnp.where(oh, neg_inf, cur)
    exps = [jnp.exp(v - vals[0]) for v in vals]
    denom = exps[0] + exps[1] + exps[2] + exps[3]
    gates = [e / denom for e in exps]

    mask = (onehots[0] | onehots[1] | onehots[2] | onehots[3]).astype(_F32)
    row_t = lax.broadcasted_iota(jnp.int32, (TM, TM), 0)
    col_t = lax.broadcasted_iota(jnp.int32, (TM, TM), 1)
    earlier = (row_t < col_t).astype(_BF16)
    before = _dot(mask.astype(_BF16), earlier) + carry[:, 0:1]
    ranks = [jnp.sum(jnp.where(oh, before, 0.0), axis=0, keepdims=True) for oh in onehots]
    carry[...] = carry[...] + jnp.sum(mask, axis=1, keepdims=True)
    counts_ref[...] = carry[...]

    out_row = lax.broadcasted_iota(jnp.int32, (2 * TOP_K, TM), 0)
    route = jnp.zeros((2 * TOP_K, TM), _F32)
    gate_out = jnp.zeros((2 * TOP_K, TM), _F32)
    for k in range(TOP_K):
        route = jnp.where(out_row == k, ids[k], route)
        route = jnp.where(out_row == TOP_K + k, ranks[k], route)
        gate_out = jnp.where(out_row == k, gates[k], gate_out)
    route_ref[...] = route.astype(jnp.int32)
    gates_ref[...] = gate_out


def _mixer_router(x2d, ln_mix_g, w_in_bf, conv_w, sgu_ln_g, sgu_ln_b, sgu_w, sgu_b_full,
                  gn_conv, gn_sgu, w_out_bf, ln_ffn_g, wr_split, br_pad):
    def full(a):
        return pl.BlockSpec(a.shape, lambda i: (0,) * a.ndim)

    row_blk = lambda w: pl.BlockSpec((TM, w), lambda i: (i, 0))
    col_blk = pl.BlockSpec((2 * TOP_K, TM), lambda i: (0, i))
    ins = [x2d, ln_mix_g, w_in_bf, conv_w, sgu_ln_g, sgu_ln_b, sgu_w, sgu_b_full,
           gn_conv, gn_sgu, w_out_bf, ln_ffn_g, wr_split, br_pad]
    in_specs = [row_blk(D_MODEL)] + [full(a) for a in ins[1:]]
    return pl.pallas_call(
        _mixer_router_kernel,
        grid=(N_TOK // TM,),
        in_specs=in_specs,
        out_specs=[row_blk(D_MODEL), row_blk(D_MODEL), col_blk, col_blk,
                   pl.BlockSpec((N_EXPERTS, LANES), lambda i: (0, 0))],
        out_shape=[jax.ShapeDtypeStruct((N_TOK, D_MODEL), _F32),
                   jax.ShapeDtypeStruct((N_TOK, D_MODEL), _F32),
                   jax.ShapeDtypeStruct((2 * TOP_K, N_TOK), jnp.int32),
                   jax.ShapeDtypeStruct((2 * TOP_K, N_TOK), _F32),
                   jax.ShapeDtypeStruct((N_EXPERTS, LANES), _F32)],
        scratch_shapes=[pltpu.VMEM((TM + SUBLANES, D_CONV), _F32),
                        pltpu.VMEM((N_EXPERTS, LANES), _F32),
                        pltpu.VMEM((TM, D_MODEL), _BF16)],
        compiler_params=pltpu.CompilerParams(dimension_semantics=("arbitrary",),
                                             vmem_limit_bytes=VMEM_LIMIT),
        name="mixer_router",
    )(*ins)


def _dispatch_kernel(dest_ref, x_hbm, bufx_hbm, x_in, in_sem, out_sem):
    i = pl.program_id(0)
    n_steps = pl.num_programs(0)

    def fetch(blk, slot):
        r = pl.multiple_of(blk * TD, TD)
        return pltpu.make_async_copy(x_hbm.at[pl.ds(r, TD)], x_in.at[slot], in_sem.at[slot])

    def row_copy(blk, slot, t8, u, k):
        t = pl.multiple_of(t8 * SUBLANES, SUBLANES) + u
        dst = dest_ref[(blk * TD + t) * TOP_K + k]
        return pltpu.make_async_copy(x_in.at[slot, pl.ds(t, 1)], bufx_hbm.at[pl.ds(dst, 1)],
                                     out_sem.at[blk % 2])

    def scatter(blk, slot, start):
        def body(t8, carry):
            for u in range(SUBLANES):
                for k in range(TOP_K):
                    copy = row_copy(blk, slot, t8, u, k)
                    copy.start() if start else copy.wait()
            return carry
        lax.fori_loop(0, TD // SUBLANES, body, 0)

    def in_slot_of(blk, fn):
        for q in range(IN_SLOTS):
            @pl.when(blk % IN_SLOTS == q)
            def _(q=q):
                fn(q)

    @pl.when(i == 0)
    def _():
        fetch(0, 0).start()
        fetch(1, 1).start()

    in_slot_of(i, lambda q: fetch(i, q).wait())
    in_slot_of(i, lambda q: scatter(i, q, True))

    @pl.when(i >= 1)
    def _():
        in_slot_of(i - 1, lambda q: scatter(i - 1, q, False))

    @pl.when(i + 2 < n_steps)
    def _():
        in_slot_of(i + 2, lambda q: fetch(i + 2, q).start())

    @pl.when(i == n_steps - 1)
    def _():
        in_slot_of(i, lambda q: scatter(i, q, False))


def _dispatch(dest, xn2):
    return pl.pallas_call(
        _dispatch_kernel,
        grid_spec=pltpu.PrefetchScalarGridSpec(
            num_scalar_prefetch=1,
            grid=(N_TOK // TD,),
            in_specs=[pl.BlockSpec(memory_space=pl.ANY)],
            out_specs=pl.BlockSpec(memory_space=pl.ANY),
            scratch_shapes=[pltpu.VMEM((IN_SLOTS, TD, D_MODEL), _F32),
                            pltpu.SemaphoreType.DMA((IN_SLOTS,)),
                            pltpu.SemaphoreType.DMA((2,))],
        ),
        out_shape=jax.ShapeDtypeStruct((MAX_SUPER * SUPER_ROWS, D_MODEL), _F32),
        compiler_params=pltpu.CompilerParams(dimension_semantics=("arbitrary",),
                                             vmem_limit_bytes=VMEM_LIMIT),
        name="dispatch",
    )(dest, xn2)


def _expert_kernel(sup_e_ref, sup_blk_ref, nchunk_ref, nhalf_ref, valid_ref,
                   x_hbm, wg_ref, wu_ref, bgu_ref, wd_ref, bd_ref, out_hbm,
                   x_stage, x_bf, acc, x_sem, o_sem):
    s = pl.program_id(0)
    j = pl.program_id(1)
    n_super = pl.num_programs(0)
    n_chunks = nchunk_ref[s]
    cur = s % 2

    def x_copy(sup, c):
        r = pl.multiple_of(sup_blk_ref[sup] * SUPER_ROWS + c * ROW_CHUNK, ROW_CHUNK)
        slot = c % X_SLOTS
        return pltpu.make_async_copy(x_hbm.at[pl.ds(r, ROW_CHUNK)], x_stage.at[slot],
                                     x_sem.at[slot])

    def keep_tokens(sup, c, chunk):
        row = c * ROW_CHUNK + lax.broadcasted_iota(jnp.int32, (ROW_CHUNK, 1), 0)
        return jnp.where(row < valid_ref[sup], chunk, 0.0).astype(_BF16)

    def x_finish(sup, c):
        x_copy(sup, c).wait()
        rows = pl.ds(pl.multiple_of(c * ROW_CHUNK, ROW_CHUNK), ROW_CHUNK)
        x_bf[sup % 2, rows, :] = keep_tokens(sup, c, x_stage[c % X_SLOTS])

    def first_copy(c):
        rows = pl.ds(c * ROW_CHUNK, ROW_CHUNK)
        r = pl.multiple_of(sup_blk_ref[0] * SUPER_ROWS + c * ROW_CHUNK, ROW_CHUNK)
        return pltpu.make_async_copy(x_hbm.at[pl.ds(r, ROW_CHUNK)], acc.at[1, rows], x_sem.at[0])

    def out_copy(sup, c):
        r = pl.multiple_of(sup_blk_ref[sup] * SUPER_ROWS + c * ROW_CHUNK, ROW_CHUNK)
        rows = pl.ds(pl.multiple_of(c * ROW_CHUNK, ROW_CHUNK), ROW_CHUNK)
        return pltpu.make_async_copy(acc.at[sup % 2, rows], out_hbm.at[pl.ds(r, ROW_CHUNK)],
                                     o_sem.at[sup % 2])

    def for_each_chunk(sup, fn):
        for c in range(CHUNKS_PER_SUPER):
            @pl.when(c < nchunk_ref[sup])
            def _(c=c):
                fn(sup, c)

    @pl.when((s == 0) & (j == 0))
    def _():
        for_each_chunk(0, lambda sup, c: first_copy(c).start())

        def finish(sup, c):
            first_copy(c).wait()
            rows = pl.ds(c * ROW_CHUNK, ROW_CHUNK)
            x_bf[0, rows, :] = keep_tokens(0, c, acc[1, rows, :])
        for_each_chunk(0, finish)

    @pl.when((s >= 2) & (j == 0))
    def _():
        for_each_chunk(jnp.maximum(s - 2, 0), lambda sup, c: out_copy(sup, c).wait())

    prev = jnp.maximum(s - 1, 0)

    @pl.when((s >= 1) & (j < nchunk_ref[prev]))
    def _():
        out_copy(prev, j).start()

    nxt = jnp.minimum(s + 1, n_super - 1)
    n_next = jnp.where(s + 1 < n_super, nchunk_ref[nxt], 0)

    @pl.when((j >= 1) & (j - 1 < n_next))
    def _():
        x_finish(nxt, j - 1)

    @pl.when(j < n_next)
    def _():
        x_copy(nxt, j).start()

    @pl.when(n_chunks > 0)
    def _():
        def partial_out(n_rows):
            rows = pl.ds(0, n_rows)
            xb = x_bf[cur, rows, :]
            gate = _dot(xb, wg_ref[0].astype(_BF16)) + bgu_ref[0, pl.ds(j, 1), :]
            up = _dot(xb, wu_ref[0].astype(_BF16)) + bgu_ref[0, pl.ds(N_FF_TILES + j, 1), :]
            gate = jnp.minimum(gate, SWIGLU_LIMIT)
            up = jnp.clip(up, -SWIGLU_LIMIT, SWIGLU_LIMIT)
            glu = gate * jax.nn.sigmoid(gate * SWIGLU_ALPHA)
            act = ((up + 1.0) * glu).astype(_BF16)
            return rows, _dot(act, wd_ref[0].astype(_BF16))

        def sweep(update):
            n_half = nhalf_ref[s]
            for n in range(1, SUPER_ROWS // HALF_CHUNK + 1):
                @pl.when(n_half == n)
                def _(n=n):
                    update(n * HALF_CHUNK)

        def accumulate(n_rows):
            rows, o = partial_out(n_rows)
            acc[cur, rows, :] = jnp.where(j == 0, bd_ref[0], acc[cur, rows, :]) + o

        sweep(accumulate)

    @pl.when((s == n_super - 1) & (j == N_FF_TILES - 1))
    def _():
        for_each_chunk(s, lambda sup, c: out_copy(sup, c).start())

        @pl.when(s >= 1)
        def _():
            for_each_chunk(jnp.maximum(s - 1, 0), lambda sup, c: out_copy(sup, c).wait())

        for_each_chunk(s, lambda sup, c: out_copy(sup, c).wait())


def _experts(n_super, sup_e, sup_blk, nchunk, nhalf, valid, buf_x, w_gate_up, b_gate_up3, w_down, b_down3):
    x_spec = pl.BlockSpec(memory_space=pl.ANY)
    wg_spec = pl.BlockSpec((1, D_MODEL, FF_TILE), lambda s, j, se, sb, nc, nh, va: (se[s], 0, j))
    wu_spec = pl.BlockSpec((1, D_MODEL, FF_TILE),
                           lambda s, j, se, sb, nc, nh, va: (se[s], 0, N_FF_TILES + j))
    bgu_spec = pl.BlockSpec((1, 2 * N_FF_TILES, FF_TILE), lambda s, j, se, sb, nc, nh, va: (se[s], 0, 0))
    wd_spec = pl.BlockSpec((1, FF_TILE, D_MODEL), lambda s, j, se, sb, nc, nh, va: (se[s], j, 0))
    bd_spec = pl.BlockSpec((1, 1, D_MODEL), lambda s, j, se, sb, nc, nh, va: (se[s], 0, 0))
    out_spec = pl.BlockSpec(memory_space=pl.ANY)
    return pl.pallas_call(
        _expert_kernel,
        grid_spec=pltpu.PrefetchScalarGridSpec(
            num_scalar_prefetch=5,
            grid=(n_super, N_FF_TILES),
            in_specs=[x_spec, wg_spec, wu_spec, bgu_spec, wd_spec, bd_spec],
            out_specs=out_spec,
            scratch_shapes=[pltpu.VMEM((X_SLOTS, ROW_CHUNK, D_MODEL), _F32),
                            pltpu.VMEM((2, SUPER_ROWS, D_MODEL), _BF16),
                            pltpu.VMEM((2, SUPER_ROWS, D_MODEL), _F32),
                            pltpu.SemaphoreType.DMA((X_SLOTS,)),
                            pltpu.SemaphoreType.DMA((2,))],
        ),
        out_shape=jax.ShapeDtypeStruct((MAX_SUPER * SUPER_ROWS, D_MODEL), _F32),
        compiler_params=pltpu.CompilerParams(dimension_semantics=("arbitrary", "arbitrary"),
                                             vmem_limit_bytes=VMEM_LIMIT),
        name="experts",
    )(sup_e, sup_blk, nchunk, nhalf, valid, buf_x, w_gate_up, w_gate_up, b_gate_up3, w_down, b_down3)


def _combine_kernel(dest_ref, h1_ref, gates_ref, lng_ref, y_hbm, out_ref, rows, sem):
    i = pl.program_id(0)
    n_steps = pl.num_programs(0)

    def row_copy(tile, slot, t8, u, k):
        t = pl.multiple_of(t8 * SUBLANES, SUBLANES) + u
        src = dest_ref[(tile * TC + t) * TOP_K + k]
        return pltpu.make_async_copy(y_hbm.at[pl.ds(src, 1)], rows.at[slot, k, pl.ds(t, 1)],
                                     sem.at[slot])

    def issue_tile(tile, slot):
        def body(t8, carry):
            for u in range(SUBLANES):
                for k in range(TOP_K):
                    row_copy(tile, slot, t8, u, k).start()
            return carry
        lax.fori_loop(0, TC // SUBLANES, body, 0)

    def wait_tile(tile, slot):
        def body(t8, carry):
            for u in range(SUBLANES):
                for k in range(TOP_K):
                    row_copy(tile, slot, t8, u, k).wait()
            return carry
        lax.fori_loop(0, TC // SUBLANES, body, 0)

    @pl.when(i == 0)
    def _():
        issue_tile(0, 0)

    for slot in range(2):
        @pl.when(i % 2 == slot)
        def _(slot=slot):
            @pl.when(i + 1 < n_steps)
            def _():
                issue_tile(i + 1, 1 - slot)

            wait_tile(i, slot)
            g = gates_ref[...]
            acc = h1_ref[...]
            for k in range(TOP_K):
                acc = acc + rows[slot, k] * g[:, k:k + 1]
            out_ref[...] = _rms_rows(acc, lng_ref[...])


def _combine(dest, h1, gates, ln_final_g, y_sorted):
    return pl.pallas_call(
        _combine_kernel,
        grid_spec=pltpu.PrefetchScalarGridSpec(
            num_scalar_prefetch=1,
            grid=(N_TOK // TC,),
            in_specs=[pl.BlockSpec((TC, D_MODEL), lambda i, *_: (i, 0)),
                      pl.BlockSpec((TC, TOP_K), lambda i, *_: (i, 0)),
                      pl.BlockSpec((1, D_MODEL), lambda i, *_: (0, 0)),
                      pl.BlockSpec(memory_space=pl.ANY)],
            out_specs=pl.BlockSpec((TC, D_MODEL), lambda i, *_: (i, 0)),
            scratch_shapes=[pltpu.VMEM((2, TOP_K, TC, D_MODEL), _F32),
                            pltpu.SemaphoreType.DMA((2,))],
        ),
        out_shape=jax.ShapeDtypeStruct((N_TOK, D_MODEL), _F32),
        compiler_params=pltpu.CompilerParams(dimension_semantics=("arbitrary",),
                                             vmem_limit_bytes=VMEM_LIMIT),
        name="combine",
    )(dest, h1, gates, ln_final_g, y_sorted)


def kernel(x, ln_mix_g, w_in, conv_w, sgu_ln_g, sgu_ln_b, sgu_w, sgu_b, gn_conv, gn_sgu, w_out,
           ln_ffn_g, w_router, b_router, w_gate_up, b_gate_up, w_down, b_down, ln_final_g):
    row = lambda v: v.reshape(1, -1)
    x2d = x.reshape(N_TOK, D_MODEL)

    sgu_b_full = jnp.repeat(jnp.transpose(sgu_b), SGU_HEAD_DIM, axis=1)
    wr_hi = w_router.astype(_BF16)
    wr_lo = (w_router - wr_hi.astype(_F32)).astype(_BF16)
    pad = ((0, 0), (0, LANES - N_EXPERTS))
    wr_split = jnp.concatenate([jnp.pad(wr_hi, pad), jnp.pad(wr_lo, pad)], axis=1)
    br_pad = jnp.pad(b_router, (0, LANES - N_EXPERTS)).reshape(1, LANES)

    h1, xn2, route, gates, counts = _mixer_router(
        x2d, row(ln_mix_g), w_in.astype(_BF16), conv_w, row(sgu_ln_g), row(sgu_ln_b), sgu_w,
        sgu_b_full, row(gn_conv), row(gn_sgu), w_out.astype(_BF16), row(ln_ffn_g), wr_split, br_pad)

    sizes = counts[:, 0].astype(jnp.int32)
    n_sup = (sizes + SUPER_ROWS - 1) // SUPER_ROWS
    sup_end = jnp.cumsum(n_sup)
    sup_start = sup_end - n_sup
    total_sup = sup_end[-1]
    row_start = (sup_start * SUPER_ROWS).astype(jnp.int32)
    sid = jnp.arange(MAX_SUPER, dtype=jnp.int32)
    sid_eff = jnp.minimum(sid, total_sup - 1)
    sup_e = jnp.minimum(jnp.sum((sid_eff[:, None] >= sup_end[None, :]).astype(jnp.int32), axis=1),
                        N_EXPERTS - 1)
    of_expert = sup_e[:, None] == jnp.arange(N_EXPERTS, dtype=jnp.int32)[None, :]
    rows_left = jnp.sum(jnp.where(of_expert, sizes - (sid_eff[:, None] - sup_start) * SUPER_ROWS, 0),
                        axis=1)
    valid = jnp.clip(rows_left, 0, SUPER_ROWS)
    nchunk = jnp.where(sid < total_sup, (valid + ROW_CHUNK - 1) // ROW_CHUNK, 0).astype(jnp.int32)
    nhalf = jnp.where(sid < total_sup, (valid + HALF_CHUNK - 1) // HALF_CHUNK, 0).astype(jnp.int32)
    eid = route[0:TOP_K]
    eid_start = jnp.sum(jnp.where(eid[None] == jnp.arange(N_EXPERTS, dtype=jnp.int32)[:, None, None],
                                  row_start[:, None, None], 0), axis=0)
    dest = jnp.transpose(eid_start + route[TOP_K:2 * TOP_K]).reshape(-1)
    gates = jnp.transpose(gates[0:TOP_K])

    buf_x = _dispatch(dest, xn2)
    y_sorted = _experts(total_sup, sup_e, sid_eff.astype(jnp.int32), nchunk, nhalf,
                        valid.astype(jnp.int32), buf_x, w_gate_up,
                        b_gate_up.reshape(N_EXPERTS, 2 * N_FF_TILES, FF_TILE), w_down,
                        b_down.reshape(N_EXPERTS, 1, D_MODEL))
    out = _combine(dest, h1, gates, row(ln_final_g), y_sorted)
    return out.reshape(BATCH, SEQ, D_MODEL)
```

```python
import jax
import jax.numpy as jnp
from jax import lax
from jax.experimental import pallas as pl
from jax.experimental.pallas import tpu as pltpu

D_MODEL = 2048
BATCH = 2
SEQ = 4096
N_TOK = BATCH * SEQ

D_CONV = 1024
N_CONV_GROUPS = 8
CONV_WIDTH = 3
D_SGU = 1024
N_SGU_HEADS = 8
SGU_HEAD_DIM = 128
CHUNK = 128
D_IN_PROJ = 3 * D_CONV + 2 * D_SGU

N_EXPERTS = 32
TOP_K = 4
D_FF = 2048
SWIGLU_LIMIT = 7.0
SWIGLU_ALPHA = 1.702
RMS_EPS = 1e-5
LN_EPS = 1e-5

LANES = 128
SUBLANES = 8
VMEM_LIMIT = 56 * 1024 * 1024

TM = 256
SUPER_ROWS = 1280
ROW_CHUNK = 256
FF_TILE = 256
N_FF_TILES = D_FF // FF_TILE
MAX_SUPER = (N_TOK * TOP_K + N_EXPERTS * (SUPER_ROWS - 1)) // SUPER_ROWS
CHUNKS_PER_SUPER = SUPER_ROWS // ROW_CHUNK
assert CHUNKS_PER_SUPER + 1 <= N_FF_TILES
X_SLOTS = 2
COARSE_STEP, FINE_STEP, FINE_ABOVE = 128, 64, 896
MATMUL_SIZES = (tuple(range(COARSE_STEP, FINE_ABOVE + 1, COARSE_STEP))
                + tuple(range(FINE_ABOVE + FINE_STEP, SUPER_ROWS + 1, FINE_STEP)))
TD = 256
IN_SLOTS = 3
assert N_TOK // TD >= 2
TC = 256

_F32 = jnp.float32
_BF16 = jnp.bfloat16


def _dot(a, b):
    return jnp.dot(a, b, preferred_element_type=_F32)


def _gelu_exact(x):
    return 0.5 * x * (1.0 + lax.erf(x * (2.0 ** -0.5)))


def _rms_rows(x, gain):
    return x * lax.rsqrt(jnp.mean(x * x, axis=-1, keepdims=True) + RMS_EPS) * gain


def _mixer_router_kernel(x_ref, lng_ref, win_ref, convw_ref, slg_ref, slb_ref, sw_ref, sb_ref,
                         gnc_ref, gns_ref, wout_ref, lnf_ref, wr_ref, br_ref,
                         h1_ref, xn2_ref, route_ref, gates_ref, counts_ref,
                         cbuf, carry, ybuf):
    i = pl.program_id(0)

    @pl.when(i % (SEQ // TM) == 0)
    def _():
        cbuf[0:SUBLANES, :] = jnp.zeros((SUBLANES, D_CONV), _F32)

    @pl.when(i == 0)
    def _():
        carry[...] = jnp.zeros_like(carry)

    x = x_ref[...]
    xn = _rms_rows(x, lng_ref[...]).astype(_BF16)

    b_gate = _dot(xn, win_ref[:, 0:D_CONV])
    c_gate = _dot(xn, win_ref[:, D_CONV:2 * D_CONV])
    hh = _dot(xn, win_ref[:, 2 * D_CONV:3 * D_CONV])
    ch = c_gate * hh
    cbuf[SUBLANES:SUBLANES + TM, :] = ch
    ch1 = cbuf[SUBLANES - 1:SUBLANES - 1 + TM, :]
    ch2 = cbuf[SUBLANES - 2:SUBLANES - 2 + TM, :]
    conv = convw_ref[0:1, :] * ch2 + convw_ref[1:2, :] * ch1 + convw_ref[2:3, :] * ch
    cbuf[0:SUBLANES, :] = cbuf[TM:TM + SUBLANES, :]
    y_conv = b_gate * conv
    for g in range(N_CONV_GROUPS):
        sl = slice(g * LANES, (g + 1) * LANES)
        blk = y_conv[:, sl]
        ms = jnp.mean(blk * blk, axis=-1, keepdims=True)
        ybuf[:, sl] = (blk * lax.rsqrt(ms + RMS_EPS) * gnc_ref[:, sl]).astype(_BF16)

    gu = _gelu_exact(_dot(xn, win_ref[:, 3 * D_CONV:3 * D_CONV + D_SGU]))
    gv = _gelu_exact(_dot(xn, win_ref[:, 3 * D_CONV + D_SGU:D_IN_PROJ]))
    row_c = lax.broadcasted_iota(jnp.int32, (CHUNK, CHUNK), 0)
    col_c = lax.broadcasted_iota(jnp.int32, (CHUNK, CHUNK), 1)
    causal = row_c >= col_c
    for h in range(N_SGU_HEADS):
        sl = slice(h * SGU_HEAD_DIM, (h + 1) * SGU_HEAD_DIM)
        vh = gv[:, sl]
        mu = jnp.mean(vh, axis=-1, keepdims=True)
        xc = vh - mu
        var = jnp.mean(xc * xc, axis=-1, keepdims=True)
        vn = (xc * lax.rsqrt(var + LN_EPS) * slg_ref[:, sl] + slb_ref[:, sl]).astype(_BF16)
        wm = jnp.where(causal, sw_ref[h], 0.0).astype(_BF16)
        for c in range(TM // CHUNK):
            rows = slice(c * CHUNK, (c + 1) * CHUNK)
            mixed = _dot(wm, vn[rows, :]) + sb_ref[:, sl]
            ys = gu[rows, sl] * mixed
            ms = jnp.mean(ys * ys, axis=-1, keepdims=True)
            ybuf[rows, D_CONV + h * SGU_HEAD_DIM:D_CONV + (h + 1) * SGU_HEAD_DIM] = (
                ys * lax.rsqrt(ms + RMS_EPS) * gns_ref[:, sl]).astype(_BF16)

    h1 = x + _dot(ybuf[...], wout_ref[...])
    h1_ref[...] = h1
    xn2 = _rms_rows(h1, lnf_ref[...])
    xn2_ref[...] = xn2

    x_hi = xn2.astype(_BF16)
    x_lo = (xn2 - x_hi.astype(_F32)).astype(_BF16)
    p = _dot(x_hi, wr_ref[...]) + _dot(x_lo, wr_ref[...])
    logits = p[:, :LANES] + p[:, LANES:] + br_ref[...]
    cur = jnp.transpose(logits)[0:N_EXPERTS, :]
    expert = lax.broadcasted_iota(jnp.int32, (N_EXPERTS, TM), 0).astype(_F32)
    neg_inf = jnp.float32(-jnp.inf)

    vals, ids, onehots = [], [], []
    for _ in range(TOP_K):
        m = jnp.max(cur, axis=0, keepdims=True)
        idx = jnp.min(jnp.where(cur == m, expert, float(N_EXPERTS)), axis=0, keepdims=True)
        oh = expert == idx
        vals.append(m)
        ids.append(idx)
        onehots.append(oh)
        cur = jnp.where(oh, neg_inf, cur)
    exps = [jnp.exp(v - vals[0]) for v in vals]
    denom = exps[0] + exps[1] + exps[2] + exps[3]
    gates = [e / denom for e in exps]

    mask = (onehots[0] | onehots[1] | onehots[2] | onehots[3]).astype(_F32)
    row_t = lax.broadcasted_iota(jnp.int32, (TM, TM), 0)
    col_t = lax.broadcasted_iota(jnp.int32, (TM, TM), 1)
    earlier = (row_t < col_t).astype(_BF16)
    before = _dot(mask.astype(_BF16), earlier) + carry[:, 0:1]
    ranks = [jnp.sum(jnp.where(oh, before, 0.0), axis=0, keepdims=True) for oh in onehots]
    carry[...] = carry[...] + jnp.sum(mask, axis=1, keepdims=True)
    counts_ref[...] = carry[...]

    out_row = lax.broadcasted_iota(jnp.int32, (2 * TOP_K, TM), 0)
    route = jnp.zeros((2 * TOP_K, TM), _F32)
    gate_out = jnp.zeros((2 * TOP_K, TM), _F32)
    for k in range(TOP_K):
        route = jnp.where(out_row == k, ids[k], route)
        route = jnp.where(out_row == TOP_K + k, ranks[k], route)
        gate_out = jnp.where(out_row == k, gates[k], gate_out)
    route_ref[...] = route.astype(jnp.int32)
    gates_ref[...] = gate_out


def _mixer_router(x2d, ln_mix_g, w_in_bf, conv_w, sgu_ln_g, sgu_ln_b, sgu_w, sgu_b_full,
                  gn_conv, gn_sgu, w_out_bf, ln_ffn_g, wr_split, br_pad):
    def full(a):
        return pl.BlockSpec(a.shape, lambda i: (0,) * a.ndim)

    row_blk = lambda w: pl.BlockSpec((TM, w), lambda i: (i, 0))
    col_blk = pl.BlockSpec((2 * TOP_K, TM), lambda i: (0, i))
    ins = [x2d, ln_mix_g, w_in_bf, conv_w, sgu_ln_g, sgu_ln_b, sgu_w, sgu_b_full,
           gn_conv, gn_sgu, w_out_bf, ln_ffn_g, wr_split, br_pad]
    in_specs = [row_blk(D_MODEL)] + [full(a) for a in ins[1:]]
    return pl.pallas_call(
        _mixer_router_kernel,
        grid=(N_TOK // TM,),
        in_specs=in_specs,
        out_specs=[row_blk(D_MODEL), row_blk(D_MODEL), col_blk, col_blk,
                   pl.BlockSpec((N_EXPERTS, LANES), lambda i: (0, 0))],
        out_shape=[jax.ShapeDtypeStruct((N_TOK, D_MODEL), _F32),
                   jax.ShapeDtypeStruct((N_TOK, D_MODEL), _F32),
                   jax.ShapeDtypeStruct((2 * TOP_K, N_TOK), jnp.int32),
                   jax.ShapeDtypeStruct((2 * TOP_K, N_TOK), _F32),
                   jax.ShapeDtypeStruct((N_EXPERTS, LANES), _F32)],
        scratch_shapes=[pltpu.VMEM((TM + SUBLANES, D_CONV), _F32),
                        pltpu.VMEM((N_EXPERTS, LANES), _F32),
                        pltpu.VMEM((TM, D_MODEL), _BF16)],
        compiler_params=pltpu.CompilerParams(dimension_semantics=("arbitrary",),
                                             vmem_limit_bytes=VMEM_LIMIT),
        name="mixer_router",
    )(*ins)


def _dispatch_kernel(dest_ref, x_hbm, bufx_hbm, x_in, in_sem, out_sem):
    i = pl.program_id(0)
    n_steps = pl.num_programs(0)

    def fetch(blk, slot):
        r = pl.multiple_of(blk * TD, TD)
        return pltpu.make_async_copy(x_hbm.at[pl.ds(r, TD)], x_in.at[slot], in_sem.at[slot])

    def row_copy(blk, slot, t8, u, k):
        t = pl.multiple_of(t8 * SUBLANES, SUBLANES) + u
        dst = dest_ref[(blk * TD + t) * TOP_K + k]
        return pltpu.make_async_copy(x_in.at[slot, pl.ds(t, 1)], bufx_hbm.at[pl.ds(dst, 1)],
                                     out_sem.at[blk % 2])

    def scatter(blk, slot, start):
        def body(t8, carry):
            for u in range(SUBLANES):
                for k in range(TOP_K):
                    copy = row_copy(blk, slot, t8, u, k)
                    copy.start() if start else copy.wait()
            return carry
        lax.fori_loop(0, TD // SUBLANES, body, 0)

    def in_slot_of(blk, fn):
        for q in range(IN_SLOTS):
            @pl.when(blk % IN_SLOTS == q)
            def _(q=q):
                fn(q)

    @pl.when(i == 0)
    def _():
        fetch(0, 0).start()
        fetch(1, 1).start()

    in_slot_of(i, lambda q: fetch(i, q).wait())
    in_slot_of(i, lambda q: scatter(i, q, True))

    @pl.when(i >= 1)
    def _():
        in_slot_of(i - 1, lambda q: scatter(i - 1, q, False))

    @pl.when(i + 2 < n_steps)
    def _():
        in_slot_of(i + 2, lambda q: fetch(i + 2, q).start())

    @pl.when(i == n_steps - 1)
    def _():
        in_slot_of(i, lambda q: scatter(i, q, False))


def _dispatch(dest, xn2):
    return pl.pallas_call(
        _dispatch_kernel,
        grid_spec=pltpu.PrefetchScalarGridSpec(
            num_scalar_prefetch=1,
            grid=(N_TOK // TD,),
            in_specs=[pl.BlockSpec(memory_space=pl.ANY)],
            out_specs=pl.BlockSpec(memory_space=pl.ANY),
            scratch_shapes=[pltpu.VMEM((IN_SLOTS, TD, D_MODEL), _F32),
                            pltpu.SemaphoreType.DMA((IN_SLOTS,)),
                            pltpu.SemaphoreType.DMA((2,))],
        ),
        out_shape=jax.ShapeDtypeStruct((MAX_SUPER * SUPER_ROWS, D_MODEL), _F32),
        compiler_params=pltpu.CompilerParams(dimension_semantics=("arbitrary",),
                                             vmem_limit_bytes=VMEM_LIMIT),
        name="dispatch",
    )(dest, xn2)


def _expert_kernel(sup_e_ref, sup_blk_ref, nchunk_ref, mmrows_ref, valid_ref,
                   x_hbm, wg_ref, wu_ref, bgu_ref, wd_ref, bd_ref, out_hbm,
                   x_stage, x_bf, acc, x_sem, o_sem):
    s = pl.program_id(0)
    j = pl.program_id(1)
    n_super = pl.num_programs(0)
    n_chunks = nchunk_ref[s]
    cur = s % 2

    def x_copy(sup, c):
        r = pl.multiple_of(sup_blk_ref[sup] * SUPER_ROWS + c * ROW_CHUNK, ROW_CHUNK)
        slot = c % X_SLOTS
        return pltpu.make_async_copy(x_hbm.at[pl.ds(r, ROW_CHUNK)], x_stage.at[slot],
                                     x_sem.at[slot])

    def keep_tokens(sup, c, chunk):
        row = c * ROW_CHUNK + lax.broadcasted_iota(jnp.int32, (ROW_CHUNK, 1), 0)
        return jnp.where(row < valid_ref[sup], chunk, 0.0).astype(_BF16)

    def x_finish(sup, c):
        x_copy(sup, c).wait()
        rows = pl.ds(pl.multiple_of(c * ROW_CHUNK, ROW_CHUNK), ROW_CHUNK)
        x_bf[sup % 2, rows, :] = keep_tokens(sup, c, x_stage[c % X_SLOTS])

    def first_copy(c):
        rows = pl.ds(c * ROW_CHUNK, ROW_CHUNK)
        r = pl.multiple_of(sup_blk_ref[0] * SUPER_ROWS + c * ROW_CHUNK, ROW_CHUNK)
        return pltpu.make_async_copy(x_hbm.at[pl.ds(r, ROW_CHUNK)], acc.at[1, rows], x_sem.at[0])

    def out_copy(sup, c):
        r = pl.multiple_of(sup_blk_ref[sup] * SUPER_ROWS + c * ROW_CHUNK, ROW_CHUNK)
        rows = pl.ds(pl.multiple_of(c * ROW_CHUNK, ROW_CHUNK), ROW_CHUNK)
        return pltpu.make_async_copy(acc.at[sup % 2, rows], out_hbm.at[pl.ds(r, ROW_CHUNK)],
                                     o_sem.at[sup % 2])

    def for_each_chunk(sup, fn):
        for c in range(CHUNKS_PER_SUPER):
            @pl.when(c < nchunk_ref[sup])
            def _(c=c):
                fn(sup, c)

    @pl.when((s == 0) & (j == 0))
    def _():
        for_each_chunk(0, lambda sup, c: first_copy(c).start())

        def finish(sup, c):
            first_copy(c).wait()
            rows = pl.ds(c * ROW_CHUNK, ROW_CHUNK)
            x_bf[0, rows, :] = keep_tokens(0, c, acc[1, rows, :])
        for_each_chunk(0, finish)

    @pl.when((s >= 2) & (j == 0))
    def _():
        for_each_chunk(jnp.maximum(s - 2, 0), lambda sup, c: out_copy(sup, c).wait())

    prev = jnp.maximum(s - 1, 0)

    @pl.when((s >= 1) & (j < nchunk_ref[prev]))
    def _():
        out_copy(prev, j).start()

    nxt = jnp.minimum(s + 1, n_super - 1)
    n_next = jnp.where(s + 1 < n_super, nchunk_ref[nxt], 0)

    @pl.when((j >= 1) & (j - 1 < n_next))
    def _():
        x_finish(nxt, j - 1)

    @pl.when(j < n_next)
    def _():
        x_copy(nxt, j).start()

    @pl.when(n_chunks > 0)
    def _():
        def partial_out(n_rows):
            rows = pl.ds(0, n_rows)
            xb = x_bf[cur, rows, :]
            gate = _dot(xb, wg_ref[0].astype(_BF16)) + bgu_ref[0, pl.ds(j, 1), :]
            up = _dot(xb, wu_ref[0].astype(_BF16)) + bgu_ref[0, pl.ds(N_FF_TILES + j, 1), :]
            gate = jnp.minimum(gate, SWIGLU_LIMIT)
            up = jnp.clip(up, -SWIGLU_LIMIT, SWIGLU_LIMIT)
            glu = gate * jax.nn.sigmoid(gate * SWIGLU_ALPHA)
            act = ((up + 1.0) * glu).astype(_BF16)
            return rows, _dot(act, wd_ref[0].astype(_BF16))

        def sweep(update):
            mm_rows = mmrows_ref[s]
            for n_rows in MATMUL_SIZES:
                @pl.when(mm_rows == n_rows)
                def _(n_rows=n_rows):
                    update(n_rows)

        def accumulate(n_rows):
            rows, o = partial_out(n_rows)
            acc[cur, rows, :] = jnp.where(j == 0, bd_ref[0], acc[cur, rows, :]) + o

        sweep(accumulate)

    @pl.when((s == n_super - 1) & (j == N_FF_TILES - 1))
    def _():
        for_each_chunk(s, lambda sup, c: out_copy(sup, c).start())

        @pl.when(s >= 1)
        def _():
            for_each_chunk(jnp.maximum(s - 1, 0), lambda sup, c: out_copy(sup, c).wait())

        for_each_chunk(s, lambda sup, c: out_copy(sup, c).wait())


def _experts(n_super, sup_e, sup_blk, nchunk, mm_rows, valid, buf_x, w_gate_up, b_gate_up3, w_down, b_down3):
    x_spec = pl.BlockSpec(memory_space=pl.ANY)
    wg_spec = pl.BlockSpec((1, D_MODEL, FF_TILE), lambda s, j, se, sb, nc, nh, va: (se[s], 0, j))
    wu_spec = pl.BlockSpec((1, D_MODEL, FF_TILE),
                           lambda s, j, se, sb, nc, nh, va: (se[s], 0, N_FF_TILES + j))
    bgu_spec = pl.BlockSpec((1, 2 * N_FF_TILES, FF_TILE), lambda s, j, se, sb, nc, nh, va: (se[s], 0, 0))
    wd_spec = pl.BlockSpec((1, FF_TILE, D_MODEL), lambda s, j, se, sb, nc, nh, va: (se[s], j, 0))
    bd_spec = pl.BlockSpec((1, 1, D_MODEL), lambda s, j, se, sb, nc, nh, va: (se[s], 0, 0))
    out_spec = pl.BlockSpec(memory_space=pl.ANY)
    return pl.pallas_call(
        _expert_kernel,
        grid_spec=pltpu.PrefetchScalarGridSpec(
            num_scalar_prefetch=5,
            grid=(n_super, N_FF_TILES),
            in_specs=[x_spec, wg_spec, wu_spec, bgu_spec, wd_spec, bd_spec],
            out_specs=out_spec,
            scratch_shapes=[pltpu.VMEM((X_SLOTS, ROW_CHUNK, D_MODEL), _F32),
                            pltpu.VMEM((2, SUPER_ROWS, D_MODEL), _BF16),
                            pltpu.VMEM((2, SUPER_ROWS, D_MODEL), _F32),
                            pltpu.SemaphoreType.DMA((X_SLOTS,)),
                            pltpu.SemaphoreType.DMA((2,))],
        ),
        out_shape=jax.ShapeDtypeStruct((MAX_SUPER * SUPER_ROWS, D_MODEL), _F32),
        compiler_params=pltpu.CompilerParams(dimension_semantics=("arbitrary", "arbitrary"),
                                             vmem_limit_bytes=VMEM_LIMIT),
        name="experts",
    )(sup_e, sup_blk, nchunk, mm_rows, valid, buf_x, w_gate_up, w_gate_up, b_gate_up3, w_down, b_down3)


def _combine_kernel(dest_ref, h1_ref, gates_ref, lng_ref, y_hbm, out_ref, rows, sem):
    i = pl.program_id(0)
    n_steps = pl.num_programs(0)

    def row_copy(tile, slot, t8, u, k):
        t = pl.multiple_of(t8 * SUBLANES, SUBLANES) + u
        src = dest_ref[(tile * TC + t) * TOP_K + k]
        return pltpu.make_async_copy(y_hbm.at[pl.ds(src, 1)], rows.at[slot, k, pl.ds(t, 1)],
                                     sem.at[slot])

    def issue_tile(tile, slot):
        def body(t8, carry):
            for u in range(SUBLANES):
                for k in range(TOP_K):
                    row_copy(tile, slot, t8, u, k).start()
            return carry
        lax.fori_loop(0, TC // SUBLANES, body, 0)

    def wait_tile(tile, slot):
        def body(t8, carry):
            for u in range(SUBLANES):
                for k in range(TOP_K):
                    row_copy(tile, slot, t8, u, k).wait()
            return carry
        lax.fori_loop(0, TC // SUBLANES, body, 0)

    @pl.when(i == 0)
    def _():
        issue_tile(0, 0)

    for slot in range(2):
        @pl.when(i % 2 == slot)
        def _(slot=slot):
            @pl.when(i + 1 < n_steps)
            def _():
                issue_tile(i + 1, 1 - slot)

            wait_tile(i, slot)
            g = gates_ref[...]
            acc = h1_ref[...]
            for k in range(TOP_K):
                acc = acc + rows[slot, k] * g[:, k:k + 1]
            out_ref[...] = _rms_rows(acc, lng_ref[...])


def _combine(dest, h1, gates, ln_final_g, y_sorted):
    return pl.pallas_call(
        _combine_kernel,
        grid_spec=pltpu.PrefetchScalarGridSpec(
            num_scalar_prefetch=1,
            grid=(N_TOK // TC,),
            in_specs=[pl.BlockSpec((TC, D_MODEL), lambda i, *_: (i, 0)),
                      pl.BlockSpec((TC, TOP_K), lambda i, *_: (i, 0)),
                      pl.BlockSpec((1, D_MODEL), lambda i, *_: (0, 0)),
                      pl.BlockSpec(memory_space=pl.ANY)],
            out_specs=pl.BlockSpec((TC, D_MODEL), lambda i, *_: (i, 0)),
            scratch_shapes=[pltpu.VMEM((2, TOP_K, TC, D_MODEL), _F32),
                            pltpu.SemaphoreType.DMA((2,))],
        ),
        out_shape=jax.ShapeDtypeStruct((N_TOK, D_MODEL), _F32),
        compiler_params=pltpu.CompilerParams(dimension_semantics=("arbitrary",),
                                             vmem_limit_bytes=VMEM_LIMIT),
        name="combine",
    )(dest, h1, gates, ln_final_g, y_sorted)


def kernel(x, ln_mix_g, w_in, conv_w, sgu_ln_g, sgu_ln_b, sgu_w, sgu_b, gn_conv, gn_sgu, w_out,
           ln_ffn_g, w_router, b_router, w_gate_up, b_gate_up, w_down, b_down, ln_final_g):
    row = lambda v: v.reshape(1, -1)
    x2d = x.reshape(N_TOK, D_MODEL)

    sgu_b_full = jnp.repeat(jnp.transpose(sgu_b), SGU_HEAD_DIM, axis=1)
    wr_hi = w_router.astype(_BF16)
    wr_lo = (w_router - wr_hi.astype(_F32)).astype(_BF16)
    pad = ((0, 0), (0, LANES - N_EXPERTS))
    wr_split = jnp.concatenate([jnp.pad(wr_hi, pad), jnp.pad(wr_lo, pad)], axis=1)
    br_pad = jnp.pad(b_router, (0, LANES - N_EXPERTS)).reshape(1, LANES)

    h1, xn2, route, gates, counts = _mixer_router(
        x2d, row(ln_mix_g), w_in.astype(_BF16), conv_w, row(sgu_ln_g), row(sgu_ln_b), sgu_w,
        sgu_b_full, row(gn_conv), row(gn_sgu), w_out.astype(_BF16), row(ln_ffn_g), wr_split, br_pad)

    sizes = counts[:, 0].astype(jnp.int32)
    n_sup = (sizes + SUPER_ROWS - 1) // SUPER_ROWS
    sup_end = jnp.cumsum(n_sup)
    sup_start = sup_end - n_sup
    total_sup = sup_end[-1]
    row_start = (sup_start * SUPER_ROWS).astype(jnp.int32)
    sid = jnp.arange(MAX_SUPER, dtype=jnp.int32)
    sid_eff = jnp.minimum(sid, total_sup - 1)
    sup_e = jnp.minimum(jnp.sum((sid_eff[:, None] >= sup_end[None, :]).astype(jnp.int32), axis=1),
                        N_EXPERTS - 1)
    of_expert = sup_e[:, None] == jnp.arange(N_EXPERTS, dtype=jnp.int32)[None, :]
    rows_left = jnp.sum(jnp.where(of_expert, sizes - (sid_eff[:, None] - sup_start) * SUPER_ROWS, 0),
                        axis=1)
    valid = jnp.clip(rows_left, 0, SUPER_ROWS)
    nchunk = jnp.where(sid < total_sup, (valid + ROW_CHUNK - 1) // ROW_CHUNK, 0).astype(jnp.int32)
    coarse = (valid + COARSE_STEP - 1) // COARSE_STEP * COARSE_STEP
    fine = (valid + FINE_STEP - 1) // FINE_STEP * FINE_STEP
    mm_rows = jnp.where(sid < total_sup, jnp.where(fine > FINE_ABOVE, fine, coarse), 0).astype(jnp.int32)
    eid = route[0:TOP_K]
    eid_start = jnp.sum(jnp.where(eid[None] == jnp.arange(N_EXPERTS, dtype=jnp.int32)[:, None, None],
                                  row_start[:, None, None], 0), axis=0)
    dest = jnp.transpose(eid_start + route[TOP_K:2 * TOP_K]).reshape(-1)
    gates = jnp.transpose(gates[0:TOP_K])

    buf_x = _dispatch(dest, xn2)
    y_sorted = _experts(total_sup, sup_e, sid_eff.astype(jnp.int32), nchunk, mm_rows,
                        valid.astype(jnp.int32), buf_x, w_gate_up,
                        b_gate_up.reshape(N_EXPERTS, 2 * N_FF_TILES, FF_TILE), w_down,
                        b_down.reshape(N_EXPERTS, 1, D_MODEL))
    out = _combine(dest, h1, gates, row(ln_final_g), y_sorted)
    return out.reshape(BATCH, SEQ, D_MODEL)
```

```python
import jax
import jax.numpy as jnp
from jax import lax
from jax.experimental import pallas as pl
from jax.experimental.pallas import tpu as pltpu

D_MODEL = 2048
BATCH = 2
SEQ = 4096
N_TOK = BATCH * SEQ

D_CONV = 1024
N_CONV_GROUPS = 8
CONV_WIDTH = 3
D_SGU = 1024
N_SGU_HEADS = 8
SGU_HEAD_DIM = 128
CHUNK = 128
D_IN_PROJ = 3 * D_CONV + 2 * D_SGU

N_EXPERTS = 32
TOP_K = 4
D_FF = 2048
SWIGLU_LIMIT = 7.0
SWIGLU_ALPHA = 1.702
RMS_EPS = 1e-5
LN_EPS = 1e-5

LANES = 128
SUBLANES = 8
VMEM_LIMIT = 56 * 1024 * 1024

TM = 256
N_TILES = N_TOK // TM
SUPER_ROWS = 1280
ROW_CHUNK = 256
FF_TILE = 256
N_FF_TILES = D_FF // FF_TILE
MAX_SUPER = (N_TOK * TOP_K + N_EXPERTS * (SUPER_ROWS - 1)) // SUPER_ROWS
CHUNKS_PER_SUPER = SUPER_ROWS // ROW_CHUNK
assert CHUNKS_PER_SUPER + 1 <= N_FF_TILES
X_SLOTS = 2
COARSE_STEP, FINE_STEP, FINE_ABOVE = 128, 64, 896
MATMUL_SIZES = (tuple(range(COARSE_STEP, FINE_ABOVE + 1, COARSE_STEP))
                + tuple(range(FINE_ABOVE + FINE_STEP, SUPER_ROWS + 1, FINE_STEP)))
TD = 256
IN_SLOTS = 3
assert N_TOK // TD >= 2
TC = 256

_F32 = jnp.float32
_BF16 = jnp.bfloat16


def _dot(a, b):
    return jnp.dot(a, b, preferred_element_type=_F32)


def _gelu_exact(x):
    return 0.5 * x * (1.0 + lax.erf(x * (2.0 ** -0.5)))


def _rms_rows(x, gain):
    return x * lax.rsqrt(jnp.mean(x * x, axis=-1, keepdims=True) + RMS_EPS) * gain


def _route_tile(logits, active, route_ref, gates_ref, counts_ref, carry):
    cur = jnp.transpose(logits)[0:N_EXPERTS, :]
    expert = lax.broadcasted_iota(jnp.int32, (N_EXPERTS, TM), 0).astype(_F32)
    neg_inf = jnp.float32(-jnp.inf)

    vals, ids, onehots = [], [], []
    for _ in range(TOP_K):
        m = jnp.max(cur, axis=0, keepdims=True)
        idx = jnp.min(jnp.where(cur == m, expert, float(N_EXPERTS)), axis=0, keepdims=True)
        oh = expert == idx
        vals.append(m)
        ids.append(idx)
        onehots.append(oh)
        cur = jnp.where(oh, neg_inf, cur)
    exps = [jnp.exp(v - vals[0]) for v in vals]
    denom = exps[0] + exps[1] + exps[2] + exps[3]
    gates = [e / denom for e in exps]

    mask = jnp.where(active, (onehots[0] | onehots[1] | onehots[2] | onehots[3]).astype(_F32), 0.0)
    row_t = lax.broadcasted_iota(jnp.int32, (TM, TM), 0)
    col_t = lax.broadcasted_iota(jnp.int32, (TM, TM), 1)
    earlier = (row_t < col_t).astype(_BF16)
    before = _dot(mask.astype(_BF16), earlier) + carry[:, 0:1]
    ranks = [jnp.sum(jnp.where(oh, before, 0.0), axis=0, keepdims=True) for oh in onehots]
    carry[...] = carry[...] + jnp.sum(mask, axis=1, keepdims=True)
    counts_ref[...] = carry[...]

    out_row = lax.broadcasted_iota(jnp.int32, (2 * TOP_K, TM), 0)
    route = jnp.zeros((2 * TOP_K, TM), _F32)
    gate_out = jnp.zeros((2 * TOP_K, TM), _F32)
    for k in range(TOP_K):
        route = jnp.where(out_row == k, ids[k], route)
        route = jnp.where(out_row == TOP_K + k, ranks[k], route)
        gate_out = jnp.where(out_row == k, gates[k], gate_out)
    route_ref[...] = route.astype(jnp.int32)
    gates_ref[...] = gate_out


def _mixer_router_kernel(x_ref, lng_ref, win_ref, convw_ref, slg_ref, slb_ref, sw_ref, sb_ref,
                         gnc_ref, gns_ref, wout_ref, lnf_ref, wr_ref, br_ref,
                         h1_ref, xn2_ref, route_ref, gates_ref, counts_ref,
                         cbuf, carry, ybuf, logit_buf):
    i = pl.program_id(0)

    @pl.when(i == 0)
    def _():
        carry[...] = jnp.zeros_like(carry)
        logit_buf[...] = jnp.zeros_like(logit_buf)

    @pl.when(i % (SEQ // TM) == 0)
    def _():
        cbuf[0:SUBLANES, :] = jnp.zeros((SUBLANES, D_CONV), _F32)

    @pl.when(i == N_TILES)
    def _():
        _route_tile(logit_buf[...], True, route_ref, gates_ref, counts_ref, carry)

    @pl.when(i < N_TILES)
    def _():
        prev_logits = logit_buf[...]
        _mixer_tile(x_ref, lng_ref, win_ref, convw_ref, slg_ref, slb_ref, sw_ref, sb_ref,
                    gnc_ref, gns_ref, wout_ref, lnf_ref, wr_ref, br_ref,
                    h1_ref, xn2_ref, cbuf, ybuf, logit_buf)
        _route_tile(prev_logits, i > 0, route_ref, gates_ref, counts_ref, carry)


def _mixer_tile(x_ref, lng_ref, win_ref, convw_ref, slg_ref, slb_ref, sw_ref, sb_ref,
                gnc_ref, gns_ref, wout_ref, lnf_ref, wr_ref, br_ref,
                h1_ref, xn2_ref, cbuf, ybuf, logit_buf):
    x = x_ref[...]
    xn = _rms_rows(x, lng_ref[...]).astype(_BF16)

    b_gate = _dot(xn, win_ref[:, 0:D_CONV])
    c_gate = _dot(xn, win_ref[:, D_CONV:2 * D_CONV])
    hh = _dot(xn, win_ref[:, 2 * D_CONV:3 * D_CONV])
    ch = c_gate * hh
    cbuf[SUBLANES:SUBLANES + TM, :] = ch
    ch1 = cbuf[SUBLANES - 1:SUBLANES - 1 + TM, :]
    ch2 = cbuf[SUBLANES - 2:SUBLANES - 2 + TM, :]
    conv = convw_ref[0:1, :] * ch2 + convw_ref[1:2, :] * ch1 + convw_ref[2:3, :] * ch
    cbuf[0:SUBLANES, :] = cbuf[TM:TM + SUBLANES, :]
    y_conv = b_gate * conv
    for g in range(N_CONV_GROUPS):
        sl = slice(g * LANES, (g + 1) * LANES)
        blk = y_conv[:, sl]
        ms = jnp.mean(blk * blk, axis=-1, keepdims=True)
        ybuf[:, sl] = (blk * lax.rsqrt(ms + RMS_EPS) * gnc_ref[:, sl]).astype(_BF16)

    gu = _gelu_exact(_dot(xn, win_ref[:, 3 * D_CONV:3 * D_CONV + D_SGU]))
    gv = _gelu_exact(_dot(xn, win_ref[:, 3 * D_CONV + D_SGU:D_IN_PROJ]))
    row_c = lax.broadcasted_iota(jnp.int32, (CHUNK, CHUNK), 0)
    col_c = lax.broadcasted_iota(jnp.int32, (CHUNK, CHUNK), 1)
    causal = row_c >= col_c
    for h in range(N_SGU_HEADS):
        sl = slice(h * SGU_HEAD_DIM, (h + 1) * SGU_HEAD_DIM)
        vh = gv[:, sl]
        mu = jnp.mean(vh, axis=-1, keepdims=True)
        xc = vh - mu
        var = jnp.mean(xc * xc, axis=-1, keepdims=True)
        vn = (xc * lax.rsqrt(var + LN_EPS) * slg_ref[:, sl] + slb_ref[:, sl]).astype(_BF16)
        wm = jnp.where(causal, sw_ref[h], 0.0).astype(_BF16)
        for c in range(TM // CHUNK):
            rows = slice(c * CHUNK, (c + 1) * CHUNK)
            mixed = _dot(wm, vn[rows, :]) + sb_ref[:, sl]
            ys = gu[rows, sl] * mixed
            ms = jnp.mean(ys * ys, axis=-1, keepdims=True)
            ybuf[rows, D_CONV + h * SGU_HEAD_DIM:D_CONV + (h + 1) * SGU_HEAD_DIM] = (
                ys * lax.rsqrt(ms + RMS_EPS) * gns_ref[:, sl]).astype(_BF16)

    h1 = x + _dot(ybuf[...], wout_ref[...])
    h1_ref[...] = h1
    xn2 = _rms_rows(h1, lnf_ref[...])
    xn2_ref[...] = xn2

    x_hi = xn2.astype(_BF16)
    x_lo = (xn2 - x_hi.astype(_F32)).astype(_BF16)
    p = _dot(x_hi, wr_ref[...]) + _dot(x_lo, wr_ref[...])
    logit_buf[...] = p[:, :LANES] + p[:, LANES:] + br_ref[...]


def _mixer_router(x2d, ln_mix_g, w_in_bf, conv_w, sgu_ln_g, sgu_ln_b, sgu_w, sgu_b_full,
                  gn_conv, gn_sgu, w_out_bf, ln_ffn_g, wr_split, br_pad):
    def full(a):
        return pl.BlockSpec(a.shape, lambda i: (0,) * a.ndim)

    row_blk = lambda w: pl.BlockSpec((TM, w), lambda i: (jnp.minimum(i, N_TILES - 1), 0))
    col_blk = pl.BlockSpec((2 * TOP_K, TM), lambda i: (0, jnp.maximum(i - 1, 0)))
    ins = [x2d, ln_mix_g, w_in_bf, conv_w, sgu_ln_g, sgu_ln_b, sgu_w, sgu_b_full,
           gn_conv, gn_sgu, w_out_bf, ln_ffn_g, wr_split, br_pad]
    in_specs = [row_blk(D_MODEL)] + [full(a) for a in ins[1:]]
    return pl.pallas_call(
        _mixer_router_kernel,
        grid=(N_TILES + 1,),
        in_specs=in_specs,
        out_specs=[row_blk(D_MODEL), row_blk(D_MODEL), col_blk, col_blk,
                   pl.BlockSpec((N_EXPERTS, LANES), lambda i: (0, 0))],
        out_shape=[jax.ShapeDtypeStruct((N_TOK, D_MODEL), _F32),
                   jax.ShapeDtypeStruct((N_TOK, D_MODEL), _F32),
                   jax.ShapeDtypeStruct((2 * TOP_K, N_TOK), jnp.int32),
                   jax.ShapeDtypeStruct((2 * TOP_K, N_TOK), _F32),
                   jax.ShapeDtypeStruct((N_EXPERTS, LANES), _F32)],
        scratch_shapes=[pltpu.VMEM((TM + SUBLANES, D_CONV), _F32),
                        pltpu.VMEM((N_EXPERTS, LANES), _F32),
                        pltpu.VMEM((TM, D_MODEL), _BF16),
                        pltpu.VMEM((TM, LANES), _F32)],
        compiler_params=pltpu.CompilerParams(dimension_semantics=("arbitrary",),
                                             vmem_limit_bytes=VMEM_LIMIT),
        name="mixer_router",
    )(*ins)


def _dispatch_kernel(dest_ref, x_hbm, bufx_hbm, x_in, in_sem, out_sem):
    i = pl.program_id(0)
    n_steps = pl.num_programs(0)

    def fetch(blk, slot):
        r = pl.multiple_of(blk * TD, TD)
        return pltpu.make_async_copy(x_hbm.at[pl.ds(r, TD)], x_in.at[slot], in_sem.at[slot])

    def row_copy(blk, slot, t8, u, k):
        t = pl.multiple_of(t8 * SUBLANES, SUBLANES) + u
        dst = dest_ref[(blk * TD + t) * TOP_K + k]
        return pltpu.make_async_copy(x_in.at[slot, pl.ds(t, 1)], bufx_hbm.at[pl.ds(dst, 1)],
                                     out_sem.at[blk % 2])

    def scatter(blk, slot, start):
        def body(t8, carry):
            for u in range(SUBLANES):
                for k in range(TOP_K):
                    copy = row_copy(blk, slot, t8, u, k)
                    copy.start() if start else copy.wait()
            return carry
        lax.fori_loop(0, TD // SUBLANES, body, 0)

    def in_slot_of(blk, fn):
        for q in range(IN_SLOTS):
            @pl.when(blk % IN_SLOTS == q)
            def _(q=q):
                fn(q)

    @pl.when(i == 0)
    def _():
        fetch(0, 0).start()
        fetch(1, 1).start()

    in_slot_of(i, lambda q: fetch(i, q).wait())
    in_slot_of(i, lambda q: scatter(i, q, True))

    @pl.when(i >= 1)
    def _():
        in_slot_of(i - 1, lambda q: scatter(i - 1, q, False))

    @pl.when(i + 2 < n_steps)
    def _():
        in_slot_of(i + 2, lambda q: fetch(i + 2, q).start())

    @pl.when(i == n_steps - 1)
    def _():
        in_slot_of(i, lambda q: scatter(i, q, False))


def _dispatch(dest, xn2):
    return pl.pallas_call(
        _dispatch_kernel,
        grid_spec=pltpu.PrefetchScalarGridSpec(
            num_scalar_prefetch=1,
            grid=(N_TOK // TD,),
            in_specs=[pl.BlockSpec(memory_space=pl.ANY)],
            out_specs=pl.BlockSpec(memory_space=pl.ANY),
            scratch_shapes=[pltpu.VMEM((IN_SLOTS, TD, D_MODEL), _F32),
                            pltpu.SemaphoreType.DMA((IN_SLOTS,)),
                            pltpu.SemaphoreType.DMA((2,))],
        ),
        out_shape=jax.ShapeDtypeStruct((MAX_SUPER * SUPER_ROWS, D_MODEL), _F32),
        compiler_params=pltpu.CompilerParams(dimension_semantics=("arbitrary",),
                                             vmem_limit_bytes=VMEM_LIMIT),
        name="dispatch",
    )(dest, xn2)


def _expert_kernel(sup_e_ref, sup_blk_ref, nchunk_ref, mmrows_ref, valid_ref,
                   x_hbm, wg_ref, wu_ref, bgu_ref, wd_ref, bd_ref, out_hbm,
                   x_stage, x_bf, acc, x_sem, o_sem):
    s = pl.program_id(0)
    j = pl.program_id(1)
    n_super = pl.num_programs(0)
    n_chunks = nchunk_ref[s]
    cur = s % 2

    def x_copy(sup, c):
        r = pl.multiple_of(sup_blk_ref[sup] * SUPER_ROWS + c * ROW_CHUNK, ROW_CHUNK)
        slot = c % X_SLOTS
        return pltpu.make_async_copy(x_hbm.at[pl.ds(r, ROW_CHUNK)], x_stage.at[slot],
                                     x_sem.at[slot])

    def keep_tokens(sup, c, chunk):
        row = c * ROW_CHUNK + lax.broadcasted_iota(jnp.int32, (ROW_CHUNK, 1), 0)
        return jnp.where(row < valid_ref[sup], chunk, 0.0).astype(_BF16)

    def x_finish(sup, c):
        x_copy(sup, c).wait()
        rows = pl.ds(pl.multiple_of(c * ROW_CHUNK, ROW_CHUNK), ROW_CHUNK)
        x_bf[sup % 2, rows, :] = keep_tokens(sup, c, x_stage[c % X_SLOTS])

    def first_copy(c):
        rows = pl.ds(c * ROW_CHUNK, ROW_CHUNK)
        r = pl.multiple_of(sup_blk_ref[0] * SUPER_ROWS + c * ROW_CHUNK, ROW_CHUNK)
        return pltpu.make_async_copy(x_hbm.at[pl.ds(r, ROW_CHUNK)], acc.at[1, rows], x_sem.at[0])

    def out_copy(sup, c):
        r = pl.multiple_of(sup_blk_ref[sup] * SUPER_ROWS + c * ROW_CHUNK, ROW_CHUNK)
        rows = pl.ds(pl.multiple_of(c * ROW_CHUNK, ROW_CHUNK), ROW_CHUNK)
        return pltpu.make_async_copy(acc.at[sup % 2, rows], out_hbm.at[pl.ds(r, ROW_CHUNK)],
                                     o_sem.at[sup % 2])

    def for_each_chunk(sup, fn):
        for c in range(CHUNKS_PER_SUPER):
            @pl.when(c < nchunk_ref[sup])
            def _(c=c):
                fn(sup, c)

    @pl.when((s == 0) & (j == 0))
    def _():
        for_each_chunk(0, lambda sup, c: first_copy(c).start())

        def finish(sup, c):
            first_copy(c).wait()
            rows = pl.ds(c * ROW_CHUNK, ROW_CHUNK)
            x_bf[0, rows, :] = keep_tokens(0, c, acc[1, rows, :])
        for_each_chunk(0, finish)

    @pl.when((s >= 2) & (j == 0))
    def _():
        for_each_chunk(jnp.maximum(s - 2, 0), lambda sup, c: out_copy(sup, c).wait())

    prev = jnp.maximum(s - 1, 0)

    @pl.when((s >= 1) & (j < nchunk_ref[prev]))
    def _():
        out_copy(prev, j).start()

    nxt = jnp.minimum(s + 1, n_super - 1)
    n_next = jnp.where(s + 1 < n_super, nchunk_ref[nxt], 0)

    @pl.when((j >= 1) & (j - 1 < n_next))
    def _():
        x_finish(nxt, j - 1)

    @pl.when(j < n_next)
    def _():
        x_copy(nxt, j).start()

    @pl.when(n_chunks > 0)
    def _():
        def partial_out(n_rows):
            rows = pl.ds(0, n_rows)
            xb = x_bf[cur, rows, :]
            gate = _dot(xb, wg_ref[0].astype(_BF16)) + bgu_ref[0, pl.ds(j, 1), :]
            up = _dot(xb, wu_ref[0].astype(_BF16)) + bgu_ref[0, pl.ds(N_FF_TILES + j, 1), :]
            gate = jnp.minimum(gate, SWIGLU_LIMIT)
            up = jnp.clip(up, -SWIGLU_LIMIT, SWIGLU_LIMIT)
            glu = gate * jax.nn.sigmoid(gate * SWIGLU_ALPHA)
            act = ((up + 1.0) * glu).astype(_BF16)
            return rows, _dot(act, wd_ref[0].astype(_BF16))

        def sweep(update):
            mm_rows = mmrows_ref[s]
            for n_rows in MATMUL_SIZES:
                @pl.when(mm_rows == n_rows)
                def _(n_rows=n_rows):
                    update(n_rows)

        def accumulate(n_rows):
            rows, o = partial_out(n_rows)
            acc[cur, rows, :] = jnp.where(j == 0, bd_ref[0], acc[cur, rows, :]) + o

        sweep(accumulate)

    @pl.when((s == n_super - 1) & (j == N_FF_TILES - 1))
    def _():
        for_each_chunk(s, lambda sup, c: out_copy(sup, c).start())

        @pl.when(s >= 1)
        def _():
            for_each_chunk(jnp.maximum(s - 1, 0), lambda sup, c: out_copy(sup, c).wait())

        for_each_chunk(s, lambda sup, c: out_copy(sup, c).wait())


def _experts(n_super, sup_e, sup_blk, nchunk, mm_rows, valid, buf_x, w_gate_up, b_gate_up3, w_down, b_down3):
    x_spec = pl.BlockSpec(memory_space=pl.ANY)
    wg_spec = pl.BlockSpec((1, D_MODEL, FF_TILE), lambda s, j, se, sb, nc, nh, va: (se[s], 0, j))
    wu_spec = pl.BlockSpec((1, D_MODEL, FF_TILE),
                           lambda s, j, se, sb, nc, nh, va: (se[s], 0, N_FF_TILES + j))
    bgu_spec = pl.BlockSpec((1, 2 * N_FF_TILES, FF_TILE), lambda s, j, se, sb, nc, nh, va: (se[s], 0, 0))
    wd_spec = pl.BlockSpec((1, FF_TILE, D_MODEL), lambda s, j, se, sb, nc, nh, va: (se[s], j, 0))
    bd_spec = pl.BlockSpec((1, 1, D_MODEL), lambda s, j, se, sb, nc, nh, va: (se[s], 0, 0))
    out_spec = pl.BlockSpec(memory_space=pl.ANY)
    return pl.pallas_call(
        _expert_kernel,
        grid_spec=pltpu.PrefetchScalarGridSpec(
            num_scalar_prefetch=5,
            grid=(n_super, N_FF_TILES),
            in_specs=[x_spec, wg_spec, wu_spec, bgu_spec, wd_spec, bd_spec],
            out_specs=out_spec,
            scratch_shapes=[pltpu.VMEM((X_SLOTS, ROW_CHUNK, D_MODEL), _F32),
                            pltpu.VMEM((2, SUPER_ROWS, D_MODEL), _BF16),
                            pltpu.VMEM((2, SUPER_ROWS, D_MODEL), _F32),
                            pltpu.SemaphoreType.DMA((X_SLOTS,)),
                            pltpu.SemaphoreType.DMA((2,))],
        ),
        out_shape=jax.ShapeDtypeStruct((MAX_SUPER * SUPER_ROWS, D_MODEL), _F32),
        compiler_params=pltpu.CompilerParams(dimension_semantics=("arbitrary", "arbitrary"),
                                             vmem_limit_bytes=VMEM_LIMIT),
        name="experts",
    )(sup_e, sup_blk, nchunk, mm_rows, valid, buf_x, w_gate_up, w_gate_up, b_gate_up3, w_down, b_down3)


def _combine_kernel(dest_ref, h1_ref, gates_ref, lng_ref, y_hbm, out_ref, rows, sem):
    i = pl.program_id(0)
    n_steps = pl.num_programs(0)

    def row_copy(tile, slot, t8, u, k):
        t = pl.multiple_of(t8 * SUBLANES, SUBLANES) + u
        src = dest_ref[(tile * TC + t) * TOP_K + k]
        return pltpu.make_async_copy(y_hbm.at[pl.ds(src, 1)], rows.at[slot, k, pl.ds(t, 1)],
                                     sem.at[slot])

    def issue_tile(tile, slot):
        def body(t8, carry):
            for u in range(SUBLANES):
                for k in range(TOP_K):
                    row_copy(tile, slot, t8, u, k).start()
            return carry
        lax.fori_loop(0, TC // SUBLANES, body, 0)

    def wait_tile(tile, slot):
        def body(t8, carry):
            for u in range(SUBLANES):
                for k in range(TOP_K):
                    row_copy(tile, slot, t8, u, k).wait()
            return carry
        lax.fori_loop(0, TC // SUBLANES, body, 0)

    @pl.when(i == 0)
    def _():
        issue_tile(0, 0)

    for slot in range(2):
        @pl.when(i % 2 == slot)
        def _(slot=slot):
            @pl.when(i + 1 < n_steps)
            def _():
                issue_tile(i + 1, 1 - slot)

            wait_tile(i, slot)
            g = gates_ref[...]
            acc = h1_ref[...]
            for k in range(TOP_K):
                acc = acc + rows[slot, k] * g[:, k:k + 1]
            out_ref[...] = _rms_rows(acc, lng_ref[...])


def _combine(dest, h1, gates, ln_final_g, y_sorted):
    return pl.pallas_call(
        _combine_kernel,
        grid_spec=pltpu.PrefetchScalarGridSpec(
            num_scalar_prefetch=1,
            grid=(N_TOK // TC,),
            in_specs=[pl.BlockSpec((TC, D_MODEL), lambda i, *_: (i, 0)),
                      pl.BlockSpec((TC, TOP_K), lambda i, *_: (i, 0)),
                      pl.BlockSpec((1, D_MODEL), lambda i, *_: (0, 0)),
                      pl.BlockSpec(memory_space=pl.ANY)],
            out_specs=pl.BlockSpec((TC, D_MODEL), lambda i, *_: (i, 0)),
            scratch_shapes=[pltpu.VMEM((2, TOP_K, TC, D_MODEL), _F32),
                            pltpu.SemaphoreType.DMA((2,))],
        ),
        out_shape=jax.ShapeDtypeStruct((N_TOK, D_MODEL), _F32),
        compiler_params=pltpu.CompilerParams(dimension_semantics=("arbitrary",),
                                             vmem_limit_bytes=VMEM_LIMIT),
        name="combine",
    )(dest, h1, gates, ln_final_g, y_sorted)


def kernel(x, ln_mix_g, w_in, conv_w, sgu_ln_g, sgu_ln_b, sgu_w, sgu_b, gn_conv, gn_sgu, w_out,
           ln_ffn_g, w_router, b_router, w_gate_up, b_gate_up, w_down, b_down, ln_final_g):
    row = lambda v: v.reshape(1, -1)
    x2d = x.reshape(N_TOK, D_MODEL)

    sgu_b_full = jnp.repeat(jnp.transpose(sgu_b), SGU_HEAD_DIM, axis=1)
    wr_hi = w_router.astype(_BF16)
    wr_lo = (w_router - wr_hi.astype(_F32)).astype(_BF16)
    pad = ((0, 0), (0, LANES - N_EXPERTS))
    wr_split = jnp.concatenate([jnp.pad(wr_hi, pad), jnp.pad(wr_lo, pad)], axis=1)
    br_pad = jnp.pad(b_router, (0, LANES - N_EXPERTS)).reshape(1, LANES)

    h1, xn2, route, gates, counts = _mixer_router(
        x2d, row(ln_mix_g), w_in.astype(_BF16), conv_w, row(sgu_ln_g), row(sgu_ln_b), sgu_w,
        sgu_b_full, row(gn_conv), row(gn_sgu), w_out.astype(_BF16), row(ln_ffn_g), wr_split, br_pad)

    sizes = counts[:, 0].astype(jnp.int32)
    n_sup = (sizes + SUPER_ROWS - 1) // SUPER_ROWS
    sup_end = jnp.cumsum(n_sup)
    sup_start = sup_end - n_sup
    total_sup = sup_end[-1]
    row_start = (sup_start * SUPER_ROWS).astype(jnp.int32)
    sid = jnp.arange(MAX_SUPER, dtype=jnp.int32)
    sid_eff = jnp.minimum(sid, total_sup - 1)
    sup_e = jnp.minimum(jnp.sum((sid_eff[:, None] >= sup_end[None, :]).astype(jnp.int32), axis=1),
                        N_EXPERTS - 1)
    of_expert = sup_e[:, None] == jnp.arange(N_EXPERTS, dtype=jnp.int32)[None, :]
    rows_left = jnp.sum(jnp.where(of_expert, sizes - (sid_eff[:, None] - sup_start) * SUPER_ROWS, 0),
                        axis=1)
    valid = jnp.clip(rows_left, 0, SUPER_ROWS)
    nchunk = jnp.where(sid < total_sup, (valid + ROW_CHUNK - 1) // ROW_CHUNK, 0).astype(jnp.int32)
    coarse = (valid + COARSE_STEP - 1) // COARSE_STEP * COARSE_STEP
    fine = (valid + FINE_STEP - 1) // FINE_STEP * FINE_STEP
    mm_rows = jnp.where(sid < total_sup, jnp.where(fine > FINE_ABOVE, fine, coarse), 0).astype(jnp.int32)
    eid = route[0:TOP_K]
    eid_start = jnp.sum(jnp.where(eid[None] == jnp.arange(N_EXPERTS, dtype=jnp.int32)[:, None, None],
                                  row_start[:, None, None], 0), axis=0)
    dest = jnp.transpose(eid_start + route[TOP_K:2 * TOP_K]).reshape(-1)
    gates = jnp.transpose(gates[0:TOP_K])

    buf_x = _dispatch(dest, xn2)
    y_sorted = _experts(total_sup, sup_e, sid_eff.astype(jnp.int32), nchunk, mm_rows,
                        valid.astype(jnp.int32), buf_x, w_gate_up,
                        b_gate_up.reshape(N_EXPERTS, 2 * N_FF_TILES, FF_TILE), w_down,
                        b_down.reshape(N_EXPERTS, 1, D_MODEL))
    out = _combine(dest, h1, gates, row(ln_final_g), y_sorted)
    return out.reshape(BATCH, SEQ, D_MODEL)
```

```python
import jax
import jax.numpy as jnp
from jax import lax
from jax.experimental import pallas as pl
from jax.experimental.pallas import tpu as pltpu

D_MODEL = 2048
BATCH = 2
SEQ = 4096
N_TOK = BATCH * SEQ

D_CONV = 1024
N_CONV_GROUPS = 8
CONV_WIDTH = 3
D_SGU = 1024
N_SGU_HEADS = 8
SGU_HEAD_DIM = 128
CHUNK = 128
D_IN_PROJ = 3 * D_CONV + 2 * D_SGU

N_EXPERTS = 32
TOP_K = 4
D_FF = 2048
SWIGLU_LIMIT = 7.0
SWIGLU_ALPHA = 1.702
RMS_EPS = 1e-5
LN_EPS = 1e-5

LANES = 128
SUBLANES = 8
VMEM_LIMIT = 56 * 1024 * 1024

TM = 256
N_TILES = N_TOK // TM
SUPER_ROWS = 1280
ROW_CHUNK = 256
FF_TILE = 256
N_FF_TILES = D_FF // FF_TILE
MAX_SUPER = (N_TOK * TOP_K + N_EXPERTS * (SUPER_ROWS - 1)) // SUPER_ROWS
CHUNKS_PER_SUPER = SUPER_ROWS // ROW_CHUNK
assert CHUNKS_PER_SUPER + 1 <= N_FF_TILES
X_SLOTS = 2
COARSE_STEP, FINE_STEP, FINE_ABOVE = 128, 64, 896
MATMUL_SIZES = (tuple(range(COARSE_STEP, FINE_ABOVE + 1, COARSE_STEP))
                + tuple(range(FINE_ABOVE + FINE_STEP, SUPER_ROWS + 1, FINE_STEP)))
TD = 256
IN_SLOTS = 3
assert N_TOK // TD >= 2
TC = 256

_F32 = jnp.float32
_BF16 = jnp.bfloat16


def _dot(a, b):
    return jnp.dot(a, b, preferred_element_type=_F32)


def _gelu_exact(x):
    return 0.5 * x * (1.0 + lax.erf(x * (2.0 ** -0.5)))


def _rms_rows(x, gain):
    return x * lax.rsqrt(jnp.mean(x * x, axis=-1, keepdims=True) + RMS_EPS) * gain


def _route_tile(logits, active, route_ref, gates_ref, counts_ref, carry):
    cur = jnp.transpose(logits)[0:N_EXPERTS, :]
    expert = lax.broadcasted_iota(jnp.int32, (N_EXPERTS, TM), 0).astype(_F32)
    neg_inf = jnp.float32(-jnp.inf)

    vals, ids, onehots = [], [], []
    for _ in range(TOP_K):
        m = jnp.max(cur, axis=0, keepdims=True)
        idx = jnp.min(jnp.where(cur == m, expert, float(N_EXPERTS)), axis=0, keepdims=True)
        oh = expert == idx
        vals.append(m)
        ids.append(idx)
        onehots.append(oh)
        cur = jnp.where(oh, neg_inf, cur)
    exps = [jnp.exp(v - vals[0]) for v in vals]
    denom = exps[0] + exps[1] + exps[2] + exps[3]
    gates = [e / denom for e in exps]

    mask = jnp.where(active, (onehots[0] | onehots[1] | onehots[2] | onehots[3]).astype(_F32), 0.0)
    row_t = lax.broadcasted_iota(jnp.int32, (TM, TM), 0)
    col_t = lax.broadcasted_iota(jnp.int32, (TM, TM), 1)
    earlier = (row_t < col_t).astype(_BF16)
    before = _dot(mask.astype(_BF16), earlier) + carry[:, 0:1]
    ranks = [jnp.sum(jnp.where(oh, before, 0.0), axis=0, keepdims=True) for oh in onehots]
    carry[...] = carry[...] + jnp.sum(mask, axis=1, keepdims=True)
    counts_ref[...] = carry[...]

    out_row = lax.broadcasted_iota(jnp.int32, (2 * TOP_K, TM), 0)
    route = jnp.zeros((2 * TOP_K, TM), _F32)
    gate_out = jnp.zeros((2 * TOP_K, TM), _F32)
    for k in range(TOP_K):
        route = jnp.where(out_row == k, ids[k], route)
        route = jnp.where(out_row == TOP_K + k, ranks[k], route)
        gate_out = jnp.where(out_row == k, gates[k], gate_out)
    route_ref[...] = route.astype(jnp.int32)
    gates_ref[...] = gate_out


def _mixer_router_kernel(x_ref, lng_ref, win_ref, convw_ref, slg_ref, slb_ref, sw_ref, sb_ref,
                         gnc_ref, gns_ref, wout_ref, lnf_ref, wr_ref, br_ref,
                         h1_ref, xn2_ref, route_ref, gates_ref, counts_ref,
                         cbuf, carry, ybuf, logit_buf):
    i = pl.program_id(0)

    @pl.when(i == 0)
    def _():
        carry[...] = jnp.zeros_like(carry)
        logit_buf[...] = jnp.zeros_like(logit_buf)

    @pl.when(i % (SEQ // TM) == 0)
    def _():
        cbuf[0:SUBLANES, :] = jnp.zeros((SUBLANES, D_CONV), _F32)

    @pl.when(i == N_TILES)
    def _():
        _route_tile(logit_buf[...], True, route_ref, gates_ref, counts_ref, carry)

    @pl.when(i < N_TILES)
    def _():
        prev_logits = logit_buf[...]
        _mixer_tile(x_ref, lng_ref, win_ref, convw_ref, slg_ref, slb_ref, sw_ref, sb_ref,
                    gnc_ref, gns_ref, wout_ref, lnf_ref, wr_ref, br_ref,
                    h1_ref, xn2_ref, cbuf, ybuf, logit_buf)
        _route_tile(prev_logits, i > 0, route_ref, gates_ref, counts_ref, carry)


def _mixer_tile(x_ref, lng_ref, win_ref, convw_ref, slg_ref, slb_ref, sw_ref, sb_ref,
                gnc_ref, gns_ref, wout_ref, lnf_ref, wr_ref, br_ref,
                h1_ref, xn2_ref, cbuf, ybuf, logit_buf):
    x = x_ref[...]
    xn = _rms_rows(x, lng_ref[...]).astype(_BF16)

    b_gate = _dot(xn, win_ref[:, 0:D_CONV])
    c_gate = _dot(xn, win_ref[:, D_CONV:2 * D_CONV])
    hh = _dot(xn, win_ref[:, 2 * D_CONV:3 * D_CONV])
    ch = c_gate * hh
    cbuf[SUBLANES:SUBLANES + TM, :] = ch
    ch1 = cbuf[SUBLANES - 1:SUBLANES - 1 + TM, :]
    ch2 = cbuf[SUBLANES - 2:SUBLANES - 2 + TM, :]
    conv = convw_ref[0:1, :] * ch2 + convw_ref[1:2, :] * ch1 + convw_ref[2:3, :] * ch
    cbuf[0:SUBLANES, :] = cbuf[TM:TM + SUBLANES, :]
    y_conv = b_gate * conv
    for g in range(N_CONV_GROUPS):
        sl = slice(g * LANES, (g + 1) * LANES)
        blk = y_conv[:, sl]
        ms = jnp.mean(blk * blk, axis=-1, keepdims=True)
        ybuf[:, sl] = (blk * lax.rsqrt(ms + RMS_EPS) * gnc_ref[:, sl]).astype(_BF16)

    gu = _gelu_exact(_dot(xn, win_ref[:, 3 * D_CONV:3 * D_CONV + D_SGU]))
    gv = _gelu_exact(_dot(xn, win_ref[:, 3 * D_CONV + D_SGU:D_IN_PROJ]))
    row_c = lax.broadcasted_iota(jnp.int32, (CHUNK, CHUNK), 0)
    col_c = lax.broadcasted_iota(jnp.int32, (CHUNK, CHUNK), 1)
    causal = row_c >= col_c
    for h in range(N_SGU_HEADS):
        sl = slice(h * SGU_HEAD_DIM, (h + 1) * SGU_HEAD_DIM)
        vh = gv[:, sl]
        mu = jnp.mean(vh, axis=-1, keepdims=True)
        xc = vh - mu
        var = jnp.mean(xc * xc, axis=-1, keepdims=True)
        vn = (xc * lax.rsqrt(var + LN_EPS) * slg_ref[:, sl] + slb_ref[:, sl]).astype(_BF16)
        wm = jnp.where(causal, sw_ref[h], 0.0).astype(_BF16)
        for c in range(TM // CHUNK):
            rows = slice(c * CHUNK, (c + 1) * CHUNK)
            mixed = _dot(wm, vn[rows, :]) + sb_ref[:, sl]
            ys = gu[rows, sl] * mixed
            ms = jnp.mean(ys * ys, axis=-1, keepdims=True)
            ybuf[rows, D_CONV + h * SGU_HEAD_DIM:D_CONV + (h + 1) * SGU_HEAD_DIM] = (
                ys * lax.rsqrt(ms + RMS_EPS) * gns_ref[:, sl]).astype(_BF16)

    h1 = x + _dot(ybuf[...], wout_ref[...])
    h1_ref[...] = h1
    xn2 = _rms_rows(h1, lnf_ref[...])
    xn2_ref[...] = xn2

    x_hi = xn2.astype(_BF16)
    x_lo = (xn2 - x_hi.astype(_F32)).astype(_BF16)
    p = _dot(x_hi, wr_ref[...]) + _dot(x_lo, wr_ref[...])
    logit_buf[...] = p[:, :LANES] + p[:, LANES:] + br_ref[...]


def _mixer_router(x2d, ln_mix_g, w_in_bf, conv_w, sgu_ln_g, sgu_ln_b, sgu_w, sgu_b_full,
                  gn_conv, gn_sgu, w_out_bf, ln_ffn_g, wr_split, br_pad):
    def full(a):
        return pl.BlockSpec(a.shape, lambda i: (0,) * a.ndim)

    row_blk = lambda w: pl.BlockSpec((TM, w), lambda i: (jnp.minimum(i, N_TILES - 1), 0))
    col_blk = pl.BlockSpec((2 * TOP_K, TM), lambda i: (0, jnp.maximum(i - 1, 0)))
    ins = [x2d, ln_mix_g, w_in_bf, conv_w, sgu_ln_g, sgu_ln_b, sgu_w, sgu_b_full,
           gn_conv, gn_sgu, w_out_bf, ln_ffn_g, wr_split, br_pad]
    in_specs = [row_blk(D_MODEL)] + [full(a) for a in ins[1:]]
    return pl.pallas_call(
        _mixer_router_kernel,
        grid=(N_TILES + 1,),
        in_specs=in_specs,
        out_specs=[row_blk(D_MODEL), row_blk(D_MODEL), col_blk, col_blk,
                   pl.BlockSpec((N_EXPERTS, LANES), lambda i: (0, 0))],
        out_shape=[jax.ShapeDtypeStruct((N_TOK, D_MODEL), _F32),
                   jax.ShapeDtypeStruct((N_TOK, D_MODEL), _F32),
                   jax.ShapeDtypeStruct((2 * TOP_K, N_TOK), jnp.int32),
                   jax.ShapeDtypeStruct((2 * TOP_K, N_TOK), _F32),
                   jax.ShapeDtypeStruct((N_EXPERTS, LANES), _F32)],
        scratch_shapes=[pltpu.VMEM((TM + SUBLANES, D_CONV), _F32),
                        pltpu.VMEM((N_EXPERTS, LANES), _F32),
                        pltpu.VMEM((TM, D_MODEL), _BF16),
                        pltpu.VMEM((TM, LANES), _F32)],
        compiler_params=pltpu.CompilerParams(dimension_semantics=("arbitrary",),
                                             vmem_limit_bytes=VMEM_LIMIT),
        name="mixer_router",
    )(*ins)


def _dispatch_kernel(dest_ref, x_hbm, bufx_hbm, x_in, in_sem, out_sem):
    i = pl.program_id(0)
    n_steps = pl.num_programs(0)

    def fetch(blk, slot):
        r = pl.multiple_of(blk * TD, TD)
        return pltpu.make_async_copy(x_hbm.at[pl.ds(r, TD)], x_in.at[slot], in_sem.at[slot])

    def row_copy(blk, slot, t8, u, k):
        t = t8 * SUBLANES + u if isinstance(t8, int) else pl.multiple_of(t8 * SUBLANES, SUBLANES) + u
        dst = dest_ref[(blk * TD + t) * TOP_K + k]
        return pltpu.make_async_copy(x_in.at[slot, pl.ds(t, 1)], bufx_hbm.at[pl.ds(dst, 1)],
                                     out_sem.at[blk % 2])

    def scatter(blk, slot, start):
        if start:
            for t8 in range(TD // SUBLANES):
                for u in range(SUBLANES):
                    for k in range(TOP_K):
                        row_copy(blk, slot, t8, u, k).start()
            return

        def body(t8, carry):
            for u in range(SUBLANES):
                for k in range(TOP_K):
                    row_copy(blk, slot, t8, u, k).wait()
            return carry
        lax.fori_loop(0, TD // SUBLANES, body, 0)

    def in_slot_of(blk, fn):
        for q in range(IN_SLOTS):
            @pl.when(blk % IN_SLOTS == q)
            def _(q=q):
                fn(q)

    @pl.when(i == 0)
    def _():
        fetch(0, 0).start()
        fetch(1, 1).start()

    in_slot_of(i, lambda q: fetch(i, q).wait())
    in_slot_of(i, lambda q: scatter(i, q, True))

    @pl.when(i >= 1)
    def _():
        in_slot_of(i - 1, lambda q: scatter(i - 1, q, False))

    @pl.when(i + 2 < n_steps)
    def _():
        in_slot_of(i + 2, lambda q: fetch(i + 2, q).start())

    @pl.when(i == n_steps - 1)
    def _():
        in_slot_of(i, lambda q: scatter(i, q, False))


def _dispatch(dest, xn2):
    return pl.pallas_call(
        _dispatch_kernel,
        grid_spec=pltpu.PrefetchScalarGridSpec(
            num_scalar_prefetch=1,
            grid=(N_TOK // TD,),
            in_specs=[pl.BlockSpec(memory_space=pl.ANY)],
            out_specs=pl.BlockSpec(memory_space=pl.ANY),
            scratch_shapes=[pltpu.VMEM((IN_SLOTS, TD, D_MODEL), _F32),
                            pltpu.SemaphoreType.DMA((IN_SLOTS,)),
                            pltpu.SemaphoreType.DMA((2,))],
        ),
        out_shape=jax.ShapeDtypeStruct((MAX_SUPER * SUPER_ROWS, D_MODEL), _F32),
        compiler_params=pltpu.CompilerParams(dimension_semantics=("arbitrary",),
                                             vmem_limit_bytes=VMEM_LIMIT),
        name="dispatch",
    )(dest, xn2)


def _expert_kernel(sup_e_ref, sup_blk_ref, nchunk_ref, mmrows_ref, valid_ref,
                   x_hbm, wg_ref, wu_ref, bgu_ref, wd_ref, bd_ref, out_hbm,
                   x_stage, x_bf, acc, x_sem, o_sem):
    s = pl.program_id(0)
    j = pl.program_id(1)
    n_super = pl.num_programs(0)
    n_chunks = nchunk_ref[s]
    cur = s % 2

    def x_copy(sup, c):
        r = pl.multiple_of(sup_blk_ref[sup] * SUPER_ROWS + c * ROW_CHUNK, ROW_CHUNK)
        slot = c % X_SLOTS
        return pltpu.make_async_copy(x_hbm.at[pl.ds(r, ROW_CHUNK)], x_stage.at[slot],
                                     x_sem.at[slot])

    def keep_tokens(sup, c, chunk):
        row = c * ROW_CHUNK + lax.broadcasted_iota(jnp.int32, (ROW_CHUNK, 1), 0)
        return jnp.where(row < valid_ref[sup], chunk, 0.0).astype(_BF16)

    def x_finish(sup, c):
        x_copy(sup, c).wait()
        rows = pl.ds(pl.multiple_of(c * ROW_CHUNK, ROW_CHUNK), ROW_CHUNK)
        x_bf[sup % 2, rows, :] = keep_tokens(sup, c, x_stage[c % X_SLOTS])

    def first_copy(c):
        rows = pl.ds(c * ROW_CHUNK, ROW_CHUNK)
        r = pl.multiple_of(sup_blk_ref[0] * SUPER_ROWS + c * ROW_CHUNK, ROW_CHUNK)
        return pltpu.make_async_copy(x_hbm.at[pl.ds(r, ROW_CHUNK)], acc.at[1, rows], x_sem.at[0])

    def out_copy(sup, c):
        r = pl.multiple_of(sup_blk_ref[sup] * SUPER_ROWS + c * ROW_CHUNK, ROW_CHUNK)
        rows = pl.ds(pl.multiple_of(c * ROW_CHUNK, ROW_CHUNK), ROW_CHUNK)
        return pltpu.make_async_copy(acc.at[sup % 2, rows], out_hbm.at[pl.ds(r, ROW_CHUNK)],
                                     o_sem.at[sup % 2])

    def for_each_chunk(sup, fn):
        for c in range(CHUNKS_PER_SUPER):
            @pl.when(c < nchunk_ref[sup])
            def _(c=c):
                fn(sup, c)

    @pl.when((s == 0) & (j == 0))
    def _():
        for_each_chunk(0, lambda sup, c: first_copy(c).start())

        def finish(sup, c):
            first_copy(c).wait()
            rows = pl.ds(c * ROW_CHUNK, ROW_CHUNK)
            x_bf[0, rows, :] = keep_tokens(0, c, acc[1, rows, :])
        for_each_chunk(0, finish)

    @pl.when((s >= 2) & (j == 0))
    def _():
        for_each_chunk(jnp.maximum(s - 2, 0), lambda sup, c: out_copy(sup, c).wait())

    prev = jnp.maximum(s - 1, 0)

    @pl.when((s >= 1) & (j < nchunk_ref[prev]))
    def _():
        out_copy(prev, j).start()

    nxt = jnp.minimum(s + 1, n_super - 1)
    n_next = jnp.where(s + 1 < n_super, nchunk_ref[nxt], 0)

    @pl.when((j >= 1) & (j - 1 < n_next))
    def _():
        x_finish(nxt, j - 1)

    @pl.when(j < n_next)
    def _():
        x_copy(nxt, j).start()

    @pl.when(n_chunks > 0)
    def _():
        def partial_out(n_rows):
            rows = pl.ds(0, n_rows)
            xb = x_bf[cur, rows, :]
            gate = _dot(xb, wg_ref[0].astype(_BF16)) + bgu_ref[0, pl.ds(j, 1), :]
            up = _dot(xb, wu_ref[0].astype(_BF16)) + bgu_ref[0, pl.ds(N_FF_TILES + j, 1), :]
            gate = jnp.minimum(gate, SWIGLU_LIMIT)
            up = jnp.clip(up, -SWIGLU_LIMIT, SWIGLU_LIMIT)
            glu = gate * jax.nn.sigmoid(gate * SWIGLU_ALPHA)
            act = ((up + 1.0) * glu).astype(_BF16)
            return rows, _dot(act, wd_ref[0].astype(_BF16))

        def sweep(update):
            mm_rows = mmrows_ref[s]
            for n_rows in MATMUL_SIZES:
                @pl.when(mm_rows == n_rows)
                def _(n_rows=n_rows):
                    update(n_rows)

        def accumulate(n_rows):
            rows, o = partial_out(n_rows)
            acc[cur, rows, :] = jnp.where(j == 0, bd_ref[0], acc[cur, rows, :]) + o

        sweep(accumulate)

    @pl.when((s == n_super - 1) & (j == N_FF_TILES - 1))
    def _():
        for_each_chunk(s, lambda sup, c: out_copy(sup, c).start())

        @pl.when(s >= 1)
        def _():
            for_each_chunk(jnp.maximum(s - 1, 0), lambda sup, c: out_copy(sup, c).wait())

        for_each_chunk(s, lambda sup, c: out_copy(sup, c).wait())


def _experts(n_super, sup_e, sup_blk, nchunk, mm_rows, valid, buf_x, w_gate_up, b_gate_up3, w_down, b_down3):
    x_spec = pl.BlockSpec(memory_space=pl.ANY)
    wg_spec = pl.BlockSpec((1, D_MODEL, FF_TILE), lambda s, j, se, sb, nc, nh, va: (se[s], 0, j))
    wu_spec = pl.BlockSpec((1, D_MODEL, FF_TILE),
                           lambda s, j, se, sb, nc, nh, va: (se[s], 0, N_FF_TILES + j))
    bgu_spec = pl.BlockSpec((1, 2 * N_FF_TILES, FF_TILE), lambda s, j, se, sb, nc, nh, va: (se[s], 0, 0))
    wd_spec = pl.BlockSpec((1, FF_TILE, D_MODEL), lambda s, j, se, sb, nc, nh, va: (se[s], j, 0))
    bd_spec = pl.BlockSpec((1, 1, D_MODEL), lambda s, j, se, sb, nc, nh, va: (se[s], 0, 0))
    out_spec = pl.BlockSpec(memory_space=pl.ANY)
    return pl.pallas_call(
        _expert_kernel,
        grid_spec=pltpu.PrefetchScalarGridSpec(
            num_scalar_prefetch=5,
            grid=(n_super, N_FF_TILES),
            in_specs=[x_spec, wg_spec, wu_spec, bgu_spec, wd_spec, bd_spec],
            out_specs=out_spec,
            scratch_shapes=[pltpu.VMEM((X_SLOTS, ROW_CHUNK, D_MODEL), _F32),
                            pltpu.VMEM((2, SUPER_ROWS, D_MODEL), _BF16),
                            pltpu.VMEM((2, SUPER_ROWS, D_MODEL), _F32),
                            pltpu.SemaphoreType.DMA((X_SLOTS,)),
                            pltpu.SemaphoreType.DMA((2,))],
        ),
        out_shape=jax.ShapeDtypeStruct((MAX_SUPER * SUPER_ROWS, D_MODEL), _F32),
        compiler_params=pltpu.CompilerParams(dimension_semantics=("arbitrary", "arbitrary"),
                                             vmem_limit_bytes=VMEM_LIMIT),
        name="experts",
    )(sup_e, sup_blk, nchunk, mm_rows, valid, buf_x, w_gate_up, w_gate_up, b_gate_up3, w_down, b_down3)


def _combine_kernel(dest_ref, h1_ref, gates_ref, lng_ref, y_hbm, out_ref, rows, sem):
    i = pl.program_id(0)
    n_steps = pl.num_programs(0)

    def row_copy(tile, slot, t8, u, k):
        t = t8 * SUBLANES + u if isinstance(t8, int) else pl.multiple_of(t8 * SUBLANES, SUBLANES) + u
        src = dest_ref[(tile * TC + t) * TOP_K + k]
        return pltpu.make_async_copy(y_hbm.at[pl.ds(src, 1)], rows.at[slot, k, pl.ds(t, 1)],
                                     sem.at[slot])

    def issue_tile(tile, slot):
        for t8 in range(TC // SUBLANES):
            for u in range(SUBLANES):
                for k in range(TOP_K):
                    row_copy(tile, slot, t8, u, k).start()

    def wait_tile(tile, slot):
        def body(t8, carry):
            for u in range(SUBLANES):
                for k in range(TOP_K):
                    row_copy(tile, slot, t8, u, k).wait()
            return carry
        lax.fori_loop(0, TC // SUBLANES, body, 0)

    @pl.when(i == 0)
    def _():
        issue_tile(0, 0)

    for slot in range(2):
        @pl.when(i % 2 == slot)
        def _(slot=slot):
            @pl.when(i + 1 < n_steps)
            def _():
                issue_tile(i + 1, 1 - slot)

            wait_tile(i, slot)
            g = gates_ref[...]
            acc = h1_ref[...]
            for k in range(TOP_K):
                acc = acc + rows[slot, k] * g[:, k:k + 1]
            out_ref[...] = _rms_rows(acc, lng_ref[...])


def _combine(dest, h1, gates, ln_final_g, y_sorted):
    return pl.pallas_call(
        _combine_kernel,
        grid_spec=pltpu.PrefetchScalarGridSpec(
            num_scalar_prefetch=1,
            grid=(N_TOK // TC,),
            in_specs=[pl.BlockSpec((TC, D_MODEL), lambda i, *_: (i, 0)),
                      pl.BlockSpec((TC, TOP_K), lambda i, *_: (i, 0)),
                      pl.BlockSpec((1, D_MODEL), lambda i, *_: (0, 0)),
                      pl.BlockSpec(memory_space=pl.ANY)],
            out_specs=pl.BlockSpec((TC, D_MODEL), lambda i, *_: (i, 0)),
            scratch_shapes=[pltpu.VMEM((2, TOP_K, TC, D_MODEL), _F32),
                            pltpu.SemaphoreType.DMA((2,))],
        ),
        out_shape=jax.ShapeDtypeStruct((N_TOK, D_MODEL), _F32),
        compiler_params=pltpu.CompilerParams(dimension_semantics=("arbitrary",),
                                             vmem_limit_bytes=VMEM_LIMIT),
        name="combine",
    )(dest, h1, gates, ln_final_g, y_sorted)


def kernel(x, ln_mix_g, w_in, conv_w, sgu_ln_g, sgu_ln_b, sgu_w, sgu_b, gn_conv, gn_sgu, w_out,
           ln_ffn_g, w_router, b_router, w_gate_up, b_gate_up, w_down, b_down, ln_final_g):
    row = lambda v: v.reshape(1, -1)
    x2d = x.reshape(N_TOK, D_MODEL)

    sgu_b_full = jnp.repeat(jnp.transpose(sgu_b), SGU_HEAD_DIM, axis=1)
    wr_hi = w_router.astype(_BF16)
    wr_lo = (w_router - wr_hi.astype(_F32)).astype(_BF16)
    pad = ((0, 0), (0, LANES - N_EXPERTS))
    wr_split = jnp.concatenate([jnp.pad(wr_hi, pad), jnp.pad(wr_lo, pad)], axis=1)
    br_pad = jnp.pad(b_router, (0, LANES - N_EXPERTS)).reshape(1, LANES)

    h1, xn2, route, gates, counts = _mixer_router(
        x2d, row(ln_mix_g), w_in.astype(_BF16), conv_w, row(sgu_ln_g), row(sgu_ln_b), sgu_w,
        sgu_b_full, row(gn_conv), row(gn_sgu), w_out.astype(_BF16), row(ln_ffn_g), wr_split, br_pad)

    sizes = counts[:, 0].astype(jnp.int32)
    n_sup = (sizes + SUPER_ROWS - 1) // SUPER_ROWS
    sup_end = jnp.cumsum(n_sup)
    sup_start = sup_end - n_sup
    total_sup = sup_end[-1]
    row_start = (sup_start * SUPER_ROWS).astype(jnp.int32)
    sid = jnp.arange(MAX_SUPER, dtype=jnp.int32)
    sid_eff = jnp.minimum(sid, total_sup - 1)
    sup_e = jnp.minimum(jnp.sum((sid_eff[:, None] >= sup_end[None, :]).astype(jnp.int32), axis=1),
                        N_EXPERTS - 1)
    of_expert = sup_e[:, None] == jnp.arange(N_EXPERTS, dtype=jnp.int32)[None, :]
    rows_left = jnp.sum(jnp.where(of_expert, sizes - (sid_eff[:, None] - sup_start) * SUPER_ROWS, 0),
                        axis=1)
    valid = jnp.clip(rows_left, 0, SUPER_ROWS)
    nchunk = jnp.where(sid < total_sup, (valid + ROW_CHUNK - 1) // ROW_CHUNK, 0).astype(jnp.int32)
    coarse = (valid + COARSE_STEP - 1) // COARSE_STEP * COARSE_STEP
    fine = (valid + FINE_STEP - 1) // FINE_STEP * FINE_STEP
    mm_rows = jnp.where(sid < total_sup, jnp.where(fine > FINE_ABOVE, fine, coarse), 0).astype(jnp.int32)
    eid = route[0:TOP_K]
    eid_start = jnp.sum(jnp.where(eid[None] == jnp.arange(N_EXPERTS, dtype=jnp.int32)[:, None, None],
                                  row_start[:, None, None], 0), axis=0)
    dest = jnp.transpose(eid_start + route[TOP_K:2 * TOP_K]).reshape(-1)
    gates = jnp.transpose(gates[0:TOP_K])

    buf_x = _dispatch(dest, xn2)
    y_sorted = _experts(total_sup, sup_e, sid_eff.astype(jnp.int32), nchunk, mm_rows,
                        valid.astype(jnp.int32), buf_x, w_gate_up,
                        b_gate_up.reshape(N_EXPERTS, 2 * N_FF_TILES, FF_TILE), w_down,
                        b_down.reshape(N_EXPERTS, 1, D_MODEL))
    out = _combine(dest, h1, gates, row(ln_final_g), y_sorted)
    return out.reshape(BATCH, SEQ, D_MODEL)
```

```python
import jax
import jax.numpy as jnp
from jax import lax
from jax.experimental import pallas as pl
from jax.experimental.pallas import tpu as pltpu

D_MODEL = 2048
BATCH = 2
SEQ = 4096
N_TOK = BATCH * SEQ

D_CONV = 1024
N_CONV_GROUPS = 8
CONV_WIDTH = 3
D_SGU = 1024
N_SGU_HEADS = 8
SGU_HEAD_DIM = 128
CHUNK = 128
D_IN_PROJ = 3 * D_CONV + 2 * D_SGU

N_EXPERTS = 32
TOP_K = 4
D_FF = 2048
SWIGLU_LIMIT = 7.0
SWIGLU_ALPHA = 1.702
RMS_EPS = 1e-5
LN_EPS = 1e-5

LANES = 128
SUBLANES = 8
VMEM_LIMIT = 56 * 1024 * 1024

TM = 256
N_TILES = N_TOK // TM
SUPER_ROWS = 1280
ROW_CHUNK = 256
FF_TILE = 256
N_FF_TILES = D_FF // FF_TILE
MAX_SUPER = (N_TOK * TOP_K + N_EXPERTS * (SUPER_ROWS - 1)) // SUPER_ROWS
CHUNKS_PER_SUPER = SUPER_ROWS // ROW_CHUNK
assert CHUNKS_PER_SUPER + 1 <= N_FF_TILES
X_SLOTS = 2
COARSE_STEP, FINE_STEP, FINE_ABOVE = 128, 64, 896
MATMUL_SIZES = (tuple(range(COARSE_STEP, FINE_ABOVE + 1, COARSE_STEP))
                + tuple(range(FINE_ABOVE + FINE_STEP, SUPER_ROWS + 1, FINE_STEP)))
TD = 256
TC = 256

_F32 = jnp.float32
_BF16 = jnp.bfloat16


def _dot(a, b):
    return jnp.dot(a, b, preferred_element_type=_F32)


def _gelu_exact(x):
    return 0.5 * x * (1.0 + lax.erf(x * (2.0 ** -0.5)))


def _rms_rows(x, gain):
    return x * lax.rsqrt(jnp.mean(x * x, axis=-1, keepdims=True) + RMS_EPS) * gain


def _route_tile(logits, active, route_ref, gates_ref, counts_ref, carry):
    cur = jnp.transpose(logits)[0:N_EXPERTS, :]
    expert = lax.broadcasted_iota(jnp.int32, (N_EXPERTS, TM), 0).astype(_F32)
    neg_inf = jnp.float32(-jnp.inf)

    vals, ids, onehots = [], [], []
    for _ in range(TOP_K):
        m = jnp.max(cur, axis=0, keepdims=True)
        idx = jnp.min(jnp.where(cur == m, expert, float(N_EXPERTS)), axis=0, keepdims=True)
        oh = expert == idx
        vals.append(m)
        ids.append(idx)
        onehots.append(oh)
        cur = jnp.where(oh, neg_inf, cur)
    exps = [jnp.exp(v - vals[0]) for v in vals]
    denom = exps[0] + exps[1] + exps[2] + exps[3]
    gates = [e / denom for e in exps]

    mask = jnp.where(active, (onehots[0] | onehots[1] | onehots[2] | onehots[3]).astype(_F32), 0.0)
    row_t = lax.broadcasted_iota(jnp.int32, (TM, TM), 0)
    col_t = lax.broadcasted_iota(jnp.int32, (TM, TM), 1)
    earlier = (row_t < col_t).astype(_BF16)
    before = _dot(mask.astype(_BF16), earlier) + carry[:, 0:1]
    ranks = [jnp.sum(jnp.where(oh, before, 0.0), axis=0, keepdims=True) for oh in onehots]
    carry[...] = carry[...] + jnp.sum(mask, axis=1, keepdims=True)
    counts_ref[...] = carry[...]

    out_row = lax.broadcasted_iota(jnp.int32, (2 * TOP_K, TM), 0)
    route = jnp.zeros((2 * TOP_K, TM), _F32)
    gate_out = jnp.zeros((2 * TOP_K, TM), _F32)
    for k in range(TOP_K):
        route = jnp.where(out_row == k, ids[k], route)
        route = jnp.where(out_row == TOP_K + k, ranks[k], route)
        gate_out = jnp.where(out_row == k, gates[k], gate_out)
    route_ref[...] = route.astype(jnp.int32)
    gates_ref[...] = gate_out


def _mixer_router_kernel(x_ref, lng_ref, win_ref, convw_ref, slg_ref, slb_ref, sw_ref, sb_ref,
                         gnc_ref, gns_ref, wout_ref, lnf_ref, wr_ref, br_ref,
                         h1_ref, xn2_ref, route_ref, gates_ref, counts_ref,
                         cbuf, carry, ybuf, logit_buf):
    i = pl.program_id(0)

    @pl.when(i == 0)
    def _():
        carry[...] = jnp.zeros_like(carry)
        logit_buf[...] = jnp.zeros_like(logit_buf)

    @pl.when(i % (SEQ // TM) == 0)
    def _():
        cbuf[0:SUBLANES, :] = jnp.zeros((SUBLANES, D_CONV), _F32)

    @pl.when(i == N_TILES)
    def _():
        _route_tile(logit_buf[...], True, route_ref, gates_ref, counts_ref, carry)

    @pl.when(i < N_TILES)
    def _():
        prev_logits = logit_buf[...]
        _mixer_tile(x_ref, lng_ref, win_ref, convw_ref, slg_ref, slb_ref, sw_ref, sb_ref,
                    gnc_ref, gns_ref, wout_ref, lnf_ref, wr_ref, br_ref,
                    h1_ref, xn2_ref, cbuf, ybuf, logit_buf)
        _route_tile(prev_logits, i > 0, route_ref, gates_ref, counts_ref, carry)


def _mixer_tile(x_ref, lng_ref, win_ref, convw_ref, slg_ref, slb_ref, sw_ref, sb_ref,
                gnc_ref, gns_ref, wout_ref, lnf_ref, wr_ref, br_ref,
                h1_ref, xn2_ref, cbuf, ybuf, logit_buf):
    x = x_ref[...]
    xn = _rms_rows(x, lng_ref[...]).astype(_BF16)

    b_gate = _dot(xn, win_ref[:, 0:D_CONV])
    c_gate = _dot(xn, win_ref[:, D_CONV:2 * D_CONV])
    hh = _dot(xn, win_ref[:, 2 * D_CONV:3 * D_CONV])
    ch = c_gate * hh
    cbuf[SUBLANES:SUBLANES + TM, :] = ch
    ch1 = cbuf[SUBLANES - 1:SUBLANES - 1 + TM, :]
    ch2 = cbuf[SUBLANES - 2:SUBLANES - 2 + TM, :]
    conv = convw_ref[0:1, :] * ch2 + convw_ref[1:2, :] * ch1 + convw_ref[2:3, :] * ch
    cbuf[0:SUBLANES, :] = cbuf[TM:TM + SUBLANES, :]
    y_conv = b_gate * conv
    for g in range(N_CONV_GROUPS):
        sl = slice(g * LANES, (g + 1) * LANES)
        blk = y_conv[:, sl]
        ms = jnp.mean(blk * blk, axis=-1, keepdims=True)
        ybuf[:, sl] = (blk * lax.rsqrt(ms + RMS_EPS) * gnc_ref[:, sl]).astype(_BF16)

    gu = _gelu_exact(_dot(xn, win_ref[:, 3 * D_CONV:3 * D_CONV + D_SGU]))
    gv = _gelu_exact(_dot(xn, win_ref[:, 3 * D_CONV + D_SGU:D_IN_PROJ]))
    row_c = lax.broadcasted_iota(jnp.int32, (CHUNK, CHUNK), 0)
    col_c = lax.broadcasted_iota(jnp.int32, (CHUNK, CHUNK), 1)
    causal = row_c >= col_c
    for h in range(N_SGU_HEADS):
        sl = slice(h * SGU_HEAD_DIM, (h + 1) * SGU_HEAD_DIM)
        vh = gv[:, sl]
        mu = jnp.mean(vh, axis=-1, keepdims=True)
        xc = vh - mu
        var = jnp.mean(xc * xc, axis=-1, keepdims=True)
        vn = (xc * lax.rsqrt(var + LN_EPS) * slg_ref[:, sl] + slb_ref[:, sl]).astype(_BF16)
        wm = jnp.where(causal, sw_ref[h], 0.0).astype(_BF16)
        for c in range(TM // CHUNK):
            rows = slice(c * CHUNK, (c + 1) * CHUNK)
            mixed = _dot(wm, vn[rows, :]) + sb_ref[:, sl]
            ys = gu[rows, sl] * mixed
            ms = jnp.mean(ys * ys, axis=-1, keepdims=True)
            ybuf[rows, D_CONV + h * SGU_HEAD_DIM:D_CONV + (h + 1) * SGU_HEAD_DIM] = (
                ys * lax.rsqrt(ms + RMS_EPS) * gns_ref[:, sl]).astype(_BF16)

    h1 = x + _dot(ybuf[...], wout_ref[...])
    h1_ref[...] = h1
    xn2 = _rms_rows(h1, lnf_ref[...])
    xn2_ref[...] = xn2

    x_hi = xn2.astype(_BF16)
    x_lo = (xn2 - x_hi.astype(_F32)).astype(_BF16)
    p = _dot(x_hi, wr_ref[...]) + _dot(x_lo, wr_ref[...])
    logit_buf[...] = p[:, :LANES] + p[:, LANES:] + br_ref[...]


def _mixer_router(x2d, ln_mix_g, w_in_bf, conv_w, sgu_ln_g, sgu_ln_b, sgu_w, sgu_b_full,
                  gn_conv, gn_sgu, w_out_bf, ln_ffn_g, wr_split, br_pad):
    def full(a):
        return pl.BlockSpec(a.shape, lambda i: (0,) * a.ndim)

    row_blk = lambda w: pl.BlockSpec((TM, w), lambda i: (jnp.minimum(i, N_TILES - 1), 0))
    col_blk = pl.BlockSpec((2 * TOP_K, TM), lambda i: (0, jnp.maximum(i - 1, 0)))
    ins = [x2d, ln_mix_g, w_in_bf, conv_w, sgu_ln_g, sgu_ln_b, sgu_w, sgu_b_full,
           gn_conv, gn_sgu, w_out_bf, ln_ffn_g, wr_split, br_pad]
    in_specs = [row_blk(D_MODEL)] + [full(a) for a in ins[1:]]
    return pl.pallas_call(
        _mixer_router_kernel,
        grid=(N_TILES + 1,),
        in_specs=in_specs,
        out_specs=[row_blk(D_MODEL), row_blk(D_MODEL), col_blk, col_blk,
                   pl.BlockSpec((N_EXPERTS, LANES), lambda i: (0, 0))],
        out_shape=[jax.ShapeDtypeStruct((N_TOK, D_MODEL), _F32),
                   jax.ShapeDtypeStruct((N_TOK, D_MODEL), _F32),
                   jax.ShapeDtypeStruct((2 * TOP_K, N_TOK), jnp.int32),
                   jax.ShapeDtypeStruct((2 * TOP_K, N_TOK), _F32),
                   jax.ShapeDtypeStruct((N_EXPERTS, LANES), _F32)],
        scratch_shapes=[pltpu.VMEM((TM + SUBLANES, D_CONV), _F32),
                        pltpu.VMEM((N_EXPERTS, LANES), _F32),
                        pltpu.VMEM((TM, D_MODEL), _BF16),
                        pltpu.VMEM((TM, LANES), _F32)],
        compiler_params=pltpu.CompilerParams(dimension_semantics=("arbitrary",),
                                             vmem_limit_bytes=VMEM_LIMIT),
        name="mixer_router",
    )(*ins)


def _dispatch_kernel(dest_ref, x_hbm, bufx_hbm, x_in, x_pk, in_sem, out_sem):
    i = pl.program_id(0)
    n_steps = pl.num_programs(0)

    def fetch(blk, slot):
        r = pl.multiple_of(blk * TD, TD)
        return pltpu.make_async_copy(x_hbm.at[pl.ds(r, TD)], x_in.at[slot], in_sem.at[slot])

    def row_copy(blk, slot, t8, u, k):
        t = t8 * SUBLANES + u if isinstance(t8, int) else pl.multiple_of(t8 * SUBLANES, SUBLANES) + u
        dst = dest_ref[(blk * TD + t) * TOP_K + k]
        return pltpu.make_async_copy(x_pk.at[slot, pl.ds(t, 1)], bufx_hbm.at[pl.ds(dst, 1)],
                                     out_sem.at[slot])

    def scatter(blk, slot, start):
        if start:
            for t8 in range(TD // SUBLANES):
                for u in range(SUBLANES):
                    for k in range(TOP_K):
                        row_copy(blk, slot, t8, u, k).start()
            return

        def body(t8, carry):
            for u in range(SUBLANES):
                for k in range(TOP_K):
                    row_copy(blk, slot, t8, u, k).wait()
            return carry
        lax.fori_loop(0, TD // SUBLANES, body, 0)

    def pack(slot):
        rounded = x_in[slot].astype(_BF16).astype(_F32)
        lo = lax.shift_right_logical(pltpu.bitcast(rounded[:, :D_MODEL // 2], jnp.uint32), jnp.uint32(16))
        hi = pltpu.bitcast(rounded[:, D_MODEL // 2:], jnp.uint32) & jnp.uint32(0xFFFF0000)
        x_pk[slot] = lo | hi

    @pl.when(i == 0)
    def _():
        fetch(0, 0).start()

    for slot in range(2):
        @pl.when(i % 2 == slot)
        def _(slot=slot):
            @pl.when(i + 1 < n_steps)
            def _():
                fetch(i + 1, 1 - slot).start()

            fetch(i, slot).wait()
            pack(slot)
            scatter(i, slot, True)

            @pl.when(i >= 1)
            def _():
                scatter(i - 1, 1 - slot, False)

            @pl.when(i == n_steps - 1)
            def _():
                scatter(i, slot, False)


def _dispatch(dest, xn2):
    return pl.pallas_call(
        _dispatch_kernel,
        grid_spec=pltpu.PrefetchScalarGridSpec(
            num_scalar_prefetch=1,
            grid=(N_TOK // TD,),
            in_specs=[pl.BlockSpec(memory_space=pl.ANY)],
            out_specs=pl.BlockSpec(memory_space=pl.ANY),
            scratch_shapes=[pltpu.VMEM((2, TD, D_MODEL), _F32),
                            pltpu.VMEM((2, TD, D_MODEL // 2), jnp.uint32),
                            pltpu.SemaphoreType.DMA((2,)),
                            pltpu.SemaphoreType.DMA((2,))],
        ),
        out_shape=jax.ShapeDtypeStruct((MAX_SUPER * SUPER_ROWS, D_MODEL // 2), jnp.uint32),
        compiler_params=pltpu.CompilerParams(dimension_semantics=("arbitrary",),
                                             vmem_limit_bytes=VMEM_LIMIT),
        name="dispatch",
    )(dest, xn2)


def _expert_kernel(sup_e_ref, sup_blk_ref, nchunk_ref, mmrows_ref, valid_ref,
                   x_hbm, wg_ref, wu_ref, bgu_ref, wd_ref, bd_ref, out_hbm,
                   x_stage, x_bf, acc, x_sem, o_sem):
    s = pl.program_id(0)
    j = pl.program_id(1)
    n_super = pl.num_programs(0)
    n_chunks = nchunk_ref[s]
    cur = s % 2

    def x_copy(sup, c):
        r = pl.multiple_of(sup_blk_ref[sup] * SUPER_ROWS + c * ROW_CHUNK, ROW_CHUNK)
        slot = c % X_SLOTS
        return pltpu.make_async_copy(x_hbm.at[pl.ds(r, ROW_CHUNK)], x_stage.at[slot],
                                     x_sem.at[slot])

    def x_finish(sup, c):
        x_copy(sup, c).wait()
        words = x_stage[c % X_SLOTS]
        row = c * ROW_CHUNK + lax.broadcasted_iota(jnp.int32, (ROW_CHUNK, 1), 0)
        keep = row < valid_ref[sup]
        lo = pltpu.bitcast(lax.shift_left(words, jnp.uint32(16)), _F32)
        hi = pltpu.bitcast(words & jnp.uint32(0xFFFF0000), _F32)
        rows = pl.ds(pl.multiple_of(c * ROW_CHUNK, ROW_CHUNK), ROW_CHUNK)
        x_bf[sup % 2, rows, 0:D_MODEL // 2] = jnp.where(keep, lo, 0.0).astype(_BF16)
        x_bf[sup % 2, rows, D_MODEL // 2:D_MODEL] = jnp.where(keep, hi, 0.0).astype(_BF16)

    def out_copy(sup, c):
        r = pl.multiple_of(sup_blk_ref[sup] * SUPER_ROWS + c * ROW_CHUNK, ROW_CHUNK)
        rows = pl.ds(pl.multiple_of(c * ROW_CHUNK, ROW_CHUNK), ROW_CHUNK)
        return pltpu.make_async_copy(acc.at[sup % 2, rows], out_hbm.at[pl.ds(r, ROW_CHUNK)],
                                     o_sem.at[sup % 2])

    def for_each_chunk(sup, fn):
        for c in range(CHUNKS_PER_SUPER):
            @pl.when(c < nchunk_ref[sup])
            def _(c=c):
                fn(sup, c)

    @pl.when((s == 0) & (j == 0))
    def _():
        def load(sup, c):
            x_copy(sup, c).start()
            x_finish(sup, c)
        for_each_chunk(0, load)

    @pl.when((s >= 2) & (j == 0))
    def _():
        for_each_chunk(jnp.maximum(s - 2, 0), lambda sup, c: out_copy(sup, c).wait())

    prev = jnp.maximum(s - 1, 0)

    @pl.when((s >= 1) & (j < nchunk_ref[prev]))
    def _():
        out_copy(prev, j).start()

    nxt = jnp.minimum(s + 1, n_super - 1)
    n_next = jnp.where(s + 1 < n_super, nchunk_ref[nxt], 0)

    @pl.when((j >= 1) & (j - 1 < n_next))
    def _():
        x_finish(nxt, j - 1)

    @pl.when(j < n_next)
    def _():
        x_copy(nxt, j).start()

    @pl.when(n_chunks > 0)
    def _():
        def partial_out(n_rows):
            rows = pl.ds(0, n_rows)
            xb = x_bf[cur, rows, :]
            gate = _dot(xb, wg_ref[0].astype(_BF16)) + bgu_ref[0, pl.ds(j, 1), :]
            up = _dot(xb, wu_ref[0].astype(_BF16)) + bgu_ref[0, pl.ds(N_FF_TILES + j, 1), :]
            gate = jnp.minimum(gate, SWIGLU_LIMIT)
            up = jnp.clip(up, -SWIGLU_LIMIT, SWIGLU_LIMIT)
            glu = gate * jax.nn.sigmoid(gate * SWIGLU_ALPHA)
            act = ((up + 1.0) * glu).astype(_BF16)
            return rows, _dot(act, wd_ref[0].astype(_BF16))

        def sweep(update):
            mm_rows = mmrows_ref[s]
            for n_rows in MATMUL_SIZES:
                @pl.when(mm_rows == n_rows)
                def _(n_rows=n_rows):
                    update(n_rows)

        def accumulate(n_rows):
            rows, o = partial_out(n_rows)
            acc[cur, rows, :] = jnp.where(j == 0, bd_ref[0], acc[cur, rows, :]) + o

        sweep(accumulate)

    @pl.when((s == n_super - 1) & (j == N_FF_TILES - 1))
    def _():
        for_each_chunk(s, lambda sup, c: out_copy(sup, c).start())

        @pl.when(s >= 1)
        def _():
            for_each_chunk(jnp.maximum(s - 1, 0), lambda sup, c: out_copy(sup, c).wait())

        for_each_chunk(s, lambda sup, c: out_copy(sup, c).wait())


def _experts(n_super, sup_e, sup_blk, nchunk, mm_rows, valid, buf_x, w_gate_up, b_gate_up3, w_down, b_down3):
    x_spec = pl.BlockSpec(memory_space=pl.ANY)
    wg_spec = pl.BlockSpec((1, D_MODEL, FF_TILE), lambda s, j, se, sb, nc, nh, va: (se[s], 0, j))
    wu_spec = pl.BlockSpec((1, D_MODEL, FF_TILE),
                           lambda s, j, se, sb, nc, nh, va: (se[s], 0, N_FF_TILES + j))
    bgu_spec = pl.BlockSpec((1, 2 * N_FF_TILES, FF_TILE), lambda s, j, se, sb, nc, nh, va: (se[s], 0, 0))
    wd_spec = pl.BlockSpec((1, FF_TILE, D_MODEL), lambda s, j, se, sb, nc, nh, va: (se[s], j, 0))
    bd_spec = pl.BlockSpec((1, 1, D_MODEL), lambda s, j, se, sb, nc, nh, va: (se[s], 0, 0))
    out_spec = pl.BlockSpec(memory_space=pl.ANY)
    return pl.pallas_call(
        _expert_kernel,
        grid_spec=pltpu.PrefetchScalarGridSpec(
            num_scalar_prefetch=5,
            grid=(n_super, N_FF_TILES),
            in_specs=[x_spec, wg_spec, wu_spec, bgu_spec, wd_spec, bd_spec],
            out_specs=out_spec,
            scratch_shapes=[pltpu.VMEM((X_SLOTS, ROW_CHUNK, D_MODEL // 2), jnp.uint32),
                            pltpu.VMEM((2, SUPER_ROWS, D_MODEL), _BF16),
                            pltpu.VMEM((2, SUPER_ROWS, D_MODEL), _F32),
                            pltpu.SemaphoreType.DMA((X_SLOTS,)),
                            pltpu.SemaphoreType.DMA((2,))],
        ),
        out_shape=jax.ShapeDtypeStruct((MAX_SUPER * SUPER_ROWS, D_MODEL), _F32),
        compiler_params=pltpu.CompilerParams(dimension_semantics=("arbitrary", "arbitrary"),
                                             vmem_limit_bytes=VMEM_LIMIT),
        name="experts",
    )(sup_e, sup_blk, nchunk, mm_rows, valid, buf_x, w_gate_up, w_gate_up, b_gate_up3, w_down, b_down3)


def _combine_kernel(dest_ref, h1_ref, gates_ref, lng_ref, y_hbm, out_ref, rows, sem):
    i = pl.program_id(0)
    n_steps = pl.num_programs(0)

    def row_copy(tile, slot, t8, u, k):
        t = t8 * SUBLANES + u if isinstance(t8, int) else pl.multiple_of(t8 * SUBLANES, SUBLANES) + u
        src = dest_ref[(tile * TC + t) * TOP_K + k]
        return pltpu.make_async_copy(y_hbm.at[pl.ds(src, 1)], rows.at[slot, k, pl.ds(t, 1)],
                                     sem.at[slot])

    def issue_tile(tile, slot):
        for t8 in range(TC // SUBLANES):
            for u in range(SUBLANES):
                for k in range(TOP_K):
                    row_copy(tile, slot, t8, u, k).start()

    def wait_tile(tile, slot):
        def body(t8, carry):
            for u in range(SUBLANES):
                for k in range(TOP_K):
                    row_copy(tile, slot, t8, u, k).wait()
            return carry
        lax.fori_loop(0, TC // SUBLANES, body, 0)

    @pl.when(i == 0)
    def _():
        issue_tile(0, 0)

    for slot in range(2):
        @pl.when(i % 2 == slot)
        def _(slot=slot):
            @pl.when(i + 1 < n_steps)
            def _():
                issue_tile(i + 1, 1 - slot)

            wait_tile(i, slot)
            g = gates_ref[...]
            acc = h1_ref[...]
            for k in range(TOP_K):
                acc = acc + rows[slot, k] * g[:, k:k + 1]
            out_ref[...] = _rms_rows(acc, lng_ref[...])


def _combine(dest, h1, gates, ln_final_g, y_sorted):
    return pl.pallas_call(
        _combine_kernel,
        grid_spec=pltpu.PrefetchScalarGridSpec(
            num_scalar_prefetch=1,
            grid=(N_TOK // TC,),
            in_specs=[pl.BlockSpec((TC, D_MODEL), lambda i, *_: (i, 0)),
                      pl.BlockSpec((TC, TOP_K), lambda i, *_: (i, 0)),
                      pl.BlockSpec((1, D_MODEL), lambda i, *_: (0, 0)),
                      pl.BlockSpec(memory_space=pl.ANY)],
            out_specs=pl.BlockSpec((TC, D_MODEL), lambda i, *_: (i, 0)),
            scratch_shapes=[pltpu.VMEM((2, TOP_K, TC, D_MODEL), _F32),
                            pltpu.SemaphoreType.DMA((2,))],
        ),
        out_shape=jax.ShapeDtypeStruct((N_TOK, D_MODEL), _F32),
        compiler_params=pltpu.CompilerParams(dimension_semantics=("arbitrary",),
                                             vmem_limit_bytes=VMEM_LIMIT),
        name="combine",
    )(dest, h1, gates, ln_final_g, y_sorted)


def kernel(x, ln_mix_g, w_in, conv_w, sgu_ln_g, sgu_ln_b, sgu_w, sgu_b, gn_conv, gn_sgu, w_out,
           ln_ffn_g, w_router, b_router, w_gate_up, b_gate_up, w_down, b_down, ln_final_g):
    row = lambda v: v.reshape(1, -1)
    x2d = x.reshape(N_TOK, D_MODEL)

    sgu_b_full = jnp.repeat(jnp.transpose(sgu_b), SGU_HEAD_DIM, axis=1)
    wr_hi = w_router.astype(_BF16)
    wr_lo = (w_router - wr_hi.astype(_F32)).astype(_BF16)
    pad = ((0, 0), (0, LANES - N_EXPERTS))
    wr_split = jnp.concatenate([jnp.pad(wr_hi, pad), jnp.pad(wr_lo, pad)], axis=1)
    br_pad = jnp.pad(b_router, (0, LANES - N_EXPERTS)).reshape(1, LANES)

    h1, xn2, route, gates, counts = _mixer_router(
        x2d, row(ln_mix_g), w_in.astype(_BF16), conv_w, row(sgu_ln_g), row(sgu_ln_b), sgu_w,
        sgu_b_full, row(gn_conv), row(gn_sgu), w_out.astype(_BF16), row(ln_ffn_g), wr_split, br_pad)

    sizes = counts[:, 0].astype(jnp.int32)
    n_sup = (sizes + SUPER_ROWS - 1) // SUPER_ROWS
    sup_end = jnp.cumsum(n_sup)
    sup_start = sup_end - n_sup
    total_sup = sup_end[-1]
    row_start = (sup_start * SUPER_ROWS).astype(jnp.int32)
    sid = jnp.arange(MAX_SUPER, dtype=jnp.int32)
    sid_eff = jnp.minimum(sid, total_sup - 1)
    sup_e = jnp.minimum(jnp.sum((sid_eff[:, None] >= sup_end[None, :]).astype(jnp.int32), axis=1),
                        N_EXPERTS - 1)
    of_expert = sup_e[:, None] == jnp.arange(N_EXPERTS, dtype=jnp.int32)[None, :]
    rows_left = jnp.sum(jnp.where(of_expert, sizes - (sid_eff[:, None] - sup_start) * SUPER_ROWS, 0),
                        axis=1)
    valid = jnp.clip(rows_left, 0, SUPER_ROWS)
    nchunk = jnp.where(sid < total_sup, (valid + ROW_CHUNK - 1) // ROW_CHUNK, 0).astype(jnp.int32)
    coarse = (valid + COARSE_STEP - 1) // COARSE_STEP * COARSE_STEP
    fine = (valid + FINE_STEP - 1) // FINE_STEP * FINE_STEP
    mm_rows = jnp.where(sid < total_sup, jnp.where(fine > FINE_ABOVE, fine, coarse), 0).astype(jnp.int32)
    eid = route[0:TOP_K]
    eid_start = jnp.sum(jnp.where(eid[None] == jnp.arange(N_EXPERTS, dtype=jnp.int32)[:, None, None],
                                  row_start[:, None, None], 0), axis=0)
    dest = jnp.transpose(eid_start + route[TOP_K:2 * TOP_K]).reshape(-1)
    gates = jnp.transpose(gates[0:TOP_K])

    buf_x = _dispatch(dest, xn2)
    y_sorted = _experts(total_sup, sup_e, sid_eff.astype(jnp.int32), nchunk, mm_rows,
                        valid.astype(jnp.int32), buf_x, w_gate_up,
                        b_gate_up.reshape(N_EXPERTS, 2 * N_FF_TILES, FF_TILE), w_down,
                        b_down.reshape(N_EXPERTS, 1, D_MODEL))
    out = _combine(dest, h1, gates, row(ln_final_g), y_sorted)
    return out.reshape(BATCH, SEQ, D_MODEL)
```

```python
import jax
import jax.numpy as jnp
from jax import lax
from jax.experimental import pallas as pl
from jax.experimental.pallas import tpu as pltpu

D_MODEL = 2048
BATCH = 2
SEQ = 4096
N_TOK = BATCH * SEQ

D_CONV = 1024
N_CONV_GROUPS = 8
CONV_WIDTH = 3
D_SGU = 1024
N_SGU_HEADS = 8
SGU_HEAD_DIM = 128
CHUNK = 128
D_IN_PROJ = 3 * D_CONV + 2 * D_SGU

N_EXPERTS = 32
TOP_K = 4
D_FF = 2048
SWIGLU_LIMIT = 7.0
SWIGLU_ALPHA = 1.702
RMS_EPS = 1e-5
LN_EPS = 1e-5

LANES = 128
SUBLANES = 8
VMEM_LIMIT = 56 * 1024 * 1024

TM = 256
N_TILES = N_TOK // TM
SUPER_ROWS = 1280
ROW_CHUNK = 256
FF_TILE = 256
N_FF_TILES = D_FF // FF_TILE
MAX_SUPER = (N_TOK * TOP_K + N_EXPERTS * (SUPER_ROWS - 1)) // SUPER_ROWS
CHUNKS_PER_SUPER = SUPER_ROWS // ROW_CHUNK
assert CHUNKS_PER_SUPER + 1 <= N_FF_TILES
X_SLOTS = 2
COARSE_STEP, FINE_STEP, FINE_ABOVE = 128, 64, 896
MATMUL_SIZES = (tuple(range(COARSE_STEP, FINE_ABOVE + 1, COARSE_STEP))
                + tuple(range(FINE_ABOVE + FINE_STEP, SUPER_ROWS + 1, FINE_STEP)))
TD = 256
N_BUF = 3
assert N_TOK // TD >= 2
TC = 256

_F32 = jnp.float32
_BF16 = jnp.bfloat16


def _dot(a, b):
    return jnp.dot(a, b, preferred_element_type=_F32)


def _gelu_exact(x):
    return 0.5 * x * (1.0 + lax.erf(x * (2.0 ** -0.5)))


def _rms_rows(x, gain):
    return x * lax.rsqrt(jnp.mean(x * x, axis=-1, keepdims=True) + RMS_EPS) * gain


def _route_tile(logits, active, route_ref, gates_ref, counts_ref, carry):
    cur = jnp.transpose(logits)[0:N_EXPERTS, :]
    expert = lax.broadcasted_iota(jnp.int32, (N_EXPERTS, TM), 0).astype(_F32)
    neg_inf = jnp.float32(-jnp.inf)

    vals, ids, onehots = [], [], []
    for _ in range(TOP_K):
        m = jnp.max(cur, axis=0, keepdims=True)
        idx = jnp.min(jnp.where(cur == m, expert, float(N_EXPERTS)), axis=0, keepdims=True)
        oh = expert == idx
        vals.append(m)
        ids.append(idx)
        onehots.append(oh)
        cur = jnp.where(oh, neg_inf, cur)
    exps = [jnp.exp(v - vals[0]) for v in vals]
    denom = exps[0] + exps[1] + exps[2] + exps[3]
    gates = [e / denom for e in exps]

    mask = jnp.where(active, (onehots[0] | onehots[1] | onehots[2] | onehots[3]).astype(_F32), 0.0)
    row_t = lax.broadcasted_iota(jnp.int32, (TM, TM), 0)
    col_t = lax.broadcasted_iota(jnp.int32, (TM, TM), 1)
    earlier = (row_t < col_t).astype(_BF16)
    before = _dot(mask.astype(_BF16), earlier) + carry[:, 0:1]
    ranks = [jnp.sum(jnp.where(oh, before, 0.0), axis=0, keepdims=True) for oh in onehots]
    carry[...] = carry[...] + jnp.sum(mask, axis=1, keepdims=True)
    counts_ref[...] = carry[...]

    out_row = lax.broadcasted_iota(jnp.int32, (2 * TOP_K, TM), 0)
    route = jnp.zeros((2 * TOP_K, TM), _F32)
    gate_out = jnp.zeros((2 * TOP_K, TM), _F32)
    for k in range(TOP_K):
        route = jnp.where(out_row == k, ids[k], route)
        route = jnp.where(out_row == TOP_K + k, ranks[k], route)
        gate_out = jnp.where(out_row == k, gates[k], gate_out)
    route_ref[...] = route.astype(jnp.int32)
    gates_ref[...] = gate_out


def _mixer_router_kernel(x_ref, lng_ref, win_ref, convw_ref, slg_ref, slb_ref, sw_ref, sb_ref,
                         gnc_ref, gns_ref, wout_ref, lnf_ref, wr_ref, br_ref,
                         h1_ref, xn2_ref, route_ref, gates_ref, counts_ref,
                         cbuf, carry, ybuf, logit_buf):
    i = pl.program_id(0)

    @pl.when(i == 0)
    def _():
        carry[...] = jnp.zeros_like(carry)
        logit_buf[...] = jnp.zeros_like(logit_buf)

    @pl.when(i % (SEQ // TM) == 0)
    def _():
        cbuf[0:SUBLANES, :] = jnp.zeros((SUBLANES, D_CONV), _F32)

    @pl.when(i == N_TILES)
    def _():
        _route_tile(logit_buf[...], True, route_ref, gates_ref, counts_ref, carry)

    @pl.when(i < N_TILES)
    def _():
        prev_logits = logit_buf[...]
        _mixer_tile(x_ref, lng_ref, win_ref, convw_ref, slg_ref, slb_ref, sw_ref, sb_ref,
                    gnc_ref, gns_ref, wout_ref, lnf_ref, wr_ref, br_ref,
                    h1_ref, xn2_ref, cbuf, ybuf, logit_buf)
        _route_tile(prev_logits, i > 0, route_ref, gates_ref, counts_ref, carry)


def _mixer_tile(x_ref, lng_ref, win_ref, convw_ref, slg_ref, slb_ref, sw_ref, sb_ref,
                gnc_ref, gns_ref, wout_ref, lnf_ref, wr_ref, br_ref,
                h1_ref, xn2_ref, cbuf, ybuf, logit_buf):
    x = x_ref[...]
    xn = _rms_rows(x, lng_ref[...]).astype(_BF16)

    b_gate = _dot(xn, win_ref[:, 0:D_CONV])
    c_gate = _dot(xn, win_ref[:, D_CONV:2 * D_CONV])
    hh = _dot(xn, win_ref[:, 2 * D_CONV:3 * D_CONV])
    ch = c_gate * hh
    cbuf[SUBLANES:SUBLANES + TM, :] = ch
    ch1 = cbuf[SUBLANES - 1:SUBLANES - 1 + TM, :]
    ch2 = cbuf[SUBLANES - 2:SUBLANES - 2 + TM, :]
    conv = convw_ref[0:1, :] * ch2 + convw_ref[1:2, :] * ch1 + convw_ref[2:3, :] * ch
    cbuf[0:SUBLANES, :] = cbuf[TM:TM + SUBLANES, :]
    y_conv = b_gate * conv
    for g in range(N_CONV_GROUPS):
        sl = slice(g * LANES, (g + 1) * LANES)
        blk = y_conv[:, sl]
        ms = jnp.mean(blk * blk, axis=-1, keepdims=True)
        ybuf[:, sl] = (blk * lax.rsqrt(ms + RMS_EPS) * gnc_ref[:, sl]).astype(_BF16)

    gu = _gelu_exact(_dot(xn, win_ref[:, 3 * D_CONV:3 * D_CONV + D_SGU]))
    gv = _gelu_exact(_dot(xn, win_ref[:, 3 * D_CONV + D_SGU:D_IN_PROJ]))
    row_c = lax.broadcasted_iota(jnp.int32, (CHUNK, CHUNK), 0)
    col_c = lax.broadcasted_iota(jnp.int32, (CHUNK, CHUNK), 1)
    causal = row_c >= col_c
    for h in range(N_SGU_HEADS):
        sl = slice(h * SGU_HEAD_DIM, (h + 1) * SGU_HEAD_DIM)
        vh = gv[:, sl]
        mu = jnp.mean(vh, axis=-1, keepdims=True)
        xc = vh - mu
        var = jnp.mean(xc * xc, axis=-1, keepdims=True)
        vn = (xc * lax.rsqrt(var + LN_EPS) * slg_ref[:, sl] + slb_ref[:, sl]).astype(_BF16)
        wm = jnp.where(causal, sw_ref[h], 0.0).astype(_BF16)
        for c in range(TM // CHUNK):
            rows = slice(c * CHUNK, (c + 1) * CHUNK)
            mixed = _dot(wm, vn[rows, :]) + sb_ref[:, sl]
            ys = gu[rows, sl] * mixed
            ms = jnp.mean(ys * ys, axis=-1, keepdims=True)
            ybuf[rows, D_CONV + h * SGU_HEAD_DIM:D_CONV + (h + 1) * SGU_HEAD_DIM] = (
                ys * lax.rsqrt(ms + RMS_EPS) * gns_ref[:, sl]).astype(_BF16)

    h1 = x + _dot(ybuf[...], wout_ref[...])
    h1_ref[...] = h1
    xn2 = _rms_rows(h1, lnf_ref[...])

    x_hi = xn2.astype(_BF16)
    rounded = x_hi.astype(_F32)
    xn2_ref[...] = (
        lax.shift_right_logical(pltpu.bitcast(rounded[:, :D_MODEL // 2], jnp.uint32), jnp.uint32(16))
        | (pltpu.bitcast(rounded[:, D_MODEL // 2:], jnp.uint32) & jnp.uint32(0xFFFF0000)))
    x_lo = (xn2 - x_hi.astype(_F32)).astype(_BF16)
    p = _dot(x_hi, wr_ref[...]) + _dot(x_lo, wr_ref[...])
    logit_buf[...] = p[:, :LANES] + p[:, LANES:] + br_ref[...]


def _mixer_router(x2d, ln_mix_g, w_in_bf, conv_w, sgu_ln_g, sgu_ln_b, sgu_w, sgu_b_full,
                  gn_conv, gn_sgu, w_out_bf, ln_ffn_g, wr_split, br_pad):
    def full(a):
        return pl.BlockSpec(a.shape, lambda i: (0,) * a.ndim)

    row_blk = lambda w: pl.BlockSpec((TM, w), lambda i: (jnp.minimum(i, N_TILES - 1), 0))
    col_blk = pl.BlockSpec((2 * TOP_K, TM), lambda i: (0, jnp.maximum(i - 1, 0)))
    ins = [x2d, ln_mix_g, w_in_bf, conv_w, sgu_ln_g, sgu_ln_b, sgu_w, sgu_b_full,
           gn_conv, gn_sgu, w_out_bf, ln_ffn_g, wr_split, br_pad]
    in_specs = [row_blk(D_MODEL)] + [full(a) for a in ins[1:]]
    return pl.pallas_call(
        _mixer_router_kernel,
        grid=(N_TILES + 1,),
        in_specs=in_specs,
        out_specs=[row_blk(D_MODEL), row_blk(D_MODEL // 2), col_blk, col_blk,
                   pl.BlockSpec((N_EXPERTS, LANES), lambda i: (0, 0))],
        out_shape=[jax.ShapeDtypeStruct((N_TOK, D_MODEL), _F32),
                   jax.ShapeDtypeStruct((N_TOK, D_MODEL // 2), jnp.uint32),
                   jax.ShapeDtypeStruct((2 * TOP_K, N_TOK), jnp.int32),
                   jax.ShapeDtypeStruct((2 * TOP_K, N_TOK), _F32),
                   jax.ShapeDtypeStruct((N_EXPERTS, LANES), _F32)],
        scratch_shapes=[pltpu.VMEM((TM + SUBLANES, D_CONV), _F32),
                        pltpu.VMEM((N_EXPERTS, LANES), _F32),
                        pltpu.VMEM((TM, D_MODEL), _BF16),
                        pltpu.VMEM((TM, LANES), _F32)],
        compiler_params=pltpu.CompilerParams(dimension_semantics=("arbitrary",),
                                             vmem_limit_bytes=VMEM_LIMIT),
        name="mixer_router",
    )(*ins)


def _dispatch_kernel(dest_ref, x_hbm, bufx_hbm, x_pk, in_sem, out_sem):
    i = pl.program_id(0)
    n_steps = pl.num_programs(0)

    def fetch(blk, slot):
        r = pl.multiple_of(blk * TD, TD)
        return pltpu.make_async_copy(x_hbm.at[pl.ds(r, TD)], x_pk.at[slot], in_sem.at[slot])

    def row_copy(blk, slot, t8, u, k):
        t = t8 * SUBLANES + u if isinstance(t8, int) else pl.multiple_of(t8 * SUBLANES, SUBLANES) + u
        dst = dest_ref[(blk * TD + t) * TOP_K + k]
        return pltpu.make_async_copy(x_pk.at[slot, pl.ds(t, 1)], bufx_hbm.at[pl.ds(dst, 1)],
                                     out_sem.at[slot])

    def scatter(blk, slot, start):
        if start:
            for t8 in range(TD // SUBLANES):
                for u in range(SUBLANES):
                    for k in range(TOP_K):
                        row_copy(blk, slot, t8, u, k).start()
            return

        def body(t8, carry):
            for u in range(SUBLANES):
                for k in range(TOP_K):
                    row_copy(blk, slot, t8, u, k).wait()
            return carry
        lax.fori_loop(0, TD // SUBLANES, body, 0)

    @pl.when(i == 0)
    def _():
        fetch(0, 0).start()
        fetch(1, 1).start()

    for slot in range(N_BUF):
        @pl.when(i % N_BUF == slot)
        def _(slot=slot):
            fetch(i, slot).wait()
            scatter(i, slot, True)

            @pl.when(i >= 1)
            def _():
                scatter(i - 1, (slot - 1) % N_BUF, False)

            @pl.when(i + 2 < n_steps)
            def _():
                fetch(i + 2, (slot + 2) % N_BUF).start()

            @pl.when(i == n_steps - 1)
            def _():
                scatter(i, slot, False)


def _dispatch(dest, xn2):
    return pl.pallas_call(
        _dispatch_kernel,
        grid_spec=pltpu.PrefetchScalarGridSpec(
            num_scalar_prefetch=1,
            grid=(N_TOK // TD,),
            in_specs=[pl.BlockSpec(memory_space=pl.ANY)],
            out_specs=pl.BlockSpec(memory_space=pl.ANY),
            scratch_shapes=[pltpu.VMEM((N_BUF, TD, D_MODEL // 2), jnp.uint32),
                            pltpu.SemaphoreType.DMA((N_BUF,)),
                            pltpu.SemaphoreType.DMA((N_BUF,))],
        ),
        out_shape=jax.ShapeDtypeStruct((MAX_SUPER * SUPER_ROWS, D_MODEL // 2), jnp.uint32),
        compiler_params=pltpu.CompilerParams(dimension_semantics=("arbitrary",),
                                             vmem_limit_bytes=VMEM_LIMIT),
        name="dispatch",
    )(dest, xn2)


def _expert_kernel(sup_e_ref, sup_blk_ref, nchunk_ref, mmrows_ref, valid_ref,
                   x_hbm, wg_ref, wu_ref, bgu_ref, wd_ref, bd_ref, out_hbm,
                   x_stage, x_bf, acc, x_sem, o_sem):
    s = pl.program_id(0)
    j = pl.program_id(1)
    n_super = pl.num_programs(0)
    n_chunks = nchunk_ref[s]
    cur = s % 2

    def x_copy(sup, c):
        r = pl.multiple_of(sup_blk_ref[sup] * SUPER_ROWS + c * ROW_CHUNK, ROW_CHUNK)
        slot = c % X_SLOTS
        return pltpu.make_async_copy(x_hbm.at[pl.ds(r, ROW_CHUNK)], x_stage.at[slot],
                                     x_sem.at[slot])

    def x_finish(sup, c):
        x_copy(sup, c).wait()
        words = x_stage[c % X_SLOTS]
        row = c * ROW_CHUNK + lax.broadcasted_iota(jnp.int32, (ROW_CHUNK, 1), 0)
        keep = row < valid_ref[sup]
        lo = pltpu.bitcast(lax.shift_left(words, jnp.uint32(16)), _F32)
        hi = pltpu.bitcast(words & jnp.uint32(0xFFFF0000), _F32)
        rows = pl.ds(pl.multiple_of(c * ROW_CHUNK, ROW_CHUNK), ROW_CHUNK)
        x_bf[sup % 2, rows, 0:D_MODEL // 2] = jnp.where(keep, lo, 0.0).astype(_BF16)
        x_bf[sup % 2, rows, D_MODEL // 2:D_MODEL] = jnp.where(keep, hi, 0.0).astype(_BF16)

    def out_copy(sup, c):
        r = pl.multiple_of(sup_blk_ref[sup] * SUPER_ROWS + c * ROW_CHUNK, ROW_CHUNK)
        rows = pl.ds(pl.multiple_of(c * ROW_CHUNK, ROW_CHUNK), ROW_CHUNK)
        return pltpu.make_async_copy(acc.at[sup % 2, rows], out_hbm.at[pl.ds(r, ROW_CHUNK)],
                                     o_sem.at[sup % 2])

    def for_each_chunk(sup, fn):
        for c in range(CHUNKS_PER_SUPER):
            @pl.when(c < nchunk_ref[sup])
            def _(c=c):
                fn(sup, c)

    @pl.when((s == 0) & (j == 0))
    def _():
        def load(sup, c):
            x_copy(sup, c).start()
            x_finish(sup, c)
        for_each_chunk(0, load)

    @pl.when((s >= 2) & (j == 0))
    def _():
        for_each_chunk(jnp.maximum(s - 2, 0), lambda sup, c: out_copy(sup, c).wait())

    prev = jnp.maximum(s - 1, 0)

    @pl.when((s >= 1) & (j < nchunk_ref[prev]))
    def _():
        out_copy(prev, j).start()

    nxt = jnp.minimum(s + 1, n_super - 1)
    n_next = jnp.where(s + 1 < n_super, nchunk_ref[nxt], 0)

    @pl.when((j >= 1) & (j - 1 < n_next))
    def _():
        x_finish(nxt, j - 1)

    @pl.when(j < n_next)
    def _():
        x_copy(nxt, j).start()

    @pl.when(n_chunks > 0)
    def _():
        def partial_out(n_rows):
            rows = pl.ds(0, n_rows)
            xb = x_bf[cur, rows, :]
            gate = _dot(xb, wg_ref[0].astype(_BF16)) + bgu_ref[0, pl.ds(j, 1), :]
            up = _dot(xb, wu_ref[0].astype(_BF16)) + bgu_ref[0, pl.ds(N_FF_TILES + j, 1), :]
            gate = jnp.minimum(gate, SWIGLU_LIMIT)
            up = jnp.clip(up, -SWIGLU_LIMIT, SWIGLU_LIMIT)
            glu = gate * jax.nn.sigmoid(gate * SWIGLU_ALPHA)
            act = ((up + 1.0) * glu).astype(_BF16)
            return rows, _dot(act, wd_ref[0].astype(_BF16))

        def sweep(update):
            mm_rows = mmrows_ref[s]
            for n_rows in MATMUL_SIZES:
                @pl.when(mm_rows == n_rows)
                def _(n_rows=n_rows):
                    update(n_rows)

        def accumulate(n_rows):
            rows, o = partial_out(n_rows)
            acc[cur, rows, :] = jnp.where(j == 0, bd_ref[0], acc[cur, rows, :]) + o

        sweep(accumulate)

    @pl.when((s == n_super - 1) & (j == N_FF_TILES - 1))
    def _():
        for_each_chunk(s, lambda sup, c: out_copy(sup, c).start())

        @pl.when(s >= 1)
        def _():
            for_each_chunk(jnp.maximum(s - 1, 0), lambda sup, c: out_copy(sup, c).wait())

        for_each_chunk(s, lambda sup, c: out_copy(sup, c).wait())


def _experts(n_super, sup_e, sup_blk, nchunk, mm_rows, valid, buf_x, w_gate_up, b_gate_up3, w_down, b_down3):
    x_spec = pl.BlockSpec(memory_space=pl.ANY)
    wg_spec = pl.BlockSpec((1, D_MODEL, FF_TILE), lambda s, j, se, sb, nc, nh, va: (se[s], 0, j))
    wu_spec = pl.BlockSpec((1, D_MODEL, FF_TILE),
                           lambda s, j, se, sb, nc, nh, va: (se[s], 0, N_FF_TILES + j))
    bgu_spec = pl.BlockSpec((1, 2 * N_FF_TILES, FF_TILE), lambda s, j, se, sb, nc, nh, va: (se[s], 0, 0))
    wd_spec = pl.BlockSpec((1, FF_TILE, D_MODEL), lambda s, j, se, sb, nc, nh, va: (se[s], j, 0))
    bd_spec = pl.BlockSpec((1, 1, D_MODEL), lambda s, j, se, sb, nc, nh, va: (se[s], 0, 0))
    out_spec = pl.BlockSpec(memory_space=pl.ANY)
    return pl.pallas_call(
        _expert_kernel,
        grid_spec=pltpu.PrefetchScalarGridSpec(
            num_scalar_prefetch=5,
            grid=(n_super, N_FF_TILES),
            in_specs=[x_spec, wg_spec, wu_spec, bgu_spec, wd_spec, bd_spec],
            out_specs=out_spec,
            scratch_shapes=[pltpu.VMEM((X_SLOTS, ROW_CHUNK, D_MODEL // 2), jnp.uint32),
                            pltpu.VMEM((2, SUPER_ROWS, D_MODEL), _BF16),
                            pltpu.VMEM((2, SUPER_ROWS, D_MODEL), _F32),
                            pltpu.SemaphoreType.DMA((X_SLOTS,)),
                            pltpu.SemaphoreType.DMA((2,))],
        ),
        out_shape=jax.ShapeDtypeStruct((MAX_SUPER * SUPER_ROWS, D_MODEL), _F32),
        compiler_params=pltpu.CompilerParams(dimension_semantics=("arbitrary", "arbitrary"),
                                             vmem_limit_bytes=VMEM_LIMIT),
        name="experts",
    )(sup_e, sup_blk, nchunk, mm_rows, valid, buf_x, w_gate_up, w_gate_up, b_gate_up3, w_down, b_down3)


def _combine_kernel(dest_ref, h1_ref, gates_ref, lng_ref, y_hbm, out_ref, rows, sem):
    i = pl.program_id(0)
    n_steps = pl.num_programs(0)

    def row_copy(tile, slot, t8, u, k):
        t = t8 * SUBLANES + u if isinstance(t8, int) else pl.multiple_of(t8 * SUBLANES, SUBLANES) + u
        src = dest_ref[(tile * TC + t) * TOP_K + k]
        return pltpu.make_async_copy(y_hbm.at[pl.ds(src, 1)], rows.at[slot, k, pl.ds(t, 1)],
                                     sem.at[slot])

    def issue_tile(tile, slot):
        for t8 in range(TC // SUBLANES):
            for u in range(SUBLANES):
                for k in range(TOP_K):
                    row_copy(tile, slot, t8, u, k).start()

    def wait_tile(tile, slot):
        def body(t8, carry):
            for u in range(SUBLANES):
                for k in range(TOP_K):
                    row_copy(tile, slot, t8, u, k).wait()
            return carry
        lax.fori_loop(0, TC // SUBLANES, body, 0)

    @pl.when(i == 0)
    def _():
        issue_tile(0, 0)

    for slot in range(2):
        @pl.when(i % 2 == slot)
        def _(slot=slot):
            @pl.when(i + 1 < n_steps)
            def _():
                issue_tile(i + 1, 1 - slot)

            wait_tile(i, slot)
            g = gates_ref[...]
            acc = h1_ref[...]
            for k in range(TOP_K):
                acc = acc + rows[slot, k] * g[:, k:k + 1]
            out_ref[...] = _rms_rows(acc, lng_ref[...])


def _combine(dest, h1, gates, ln_final_g, y_sorted):
    return pl.pallas_call(
        _combine_kernel,
        grid_spec=pltpu.PrefetchScalarGridSpec(
            num_scalar_prefetch=1,
            grid=(N_TOK // TC,),
            in_specs=[pl.BlockSpec((TC, D_MODEL), lambda i, *_: (i, 0)),
                      pl.BlockSpec((TC, TOP_K), lambda i, *_: (i, 0)),
                      pl.BlockSpec((1, D_MODEL), lambda i, *_: (0, 0)),
                      pl.BlockSpec(memory_space=pl.ANY)],
            out_specs=pl.BlockSpec((TC, D_MODEL), lambda i, *_: (i, 0)),
            scratch_shapes=[pltpu.VMEM((2, TOP_K, TC, D_MODEL), _F32),
                            pltpu.SemaphoreType.DMA((2,))],
        ),
        out_shape=jax.ShapeDtypeStruct((N_TOK, D_MODEL), _F32),
        compiler_params=pltpu.CompilerParams(dimension_semantics=("arbitrary",),
                                             vmem_limit_bytes=VMEM_LIMIT),
        name="combine",
    )(dest, h1, gates, ln_final_g, y_sorted)


def kernel(x, ln_mix_g, w_in, conv_w, sgu_ln_g, sgu_ln_b, sgu_w, sgu_b, gn_conv, gn_sgu, w_out,
           ln_ffn_g, w_router, b_router, w_gate_up, b_gate_up, w_down, b_down, ln_final_g):
    row = lambda v: v.reshape(1, -1)
    x2d = x.reshape(N_TOK, D_MODEL)

    sgu_b_full = jnp.repeat(jnp.transpose(sgu_b), SGU_HEAD_DIM, axis=1)
    wr_hi = w_router.astype(_BF16)
    wr_lo = (w_router - wr_hi.astype(_F32)).astype(_BF16)
    pad = ((0, 0), (0, LANES - N_EXPERTS))
    wr_split = jnp.concatenate([jnp.pad(wr_hi, pad), jnp.pad(wr_lo, pad)], axis=1)
    br_pad = jnp.pad(b_router, (0, LANES - N_EXPERTS)).reshape(1, LANES)

    h1, xn2, route, gates, counts = _mixer_router(
        x2d, row(ln_mix_g), w_in.astype(_BF16), conv_w, row(sgu_ln_g), row(sgu_ln_b), sgu_w,
        sgu_b_full, row(gn_conv), row(gn_sgu), w_out.astype(_BF16), row(ln_ffn_g), wr_split, br_pad)

    sizes = counts[:, 0].astype(jnp.int32)
    n_sup = (sizes + SUPER_ROWS - 1) // SUPER_ROWS
    sup_end = jnp.cumsum(n_sup)
    sup_start = sup_end - n_sup
    total_sup = sup_end[-1]
    row_start = (sup_start * SUPER_ROWS).astype(jnp.int32)
    sid = jnp.arange(MAX_SUPER, dtype=jnp.int32)
    sid_eff = jnp.minimum(sid, total_sup - 1)
    sup_e = jnp.minimum(jnp.sum((sid_eff[:, None] >= sup_end[None, :]).astype(jnp.int32), axis=1),
                        N_EXPERTS - 1)
    of_expert = sup_e[:, None] == jnp.arange(N_EXPERTS, dtype=jnp.int32)[None, :]
    rows_left = jnp.sum(jnp.where(of_expert, sizes - (sid_eff[:, None] - sup_start) * SUPER_ROWS, 0),
                        axis=1)
    valid = jnp.clip(rows_left, 0, SUPER_ROWS)
    nchunk = jnp.where(sid < total_sup, (valid + ROW_CHUNK - 1) // ROW_CHUNK, 0).astype(jnp.int32)
    coarse = (valid + COARSE_STEP - 1) // COARSE_STEP * COARSE_STEP
    fine = (valid + FINE_STEP - 1) // FINE_STEP * FINE_STEP
    mm_rows = jnp.where(sid < total_sup, jnp.where(fine > FINE_ABOVE, fine, coarse), 0).astype(jnp.int32)
    eid = route[0:TOP_K]
    eid_start = jnp.sum(jnp.where(eid[None] == jnp.arange(N_EXPERTS, dtype=jnp.int32)[:, None, None],
                                  row_start[:, None, None], 0), axis=0)
    dest = jnp.transpose(eid_start + route[TOP_K:2 * TOP_K]).reshape(-1)
    gates = jnp.transpose(gates[0:TOP_K])

    buf_x = _dispatch(dest, xn2)
    y_sorted = _experts(total_sup, sup_e, sid_eff.astype(jnp.int32), nchunk, mm_rows,
                        valid.astype(jnp.int32), buf_x, w_gate_up,
                        b_gate_up.reshape(N_EXPERTS, 2 * N_FF_TILES, FF_TILE), w_down,
                        b_down.reshape(N_EXPERTS, 1, D_MODEL))
    out = _combine(dest, h1, gates, row(ln_final_g), y_sorted)
    return out.reshape(BATCH, SEQ, D_MODEL)
```

```python
import jax
import jax.numpy as jnp
from jax import lax
from jax.experimental import pallas as pl
from jax.experimental.pallas import tpu as pltpu

D_MODEL = 2048
BATCH = 2
SEQ = 4096
N_TOK = BATCH * SEQ

D_CONV = 1024
N_CONV_GROUPS = 8
CONV_WIDTH = 3
D_SGU = 1024
N_SGU_HEADS = 8
SGU_HEAD_DIM = 128
CHUNK = 128
D_IN_PROJ = 3 * D_CONV + 2 * D_SGU

N_EXPERTS = 32
TOP_K = 4
D_FF = 2048
SWIGLU_LIMIT = 7.0
SWIGLU_ALPHA = 1.702
RMS_EPS = 1e-5
LN_EPS = 1e-5

LANES = 128
SUBLANES = 8
VMEM_LIMIT = 56 * 1024 * 1024

TM = 256
N_TILES = N_TOK // TM
SUPER_ROWS = 1280
ROW_CHUNK = 256
FF_TILE = 256
N_FF_TILES = D_FF // FF_TILE
MAX_SUPER = (N_TOK * TOP_K + N_EXPERTS * (SUPER_ROWS - 1)) // SUPER_ROWS
CHUNKS_PER_SUPER = SUPER_ROWS // ROW_CHUNK
assert CHUNKS_PER_SUPER + 1 <= N_FF_TILES
X_SLOTS = 2
COARSE_STEP, FINE_STEP, FINE_ABOVE = 128, 64, 896
MATMUL_SIZES = (tuple(range(COARSE_STEP, FINE_ABOVE + 1, COARSE_STEP))
                + tuple(range(FINE_ABOVE + FINE_STEP, SUPER_ROWS + 1, FINE_STEP)))
TD = 256
TC = 256

_F32 = jnp.float32
_BF16 = jnp.bfloat16


def _dot(a, b):
    return jnp.dot(a, b, preferred_element_type=_F32)


def _gelu_exact(x):
    return 0.5 * x * (1.0 + lax.erf(x * (2.0 ** -0.5)))


def _rms_rows(x, gain):
    return x * lax.rsqrt(jnp.mean(x * x, axis=-1, keepdims=True) + RMS_EPS) * gain


def _route_tile(logits, active, route_ref, gates_ref, counts_ref, carry):
    cur = jnp.transpose(logits)[0:N_EXPERTS, :]
    expert = lax.broadcasted_iota(jnp.int32, (N_EXPERTS, TM), 0).astype(_F32)
    neg_inf = jnp.float32(-jnp.inf)

    vals, ids, onehots = [], [], []
    for _ in range(TOP_K):
        m = jnp.max(cur, axis=0, keepdims=True)
        idx = jnp.min(jnp.where(cur == m, expert, float(N_EXPERTS)), axis=0, keepdims=True)
        oh = expert == idx
        vals.append(m)
        ids.append(idx)
        onehots.append(oh)
        cur = jnp.where(oh, neg_inf, cur)
    exps = [jnp.exp(v - vals[0]) for v in vals]
    denom = exps[0] + exps[1] + exps[2] + exps[3]
    gates = [e / denom for e in exps]

    mask = jnp.where(active, (onehots[0] | onehots[1] | onehots[2] | onehots[3]).astype(_F32), 0.0)
    row_t = lax.broadcasted_iota(jnp.int32, (TM, TM), 0)
    col_t = lax.broadcasted_iota(jnp.int32, (TM, TM), 1)
    earlier = (row_t < col_t).astype(_BF16)
    before = _dot(mask.astype(_BF16), earlier) + carry[:, 0:1]
    ranks = [jnp.sum(jnp.where(oh, before, 0.0), axis=0, keepdims=True) for oh in onehots]
    carry[...] = carry[...] + jnp.sum(mask, axis=1, keepdims=True)
    counts_ref[...] = carry[...]

    out_row = lax.broadcasted_iota(jnp.int32, (2 * TOP_K, TM), 0)
    route = jnp.zeros((2 * TOP_K, TM), _F32)
    gate_out = jnp.zeros((2 * TOP_K, TM), _F32)
    for k in range(TOP_K):
        route = jnp.where(out_row == k, ids[k], route)
        route = jnp.where(out_row == TOP_K + k, ranks[k], route)
        gate_out = jnp.where(out_row == k, gates[k], gate_out)
    route_ref[...] = route.astype(jnp.int32)
    gates_ref[...] = gate_out


def _mixer_router_kernel(x_ref, lng_ref, win_ref, convw_ref, slg_ref, slb_ref, sw_ref, sb_ref,
                         gnc_ref, gns_ref, wout_ref, lnf_ref, wr_ref, br_ref,
                         h1_ref, xn2_ref, route_ref, gates_ref, counts_ref,
                         cbuf, carry, ybuf, logit_buf):
    i = pl.program_id(0)

    @pl.when(i == 0)
    def _():
        carry[...] = jnp.zeros_like(carry)
        logit_buf[...] = jnp.zeros_like(logit_buf)

    @pl.when(i % (SEQ // TM) == 0)
    def _():
        cbuf[0:SUBLANES, :] = jnp.zeros((SUBLANES, D_CONV), _F32)

    @pl.when(i == N_TILES)
    def _():
        _route_tile(logit_buf[...], True, route_ref, gates_ref, counts_ref, carry)

    @pl.when(i < N_TILES)
    def _():
        prev_logits = logit_buf[...]
        _mixer_tile(x_ref, lng_ref, win_ref, convw_ref, slg_ref, slb_ref, sw_ref, sb_ref,
                    gnc_ref, gns_ref, wout_ref, lnf_ref, wr_ref, br_ref,
                    h1_ref, xn2_ref, cbuf, ybuf, logit_buf)
        _route_tile(prev_logits, i > 0, route_ref, gates_ref, counts_ref, carry)


def _mixer_tile(x_ref, lng_ref, win_ref, convw_ref, slg_ref, slb_ref, sw_ref, sb_ref,
                gnc_ref, gns_ref, wout_ref, lnf_ref, wr_ref, br_ref,
                h1_ref, xn2_ref, cbuf, ybuf, logit_buf):
    x = x_ref[...]
    xn = _rms_rows(x, lng_ref[...]).astype(_BF16)

    b_gate = _dot(xn, win_ref[:, 0:D_CONV])
    c_gate = _dot(xn, win_ref[:, D_CONV:2 * D_CONV])
    hh = _dot(xn, win_ref[:, 2 * D_CONV:3 * D_CONV])
    ch = c_gate * hh
    cbuf[SUBLANES:SUBLANES + TM, :] = ch
    ch1 = cbuf[SUBLANES - 1:SUBLANES - 1 + TM, :]
    ch2 = cbuf[SUBLANES - 2:SUBLANES - 2 + TM, :]
    conv = convw_ref[0:1, :] * ch2 + convw_ref[1:2, :] * ch1 + convw_ref[2:3, :] * ch
    cbuf[0:SUBLANES, :] = cbuf[TM:TM + SUBLANES, :]
    y_conv = b_gate * conv
    for g in range(N_CONV_GROUPS):
        sl = slice(g * LANES, (g + 1) * LANES)
        blk = y_conv[:, sl]
        ms = jnp.mean(blk * blk, axis=-1, keepdims=True)
        ybuf[:, sl] = (blk * lax.rsqrt(ms + RMS_EPS) * gnc_ref[:, sl]).astype(_BF16)

    gu = _gelu_exact(_dot(xn, win_ref[:, 3 * D_CONV:3 * D_CONV + D_SGU]))
    gv = _gelu_exact(_dot(xn, win_ref[:, 3 * D_CONV + D_SGU:D_IN_PROJ]))
    row_c = lax.broadcasted_iota(jnp.int32, (CHUNK, CHUNK), 0)
    col_c = lax.broadcasted_iota(jnp.int32, (CHUNK, CHUNK), 1)
    causal = row_c >= col_c
    for h in range(N_SGU_HEADS):
        sl = slice(h * SGU_HEAD_DIM, (h + 1) * SGU_HEAD_DIM)
        vh = gv[:, sl]
        mu = jnp.mean(vh, axis=-1, keepdims=True)
        xc = vh - mu
        var = jnp.mean(xc * xc, axis=-1, keepdims=True)
        vn = (xc * lax.rsqrt(var + LN_EPS) * slg_ref[:, sl] + slb_ref[:, sl]).astype(_BF16)
        wm = jnp.where(causal, sw_ref[h], 0.0).astype(_BF16)
        for c in range(TM // CHUNK):
            rows = slice(c * CHUNK, (c + 1) * CHUNK)
            mixed = _dot(wm, vn[rows, :]) + sb_ref[:, sl]
            ys = gu[rows, sl] * mixed
            ms = jnp.mean(ys * ys, axis=-1, keepdims=True)
            ybuf[rows, D_CONV + h * SGU_HEAD_DIM:D_CONV + (h + 1) * SGU_HEAD_DIM] = (
                ys * lax.rsqrt(ms + RMS_EPS) * gns_ref[:, sl]).astype(_BF16)

    h1 = x + _dot(ybuf[...], wout_ref[...])
    h1_ref[...] = h1
    xn2 = _rms_rows(h1, lnf_ref[...])
    xn2_ref[...] = xn2

    logit_buf[...] = _dot(xn2.astype(_BF16), wr_ref[...]) + br_ref[...]


def _mixer_router(x2d, ln_mix_g, w_in_bf, conv_w, sgu_ln_g, sgu_ln_b, sgu_w, sgu_b_full,
                  gn_conv, gn_sgu, w_out_bf, ln_ffn_g, wr_split, br_pad):
    def full(a):
        return pl.BlockSpec(a.shape, lambda i: (0,) * a.ndim)

    row_blk = lambda w: pl.BlockSpec((TM, w), lambda i: (jnp.minimum(i, N_TILES - 1), 0))
    col_blk = pl.BlockSpec((2 * TOP_K, TM), lambda i: (0, jnp.maximum(i - 1, 0)))
    ins = [x2d, ln_mix_g, w_in_bf, conv_w, sgu_ln_g, sgu_ln_b, sgu_w, sgu_b_full,
           gn_conv, gn_sgu, w_out_bf, ln_ffn_g, wr_split, br_pad]
    in_specs = [row_blk(D_MODEL)] + [full(a) for a in ins[1:]]
    return pl.pallas_call(
        _mixer_router_kernel,
        grid=(N_TILES + 1,),
        in_specs=in_specs,
        out_specs=[row_blk(D_MODEL), row_blk(D_MODEL), col_blk, col_blk,
                   pl.BlockSpec((N_EXPERTS, LANES), lambda i: (0, 0))],
        out_shape=[jax.ShapeDtypeStruct((N_TOK, D_MODEL), _F32),
                   jax.ShapeDtypeStruct((N_TOK, D_MODEL), _F32),
                   jax.ShapeDtypeStruct((2 * TOP_K, N_TOK), jnp.int32),
                   jax.ShapeDtypeStruct((2 * TOP_K, N_TOK), _F32),
                   jax.ShapeDtypeStruct((N_EXPERTS, LANES), _F32)],
        scratch_shapes=[pltpu.VMEM((TM + SUBLANES, D_CONV), _F32),
                        pltpu.VMEM((N_EXPERTS, LANES), _F32),
                        pltpu.VMEM((TM, D_MODEL), _BF16),
                        pltpu.VMEM((TM, LANES), _F32)],
        compiler_params=pltpu.CompilerParams(dimension_semantics=("arbitrary",),
                                             vmem_limit_bytes=VMEM_LIMIT),
        name="mixer_router",
    )(*ins)


def _dispatch_kernel(dest_ref, x_hbm, bufx_hbm, x_in, x_pk, in_sem, out_sem):
    i = pl.program_id(0)
    n_steps = pl.num_programs(0)

    def fetch(blk, slot):
        r = pl.multiple_of(blk * TD, TD)
        return pltpu.make_async_copy(x_hbm.at[pl.ds(r, TD)], x_in.at[slot], in_sem.at[slot])

    def row_copy(blk, slot, t8, u, k):
        t = t8 * SUBLANES + u if isinstance(t8, int) else pl.multiple_of(t8 * SUBLANES, SUBLANES) + u
        dst = dest_ref[(blk * TD + t) * TOP_K + k]
        return pltpu.make_async_copy(x_pk.at[slot, pl.ds(t, 1)], bufx_hbm.at[pl.ds(dst, 1)],
                                     out_sem.at[slot])

    def scatter(blk, slot, start):
        if start:
            for t8 in range(TD // SUBLANES):
                for u in range(SUBLANES):
                    for k in range(TOP_K):
                        row_copy(blk, slot, t8, u, k).start()
            return

        def body(t8, carry):
            for u in range(SUBLANES):
                for k in range(TOP_K):
                    row_copy(blk, slot, t8, u, k).wait()
            return carry
        lax.fori_loop(0, TD // SUBLANES, body, 0)

    def pack(slot):
        rounded = x_in[slot].astype(_BF16).astype(_F32)
        lo = lax.shift_right_logical(pltpu.bitcast(rounded[:, :D_MODEL // 2], jnp.uint32), jnp.uint32(16))
        hi = pltpu.bitcast(rounded[:, D_MODEL // 2:], jnp.uint32) & jnp.uint32(0xFFFF0000)
        x_pk[slot] = lo | hi

    @pl.when(i == 0)
    def _():
        fetch(0, 0).start()

    for slot in range(2):
        @pl.when(i % 2 == slot)
        def _(slot=slot):
            @pl.when(i + 1 < n_steps)
            def _():
                fetch(i + 1, 1 - slot).start()

            fetch(i, slot).wait()
            pack(slot)
            scatter(i, slot, True)

            @pl.when(i >= 1)
            def _():
                scatter(i - 1, 1 - slot, False)

            @pl.when(i == n_steps - 1)
            def _():
                scatter(i, slot, False)


def _dispatch(dest, xn2):
    return pl.pallas_call(
        _dispatch_kernel,
        grid_spec=pltpu.PrefetchScalarGridSpec(
            num_scalar_prefetch=1,
            grid=(N_TOK // TD,),
            in_specs=[pl.BlockSpec(memory_space=pl.ANY)],
            out_specs=pl.BlockSpec(memory_space=pl.ANY),
            scratch_shapes=[pltpu.VMEM((2, TD, D_MODEL), _F32),
                            pltpu.VMEM((2, TD, D_MODEL // 2), jnp.uint32),
                            pltpu.SemaphoreType.DMA((2,)),
                            pltpu.SemaphoreType.DMA((2,))],
        ),
        out_shape=jax.ShapeDtypeStruct((MAX_SUPER * SUPER_ROWS, D_MODEL // 2), jnp.uint32),
        compiler_params=pltpu.CompilerParams(dimension_semantics=("arbitrary",),
                                             vmem_limit_bytes=VMEM_LIMIT),
        name="dispatch",
    )(dest, xn2)


def _expert_kernel(sup_e_ref, sup_blk_ref, nchunk_ref, mmrows_ref, valid_ref,
                   x_hbm, wg_ref, wu_ref, bgu_ref, wd_ref, bd_ref, out_hbm,
                   x_stage, x_bf, acc, x_sem, o_sem):
    s = pl.program_id(0)
    j = pl.program_id(1)
    n_super = pl.num_programs(0)
    n_chunks = nchunk_ref[s]
    cur = s % 2

    def x_copy(sup, c):
        r = pl.multiple_of(sup_blk_ref[sup] * SUPER_ROWS + c * ROW_CHUNK, ROW_CHUNK)
        slot = c % X_SLOTS
        return pltpu.make_async_copy(x_hbm.at[pl.ds(r, ROW_CHUNK)], x_stage.at[slot],
                                     x_sem.at[slot])

    def x_finish(sup, c):
        x_copy(sup, c).wait()
        words = x_stage[c % X_SLOTS]
        row = c * ROW_CHUNK + lax.broadcasted_iota(jnp.int32, (ROW_CHUNK, 1), 0)
        keep = row < valid_ref[sup]
        lo = pltpu.bitcast(lax.shift_left(words, jnp.uint32(16)), _F32)
        hi = pltpu.bitcast(words & jnp.uint32(0xFFFF0000), _F32)
        rows = pl.ds(pl.multiple_of(c * ROW_CHUNK, ROW_CHUNK), ROW_CHUNK)
        x_bf[sup % 2, rows, 0:D_MODEL // 2] = jnp.where(keep, lo, 0.0).astype(_BF16)
        x_bf[sup % 2, rows, D_MODEL // 2:D_MODEL] = jnp.where(keep, hi, 0.0).astype(_BF16)

    def out_copy(sup, c):
        r = pl.multiple_of(sup_blk_ref[sup] * SUPER_ROWS + c * ROW_CHUNK, ROW_CHUNK)
        rows = pl.ds(pl.multiple_of(c * ROW_CHUNK, ROW_CHUNK), ROW_CHUNK)
        return pltpu.make_async_copy(acc.at[sup % 2, rows], out_hbm.at[pl.ds(r, ROW_CHUNK)],
                                     o_sem.at[sup % 2])

    def for_each_chunk(sup, fn):
        for c in range(CHUNKS_PER_SUPER):
            @pl.when(c < nchunk_ref[sup])
            def _(c=c):
                fn(sup, c)

    @pl.when((s == 0) & (j == 0))
    def _():
        def load(sup, c):
            x_copy(sup, c).start()
            x_finish(sup, c)
        for_each_chunk(0, load)

    @pl.when((s >= 2) & (j == 0))
    def _():
        for_each_chunk(jnp.maximum(s - 2, 0), lambda sup, c: out_copy(sup, c).wait())

    prev = jnp.maximum(s - 1, 0)

    @pl.when((s >= 1) & (j < nchunk_ref[prev]))
    def _():
        out_copy(prev, j).start()

    nxt = jnp.minimum(s + 1, n_super - 1)
    n_next = jnp.where(s + 1 < n_super, nchunk_ref[nxt], 0)

    @pl.when((j >= 1) & (j - 1 < n_next))
    def _():
        x_finish(nxt, j - 1)

    @pl.when(j < n_next)
    def _():
        x_copy(nxt, j).start()

    @pl.when(n_chunks > 0)
    def _():
        def partial_out(n_rows):
            rows = pl.ds(0, n_rows)
            xb = x_bf[cur, rows, :]
            gate = _dot(xb, wg_ref[0].astype(_BF16)) + bgu_ref[0, pl.ds(j, 1), :]
            up = _dot(xb, wu_ref[0].astype(_BF16)) + bgu_ref[0, pl.ds(N_FF_TILES + j, 1), :]
            gate = jnp.minimum(gate, SWIGLU_LIMIT)
            up = jnp.clip(up, -SWIGLU_LIMIT, SWIGLU_LIMIT)
            glu = gate * jax.nn.sigmoid(gate * SWIGLU_ALPHA)
            act = ((up + 1.0) * glu).astype(_BF16)
            return rows, _dot(act, wd_ref[0].astype(_BF16))

        def sweep(update):
            mm_rows = mmrows_ref[s]
            for n_rows in MATMUL_SIZES:
                @pl.when(mm_rows == n_rows)
                def _(n_rows=n_rows):
                    update(n_rows)

        def accumulate(n_rows):
            rows, o = partial_out(n_rows)
            acc[cur, rows, :] = jnp.where(j == 0, bd_ref[0], acc[cur, rows, :]) + o

        sweep(accumulate)

    @pl.when((s == n_super - 1) & (j == N_FF_TILES - 1))
    def _():
        for_each_chunk(s, lambda sup, c: out_copy(sup, c).start())

        @pl.when(s >= 1)
        def _():
            for_each_chunk(jnp.maximum(s - 1, 0), lambda sup, c: out_copy(sup, c).wait())

        for_each_chunk(s, lambda sup, c: out_copy(sup, c).wait())


def _experts(n_super, sup_e, sup_blk, nchunk, mm_rows, valid, buf_x, w_gate_up, b_gate_up3, w_down, b_down3):
    x_spec = pl.BlockSpec(memory_space=pl.ANY)
    wg_spec = pl.BlockSpec((1, D_MODEL, FF_TILE), lambda s, j, se, sb, nc, nh, va: (se[s], 0, j))
    wu_spec = pl.BlockSpec((1, D_MODEL, FF_TILE),
                           lambda s, j, se, sb, nc, nh, va: (se[s], 0, N_FF_TILES + j))
    bgu_spec = pl.BlockSpec((1, 2 * N_FF_TILES, FF_TILE), lambda s, j, se, sb, nc, nh, va: (se[s], 0, 0))
    wd_spec = pl.BlockSpec((1, FF_TILE, D_MODEL), lambda s, j, se, sb, nc, nh, va: (se[s], j, 0))
    bd_spec = pl.BlockSpec((1, 1, D_MODEL), lambda s, j, se, sb, nc, nh, va: (se[s], 0, 0))
    out_spec = pl.BlockSpec(memory_space=pl.ANY)
    return pl.pallas_call(
        _expert_kernel,
        grid_spec=pltpu.PrefetchScalarGridSpec(
            num_scalar_prefetch=5,
            grid=(n_super, N_FF_TILES),
            in_specs=[x_spec, wg_spec, wu_spec, bgu_spec, wd_spec, bd_spec],
            out_specs=out_spec,
            scratch_shapes=[pltpu.VMEM((X_SLOTS, ROW_CHUNK, D_MODEL // 2), jnp.uint32),
                            pltpu.VMEM((2, SUPER_ROWS, D_MODEL), _BF16),
                            pltpu.VMEM((2, SUPER_ROWS, D_MODEL), _F32),
                            pltpu.SemaphoreType.DMA((X_SLOTS,)),
                            pltpu.SemaphoreType.DMA((2,))],
        ),
        out_shape=jax.ShapeDtypeStruct((MAX_SUPER * SUPER_ROWS, D_MODEL), _F32),
        compiler_params=pltpu.CompilerParams(dimension_semantics=("arbitrary", "arbitrary"),
                                             vmem_limit_bytes=VMEM_LIMIT),
        name="experts",
    )(sup_e, sup_blk, nchunk, mm_rows, valid, buf_x, w_gate_up, w_gate_up, b_gate_up3, w_down, b_down3)


def _combine_kernel(dest_ref, h1_ref, gates_ref, lng_ref, y_hbm, out_ref, rows, sem):
    i = pl.program_id(0)
    n_steps = pl.num_programs(0)

    def row_copy(tile, slot, t8, u, k):
        t = t8 * SUBLANES + u if isinstance(t8, int) else pl.multiple_of(t8 * SUBLANES, SUBLANES) + u
        src = dest_ref[(tile * TC + t) * TOP_K + k]
        return pltpu.make_async_copy(y_hbm.at[pl.ds(src, 1)], rows.at[slot, k, pl.ds(t, 1)],
                                     sem.at[slot])

    def issue_tile(tile, slot):
        for t8 in range(TC // SUBLANES):
            for u in range(SUBLANES):
                for k in range(TOP_K):
                    row_copy(tile, slot, t8, u, k).start()

    def wait_tile(tile, slot):
        def body(t8, carry):
            for u in range(SUBLANES):
                for k in range(TOP_K):
                    row_copy(tile, slot, t8, u, k).wait()
            return carry
        lax.fori_loop(0, TC // SUBLANES, body, 0)

    @pl.when(i == 0)
    def _():
        issue_tile(0, 0)

    for slot in range(2):
        @pl.when(i % 2 == slot)
        def _(slot=slot):
            @pl.when(i + 1 < n_steps)
            def _():
                issue_tile(i + 1, 1 - slot)

            wait_tile(i, slot)
            g = gates_ref[...]
            acc = h1_ref[...]
            for k in range(TOP_K):
                acc = acc + rows[slot, k] * g[:, k:k + 1]
            out_ref[...] = _rms_rows(acc, lng_ref[...])


def _combine(dest, h1, gates, ln_final_g, y_sorted):
    return pl.pallas_call(
        _combine_kernel,
        grid_spec=pltpu.PrefetchScalarGridSpec(
            num_scalar_prefetch=1,
            grid=(N_TOK // TC,),
            in_specs=[pl.BlockSpec((TC, D_MODEL), lambda i, *_: (i, 0)),
                      pl.BlockSpec((TC, TOP_K), lambda i, *_: (i, 0)),
                      pl.BlockSpec((1, D_MODEL), lambda i, *_: (0, 0)),
                      pl.BlockSpec(memory_space=pl.ANY)],
            out_specs=pl.BlockSpec((TC, D_MODEL), lambda i, *_: (i, 0)),
            scratch_shapes=[pltpu.VMEM((2, TOP_K, TC, D_MODEL), _F32),
                            pltpu.SemaphoreType.DMA((2,))],
        ),
        out_shape=jax.ShapeDtypeStruct((N_TOK, D_MODEL), _F32),
        compiler_params=pltpu.CompilerParams(dimension_semantics=("arbitrary",),
                                             vmem_limit_bytes=VMEM_LIMIT),
        name="combine",
    )(dest, h1, gates, ln_final_g, y_sorted)


def kernel(x, ln_mix_g, w_in, conv_w, sgu_ln_g, sgu_ln_b, sgu_w, sgu_b, gn_conv, gn_sgu, w_out,
           ln_ffn_g, w_router, b_router, w_gate_up, b_gate_up, w_down, b_down, ln_final_g):
    row = lambda v: v.reshape(1, -1)
    x2d = x.reshape(N_TOK, D_MODEL)

    sgu_b_full = jnp.repeat(jnp.transpose(sgu_b), SGU_HEAD_DIM, axis=1)
    wr_split = jnp.pad(w_router.astype(_BF16), ((0, 0), (0, LANES - N_EXPERTS)))
    br_pad = jnp.pad(b_router, (0, LANES - N_EXPERTS)).reshape(1, LANES)

    h1, xn2, route, gates, counts = _mixer_router(
        x2d, row(ln_mix_g), w_in.astype(_BF16), conv_w, row(sgu_ln_g), row(sgu_ln_b), sgu_w,
        sgu_b_full, row(gn_conv), row(gn_sgu), w_out.astype(_BF16), row(ln_ffn_g), wr_split, br_pad)

    sizes = counts[:, 0].astype(jnp.int32)
    n_sup = (sizes + SUPER_ROWS - 1) // SUPER_ROWS
    sup_end = jnp.cumsum(n_sup)
    sup_start = sup_end - n_sup
    total_sup = sup_end[-1]
    row_start = (sup_start * SUPER_ROWS).astype(jnp.int32)
    sid = jnp.arange(MAX_SUPER, dtype=jnp.int32)
    sid_eff = jnp.minimum(sid, total_sup - 1)
    sup_e = jnp.minimum(jnp.sum((sid_eff[:, None] >= sup_end[None, :]).astype(jnp.int32), axis=1),
                        N_EXPERTS - 1)
    of_expert = sup_e[:, None] == jnp.arange(N_EXPERTS, dtype=jnp.int32)[None, :]
    rows_left = jnp.sum(jnp.where(of_expert, sizes - (sid_eff[:, None] - sup_start) * SUPER_ROWS, 0),
                        axis=1)
    valid = jnp.clip(rows_left, 0, SUPER_ROWS)
    nchunk = jnp.where(sid < total_sup, (valid + ROW_CHUNK - 1) // ROW_CHUNK, 0).astype(jnp.int32)
    coarse = (valid + COARSE_STEP - 1) // COARSE_STEP * COARSE_STEP
    fine = (valid + FINE_STEP - 1) // FINE_STEP * FINE_STEP
    mm_rows = jnp.where(sid < total_sup, jnp.where(fine > FINE_ABOVE, fine, coarse), 0).astype(jnp.int32)
    eid = route[0:TOP_K]
    eid_start = jnp.sum(jnp.where(eid[None] == jnp.arange(N_EXPERTS, dtype=jnp.int32)[:, None, None],
                                  row_start[:, None, None], 0), axis=0)
    dest = jnp.transpose(eid_start + route[TOP_K:2 * TOP_K]).reshape(-1)
    gates = jnp.transpose(gates[0:TOP_K])

    buf_x = _dispatch(dest, xn2)
    y_sorted = _experts(total_sup, sup_e, sid_eff.astype(jnp.int32), nchunk, mm_rows,
                        valid.astype(jnp.int32), buf_x, w_gate_up,
                        b_gate_up.reshape(N_EXPERTS, 2 * N_FF_TILES, FF_TILE), w_down,
                        b_down.reshape(N_EXPERTS, 1, D_MODEL))
    out = _combine(dest, h1, gates, row(ln_final_g), y_sorted)
    return out.reshape(BATCH, SEQ, D_MODEL)
```
